```python
import jax
import jax.numpy as jnp
from jax import lax
import numpy as np

D_MODEL = 2048
BATCH = 2
SEQ = 4096
DEPTH = 1

HEAD_DIM = 128
GDN_HEADS = 8
NSA_HEADS = 8
NSA_KV_HEADS = 2
NSA_GROUP = NSA_HEADS // NSA_KV_HEADS
GDN_WIDTH = GDN_HEADS * HEAD_DIM
NSA_WIDTH = NSA_HEADS * HEAD_DIM
NSA_KV_WIDTH = NSA_KV_HEADS * HEAD_DIM
D_MIX = GDN_WIDTH + NSA_WIDTH
CONV_WIDTH = 4
GDN_CHUNK = 64
CMP_BLOCK = 32
CMP_STRIDE = 16
CMP_HIDDEN = 256
SEL_BLOCK = 64
SEL_TOPK = 16
N_LOCAL_BLOCKS = 2
WINDOW = 512
Q_BLOCK = 128
N_EXPERTS = 64
TOP_K = 8
EXPERT_FF = 512
SHARED_FF = 512
ROUTED_SCALE = 2.5
N_MOD = 6
EPS = 1e-6
IN_SPLIT_SIZES = (3 * GDN_WIDTH, GDN_HEADS, GDN_HEADS, GDN_WIDTH, NSA_WIDTH, 6 * NSA_KV_WIDTH, 3 * NSA_HEADS)
IN_COLS = 4 * GDN_WIDTH + 2 * GDN_HEADS + NSA_WIDTH + 6 * NSA_KV_WIDTH + 3 * NSA_HEADS

kernel_name = "hybrid_gdn_nsa_moe_block"


def rms_norm(x, w):
    xf = x.astype(jnp.float32)
    y = xf * lax.rsqrt(jnp.mean(xf * xf, axis=-1, keepdims=True) + EPS)
    return (y * w.astype(jnp.float32)).astype(x.dtype)


def l2_normalize(x):
    return x * lax.rsqrt(jnp.sum(x * x, axis=-1, keepdims=True) + EPS)


def masked_softmax(s, valid):
    s = jnp.where(valid, s.astype(jnp.float32), -jnp.inf)
    m = jnp.max(s, axis=-1, keepdims=True)
    m = jnp.where(jnp.isfinite(m), m, 0.0)
    e = jnp.exp(s - m)
    return e / jnp.maximum(jnp.sum(e, axis=-1, keepdims=True), 1e-30)


def causal_depthwise_conv(x, w):
    t = x.shape[1]
    xp = jnp.pad(x, ((0, 0), (CONV_WIDTH - 1, 0), (0, 0)))
    return sum(xp[:, i:i + t] * w[i] for i in range(CONV_WIDTH))


def gated_delta_rule(q, k, v, g, beta):
    b, t, h, dk = q.shape
    dv = v.shape[-1]
    c = GDN_CHUNK
    n = t // c

    def to_chunks(a):
        return a.reshape(b, n, c, h, -1).transpose(0, 3, 1, 2, 4)

    q = to_chunks(q) * dk ** -0.5
    k = to_chunks(k)
    v = to_chunks(v)
    beta = beta.reshape(b, n, c, h).transpose(0, 3, 1, 2)
    g_cum = jnp.cumsum(g.reshape(b, n, c, h).transpose(0, 3, 1, 2), axis=-1)
    causal = jnp.tril(jnp.ones((c, c), dtype=bool))
    strict = jnp.tril(jnp.ones((c, c), dtype=bool), -1)
    decay = jnp.exp(jnp.where(causal, g_cum[..., :, None] - g_cum[..., None, :], -jnp.inf))
    k_beta = k * beta[..., None]
    a_strict = jnp.where(strict, jnp.einsum('bhncd,bhnsd->bhncs', k_beta, k) * decay, 0.0)
    lhs = a_strict + jnp.eye(c, dtype=q.dtype)
    rhs = jnp.concatenate([v * beta[..., None], k_beta * jnp.exp(g_cum)[..., None]], axis=-1)
    sol = lax.linalg.triangular_solve(lhs, rhs, left_side=True, lower=True, unit_diagonal=True)
    u, w = sol[..., :dv], sol[..., dv:]
    qk = jnp.einsum('bhncd,bhnsd->bhncs', q, k) * decay
    q_dec = q * jnp.exp(g_cum)[..., None]
    g_last = g_cum[..., -1]
    k_dec = k * jnp.exp(g_last[..., None] - g_cum)[..., None]

    def chunk_step(state, xs):
        q_c, qk_c, u_c, w_c, k_c, gl = xs
        v_new = u_c - jnp.einsum('bhcd,bhde->bhce', w_c, state)
        o_c = jnp.einsum('bhcd,bhde->bhce', q_c, state) + jnp.einsum('bhcs,bhse->bhce', qk_c, v_new)
        state = state * jnp.exp(gl)[..., None, None] + jnp.einsum('bhcd,bhce->bhde', k_c, v_new)
        return state, o_c

    xs = tuple(jnp.moveaxis(a, 2, 0) for a in (q_dec, qk, u, w, k_dec, g_last))
    state0 = jnp.zeros((b, h, dk, dv), jnp.float32)
    _, o = lax.scan(chunk_step, state0, xs)
    return o.transpose(1, 0, 3, 2, 4).reshape(b, t, h, dv)


def gated_deltanet(qkv, beta_logit, a_logit, gate, conv_w, a_log, dt_bias, norm_w):
    b, t, _ = qkv.shape
    dtype = qkv.dtype
    f32 = jnp.float32
    qkv = jax.nn.silu(causal_depthwise_conv(qkv.astype(f32), conv_w.astype(f32)))
    q, k, v = jnp.split(qkv, 3, axis=-1)
    q = l2_normalize(q.reshape(b, t, GDN_HEADS, HEAD_DIM))
    k = l2_normalize(k.reshape(b, t, GDN_HEADS, HEAD_DIM))
    v = v.reshape(b, t, GDN_HEADS, HEAD_DIM)
    beta = jax.nn.sigmoid(beta_logit.astype(f32))
    g = -jnp.exp(a_log.astype(f32)) * jax.nn.softplus(a_logit.astype(f32) + dt_bias.astype(f32))
    o = gated_delta_rule(q, k, v, g, beta)
    o = rms_norm(o, norm_w) * jax.nn.silu(gate.astype(f32).reshape(b, t, GDN_HEADS, HEAD_DIM))
    return o.reshape(b, t, GDN_WIDTH).astype(dtype)


def compress_blocks(x, pe, w1, w2):
    b, t, hk, d = x.shape
    nc = (t - CMP_BLOCK) // CMP_STRIDE + 1
    idx = np.arange(nc)[:, None] * CMP_STRIDE + np.arange(CMP_BLOCK)[None, :]
    blocks = x[:, idx] + pe[:, None, :]
    flat = blocks.transpose(0, 1, 3, 2, 4).reshape(b, nc, hk, CMP_BLOCK * d)
    return jax.nn.silu(flat @ w1) @ w2


def native_sparse_attention(q, k_cmp, v_cmp, k_slc, v_slc, k_win, v_win, gate_logits,
                            pe_k, w1_k, w2_k, pe_v, w1_v, w2_v):
    b, t, _ = q.shape
    dtype = q.dtype
    f32 = jnp.float32
    scale = HEAD_DIM ** -0.5
    qg = q.reshape(b, t, NSA_KV_HEADS, NSA_GROUP, HEAD_DIM)
    k_cmp, v_cmp, k_slc, v_slc, k_win, v_win = [
        a.reshape(b, t, NSA_KV_HEADS, HEAD_DIM) for a in (k_cmp, v_cmp, k_slc, v_slc, k_win, v_win)]
    pos = jnp.arange(t)

    kc = compress_blocks(k_cmp, pe_k, w1_k, w2_k)
    vc = compress_blocks(v_cmp, pe_v, w1_v, w2_v)
    nc = kc.shape[1]
    c_end = np.arange(nc) * CMP_STRIDE + CMP_BLOCK - 1
    s_cmp = jnp.einsum('btkgd,bckd->bkgtc', qg, kc) * scale
    p_cmp = masked_softmax(s_cmp, c_end[None, :] <= pos[:, None])
    o_cmp = jnp.einsum('bkgtc,bckd->btkgd', p_cmp, vc.astype(f32)).reshape(b, t, NSA_HEADS, HEAD_DIM)

    nsel = t // SEL_BLOCK
    n_top = min(SEL_TOPK, nsel)
    c_start = np.arange(nc)[:, None] * CMP_STRIDE
    s_start = np.arange(nsel)[None, :] * SEL_BLOCK
    overlap = ((c_start < s_start + SEL_BLOCK) & (c_start + CMP_BLOCK > s_start)).astype(np.float32)
    importance = jnp.einsum('bkgtc,cj->bktj', p_cmp, overlap)
    cur = pos // SEL_BLOCK
    blk = jnp.arange(nsel)
    dist = cur[:, None] - blk[None, :]
    forced = (blk[None, :] == 0) | ((dist >= 0) & (dist < N_LOCAL_BLOCKS))
    importance = jnp.where(forced, jnp.inf, jnp.where(dist >= 0, importance, -jnp.inf))
    _, sel_idx = lax.top_k(importance, n_top)

    nq = t // Q_BLOCK
    q_blocks = qg.reshape(b, nq, Q_BLOCK, NSA_KV_HEADS, NSA_GROUP, HEAD_DIM).transpose(1, 0, 3, 2, 4, 5)
    idx_blocks = sel_idx.reshape(b, NSA_KV_HEADS, nq, Q_BLOCK, n_top).transpose(2, 0, 1, 3, 4)
    starts = jnp.arange(nq) * Q_BLOCK
    ks_blk = k_slc.reshape(b, nsel, SEL_BLOCK, NSA_KV_HEADS, HEAD_DIM).transpose(0, 3, 1, 2, 4)
    vs_blk = v_slc.reshape(b, nsel, SEL_BLOCK, NSA_KV_HEADS, HEAD_DIM).transpose(0, 3, 1, 2, 4)
    pad = ((0, 0), (0, 0), (WINDOW, 0), (0, 0))
    kw_pad = jnp.pad(k_win.transpose(0, 2, 1, 3), pad)
    vw_pad = jnp.pad(v_win.transpose(0, 2, 1, 3), pad)
    bi = jnp.arange(b)[:, None, None, None]
    hi = jnp.arange(NSA_KV_HEADS)[None, :, None, None]
    n_keys = n_top * SEL_BLOCK

    def query_block(args):
        qb, ib, start = args
        tq = start + jnp.arange(Q_BLOCK)
        ksel = ks_blk[bi, hi, ib].reshape(b, NSA_KV_HEADS, Q_BLOCK, n_keys, HEAD_DIM)
        vsel = vs_blk[bi, hi, ib].reshape(b, NSA_KV_HEADS, Q_BLOCK, n_keys, HEAD_DIM)
        kpos = (ib[..., None] * SEL_BLOCK + jnp.arange(SEL_BLOCK)).reshape(b, NSA_KV_HEADS, Q_BLOCK, n_keys)
        s_sel = jnp.einsum('bkqgd,bkqsd->bkqgs', qb, ksel) * scale
        p_sel = masked_softmax(s_sel, (kpos <= tq[:, None])[:, :, :, None, :])
        o_sel = jnp.einsum('bkqgs,bkqsd->bkqgd', p_sel, vsel.astype(f32))
        kw = lax.dynamic_slice_in_dim(kw_pad, start, Q_BLOCK + WINDOW, axis=2)
        vw = lax.dynamic_slice_in_dim(vw_pad, start, Q_BLOCK + WINDOW, axis=2)
        wpos = start - WINDOW + jnp.arange(Q_BLOCK + WINDOW)
        valid = (wpos[None, :] <= tq[:, None]) & (wpos[None, :] > tq[:, None] - WINDOW) & (wpos[None, :] >= 0)
        s_win = jnp.einsum('bkqgd,bksd->bkqgs', qb, kw) * scale
        p_win = masked_softmax(s_win, valid[:, None, :])
        o_win = jnp.einsum('bkqgs,bksd->bkqgd', p_win, vw.astype(f32))
        return o_sel, o_win

    o_sel, o_win = lax.map(query_block, (q_blocks, idx_blocks, starts))

    def unblock(o):
        return o.transpose(1, 0, 3, 2, 4, 5).reshape(b, t, NSA_HEADS, HEAD_DIM)

    gates = jax.nn.sigmoid(gate_logits.astype(f32)).reshape(b, t, 3, NSA_HEADS)[..., None]
    o = gates[:, :, 0] * o_cmp + gates[:, :, 1] * unblock(o_sel) + gates[:, :, 2] * unblock(o_win)
    return o.reshape(b, t, NSA_WIDTH).astype(dtype)


def moe_ffn(h, router_w, router_bias, w_gate, w_up, w_down, ws_gate, ws_up, ws_down):
    b, t, d = h.shape
    x = h.reshape(b * t, d)
    n_tok = b * t
    scores = jax.nn.sigmoid((x @ router_w).astype(jnp.float32))
    _, top_idx = lax.top_k(scores + router_bias.astype(jnp.float32), TOP_K)
    s_sel = jnp.take_along_axis(scores, top_idx, axis=-1)
    weights = s_sel / jnp.sum(s_sel, axis=-1, keepdims=True) * ROUTED_SCALE
    combine = jnp.zeros((n_tok, N_EXPERTS), jnp.float32).at[jnp.arange(n_tok)[:, None], top_idx].set(weights)

    def expert_step(acc, args):
        wg, wu, wd, cw = args
        y = (jax.nn.silu(x @ wg) * (x @ wu)) @ wd
        return acc + cw[:, None].astype(y.dtype) * y, None

    routed, _ = lax.scan(expert_step, jnp.zeros_like(x), (w_gate, w_up, w_down, combine.T))
    shared = (jax.nn.silu(x @ ws_gate) * (x @ ws_up)) @ ws_down
    return (routed + shared).reshape(b, t, d)


def setup_inputs(seed: int = 0) -> dict:
    key = jax.random.key(seed)
    ks = jax.random.split(key, 32)
    f32 = jnp.float32
    nrm = lambda k, shape, s: jax.random.normal(k, shape, f32) * s
    dt = jnp.exp(jax.random.uniform(ks[8], (DEPTH, GDN_HEADS), f32, np.log(1e-3), np.log(1e-1)))
    return {
        "x": nrm(ks[0], (BATCH, SEQ, D_MODEL), 1.0),
        "c": nrm(ks[1], (BATCH, D_MODEL), 1.0),
        "w_ada": nrm(ks[2], (DEPTH, D_MODEL, N_MOD * D_MODEL), 0.5 * D_MODEL ** -0.5),
        "b_ada": nrm(ks[3], (DEPTH, N_MOD * D_MODEL), 0.01),
        "norm_attn_w": 1.0 + nrm(ks[4], (DEPTH, D_MODEL), 0.02),
        "norm_ffn_w": 1.0 + nrm(ks[5], (DEPTH, D_MODEL), 0.02),
        "norm_final_w": 1.0 + nrm(ks[6], (D_MODEL,), 0.02),
        "w_in": nrm(ks[7], (DEPTH, D_MODEL, IN_COLS), D_MODEL ** -0.5),
        "gdn_conv_w": nrm(ks[9], (DEPTH, CONV_WIDTH, 3 * GDN_WIDTH), CONV_WIDTH ** -0.5),
        "gdn_a_log": jnp.log(jax.random.uniform(ks[10], (DEPTH, GDN_HEADS), f32, 1.0, 16.0)),
        "gdn_dt_bias": dt + jnp.log(-jnp.expm1(-dt)),
        "gdn_norm_w": 1.0 + nrm(ks[11], (DEPTH, HEAD_DIM), 0.02),
        "cmp_pe_k": nrm(ks[12], (DEPTH, CMP_BLOCK, HEAD_DIM), 0.02),
        "cmp_w1_k": nrm(ks[13], (DEPTH, CMP_BLOCK * HEAD_DIM, CMP_HIDDEN), (CMP_BLOCK * HEAD_DIM) ** -0.5),
        "cmp_w2_k": nrm(ks[14], (DEPTH, CMP_HIDDEN, HEAD_DIM), CMP_HIDDEN ** -0.5),
        "cmp_pe_v": nrm(ks[15], (DEPTH, CMP_BLOCK, HEAD_DIM), 0.02),
        "cmp_w1_v": nrm(ks[16], (DEPTH, CMP_BLOCK * HEAD_DIM, CMP_HIDDEN), (CMP_BLOCK * HEAD_DIM) ** -0.5),
        "cmp_w2_v": nrm(ks[17], (DEPTH, CMP_HIDDEN, HEAD_DIM), CMP_HIDDEN ** -0.5),
        "w_out": nrm(ks[18], (DEPTH, D_MIX, D_MODEL), D_MIX ** -0.5),
        "router_w": nrm(ks[19], (DEPTH, D_MODEL, N_EXPERTS), D_MODEL ** -0.5),
        "router_bias": nrm(ks[20], (DEPTH, N_EXPERTS), 0.01),
        "expert_w_gate": nrm(ks[21], (DEPTH, N_EXPERTS, D_MODEL, EXPERT_FF), D_MODEL ** -0.5),
        "expert_w_up": nrm(ks[22], (DEPTH, N_EXPERTS, D_MODEL, EXPERT_FF), D_MODEL ** -0.5),
        "expert_w_down": nrm(ks[23], (DEPTH, N_EXPERTS, EXPERT_FF, D_MODEL), EXPERT_FF ** -0.5),
        "shared_w_gate": nrm(ks[24], (DEPTH, D_MODEL, SHARED_FF), D_MODEL ** -0.5),
        "shared_w_up": nrm(ks[25], (DEPTH, D_MODEL, SHARED_FF), D_MODEL ** -0.5),
        "shared_w_down": nrm(ks[26], (DEPTH, SHARED_FF, D_MODEL), SHARED_FF ** -0.5),
    }


def reference(x, c, w_ada, b_ada, norm_attn_w, norm_ffn_w, norm_final_w, w_in, gdn_conv_w, gdn_a_log,
              gdn_dt_bias, gdn_norm_w, cmp_pe_k, cmp_w1_k, cmp_w2_k, cmp_pe_v, cmp_w1_v, cmp_w2_v, w_out,
              router_w, router_bias, expert_w_gate, expert_w_up, expert_w_down, shared_w_gate, shared_w_up,
              shared_w_down):
    cond = jax.nn.silu(c)
    split_points = [int(p) for p in np.cumsum(IN_SPLIT_SIZES)[:-1]]
    for l in range(DEPTH):
        mod = cond @ w_ada[l] + b_ada[l]
        shift_a, scale_a, gate_a, shift_m, scale_m, gate_m = [m[:, None, :] for m in jnp.split(mod, N_MOD, axis=-1)]
        h = rms_norm(x, norm_attn_w[l]) * (1.0 + scale_a) + shift_a
        proj = h @ w_in[l]
        qkv_g, beta_g, a_g, gate_g, q_n, kv_n, gate_n = jnp.split(proj, split_points, axis=-1)
        k_cmp, v_cmp, k_slc, v_slc, k_win, v_win = jnp.split(kv_n, 6, axis=-1)
        y_gdn = gated_deltanet(qkv_g, beta_g, a_g, gate_g, gdn_conv_w[l], gdn_a_log[l], gdn_dt_bias[l], gdn_norm_w[l])
        y_nsa = native_sparse_attention(q_n, k_cmp, v_cmp, k_slc, v_slc, k_win, v_win, gate_n,
                                        cmp_pe_k[l], cmp_w1_k[l], cmp_w2_k[l], cmp_pe_v[l], cmp_w1_v[l], cmp_w2_v[l])
        x = x + gate_a * (jnp.concatenate([y_gdn, y_nsa], axis=-1) @ w_out[l])
        h = rms_norm(x, norm_ffn_w[l]) * (1.0 + scale_m) + shift_m
        x = x + gate_m * moe_ffn(h, router_w[l], router_bias[l], expert_w_gate[l], expert_w_up[l], expert_w_down[l],
                                 shared_w_gate[l], shared_w_up[l], shared_w_down[l])
    return rms_norm(x, norm_final_w)
```

```python
import functools

import jax
import jax.numpy as jnp
from jax import lax
from jax.experimental import pallas as pl
from jax.experimental.pallas import tpu as pltpu

F32 = jnp.float32
BF16 = jnp.bfloat16
HI = lax.Precision.HIGHEST

HEAD_DIM = 128
GDN_HEADS = 8
NSA_HEADS = 8
NSA_KV_HEADS = 2
NSA_GROUP = NSA_HEADS // NSA_KV_HEADS
GDN_WIDTH = GDN_HEADS * HEAD_DIM
NSA_WIDTH = NSA_HEADS * HEAD_DIM
NSA_KV_WIDTH = NSA_KV_HEADS * HEAD_DIM
CONV_WIDTH = 4
GDN_CHUNK = 64
CMP_BLOCK = 32
CMP_STRIDE = 16
SEL_BLOCK = 64
SEL_TOPK = 16
N_LOCAL_BLOCKS = 2
WINDOW = 512
Q_BLOCK = 128
TOP_K = 8
ROUTED_SCALE = 2.5
N_MOD = 6
EPS = 1e-6

COL_GDN_Q = 0
COL_GDN_K = GDN_HEADS
COL_GDN_V = 2 * GDN_HEADS
COL_GDN_GATE = 3 * GDN_HEADS
COL_NSA_Q = 4 * GDN_HEADS
COL_KV = COL_NSA_Q + NSA_HEADS
COL_SMALL = COL_KV + 6 * NSA_KV_HEADS
N_COL_BLOCKS = COL_SMALL + 1
PROJ_TN = 2304
PROJ_COLS = 3 * PROJ_TN
SMALL_BETA = 0
SMALL_A = GDN_HEADS
SMALL_NSA_GATE = 2 * GDN_HEADS

NEG_BIG = -1e30
VMEM_LIMIT = 48 * 1024 * 1024


def _cparams(semantics):
    return pltpu.CompilerParams(dimension_semantics=semantics, vmem_limit_bytes=VMEM_LIMIT)


def _silu(x):
    return x * jax.nn.sigmoid(x)


def _dot(a, b, precision=None):
    return jnp.dot(a, b, preferred_element_type=F32, precision=precision)


def _dot_nt(a, b, precision=None):
    return lax.dot_general(a, b, (((1,), (1,)), ((), ())), preferred_element_type=F32, precision=precision)


def _dot_tn(a, b, precision=None):
    return lax.dot_general(a, b, (((0,), (0,)), ((), ())), preferred_element_type=F32, precision=precision)


def _lane_pick(x, idx):
    lane = lax.broadcasted_iota(jnp.int32, x.shape, 1)
    return jnp.sum(jnp.where(lane == idx, x, 0.0), axis=-1, keepdims=True)


def _ada_kernel(ct_ref, w_ref, b_ref, o_ref):
    c = ct_ref[...]
    cond = _silu(c)
    w = w_ref[...]
    rows = [jnp.sum(w * cond[:, b:b + 1], axis=0, keepdims=True) for b in range(c.shape[1])]
    o_ref[...] = jnp.concatenate(rows, axis=0) + b_ref[...]


def _ada(c, w_ada, b_ada, tn=512):
    bsz, d = c.shape
    n = w_ada.shape[1]
    return pl.pallas_call(
        _ada_kernel,
        grid=(n // tn,),
        in_specs=[
            pl.BlockSpec((d, bsz), lambda j: (0, 0)),
            pl.BlockSpec((d, tn), lambda j: (0, j)),
            pl.BlockSpec((1, tn), lambda j: (0, j)),
        ],
        out_specs=pl.BlockSpec((bsz, tn), lambda j: (0, j)),
        out_shape=jax.ShapeDtypeStruct((bsz, n), F32),
        compiler_params=_cparams(("arbitrary",)),
        name="ada",
    )(c.T, w_ada, b_ada.reshape(1, n))


def _modulated_norm(x, nw, scale, shift):
    var = jnp.mean(x * x, axis=-1, keepdims=True)
    h = x * lax.rsqrt(var + EPS) * nw
    return h * (1.0 + scale) + shift


def _proj_kernel(x_ref, shift_ref, scale_ref, nw_ref, w_ref, o_ref):
    h = _modulated_norm(x_ref[...], nw_ref[...], scale_ref[...], shift_ref[...])
    o_ref[...] = _dot(h.astype(BF16), w_ref[...])


def _proj(x, mod4, norm_w, w_r, tm=256):
    bsz, t, d = x.shape
    n = w_r.shape[1]
    return pl.pallas_call(
        _proj_kernel,
        grid=(n // PROJ_TN, bsz, t // tm),
        in_specs=[
            pl.BlockSpec((None, tm, d), lambda j, b, i: (b, i, 0)),
            pl.BlockSpec((None, None, 1, d), lambda j, b, i: (b, 0, 0, 0)),
            pl.BlockSpec((None, None, 1, d), lambda j, b, i: (b, 1, 0, 0)),
            pl.BlockSpec((1, d), lambda j, b, i: (0, 0)),
            pl.BlockSpec((d, PROJ_TN), lambda j, b, i: (0, j)),
        ],
        out_specs=pl.BlockSpec((None, tm, PROJ_TN), lambda j, b, i: (b, i, j)),
        out_shape=jax.ShapeDtypeStruct((bsz, t, n), F32),
        compiler_params=_cparams(("arbitrary", "arbitrary", "arbitrary")),
        name="proj",
    )(x, mod4, mod4, norm_w.reshape(1, d), w_r)


def _gdn_kernel(q_ref, k_ref, v_ref, gate_ref, small_ref, hq_ref, hk_ref, hv_ref,
                cwq_ref, cwk_ref, cwv_ref, alog_ref, dtb_ref, nw_ref, o_ref, s_ref, xc_ref, *, tt):
    c = GDN_CHUNK
    nch = tt // c
    h = pl.program_id(1)
    first = pl.program_id(2) == 0

    @pl.when(first)
    def _():
        s_ref[...] = jnp.zeros_like(s_ref)

    def conv_silu(x_ref, halo_ref, cw_ref):
        xc_ref[0:8, :] = jnp.where(first, 0.0, halo_ref[...])
        xc_ref[8:tt + 8, :] = x_ref[...]
        cw = cw_ref[...]
        off = 8 - (CONV_WIDTH - 1)
        y = cw[0:1, :] * xc_ref[off:off + tt, :]
        for i in range(1, CONV_WIDTH):
            y = y + cw[i:i + 1, :] * xc_ref[off + i:off + i + tt, :]
        return _silu(y)

    q = conv_silu(q_ref, hq_ref, cwq_ref)
    k = conv_silu(k_ref, hk_ref, cwk_ref)
    v = conv_silu(v_ref, hv_ref, cwv_ref)
    q = q * lax.rsqrt(jnp.sum(q * q, axis=-1, keepdims=True) + EPS) * (HEAD_DIM ** -0.5)
    k = k * lax.rsqrt(jnp.sum(k * k, axis=-1, keepdims=True) + EPS)

    small = small_ref[...]
    beta = jax.nn.sigmoid(_lane_pick(small, SMALL_BETA + h))
    z = _lane_pick(small, SMALL_A + h) + dtb_ref[...]
    softplus = jnp.maximum(z, 0.0) + jnp.log(1.0 + jnp.exp(-jnp.abs(z)))
    g = -jnp.exp(alog_ref[...]) * softplus

    ri = lax.broadcasted_iota(jnp.int32, (tt, tt), 0)
    ci = lax.broadcasted_iota(jnp.int32, (tt, tt), 1)
    same = (ri // c) == (ci // c)
    lower = jnp.where(same & (ci <= ri), 1.0, 0.0)
    upper = jnp.where(same & (ci > ri), 1.0, 0.0)
    g_lanes = jnp.broadcast_to(g, (tt, HEAD_DIM))
    gcum = _dot(lower, g_lanes, HI)
    grest = _dot(upper, g_lanes, HI)
    kpos = lax.broadcasted_iota(jnp.int32, (tt, c), 0) % c
    jj = lax.broadcasted_iota(jnp.int32, (tt, c), 1)
    gdiff = _dot(lower, jnp.where(kpos > jj, jnp.broadcast_to(g, (tt, c)), 0.0), HI)
    decay = jnp.where(kpos >= jj, jnp.exp(gdiff), 0.0).reshape(nch, c, c)
    strict = (kpos > jj).reshape(nch, c, c)
    eye = jnp.where(kpos == jj, 1.0, 0.0).reshape(nch, c, c)

    def bmm(a, b):
        return jnp.einsum('nij,njk->nik', a, b, preferred_element_type=F32, precision=HI)

    def bmm_nt(a, b):
        return jnp.einsum('nid,njd->nij', a, b, preferred_element_type=F32, precision=HI)

    k3 = k.reshape(nch, c, HEAD_DIM)
    kb = k * beta
    a = jnp.where(strict, bmm_nt(kb.reshape(nch, c, HEAD_DIM), k3) * decay, 0.0)
    inv = eye - a
    p = bmm(a, a)
    n_sq = (c - 1).bit_length() - 1
    for lvl in range(n_sq):
        inv = inv + bmm(inv, p)
        if lvl + 1 < n_sq:
            p = bmm(p, p)
    egc = jnp.exp(gcum)
    u3 = bmm(inv, (v * beta).reshape(nch, c, HEAD_DIM))
    w3 = bmm(inv, (kb * egc).reshape(nch, c, HEAD_DIM))
    qk3 = bmm_nt(q.reshape(nch, c, HEAD_DIM), k3) * decay
    qd3 = (q * egc).reshape(nch, c, HEAD_DIM)
    kd3 = (k * jnp.exp(grest)).reshape(nch, c, HEAD_DIM)
    gl3 = egc.reshape(nch, c, HEAD_DIM)

    state = s_ref[...]
    outs = []
    for n in range(nch):
        v_new = u3[n] - _dot(w3[n], state, HI)
        outs.append(_dot(qd3[n], state, HI) + _dot(qk3[n], v_new, HI))
        state = state * gl3[n, c - 1:c, :] + _dot_tn(kd3[n], v_new, HI)
    s_ref[...] = state
    o = jnp.concatenate(outs, axis=0)
    o = o * lax.rsqrt(jnp.mean(o * o, axis=-1, keepdims=True) + EPS) * nw_ref[...]
    o_ref[...] = (o * _silu(gate_ref[...])).astype(o_ref.dtype)


def _gdn(proj, conv_w, a_log, dt_bias, norm_w, tt=256):
    bsz, t, _ = proj.shape
    hd = HEAD_DIM

    def col(base):
        return pl.BlockSpec((None, tt, hd), lambda b, h, i: (b, i, base + h))

    def halo(base):
        return pl.BlockSpec((None, 8, hd), lambda b, h, i: (b, jnp.maximum(i * (tt // 8) - 1, 0), base + h))

    def cw(base):
        return pl.BlockSpec((CONV_WIDTH, hd), lambda b, h, i: (0, base + h))

    scalar = pl.BlockSpec((None, 1, 1), lambda b, h, i: (h, 0, 0))
    return pl.pallas_call(
        functools.partial(_gdn_kernel, tt=tt),
        grid=(bsz, GDN_HEADS, t // tt),
        in_specs=[
            col(COL_GDN_Q), col(COL_GDN_K), col(COL_GDN_V), col(COL_GDN_GATE),
            pl.BlockSpec((None, tt, hd), lambda b, h, i: (b, i, COL_SMALL)),
            halo(COL_GDN_Q), halo(COL_GDN_K), halo(COL_GDN_V),
            cw(COL_GDN_Q), cw(COL_GDN_K), cw(COL_GDN_V),
            scalar, scalar,
            pl.BlockSpec((1, hd), lambda b, h, i: (0, 0)),
        ],
        out_specs=pl.BlockSpec((None, tt, hd), lambda b, h, i: (b, i, h)),
        out_shape=jax.ShapeDtypeStruct((bsz, t, GDN_WIDTH), BF16),
        scratch_shapes=[pltpu.VMEM((hd, hd), F32), pltpu.VMEM((tt + 8, hd), F32)],
        compiler_params=_cparams(("arbitrary", "arbitrary", "arbitrary")),
        name="gdn",
    )(proj, proj, proj, proj, proj, proj, proj, proj, conv_w, conv_w, conv_w,
      a_log.reshape(GDN_HEADS, 1, 1), dt_bias.reshape(GDN_HEADS, 1, 1), norm_w.reshape(1, hd))


def _compress_kernel(x_ref, pe_ref, w1_ref, w2_ref, o_ref):
    x = x_ref[...]
    half = x.shape[1]
    a = _dot(x + pe_ref[0:1, :], w1_ref[0:half, :], HI)
    b = _dot(x + pe_ref[1:2, :], w1_ref[half:2 * half, :], HI)
    hid = _silu(a + pltpu.roll(b, b.shape[0] - 1, 0))
    o_ref[...] = _dot(hid, w2_ref[...], HI)


def _compress(x16, pe, w1, w2):
    bsz, hk, r, wdt = x16.shape
    hid = w1.shape[1]
    return pl.pallas_call(
        _compress_kernel,
        grid=(bsz, hk),
        in_specs=[
            pl.BlockSpec((None, None, r, wdt), lambda b, h: (b, h, 0, 0)),
            pl.BlockSpec((2, wdt), lambda b, h: (0, 0)),
            pl.BlockSpec((2 * wdt, hid), lambda b, h: (0, 0)),
            pl.BlockSpec((hid, HEAD_DIM), lambda b, h: (0, 0)),
        ],
        out_specs=pl.BlockSpec((None, None, r, HEAD_DIM), lambda b, h: (b, h, 0, 0)),
        out_shape=jax.ShapeDtypeStruct((bsz, hk, r, HEAD_DIM), F32),
        compiler_params=_cparams(("arbitrary", "arbitrary")),
        name="compress",
    )(x16, pe.reshape(2, wdt), w1, w2)


def _cmpattn_kernel(q_ref, kc_ref, vc_ref, o_ref, sel_ref, *, tq, nc, nsel):
    ncp = kc_ref.shape[0]
    scale = HEAD_DIM ** -0.5
    t0 = pl.program_id(2) * tq
    kc = kc_ref[...]
    vc = vc_ref[...]
    pos_r = t0 + lax.broadcasted_iota(jnp.int32, (tq, ncp), 0)
    c_r = lax.broadcasted_iota(jnp.int32, (tq, ncp), 1)
    valid_r = (c_r * CMP_STRIDE + CMP_BLOCK - 1 <= pos_r) & (c_r < nc)
    pos_c = t0 + lax.broadcasted_iota(jnp.int32, (ncp, tq), 1)
    c_c = lax.broadcasted_iota(jnp.int32, (ncp, tq), 0)
    valid_c = (c_c * CMP_STRIDE + CMP_BLOCK - 1 <= pos_c) & (c_c < nc)
    psum = jnp.zeros((ncp, tq), F32)
    for g in range(NSA_GROUP):
        qg = q_ref[:, g * HEAD_DIM:(g + 1) * HEAD_DIM]
        s = jnp.where(valid_r, _dot_nt(qg, kc, HI) * scale, NEG_BIG)
        e = jnp.where(valid_r, jnp.exp(s - jnp.max(s, axis=-1, keepdims=True)), 0.0)
        p = e * (1.0 / jnp.maximum(jnp.sum(e, axis=-1, keepdims=True), 1e-30))
        o_ref[:, g * HEAD_DIM:(g + 1) * HEAD_DIM] = _dot(p, vc, HI)
        st = jnp.where(valid_c, _dot_nt(kc, qg, HI) * scale, NEG_BIG)
        et = jnp.where(valid_c, jnp.exp(st - jnp.max(st, axis=0, keepdims=True)), 0.0)
        psum = psum + et * (1.0 / jnp.maximum(jnp.sum(et, axis=0, keepdims=True), 1e-30))
    j_o = lax.broadcasted_iota(jnp.int32, (nsel, ncp), 0) * SEL_BLOCK
    c_o = lax.broadcasted_iota(jnp.int32, (nsel, ncp), 1) * CMP_STRIDE
    overlap = jnp.where((c_o < j_o + SEL_BLOCK) & (c_o + CMP_BLOCK > j_o), 1.0, 0.0)
    imp = _dot(overlap, psum, HI)
    blk = lax.broadcasted_iota(jnp.int32, (nsel, tq), 0)
    dist = (t0 + lax.broadcasted_iota(jnp.int32, (nsel, tq), 1)) // SEL_BLOCK - blk
    forced = (blk == 0) | ((dist >= 0) & (dist < N_LOCAL_BLOCKS))
    val = jnp.where(forced, -NEG_BIG, jnp.where(dist >= 0, imp, -1.0))
    rank = jnp.zeros((nsel, tq), F32)
    for j in range(nsel):
        row = val[j:j + 1, :]
        rank = rank + jnp.where((row > val) | ((row == val) & (blk > j)), 1.0, 0.0)
    sel = jnp.where((rank < min(SEL_TOPK, nsel)) & (dist >= 0), 1.0, 0.0)
    sel_ref[...] = sel.T


def _cmpattn(proj, kc, vc, tq=256):
    bsz, t, _ = proj.shape
    ncp = kc.shape[2]
    nc = (t - CMP_BLOCK) // CMP_STRIDE + 1
    nsel = t // SEL_BLOCK
    gw = NSA_GROUP * HEAD_DIM
    return pl.pallas_call(
        functools.partial(_cmpattn_kernel, tq=tq, nc=nc, nsel=nsel),
        grid=(bsz, NSA_KV_HEADS, t // tq),
        in_specs=[
            pl.BlockSpec((None, tq, gw), lambda b, h, i: (b, i, COL_NSA_Q // NSA_GROUP + h)),
            pl.BlockSpec((None, None, ncp, HEAD_DIM), lambda b, h, i: (b, h, 0, 0)),
            pl.BlockSpec((None, None, ncp, HEAD_DIM), lambda b, h, i: (b, h, 0, 0)),
        ],
        out_specs=[
            pl.BlockSpec((None, tq, gw), lambda b, h, i: (b, i, h)),
            pl.BlockSpec((None, None, tq, nsel), lambda b, h, i: (b, h, i, 0)),
        ],
        out_shape=[
            jax.ShapeDtypeStruct((bsz, t, NSA_WIDTH), F32),
            jax.ShapeDtypeStruct((bsz, NSA_KV_HEADS, t, nsel), F32),
        ],
        compiler_params=_cparams(("arbitrary", "arbitrary", "arbitrary")),
        name="cmpattn",
    )(proj, kc, vc)


def _selattn_kernel(q_ref, k_ref, v_ref, sel_ref, o_ref, m_ref, l_ref, acc_ref, *, tq, tk, nsel):
    g = NSA_GROUP
    qi = pl.program_id(2)
    kv = pl.program_id(3)
    scale = HEAD_DIM ** -0.5

    @pl.when(kv == 0)
    def _():
        m_ref[...] = jnp.full_like(m_ref, NEG_BIG)
        l_ref[...] = jnp.zeros_like(l_ref)
        acc_ref[...] = jnp.zeros_like(acc_ref)

    @pl.when(kv * tk <= qi * tq + tq - 1)
    def _():
        q4 = jnp.concatenate([q_ref[:, i * HEAD_DIM:(i + 1) * HEAD_DIM] for i in range(g)], axis=0)
        s = _dot_nt(q4.astype(BF16), k_ref[...].astype(BF16)) * scale
        key = kv * tk + lax.broadcasted_iota(jnp.int32, (nsel, tk), 1)
        blk = lax.broadcasted_iota(jnp.int32, (nsel, tk), 0)
        expand = jnp.where(key // SEL_BLOCK == blk, 1.0, 0.0).astype(BF16)
        picked = _dot(sel_ref[...].astype(BF16), expand)
        pos = qi * tq + lax.broadcasted_iota(jnp.int32, (tq, tk), 0)
        kpos = kv * tk + lax.broadcasted_iota(jnp.int32, (tq, tk), 1)
        mask = ((picked > 0.5) & (kpos <= pos))[None]
        s = jnp.where(mask, s.reshape(g, tq, tk), NEG_BIG)
        m_prev = m_ref[...]
        m_new = jnp.maximum(m_prev, jnp.max(s, axis=-1, keepdims=True))
        alpha = jnp.exp(m_prev - m_new)
        p = jnp.where(mask, jnp.exp(s - m_new), 0.0)
        l_ref[...] = alpha * l_ref[...] + jnp.sum(p, axis=-1, keepdims=True)
        pv = _dot(p.reshape(g * tq, tk).astype(BF16), v_ref[...].astype(BF16))
        acc_ref[...] = alpha * acc_ref[...] + pv.reshape(g, tq, HEAD_DIM)
        m_ref[...] = m_new

    @pl.when(kv == pl.num_programs(3) - 1)
    def _():
        o = acc_ref[...] * (1.0 / jnp.maximum(l_ref[...], 1e-30))
        for i in range(g):
            o_ref[:, i * HEAD_DIM:(i + 1) * HEAD_DIM] = o[i]


def _selattn(proj, sel, tq=Q_BLOCK, tk=512):
    bsz, t, _ = proj.shape
    nsel = t // SEL_BLOCK
    tk = min(tk, t)
    gw = NSA_GROUP * HEAD_DIM

    def kv_spec(base):
        return pl.BlockSpec((None, tk, HEAD_DIM),
                            lambda b, h, i, j: (b, jnp.minimum(j, (i * tq + tq - 1) // tk), base + h))

    return pl.pallas_call(
        functools.partial(_selattn_kernel, tq=tq, tk=tk, nsel=nsel),
        grid=(bsz, NSA_KV_HEADS, t // tq, t // tk),
        in_specs=[
            pl.BlockSpec((None, tq, gw), lambda b, h, i, j: (b, i, COL_NSA_Q // NSA_GROUP + h)),
            kv_spec(COL_KV + 2 * NSA_KV_HEADS), kv_spec(COL_KV + 3 * NSA_KV_HEADS),
            pl.BlockSpec((None, None, tq, nsel), lambda b, h, i, j: (b, h, i, 0)),
        ],
        out_specs=pl.BlockSpec((None, tq, gw), lambda b, h, i, j: (b, i, h)),
        out_shape=jax.ShapeDtypeStruct((bsz, t, NSA_WIDTH), F32),
        scratch_shapes=[pltpu.VMEM((NSA_GROUP, tq, 1), F32), pltpu.VMEM((NSA_GROUP, tq, 1), F32),
                        pltpu.VMEM((NSA_GROUP, tq, HEAD_DIM), F32)],
        compiler_params=_cparams(("arbitrary", "arbitrary", "arbitrary", "arbitrary")),
        name="selattn",
    )(proj, proj, proj, sel)


def _winattn_kernel(*refs, tq, nblk):
    q_ref = refs[0]
    k_refs = refs[1:1 + nblk]
    v_refs = refs[1 + nblk:1 + 2 * nblk]
    ocmp_ref, osel_ref, small_ref, o_ref = refs[1 + 2 * nblk:]
    g = NSA_GROUP
    hk = pl.program_id(1)
    qi = pl.program_id(2)
    scale = HEAD_DIM ** -0.5
    nk = nblk * tq
    keys = jnp.concatenate([r[...] for r in k_refs], axis=0).astype(BF16)
    vals = jnp.concatenate([r[...] for r in v_refs], axis=0).astype(BF16)
    q4 = jnp.concatenate([q_ref[:, i * HEAD_DIM:(i + 1) * HEAD_DIM] for i in range(g)], axis=0)
    s = (_dot_nt(q4.astype(BF16), keys) * scale).reshape(g, tq, nk)
    pos = qi * tq + lax.broadcasted_iota(jnp.int32, (tq, nk), 0)
    wpos = (qi - (nblk - 1)) * tq + lax.broadcasted_iota(jnp.int32, (tq, nk), 1)
    mask = ((wpos <= pos) & (wpos > pos - WINDOW) & (wpos >= 0))[None]
    s = jnp.where(mask, s, NEG_BIG)
    e = jnp.where(mask, jnp.exp(s - jnp.max(s, axis=-1, keepdims=True)), 0.0)
    p = e * (1.0 / jnp.maximum(jnp.sum(e, axis=-1, keepdims=True), 1e-30))
    o_win = _dot(p.reshape(g * tq, nk).astype(BF16), vals).reshape(g, tq, HEAD_DIM)
    small = small_ref[...]
    for i in range(g):
        head = hk * g + i
        gates = [jax.nn.sigmoid(_lane_pick(small, SMALL_NSA_GATE + j * NSA_HEADS + head)) for j in range(3)]
        cols = slice(i * HEAD_DIM, (i + 1) * HEAD_DIM)
        o = gates[0] * ocmp_ref[:, cols] + gates[1] * osel_ref[:, cols] + gates[2] * o_win[i]
        o_ref[:, cols] = o.astype(o_ref.dtype)


def _winattn(proj, o_cmp, o_sel, tq=Q_BLOCK):
    bsz, t, _ = proj.shape
    nblk = WINDOW // tq + 1
    gw = NSA_GROUP * HEAD_DIM

    def kv_spec(base, d):
        return pl.BlockSpec((None, tq, HEAD_DIM),
                            lambda b, h, i: (b, jnp.maximum(i - (nblk - 1) + d, 0), base + h))

    grp = pl.BlockSpec((None, tq, gw), lambda b, h, i: (b, i, h))
    return pl.pallas_call(
        functools.partial(_winattn_kernel, tq=tq, nblk=nblk),
        grid=(bsz, NSA_KV_HEADS, t // tq),
        in_specs=(
            [pl.BlockSpec((None, tq, gw), lambda b, h, i: (b, i, COL_NSA_Q // NSA_GROUP + h))]
            + [kv_spec(COL_KV + 4 * NSA_KV_HEADS, d) for d in range(nblk)]
            + [kv_spec(COL_KV + 5 * NSA_KV_HEADS, d) for d in range(nblk)]
            + [grp, grp, pl.BlockSpec((None, tq, HEAD_DIM), lambda b, h, i: (b, i, COL_SMALL))]
        ),
        out_specs=grp,
        out_shape=jax.ShapeDtypeStruct((bsz, t, NSA_WIDTH), BF16),
        compiler_params=_cparams(("arbitrary", "arbitrary", "arbitrary")),
        name="winattn",
    )(proj, *([proj] * (2 * nblk)), o_cmp, o_sel, proj)


def _ffn(h, wg, wu, wd):
    act = _silu(_dot(h, wg)) * _dot(h, wu)
    return _dot(act.astype(BF16), wd)


def _outproj_kernel(yg_ref, yn_ref, x_ref, wo_ref, gate_ref, shift_ref, scale_ref, gate_m_ref, nw_ref,
                    rw_ref, rb_ref, sg_ref, su_ref, sd_ref, x1_ref, h_ref, comb_ref):
    half = yg_ref.shape[1]
    y = _dot(yg_ref[...], wo_ref[0:half, :]) + _dot(yn_ref[...], wo_ref[half:2 * half, :])
    x1 = x_ref[...] + gate_ref[...] * y
    h = _modulated_norm(x1, nw_ref[...], scale_ref[...], shift_ref[...])
    hb = h.astype(h_ref.dtype)
    h_ref[...] = hb
    x1_ref[...] = x1 + gate_m_ref[...] * _ffn(hb, sg_ref[...], su_ref[...], sd_ref[...])
    scores = jax.nn.sigmoid(_dot(h, rw_ref[...], HI))
    n_exp = scores.shape[1]
    lane = lax.broadcasted_iota(jnp.int32, scores.shape, 1)
    cand = scores + rb_ref[...]
    picked = jnp.zeros(scores.shape, jnp.bool_)
    for _ in range(TOP_K):
        best = jnp.max(cand, axis=-1, keepdims=True)
        first = jnp.min(jnp.where(cand == best, lane, n_exp), axis=-1, keepdims=True)
        hit = lane == first
        picked = picked | hit
        cand = jnp.where(hit, NEG_BIG, cand)
    s_sel = jnp.where(picked, scores, 0.0)
    comb_ref[...] = s_sel * (1.0 / jnp.sum(s_sel, axis=-1, keepdims=True)) * ROUTED_SCALE


def _outproj(y_gdn, y_nsa, x, w_out_b, mod4, norm_w, router_w, router_bias, sg, su, sd, tm=256):
    bsz, t, d = x.shape
    half = y_gdn.shape[2]
    n_exp = router_w.shape[1]
    sff = sg.shape[1]

    def mod(j):
        return pl.BlockSpec((None, None, 1, d), lambda b, i: (b, j, 0, 0))

    def const(shape):
        return pl.BlockSpec(shape, lambda b, i: (0, 0))

    row = pl.BlockSpec((None, tm, d), lambda b, i: (b, i, 0))
    return pl.pallas_call(
        _outproj_kernel,
        grid=(bsz, t // tm),
        in_specs=[
            pl.BlockSpec((None, tm, half), lambda b, i: (b, i, 0)),
            pl.BlockSpec((None, tm, half), lambda b, i: (b, i, 0)),
            row,
            const((2 * half, d)),
            mod(2), mod(3), mod(4), mod(5),
            const((1, d)), const((d, n_exp)), const((1, n_exp)),
            const((d, sff)), const((d, sff)), const((sff, d)),
        ],
        out_specs=[row, row, pl.BlockSpec((None, tm, n_exp), lambda b, i: (b, i, 0))],
        out_shape=[
            jax.ShapeDtypeStruct((bsz, t, d), F32),
            jax.ShapeDtypeStruct((bsz, t, d), BF16),
            jax.ShapeDtypeStruct((bsz, t, n_exp), F32),
        ],
        compiler_params=_cparams(("arbitrary", "arbitrary")),
        name="outproj",
    )(y_gdn, y_nsa, x, w_out_b, mod4, mod4, mod4, mod4, norm_w.reshape(1, d), router_w,
      router_bias.reshape(1, n_exp), sg, su, sd)


def _moe_kernel(h_ref, comb_ref, x1_ref, gate_ref, nw_ref, wg_ref, wu_ref, wd_ref, o_ref, acc_ref):
    e = pl.program_id(2)

    @pl.when(e == 0)
    def _():
        acc_ref[...] = jnp.zeros_like(acc_ref)

    cw = _lane_pick(comb_ref[...], e)
    acc_ref[...] += cw * _ffn(h_ref[...], wg_ref[...], wu_ref[...], wd_ref[...])

    @pl.when(e == pl.num_programs(2) - 1)
    def _():
        x2 = x1_ref[...] + gate_ref[...] * acc_ref[...]
        o_ref[...] = x2 * lax.rsqrt(jnp.mean(x2 * x2, axis=-1, keepdims=True) + EPS) * nw_ref[...]


def _moe(h, comb, x1, mod4, norm_w, wg, wu, wd, tm=512):
    bsz, t, d = x1.shape
    n_exp, _, ff = wg.shape
    row = pl.BlockSpec((None, tm, d), lambda b, i, e: (b, i, 0))

    def expert(shape):
        return pl.BlockSpec((None,) + shape, lambda b, i, e: (e, 0, 0))

    return pl.pallas_call(
        _moe_kernel,
        grid=(bsz, t // tm, n_exp),
        in_specs=[
            row,
            pl.BlockSpec((None, tm, n_exp), lambda b, i, e: (b, i, 0)),
            row,
            pl.BlockSpec((None, None, 1, d), lambda b, i, e: (b, 5, 0, 0)),
            pl.BlockSpec((1, d), lambda b, i, e: (0, 0)),
            expert((d, ff)), expert((d, ff)), expert((ff, d)),
        ],
        out_specs=row,
        out_shape=jax.ShapeDtypeStruct((bsz, t, d), F32),
        scratch_shapes=[pltpu.VMEM((tm, d), F32)],
        compiler_params=_cparams(("arbitrary", "arbitrary", "arbitrary")),
        name="moe",
    )(h, comb, x1, mod4, norm_w.reshape(1, d), wg, wu, wd)


def _reorder_w_in(w_in):
    d = w_in.shape[0]
    o_beta = 3 * GDN_WIDTH
    o_a = o_beta + GDN_HEADS
    o_gate = o_a + GDN_HEADS
    o_q = o_gate + GDN_WIDTH
    o_kv = o_q + NSA_WIDTH
    o_ng = o_kv + 6 * NSA_KV_WIDTH
    n_small = 2 * GDN_HEADS + 3 * NSA_HEADS
    parts = [w_in[:, :o_beta], w_in[:, o_gate:o_q], w_in[:, o_q:o_kv], w_in[:, o_kv:o_ng],
             w_in[:, o_beta:o_gate], w_in[:, o_ng:o_ng + 3 * NSA_HEADS]]
    used = COL_SMALL * HEAD_DIM + n_small
    parts.append(jnp.zeros((d, PROJ_COLS - used), w_in.dtype))
    return jnp.concatenate(parts, axis=1).astype(BF16)


def _to_rows16(proj, col_block):
    bsz, t, _ = proj.shape
    x = proj[:, :, col_block * HEAD_DIM:(col_block + NSA_KV_HEADS) * HEAD_DIM]
    x = x.reshape(bsz, t // CMP_STRIDE, CMP_STRIDE, NSA_KV_HEADS, HEAD_DIM).transpose(0, 3, 1, 2, 4)
    return x.reshape(bsz, NSA_KV_HEADS, t // CMP_STRIDE, CMP_STRIDE * HEAD_DIM)


def _layer(x, mod, norm_attn_w, norm_ffn_w, w_in, gdn_conv_w, gdn_a_log, gdn_dt_bias, gdn_norm_w,
           cmp_pe_k, cmp_w1_k, cmp_w2_k, cmp_pe_v, cmp_w1_v, cmp_w2_v, w_out, router_w, router_bias,
           expert_w_gate, expert_w_up, expert_w_down, shared_w_gate, shared_w_up, shared_w_down, final_w):
    bsz, t, d = x.shape
    mod4 = mod.reshape(bsz, N_MOD, 1, d)
    proj = _proj(x, mod4, norm_attn_w, _reorder_w_in(w_in))
    y_gdn = _gdn(proj, gdn_conv_w, gdn_a_log, gdn_dt_bias, gdn_norm_w)
    kc = _compress(_to_rows16(proj, COL_KV), cmp_pe_k, cmp_w1_k, cmp_w2_k)
    vc = _compress(_to_rows16(proj, COL_KV + NSA_KV_HEADS), cmp_pe_v, cmp_w1_v, cmp_w2_v)
    o_cmp, sel = _cmpattn(proj, kc, vc)
    o_sel = _selattn(proj, sel)
    y_nsa = _winattn(proj, o_cmp, o_sel)
    x1, h, comb = _outproj(y_gdn, y_nsa, x, w_out.astype(BF16), mod4, norm_ffn_w, router_w, router_bias,
                           shared_w_gate.astype(BF16), shared_w_up.astype(BF16), shared_w_down.astype(BF16))
    return _moe(h, comb, x1, mod4, final_w,
                expert_w_gate.astype(BF16), expert_w_up.astype(BF16), expert_w_down.astype(BF16))


def kernel(x, c, w_ada, b_ada, norm_attn_w, norm_ffn_w, norm_final_w, w_in, gdn_conv_w, gdn_a_log, gdn_dt_bias,
           gdn_norm_w, cmp_pe_k, cmp_w1_k, cmp_w2_k, cmp_pe_v, cmp_w1_v, cmp_w2_v, w_out, router_w, router_bias,
           expert_w_gate, expert_w_up, expert_w_down, shared_w_gate, shared_w_up, shared_w_down):
    depth = w_ada.shape[0]
    assert depth == 1, "the fused final norm assumes a single layer"
    l = 0
    mod = _ada(c, w_ada[l], b_ada[l])
    return _layer(x, mod, norm_attn_w[l], norm_ffn_w[l], w_in[l], gdn_conv_w[l], gdn_a_log[l], gdn_dt_bias[l],
                  gdn_norm_w[l], cmp_pe_k[l], cmp_w1_k[l], cmp_w2_k[l], cmp_pe_v[l], cmp_w1_v[l], cmp_w2_v[l],
                  w_out[l], router_w[l], router_bias[l], expert_w_gate[l], expert_w_up[l], expert_w_down[l],
                  shared_w_gate[l], shared_w_up[l], shared_w_down[l], norm_final_w)
```

```python
import functools

import jax
import jax.numpy as jnp
from jax import lax
from jax.experimental import pallas as pl
from jax.experimental.pallas import tpu as pltpu

F32 = jnp.float32
BF16 = jnp.bfloat16
HI = lax.Precision.HIGHEST

HEAD_DIM = 128
GDN_HEADS = 8
NSA_HEADS = 8
NSA_KV_HEADS = 2
NSA_GROUP = NSA_HEADS // NSA_KV_HEADS
GDN_WIDTH = GDN_HEADS * HEAD_DIM
NSA_WIDTH = NSA_HEADS * HEAD_DIM
NSA_KV_WIDTH = NSA_KV_HEADS * HEAD_DIM
CONV_WIDTH = 4
GDN_CHUNK = 64
CMP_BLOCK = 32
CMP_STRIDE = 16
SEL_BLOCK = 64
SEL_TOPK = 16
N_LOCAL_BLOCKS = 2
WINDOW = 512
Q_BLOCK = 128
TOP_K = 8
ROUTED_SCALE = 2.5
N_MOD = 6
EPS = 1e-6

COL_GDN_Q = 0
COL_GDN_K = GDN_HEADS
COL_GDN_V = 2 * GDN_HEADS
COL_GDN_GATE = 3 * GDN_HEADS
COL_NSA_Q = 4 * GDN_HEADS
COL_KV = COL_NSA_Q + NSA_HEADS
COL_SMALL = COL_KV + 6 * NSA_KV_HEADS
N_COL_BLOCKS = COL_SMALL + 1
PROJ_TN = 2304
PROJ_COLS = 3 * PROJ_TN
SMALL_BETA = 0
SMALL_A = GDN_HEADS
SMALL_NSA_GATE = 2 * GDN_HEADS

NEG_BIG = -1e30
VMEM_LIMIT = 48 * 1024 * 1024


def _cparams(semantics):
    return pltpu.CompilerParams(dimension_semantics=semantics, vmem_limit_bytes=VMEM_LIMIT)


def _silu(x):
    return x * jax.nn.sigmoid(x)


def _dot(a, b, precision=None):
    return jnp.dot(a, b, preferred_element_type=F32, precision=precision)


def _dot_nt(a, b, precision=None):
    return lax.dot_general(a, b, (((1,), (1,)), ((), ())), preferred_element_type=F32, precision=precision)


def _dot_tn(a, b, precision=None):
    return lax.dot_general(a, b, (((0,), (0,)), ((), ())), preferred_element_type=F32, precision=precision)


def _lane_pick(x, idx):
    lane = lax.broadcasted_iota(jnp.int32, x.shape, 1)
    return jnp.sum(jnp.where(lane == idx, x, 0.0), axis=-1, keepdims=True)


def _ada_kernel(ct_ref, w_ref, b_ref, o_ref):
    c = ct_ref[...]
    cond = _silu(c)
    w = w_ref[...]
    rows = [jnp.sum(w * cond[:, b:b + 1], axis=0, keepdims=True) for b in range(c.shape[1])]
    o_ref[...] = jnp.concatenate(rows, axis=0) + b_ref[...]


def _ada(c, w_ada, b_ada, tn=512):
    bsz, d = c.shape
    n = w_ada.shape[1]
    return pl.pallas_call(
        _ada_kernel,
        grid=(n // tn,),
        in_specs=[
            pl.BlockSpec((d, bsz), lambda j: (0, 0)),
            pl.BlockSpec((d, tn), lambda j: (0, j)),
            pl.BlockSpec((1, tn), lambda j: (0, j)),
        ],
        out_specs=pl.BlockSpec((bsz, tn), lambda j: (0, j)),
        out_shape=jax.ShapeDtypeStruct((bsz, n), F32),
        compiler_params=_cparams(("arbitrary",)),
        name="ada",
    )(c.T, w_ada, b_ada.reshape(1, n))


def _modulated_norm(x, nw, scale, shift):
    var = jnp.mean(x * x, axis=-1, keepdims=True)
    h = x * lax.rsqrt(var + EPS) * nw
    return h * (1.0 + scale) + shift


def _proj_kernel(x_ref, shift_ref, scale_ref, nw_ref, w_ref, o_ref):
    h = _modulated_norm(x_ref[...], nw_ref[...], scale_ref[...], shift_ref[...])
    o_ref[...] = _dot(h.astype(BF16), w_ref[...])


def _proj(x, mod4, norm_w, w_r, tm=256):
    bsz, t, d = x.shape
    n = w_r.shape[1]
    return pl.pallas_call(
        _proj_kernel,
        grid=(n // PROJ_TN, bsz, t // tm),
        in_specs=[
            pl.BlockSpec((None, tm, d), lambda j, b, i: (b, i, 0)),
            pl.BlockSpec((None, None, 1, d), lambda j, b, i: (b, 0, 0, 0)),
            pl.BlockSpec((None, None, 1, d), lambda j, b, i: (b, 1, 0, 0)),
            pl.BlockSpec((1, d), lambda j, b, i: (0, 0)),
            pl.BlockSpec((d, PROJ_TN), lambda j, b, i: (0, j)),
        ],
        out_specs=pl.BlockSpec((None, tm, PROJ_TN), lambda j, b, i: (b, i, j)),
        out_shape=jax.ShapeDtypeStruct((bsz, t, n), F32),
        compiler_params=_cparams(("arbitrary", "arbitrary", "arbitrary")),
        name="proj",
    )(x, mod4, mod4, norm_w.reshape(1, d), w_r)


def _gdn_kernel(q_ref, k_ref, v_ref, gate_ref, small_ref, hq_ref, hk_ref, hv_ref,
                cwq_ref, cwk_ref, cwv_ref, alog_ref, dtb_ref, nw_ref, o_ref, s_ref, xc_ref, *, tt):
    c = GDN_CHUNK
    nch = tt // c
    h = pl.program_id(1)
    first = pl.program_id(2) == 0

    @pl.when(first)
    def _():
        s_ref[...] = jnp.zeros_like(s_ref)

    def conv_silu(x_ref, halo_ref, cw_ref):
        xc_ref[0:8, :] = jnp.where(first, 0.0, halo_ref[...])
        xc_ref[8:tt + 8, :] = x_ref[...]
        cw = cw_ref[...]
        off = 8 - (CONV_WIDTH - 1)
        y = cw[0:1, :] * xc_ref[off:off + tt, :]
        for i in range(1, CONV_WIDTH):
            y = y + cw[i:i + 1, :] * xc_ref[off + i:off + i + tt, :]
        return _silu(y)

    q = conv_silu(q_ref, hq_ref, cwq_ref)
    k = conv_silu(k_ref, hk_ref, cwk_ref)
    v = conv_silu(v_ref, hv_ref, cwv_ref)
    q = q * lax.rsqrt(jnp.sum(q * q, axis=-1, keepdims=True) + EPS) * (HEAD_DIM ** -0.5)
    k = k * lax.rsqrt(jnp.sum(k * k, axis=-1, keepdims=True) + EPS)

    small = small_ref[...]
    beta = jax.nn.sigmoid(_lane_pick(small, SMALL_BETA + h))
    z = _lane_pick(small, SMALL_A + h) + dtb_ref[...]
    softplus = jnp.maximum(z, 0.0) + jnp.log(1.0 + jnp.exp(-jnp.abs(z)))
    g = -jnp.exp(alog_ref[...]) * softplus

    ri = lax.broadcasted_iota(jnp.int32, (tt, tt), 0)
    ci = lax.broadcasted_iota(jnp.int32, (tt, tt), 1)
    same = (ri // c) == (ci // c)
    lower = jnp.where(same & (ci <= ri), 1.0, 0.0)
    upper = jnp.where(same & (ci > ri), 1.0, 0.0)
    g_lanes = jnp.broadcast_to(g, (tt, HEAD_DIM))
    gcum = _dot(lower, g_lanes, HI)
    grest = _dot(upper, g_lanes, HI)
    kpos = lax.broadcasted_iota(jnp.int32, (tt, c), 0) % c
    jj = lax.broadcasted_iota(jnp.int32, (tt, c), 1)
    gdiff = _dot(lower, jnp.where(kpos > jj, jnp.broadcast_to(g, (tt, c)), 0.0), HI)
    decay = jnp.where(kpos >= jj, jnp.exp(gdiff), 0.0).reshape(nch, c, c)
    strict = (kpos > jj).reshape(nch, c, c)
    eye = jnp.where(kpos == jj, 1.0, 0.0).reshape(nch, c, c)

    def bmm(a, b):
        return jnp.einsum('nij,njk->nik', a, b, preferred_element_type=F32, precision=HI)

    def bmm_nt(a, b):
        return jnp.einsum('nid,njd->nij', a, b, preferred_element_type=F32, precision=HI)

    k3 = k.reshape(nch, c, HEAD_DIM)
    kb = k * beta
    a = jnp.where(strict, bmm_nt(kb.reshape(nch, c, HEAD_DIM), k3) * decay, 0.0)
    inv = eye - a
    p = bmm(a, a)
    n_sq = (c - 1).bit_length() - 1
    for lvl in range(n_sq):
        inv = inv + bmm(inv, p)
        if lvl + 1 < n_sq:
            p = bmm(p, p)
    egc = jnp.exp(gcum)
    u3 = bmm(inv, (v * beta).reshape(nch, c, HEAD_DIM))
    w3 = bmm(inv, (kb * egc).reshape(nch, c, HEAD_DIM))
    qk3 = bmm_nt(q.reshape(nch, c, HEAD_DIM), k3) * decay
    qd3 = (q * egc).reshape(nch, c, HEAD_DIM)
    kd3 = (k * jnp.exp(grest)).reshape(nch, c, HEAD_DIM)
    gl3 = egc.reshape(nch, c, HEAD_DIM)

    state = s_ref[...]
    outs = []
    for n in range(nch):
        v_new = u3[n] - _dot(w3[n], state, HI)
        outs.append(_dot(qd3[n], state, HI) + _dot(qk3[n], v_new, HI))
        state = state * gl3[n, c - 1:c, :] + _dot_tn(kd3[n], v_new, HI)
    s_ref[...] = state
    o = jnp.concatenate(outs, axis=0)
    o = o * lax.rsqrt(jnp.mean(o * o, axis=-1, keepdims=True) + EPS) * nw_ref[...]
    o_ref[...] = (o * _silu(gate_ref[...])).astype(o_ref.dtype)


def _gdn(proj, conv_w, a_log, dt_bias, norm_w, tt=256):
    bsz, t, _ = proj.shape
    hd = HEAD_DIM

    def col(base):
        return pl.BlockSpec((None, tt, hd), lambda b, h, i: (b, i, base + h))

    def halo(base):
        return pl.BlockSpec((None, 8, hd), lambda b, h, i: (b, jnp.maximum(i * (tt // 8) - 1, 0), base + h))

    def cw(base):
        return pl.BlockSpec((CONV_WIDTH, hd), lambda b, h, i: (0, base + h))

    scalar = pl.BlockSpec((None, 1, 1), lambda b, h, i: (h, 0, 0))
    return pl.pallas_call(
        functools.partial(_gdn_kernel, tt=tt),
        grid=(bsz, GDN_HEADS, t // tt),
        in_specs=[
            col(COL_GDN_Q), col(COL_GDN_K), col(COL_GDN_V), col(COL_GDN_GATE),
            pl.BlockSpec((None, tt, hd), lambda b, h, i: (b, i, COL_SMALL)),
            halo(COL_GDN_Q), halo(COL_GDN_K), halo(COL_GDN_V),
            cw(COL_GDN_Q), cw(COL_GDN_K), cw(COL_GDN_V),
            scalar, scalar,
            pl.BlockSpec((1, hd), lambda b, h, i: (0, 0)),
        ],
        out_specs=pl.BlockSpec((None, tt, hd), lambda b, h, i: (b, i, h)),
        out_shape=jax.ShapeDtypeStruct((bsz, t, GDN_WIDTH), BF16),
        scratch_shapes=[pltpu.VMEM((hd, hd), F32), pltpu.VMEM((tt + 8, hd), F32)],
        compiler_params=_cparams(("arbitrary", "arbitrary", "arbitrary")),
        name="gdn",
    )(proj, proj, proj, proj, proj, proj, proj, proj, conv_w, conv_w, conv_w,
      a_log.reshape(GDN_HEADS, 1, 1), dt_bias.reshape(GDN_HEADS, 1, 1), norm_w.reshape(1, hd))


def _compress_kernel(x_ref, pe_ref, w1_ref, w2_ref, o_ref):
    x = x_ref[...]
    half = x.shape[1]
    a = _dot(x + pe_ref[0:1, :], w1_ref[0:half, :], HI)
    b = _dot(x + pe_ref[1:2, :], w1_ref[half:2 * half, :], HI)
    hid = _silu(a + pltpu.roll(b, b.shape[0] - 1, 0))
    o_ref[...] = _dot(hid, w2_ref[...], HI)


def _compress(x16, pe, w1, w2):
    bsz, hk, r, wdt = x16.shape
    hid = w1.shape[1]
    return pl.pallas_call(
        _compress_kernel,
        grid=(bsz, hk),
        in_specs=[
            pl.BlockSpec((None, None, r, wdt), lambda b, h: (b, h, 0, 0)),
            pl.BlockSpec((2, wdt), lambda b, h: (0, 0)),
            pl.BlockSpec((2 * wdt, hid), lambda b, h: (0, 0)),
            pl.BlockSpec((hid, HEAD_DIM), lambda b, h: (0, 0)),
        ],
        out_specs=pl.BlockSpec((None, None, r, HEAD_DIM), lambda b, h: (b, h, 0, 0)),
        out_shape=jax.ShapeDtypeStruct((bsz, hk, r, HEAD_DIM), F32),
        compiler_params=_cparams(("arbitrary", "arbitrary")),
        name="compress",
    )(x16, pe.reshape(2, wdt), w1, w2)


def _cmpattn_kernel(q_ref, kc_ref, vc_ref, o_ref, sel_ref, *, tq, nc, nsel):
    ncp = kc_ref.shape[0]
    scale = HEAD_DIM ** -0.5
    t0 = pl.program_id(2) * tq
    kc = kc_ref[...]
    vc = vc_ref[...]
    pos_r = t0 + lax.broadcasted_iota(jnp.int32, (tq, ncp), 0)
    c_r = lax.broadcasted_iota(jnp.int32, (tq, ncp), 1)
    valid_r = (c_r * CMP_STRIDE + CMP_BLOCK - 1 <= pos_r) & (c_r < nc)
    pos_c = t0 + lax.broadcasted_iota(jnp.int32, (ncp, tq), 1)
    c_c = lax.broadcasted_iota(jnp.int32, (ncp, tq), 0)
    valid_c = (c_c * CMP_STRIDE + CMP_BLOCK - 1 <= pos_c) & (c_c < nc)
    psum = jnp.zeros((ncp, tq), F32)
    for g in range(NSA_GROUP):
        qg = q_ref[:, g * HEAD_DIM:(g + 1) * HEAD_DIM]
        s = jnp.where(valid_r, _dot_nt(qg, kc, HI) * scale, NEG_BIG)
        e = jnp.where(valid_r, jnp.exp(s - jnp.max(s, axis=-1, keepdims=True)), 0.0)
        p = e * (1.0 / jnp.maximum(jnp.sum(e, axis=-1, keepdims=True), 1e-30))
        o_ref[:, g * HEAD_DIM:(g + 1) * HEAD_DIM] = _dot(p, vc, HI)
        st = jnp.where(valid_c, _dot_nt(kc, qg, HI) * scale, NEG_BIG)
        et = jnp.where(valid_c, jnp.exp(st - jnp.max(st, axis=0, keepdims=True)), 0.0)
        psum = psum + et * (1.0 / jnp.maximum(jnp.sum(et, axis=0, keepdims=True), 1e-30))
    j_o = lax.broadcasted_iota(jnp.int32, (nsel, ncp), 0) * SEL_BLOCK
    c_o = lax.broadcasted_iota(jnp.int32, (nsel, ncp), 1) * CMP_STRIDE
    overlap = jnp.where((c_o < j_o + SEL_BLOCK) & (c_o + CMP_BLOCK > j_o), 1.0, 0.0)
    imp = _dot(overlap, psum, HI)
    blk = lax.broadcasted_iota(jnp.int32, (nsel, tq), 0)
    dist = (t0 + lax.broadcasted_iota(jnp.int32, (nsel, tq), 1)) // SEL_BLOCK - blk
    forced = (blk == 0) | ((dist >= 0) & (dist < N_LOCAL_BLOCKS))
    val = jnp.where(forced, -NEG_BIG, jnp.where(dist >= 0, imp, -1.0))
    rank = jnp.zeros((nsel, tq), F32)
    for j in range(nsel):
        row = val[j:j + 1, :]
        rank = rank + jnp.where((row > val) | ((row == val) & (blk > j)), 1.0, 0.0)
    sel = jnp.where((rank < min(SEL_TOPK, nsel)) & (dist >= 0), 1.0, 0.0)
    sel_ref[...] = sel.T


def _cmpattn(proj, kc, vc, tq=256):
    bsz, t, _ = proj.shape
    ncp = kc.shape[2]
    nc = (t - CMP_BLOCK) // CMP_STRIDE + 1
    nsel = t // SEL_BLOCK
    gw = NSA_GROUP * HEAD_DIM
    return pl.pallas_call(
        functools.partial(_cmpattn_kernel, tq=tq, nc=nc, nsel=nsel),
        grid=(bsz, NSA_KV_HEADS, t // tq),
        in_specs=[
            pl.BlockSpec((None, tq, gw), lambda b, h, i: (b, i, COL_NSA_Q // NSA_GROUP + h)),
            pl.BlockSpec((None, None, ncp, HEAD_DIM), lambda b, h, i: (b, h, 0, 0)),
            pl.BlockSpec((None, None, ncp, HEAD_DIM), lambda b, h, i: (b, h, 0, 0)),
        ],
        out_specs=[
            pl.BlockSpec((None, tq, gw), lambda b, h, i: (b, i, h)),
            pl.BlockSpec((None, None, tq, nsel), lambda b, h, i: (b, h, i, 0)),
        ],
        out_shape=[
            jax.ShapeDtypeStruct((bsz, t, NSA_WIDTH), F32),
            jax.ShapeDtypeStruct((bsz, NSA_KV_HEADS, t, nsel), F32),
        ],
        compiler_params=_cparams(("arbitrary", "arbitrary", "arbitrary")),
        name="cmpattn",
    )(proj, kc, vc)


def _selattn_kernel(q_ref, k_ref, v_ref, sel_ref, o_ref, m_ref, l_ref, acc_ref, *, tq, tk, nsel):
    g = NSA_GROUP
    qi = pl.program_id(2)
    kv = pl.program_id(3)
    scale = HEAD_DIM ** -0.5

    @pl.when(kv == 0)
    def _():
        m_ref[...] = jnp.full_like(m_ref, NEG_BIG)
        l_ref[...] = jnp.zeros_like(l_ref)
        acc_ref[...] = jnp.zeros_like(acc_ref)

    @pl.when(kv * tk <= qi * tq + tq - 1)
    def _():
        q4 = jnp.concatenate([q_ref[:, i * HEAD_DIM:(i + 1) * HEAD_DIM] for i in range(g)], axis=0)
        s = _dot_nt(q4.astype(BF16), k_ref[...].astype(BF16)) * scale
        key = kv * tk + lax.broadcasted_iota(jnp.int32, (nsel, tk), 1)
        blk = lax.broadcasted_iota(jnp.int32, (nsel, tk), 0)
        expand = jnp.where(key // SEL_BLOCK == blk, 1.0, 0.0).astype(BF16)
        picked = _dot(sel_ref[...].astype(BF16), expand)
        pos = qi * tq + lax.broadcasted_iota(jnp.int32, (tq, tk), 0)
        kpos = kv * tk + lax.broadcasted_iota(jnp.int32, (tq, tk), 1)
        mask = ((picked > 0.5) & (kpos <= pos))[None]
        s = jnp.where(mask, s.reshape(g, tq, tk), NEG_BIG)
        m_prev = m_ref[...]
        m_new = jnp.maximum(m_prev, jnp.max(s, axis=-1, keepdims=True))
        alpha = jnp.exp(m_prev - m_new)
        p = jnp.where(mask, jnp.exp(s - m_new), 0.0)
        l_ref[...] = alpha * l_ref[...] + jnp.sum(p, axis=-1, keepdims=True)
        pv = _dot(p.reshape(g * tq, tk).astype(BF16), v_ref[...].astype(BF16))
        acc_ref[...] = alpha * acc_ref[...] + pv.reshape(g, tq, HEAD_DIM)
        m_ref[...] = m_new

    @pl.when(kv == pl.num_programs(3) - 1)
    def _():
        o = acc_ref[...] * (1.0 / jnp.maximum(l_ref[...], 1e-30))
        for i in range(g):
            o_ref[:, i * HEAD_DIM:(i + 1) * HEAD_DIM] = o[i]


def _selattn(proj, sel, tq=Q_BLOCK, tk=512):
    bsz, t, _ = proj.shape
    nsel = t // SEL_BLOCK
    tk = min(tk, t)
    gw = NSA_GROUP * HEAD_DIM

    def kv_spec(base):
        return pl.BlockSpec((None, tk, HEAD_DIM),
                            lambda b, h, i, j: (b, jnp.minimum(j, (i * tq + tq - 1) // tk), base + h))

    return pl.pallas_call(
        functools.partial(_selattn_kernel, tq=tq, tk=tk, nsel=nsel),
        grid=(bsz, NSA_KV_HEADS, t // tq, t // tk),
        in_specs=[
            pl.BlockSpec((None, tq, gw), lambda b, h, i, j: (b, i, COL_NSA_Q // NSA_GROUP + h)),
            kv_spec(COL_KV + 2 * NSA_KV_HEADS), kv_spec(COL_KV + 3 * NSA_KV_HEADS),
            pl.BlockSpec((None, None, tq, nsel), lambda b, h, i, j: (b, h, i, 0)),
        ],
        out_specs=pl.BlockSpec((None, tq, gw), lambda b, h, i, j: (b, i, h)),
        out_shape=jax.ShapeDtypeStruct((bsz, t, NSA_WIDTH), F32),
        scratch_shapes=[pltpu.VMEM((NSA_GROUP, tq, 1), F32), pltpu.VMEM((NSA_GROUP, tq, 1), F32),
                        pltpu.VMEM((NSA_GROUP, tq, HEAD_DIM), F32)],
        compiler_params=_cparams(("arbitrary", "arbitrary", "arbitrary", "arbitrary")),
        name="selattn",
    )(proj, proj, proj, sel)


def _winattn_kernel(*refs, tq, nblk):
    q_ref = refs[0]
    k_refs = refs[1:1 + nblk]
    v_refs = refs[1 + nblk:1 + 2 * nblk]
    ocmp_ref, osel_ref, small_ref, o_ref = refs[1 + 2 * nblk:]
    g = NSA_GROUP
    hk = pl.program_id(1)
    qi = pl.program_id(2)
    scale = HEAD_DIM ** -0.5
    nk = nblk * tq
    keys = jnp.concatenate([r[...] for r in k_refs], axis=0).astype(BF16)
    vals = jnp.concatenate([r[...] for r in v_refs], axis=0).astype(BF16)
    q4 = jnp.concatenate([q_ref[:, i * HEAD_DIM:(i + 1) * HEAD_DIM] for i in range(g)], axis=0)
    s = (_dot_nt(q4.astype(BF16), keys) * scale).reshape(g, tq, nk)
    pos = qi * tq + lax.broadcasted_iota(jnp.int32, (tq, nk), 0)
    wpos = (qi - (nblk - 1)) * tq + lax.broadcasted_iota(jnp.int32, (tq, nk), 1)
    mask = ((wpos <= pos) & (wpos > pos - WINDOW) & (wpos >= 0))[None]
    s = jnp.where(mask, s, NEG_BIG)
    e = jnp.where(mask, jnp.exp(s - jnp.max(s, axis=-1, keepdims=True)), 0.0)
    p = e * (1.0 / jnp.maximum(jnp.sum(e, axis=-1, keepdims=True), 1e-30))
    o_win = _dot(p.reshape(g * tq, nk).astype(BF16), vals).reshape(g, tq, HEAD_DIM)
    small = small_ref[...]
    for i in range(g):
        head = hk * g + i
        gates = [jax.nn.sigmoid(_lane_pick(small, SMALL_NSA_GATE + j * NSA_HEADS + head)) for j in range(3)]
        cols = slice(i * HEAD_DIM, (i + 1) * HEAD_DIM)
        o = gates[0] * ocmp_ref[:, cols] + gates[1] * osel_ref[:, cols] + gates[2] * o_win[i]
        o_ref[:, cols] = o.astype(o_ref.dtype)


def _winattn(proj, o_cmp, o_sel, tq=Q_BLOCK):
    bsz, t, _ = proj.shape
    nblk = WINDOW // tq + 1
    gw = NSA_GROUP * HEAD_DIM

    def kv_spec(base, d):
        return pl.BlockSpec((None, tq, HEAD_DIM),
                            lambda b, h, i: (b, jnp.maximum(i - (nblk - 1) + d, 0), base + h))

    grp = pl.BlockSpec((None, tq, gw), lambda b, h, i: (b, i, h))
    return pl.pallas_call(
        functools.partial(_winattn_kernel, tq=tq, nblk=nblk),
        grid=(bsz, NSA_KV_HEADS, t // tq),
        in_specs=(
            [pl.BlockSpec((None, tq, gw), lambda b, h, i: (b, i, COL_NSA_Q // NSA_GROUP + h))]
            + [kv_spec(COL_KV + 4 * NSA_KV_HEADS, d) for d in range(nblk)]
            + [kv_spec(COL_KV + 5 * NSA_KV_HEADS, d) for d in range(nblk)]
            + [grp, grp, pl.BlockSpec((None, tq, HEAD_DIM), lambda b, h, i: (b, i, COL_SMALL))]
        ),
        out_specs=grp,
        out_shape=jax.ShapeDtypeStruct((bsz, t, NSA_WIDTH), BF16),
        compiler_params=_cparams(("arbitrary", "arbitrary", "arbitrary")),
        name="winattn",
    )(proj, *([proj] * (2 * nblk)), o_cmp, o_sel, proj)


def _ffn(h, wg, wu, wd):
    act = _silu(_dot(h, wg)) * _dot(h, wu)
    return _dot(act.astype(BF16), wd)


def _outproj_kernel(yg_ref, yn_ref, x_ref, wo_ref, gate_ref, shift_ref, scale_ref, gate_m_ref, nw_ref,
                    rw_ref, rb_ref, sg_ref, su_ref, sd_ref, x1_ref, h_ref, route_ref, count_ref, carry_ref):
    tm = x_ref.shape[0]
    half = yg_ref.shape[1]

    @pl.when((pl.program_id(0) == 0) & (pl.program_id(1) == 0))
    def _():
        carry_ref[...] = jnp.zeros_like(carry_ref)

    y = _dot(yg_ref[...], wo_ref[0:half, :]) + _dot(yn_ref[...], wo_ref[half:2 * half, :])
    x1 = x_ref[...] + gate_ref[...] * y
    h = _modulated_norm(x1, nw_ref[...], scale_ref[...], shift_ref[...])
    h_ref[...] = h
    x1_ref[...] = x1 + gate_m_ref[...] * _ffn(h.astype(BF16), sg_ref[...], su_ref[...], sd_ref[...])
    scores = jax.nn.sigmoid(_dot(h, rw_ref[...], HI))
    n_exp = scores.shape[1]
    lane = lax.broadcasted_iota(jnp.int32, scores.shape, 1)
    cand = scores + rb_ref[...]
    picked = jnp.zeros(scores.shape, jnp.bool_)
    firsts = []
    for _ in range(TOP_K):
        best = jnp.max(cand, axis=-1, keepdims=True)
        first = jnp.min(jnp.where(cand == best, lane, n_exp), axis=-1, keepdims=True)
        hit = lane == first
        picked = picked | hit
        cand = jnp.where(hit, NEG_BIG, cand)
        firsts.append(first)
    norm = ROUTED_SCALE / jnp.sum(jnp.where(picked, scores, 0.0), axis=-1, keepdims=True)
    onehot = jnp.where(picked, 1.0, 0.0)
    ri = lax.broadcasted_iota(jnp.int32, (tm, tm), 0)
    ci = lax.broadcasted_iota(jnp.int32, (tm, tm), 1)
    before = jnp.where(ci < ri, 1.0, 0.0).astype(BF16)
    pos = _dot(before, onehot.astype(BF16)) + carry_ref[...]
    carry_ref[...] += jnp.sum(onehot, axis=0, keepdims=True)
    count_ref[...] = carry_ref[...]
    slot = lax.broadcasted_iota(jnp.int32, (tm, 3 * TOP_K), 1)
    route = jnp.zeros((tm, 3 * TOP_K), F32)
    for j, first in enumerate(firsts):
        hit = lane == first
        route = jnp.where(slot == j, first.astype(F32), route)
        route = jnp.where(slot == TOP_K + j, _lane_pick(scores, first) * norm, route)
        route = jnp.where(slot == 2 * TOP_K + j, jnp.sum(jnp.where(hit, pos, 0.0), axis=-1, keepdims=True), route)
    route_ref[...] = route


def _outproj(y_gdn, y_nsa, x, w_out_b, mod4, norm_w, router_w, router_bias, sg, su, sd, tm=256):
    bsz, t, d = x.shape
    half = y_gdn.shape[2]
    n_exp = router_w.shape[1]
    sff = sg.shape[1]

    def mod(j):
        return pl.BlockSpec((None, None, 1, d), lambda b, i: (b, j, 0, 0))

    def const(shape):
        return pl.BlockSpec(shape, lambda b, i: (0, 0))

    row = pl.BlockSpec((None, tm, d), lambda b, i: (b, i, 0))
    return pl.pallas_call(
        _outproj_kernel,
        grid=(bsz, t // tm),
        in_specs=[
            pl.BlockSpec((None, tm, half), lambda b, i: (b, i, 0)),
            pl.BlockSpec((None, tm, half), lambda b, i: (b, i, 0)),
            row,
            const((2 * half, d)),
            mod(2), mod(3), mod(4), mod(5),
            const((1, d)), const((d, n_exp)), const((1, n_exp)),
            const((d, sff)), const((d, sff)), const((sff, d)),
        ],
        out_specs=[row, row, pl.BlockSpec((None, tm, 3 * TOP_K), lambda b, i: (b, i, 0)),
                   const((1, n_exp))],
        out_shape=[
            jax.ShapeDtypeStruct((bsz, t, d), F32),
            jax.ShapeDtypeStruct((bsz, t, d), F32),
            jax.ShapeDtypeStruct((bsz, t, 3 * TOP_K), F32),
            jax.ShapeDtypeStruct((1, n_exp), F32),
        ],
        scratch_shapes=[pltpu.VMEM((1, n_exp), F32)],
        compiler_params=_cparams(("arbitrary", "arbitrary")),
        name="outproj",
    )(y_gdn, y_nsa, x, w_out_b, mod4, mod4, mod4, mod4, norm_w.reshape(1, d), router_w,
      router_bias.reshape(1, n_exp), sg, su, sd)


MOE_TM = 256
MOE_GATHER_ROWS = 1024
MOE_COMBINE_TM = 128


def _moe_gather_kernel(nvalid_ref, tok_ref, h_hbm, xs_hbm, sem):
    base = pl.program_id(0) * MOE_GATHER_ROWS

    @pl.when(base < nvalid_ref[0] * MOE_TM)
    def _():
        def row_copy(r):
            return pltpu.make_async_copy(h_hbm.at[pl.ds(tok_ref[r], 1), :], xs_hbm.at[pl.ds(base + r, 1), :], sem)

        def start(r, carry):
            row_copy(r).start()
            return carry

        def wait(r, carry):
            row_copy(r).wait()
            return carry

        lax.fori_loop(0, MOE_GATHER_ROWS, start, 0)
        lax.fori_loop(0, MOE_GATHER_ROWS, wait, 0)


def _moe_gather(h2, token_of_row, nvalid):
    n_rows = token_of_row.shape[0]
    return pl.pallas_call(
        _moe_gather_kernel,
        grid_spec=pltpu.PrefetchScalarGridSpec(
            num_scalar_prefetch=1,
            grid=(n_rows // MOE_GATHER_ROWS,),
            in_specs=[
                pl.BlockSpec((MOE_GATHER_ROWS,), lambda i, nv: (i,), memory_space=pltpu.SMEM),
                pl.BlockSpec(memory_space=pl.ANY),
            ],
            out_specs=pl.BlockSpec(memory_space=pl.ANY),
            scratch_shapes=[pltpu.SemaphoreType.DMA(())],
        ),
        out_shape=jax.ShapeDtypeStruct((n_rows, h2.shape[1]), h2.dtype),
        compiler_params=_cparams(("arbitrary",)),
        name="moe_gather",
    )(nvalid, token_of_row, h2)


def _moe_expert_kernel(texp_ref, nvalid_ref, xs_ref, wg_ref, wu_ref, wd_ref, ys_ref, wgb_ref, wub_ref, wdb_ref):
    i = pl.program_id(0)
    valid = i < nvalid_ref[0]
    new_expert = (i == 0) | (texp_ref[i] != texp_ref[jnp.maximum(i - 1, 0)])

    @pl.when(valid & new_expert)
    def _():
        wgb_ref[...] = wg_ref[...].astype(BF16)
        wub_ref[...] = wu_ref[...].astype(BF16)
        wdb_ref[...] = wd_ref[...].astype(BF16)

    @pl.when(valid)
    def _():
        ys_ref[...] = _ffn(xs_ref[...].astype(BF16), wgb_ref[...], wub_ref[...], wdb_ref[...])

    @pl.when(jnp.logical_not(valid))
    def _():
        ys_ref[...] = jnp.zeros_like(ys_ref)


def _moe_experts(xs, tile_expert, nvalid, wg, wu, wd):
    n_rows, d = xs.shape
    n_exp, _, ff = wg.shape

    def expert(shape):
        return pl.BlockSpec((None,) + shape, lambda i, te, nv: (te[i], 0, 0))

    return pl.pallas_call(
        _moe_expert_kernel,
        grid_spec=pltpu.PrefetchScalarGridSpec(
            num_scalar_prefetch=2,
            grid=(n_rows // MOE_TM,),
            in_specs=[
                pl.BlockSpec((MOE_TM, d), lambda i, te, nv: (jnp.minimum(i, nv[0] - 1), 0)),
                expert((d, ff)), expert((d, ff)), expert((ff, d)),
            ],
            out_specs=pl.BlockSpec((MOE_TM, d), lambda i, te, nv: (i, 0)),
            scratch_shapes=[pltpu.VMEM((d, ff), BF16), pltpu.VMEM((d, ff), BF16), pltpu.VMEM((ff, d), BF16)],
        ),
        out_shape=jax.ShapeDtypeStruct((n_rows, d), F32),
        compiler_params=_cparams(("arbitrary",)),
        name="moe_experts",
    )(tile_expert, nvalid, xs, wg, wu, wd)


def _moe_combine_kernel(rows_ref, ys_hbm, w_ref, x1_ref, gate_ref, nw_ref, o_ref, buf_ref, sem):
    tm = x1_ref.shape[0]

    def row_copy(n):
        return pltpu.make_async_copy(ys_hbm.at[pl.ds(rows_ref[n], 1), :],
                                     buf_ref.at[n % TOP_K, pl.ds(n // TOP_K, 1), :], sem)

    def start(n, carry):
        row_copy(n).start()
        return carry

    def wait(n, carry):
        row_copy(n).wait()
        return carry

    lax.fori_loop(0, tm * TOP_K, start, 0)
    lax.fori_loop(0, tm * TOP_K, wait, 0)
    w = w_ref[...]
    routed = w[:, 0:1] * buf_ref[0]
    for j in range(1, TOP_K):
        routed = routed + w[:, j:j + 1] * buf_ref[j]
    x2 = x1_ref[...] + gate_ref[...] * routed
    o_ref[...] = x2 * lax.rsqrt(jnp.mean(x2 * x2, axis=-1, keepdims=True) + EPS) * nw_ref[...]


def _moe_combine(rows_flat, ys, w_tok, x1, mod4, norm_w):
    bsz, t, d = x1.shape
    tm = MOE_COMBINE_TM
    nt = t // tm
    row = pl.BlockSpec((None, tm, d), lambda b, i: (b, i, 0))
    return pl.pallas_call(
        _moe_combine_kernel,
        grid=(bsz, nt),
        in_specs=[
            pl.BlockSpec((tm * TOP_K,), lambda b, i: (b * nt + i,), memory_space=pltpu.SMEM),
            pl.BlockSpec(memory_space=pl.ANY),
            pl.BlockSpec((None, tm, TOP_K), lambda b, i: (b, i, 0)),
            row,
            pl.BlockSpec((None, None, 1, d), lambda b, i: (b, 5, 0, 0)),
            pl.BlockSpec((1, d), lambda b, i: (0, 0)),
        ],
        out_specs=row,
        out_shape=jax.ShapeDtypeStruct((bsz, t, d), F32),
        scratch_shapes=[pltpu.VMEM((TOP_K, tm, d), F32), pltpu.SemaphoreType.DMA(())],
        compiler_params=_cparams(("arbitrary", "arbitrary")),
        name="moe_combine",
    )(rows_flat, ys, w_tok, x1, mod4, norm_w.reshape(1, d))


def _moe(h, route, counts, x1, mod4, norm_w, wg, wu, wd):
    bsz, t, d = x1.shape
    n_tok = bsz * t
    n_exp = wg.shape[0]
    top_idx = route[..., 0:TOP_K].astype(jnp.int32).reshape(n_tok, TOP_K)
    w_tok = route[..., TOP_K:2 * TOP_K]
    top_pos = route[..., 2 * TOP_K:3 * TOP_K].astype(jnp.int32).reshape(n_tok, TOP_K)
    padded = (counts.reshape(n_exp).astype(jnp.int32) + MOE_TM - 1) // MOE_TM * MOE_TM
    group_end = jnp.cumsum(padded)
    group_start = group_end - padded
    nvalid = (group_end[-1:] // MOE_TM).astype(jnp.int32)
    rows_tok = group_start[top_idx] + top_pos
    n_rows = (n_tok * TOP_K + n_exp * MOE_TM + MOE_GATHER_ROWS - 1) // MOE_GATHER_ROWS * MOE_GATHER_ROWS
    token_of_row = jnp.zeros((n_rows,), jnp.int32).at[rows_tok.reshape(-1)].set(
        jnp.repeat(jnp.arange(n_tok, dtype=jnp.int32), TOP_K))
    tile_ids = jnp.minimum(jnp.arange(n_rows // MOE_TM, dtype=jnp.int32), nvalid[0] - 1)
    tile_expert = jnp.searchsorted(group_end, tile_ids * MOE_TM, side="right").astype(jnp.int32)
    xs = _moe_gather(h.reshape(n_tok, d), token_of_row, nvalid)
    ys = _moe_experts(xs, tile_expert, nvalid, wg, wu, wd)
    return _moe_combine(rows_tok.reshape(-1), ys, w_tok, x1, mod4, norm_w)


def _reorder_w_in(w_in):
    d = w_in.shape[0]
    o_beta = 3 * GDN_WIDTH
    o_a = o_beta + GDN_HEADS
    o_gate = o_a + GDN_HEADS
    o_q = o_gate + GDN_WIDTH
    o_kv = o_q + NSA_WIDTH
    o_ng = o_kv + 6 * NSA_KV_WIDTH
    n_small = 2 * GDN_HEADS + 3 * NSA_HEADS
    parts = [w_in[:, :o_beta], w_in[:, o_gate:o_q], w_in[:, o_q:o_kv], w_in[:, o_kv:o_ng],
             w_in[:, o_beta:o_gate], w_in[:, o_ng:o_ng + 3 * NSA_HEADS]]
    used = COL_SMALL * HEAD_DIM + n_small
    parts.append(jnp.zeros((d, PROJ_COLS - used), w_in.dtype))
    return jnp.concatenate(parts, axis=1).astype(BF16)


def _to_rows16(proj, col_block):
    bsz, t, _ = proj.shape
    x = proj[:, :, col_block * HEAD_DIM:(col_block + NSA_KV_HEADS) * HEAD_DIM]
    x = x.reshape(bsz, t // CMP_STRIDE, CMP_STRIDE, NSA_KV_HEADS, HEAD_DIM).transpose(0, 3, 1, 2, 4)
    return x.reshape(bsz, NSA_KV_HEADS, t // CMP_STRIDE, CMP_STRIDE * HEAD_DIM)


def _layer(x, mod, norm_attn_w, norm_ffn_w, w_in, gdn_conv_w, gdn_a_log, gdn_dt_bias, gdn_norm_w,
           cmp_pe_k, cmp_w1_k, cmp_w2_k, cmp_pe_v, cmp_w1_v, cmp_w2_v, w_out, router_w, router_bias,
           expert_w_gate, expert_w_up, expert_w_down, shared_w_gate, shared_w_up, shared_w_down, final_w):
    bsz, t, d = x.shape
    mod4 = mod.reshape(bsz, N_MOD, 1, d)
    proj = _proj(x, mod4, norm_attn_w, _reorder_w_in(w_in))
    y_gdn = _gdn(proj, gdn_conv_w, gdn_a_log, gdn_dt_bias, gdn_norm_w)
    kc = _compress(_to_rows16(proj, COL_KV), cmp_pe_k, cmp_w1_k, cmp_w2_k)
    vc = _compress(_to_rows16(proj, COL_KV + NSA_KV_HEADS), cmp_pe_v, cmp_w1_v, cmp_w2_v)
    o_cmp, sel = _cmpattn(proj, kc, vc)
    o_sel = _selattn(proj, sel)
    y_nsa = _winattn(proj, o_cmp, o_sel)
    x1, h, route, counts = _outproj(
        y_gdn, y_nsa, x, w_out.astype(BF16), mod4, norm_ffn_w, router_w, router_bias,
        shared_w_gate.astype(BF16), shared_w_up.astype(BF16), shared_w_down.astype(BF16))
    return _moe(h, route, counts, x1, mod4, final_w, expert_w_gate, expert_w_up, expert_w_down)


def kernel(x, c, w_ada, b_ada, norm_attn_w, norm_ffn_w, norm_final_w, w_in, gdn_conv_w, gdn_a_log, gdn_dt_bias,
           gdn_norm_w, cmp_pe_k, cmp_w1_k, cmp_w2_k, cmp_pe_v, cmp_w1_v, cmp_w2_v, w_out, router_w, router_bias,
           expert_w_gate, expert_w_up, expert_w_down, shared_w_gate, shared_w_up, shared_w_down):
    depth = w_ada.shape[0]
    assert depth == 1, "the fused final norm assumes a single layer"
    l = 0
    mod = _ada(c, w_ada[l], b_ada[l])
    return _layer(x, mod, norm_attn_w[l], norm_ffn_w[l], w_in[l], gdn_conv_w[l], gdn_a_log[l], gdn_dt_bias[l],
                  gdn_norm_w[l], cmp_pe_k[l], cmp_w1_k[l], cmp_w2_k[l], cmp_pe_v[l], cmp_w1_v[l], cmp_w2_v[l],
                  w_out[l], router_w[l], router_bias[l], expert_w_gate[l], expert_w_up[l], expert_w_down[l],
                  shared_w_gate[l], shared_w_up[l], shared_w_down[l], norm_final_w)
```

```python
import functools

import jax
import jax.numpy as jnp
from jax import lax
from jax.experimental import pallas as pl
from jax.experimental.pallas import tpu as pltpu

F32 = jnp.float32
BF16 = jnp.bfloat16
HI = lax.Precision.HIGHEST

HEAD_DIM = 128
GDN_HEADS = 8
NSA_HEADS = 8
NSA_KV_HEADS = 2
NSA_GROUP = NSA_HEADS // NSA_KV_HEADS
GDN_WIDTH = GDN_HEADS * HEAD_DIM
NSA_WIDTH = NSA_HEADS * HEAD_DIM
NSA_KV_WIDTH = NSA_KV_HEADS * HEAD_DIM
CONV_WIDTH = 4
GDN_CHUNK = 64
CMP_BLOCK = 32
CMP_STRIDE = 16
SEL_BLOCK = 64
SEL_TOPK = 16
N_LOCAL_BLOCKS = 2
WINDOW = 512
Q_BLOCK = 128
TOP_K = 8
ROUTED_SCALE = 2.5
N_MOD = 6
EPS = 1e-6

COL_GDN_Q = 0
COL_GDN_K = GDN_HEADS
COL_GDN_V = 2 * GDN_HEADS
COL_GDN_GATE = 3 * GDN_HEADS
COL_NSA_Q = 4 * GDN_HEADS
COL_KV = COL_NSA_Q + NSA_HEADS
COL_SMALL = COL_KV + 6 * NSA_KV_HEADS
N_COL_BLOCKS = COL_SMALL + 1
PROJ_TN = 2304
PROJ_COLS = 3 * PROJ_TN
SMALL_BETA = 0
SMALL_A = GDN_HEADS
SMALL_NSA_GATE = 2 * GDN_HEADS

NEG_BIG = -1e30
VMEM_LIMIT = 48 * 1024 * 1024


def _cparams(semantics):
    return pltpu.CompilerParams(dimension_semantics=semantics, vmem_limit_bytes=VMEM_LIMIT)


def _silu(x):
    return x * jax.nn.sigmoid(x)


def _dot(a, b, precision=None):
    return jnp.dot(a, b, preferred_element_type=F32, precision=precision)


def _dot_nt(a, b, precision=None):
    return lax.dot_general(a, b, (((1,), (1,)), ((), ())), preferred_element_type=F32, precision=precision)


def _dot_tn(a, b, precision=None):
    return lax.dot_general(a, b, (((0,), (0,)), ((), ())), preferred_element_type=F32, precision=precision)


def _split_bf16(x):
    hi = x.astype(BF16)
    return hi, (x - hi.astype(F32)).astype(BF16)


def _lane_pick(x, idx):
    lane = lax.broadcasted_iota(jnp.int32, x.shape, 1)
    return jnp.sum(jnp.where(lane == idx, x, 0.0), axis=-1, keepdims=True)


def _ada_kernel(ct_ref, w_ref, b_ref, o_ref):
    c = ct_ref[...]
    cond = _silu(c)
    w = w_ref[...]
    rows = [jnp.sum(w * cond[:, b:b + 1], axis=0, keepdims=True) for b in range(c.shape[1])]
    o_ref[...] = jnp.concatenate(rows, axis=0) + b_ref[...]


def _ada(c, w_ada, b_ada, tn=512):
    bsz, d = c.shape
    n = w_ada.shape[1]
    return pl.pallas_call(
        _ada_kernel,
        grid=(n // tn,),
        in_specs=[
            pl.BlockSpec((d, bsz), lambda j: (0, 0)),
            pl.BlockSpec((d, tn), lambda j: (0, j)),
            pl.BlockSpec((1, tn), lambda j: (0, j)),
        ],
        out_specs=pl.BlockSpec((bsz, tn), lambda j: (0, j)),
        out_shape=jax.ShapeDtypeStruct((bsz, n), F32),
        compiler_params=_cparams(("arbitrary",)),
        name="ada",
    )(c.T, w_ada, b_ada.reshape(1, n))


def _modulated_norm(x, nw, scale, shift):
    var = jnp.mean(x * x, axis=-1, keepdims=True)
    h = x * lax.rsqrt(var + EPS) * nw
    return h * (1.0 + scale) + shift


def _proj_kernel(x_ref, shift_ref, scale_ref, nw_ref, w_ref, o_ref):
    h = _modulated_norm(x_ref[...], nw_ref[...], scale_ref[...], shift_ref[...])
    o_ref[...] = _dot(h.astype(BF16), w_ref[...])


def _proj(x, mod4, norm_w, w_r, tm=256):
    bsz, t, d = x.shape
    n = w_r.shape[1]
    return pl.pallas_call(
        _proj_kernel,
        grid=(n // PROJ_TN, bsz, t // tm),
        in_specs=[
            pl.BlockSpec((None, tm, d), lambda j, b, i: (b, i, 0)),
            pl.BlockSpec((None, None, 1, d), lambda j, b, i: (b, 0, 0, 0)),
            pl.BlockSpec((None, None, 1, d), lambda j, b, i: (b, 1, 0, 0)),
            pl.BlockSpec((1, d), lambda j, b, i: (0, 0)),
            pl.BlockSpec((d, PROJ_TN), lambda j, b, i: (0, j)),
        ],
        out_specs=pl.BlockSpec((None, tm, PROJ_TN), lambda j, b, i: (b, i, j)),
        out_shape=jax.ShapeDtypeStruct((bsz, t, n), F32),
        compiler_params=_cparams(("arbitrary", "arbitrary", "arbitrary")),
        name="proj",
    )(x, mod4, mod4, norm_w.reshape(1, d), w_r)


def _gdn_kernel(q_ref, k_ref, v_ref, gate_ref, small_ref, hq_ref, hk_ref, hv_ref,
                cwq_ref, cwk_ref, cwv_ref, alog_ref, dtb_ref, nw_ref, o_ref, s_ref, xc_ref, *, tt, hg):
    first = pl.program_id(2) == 0

    @pl.when(first)
    def _():
        s_ref[...] = jnp.zeros_like(s_ref)

    def conv_silu(x_ref, halo_ref, cw_ref):
        xc_ref[0:8, :] = jnp.where(first, 0.0, halo_ref[...])
        xc_ref[8:tt + 8, :] = x_ref[...]
        cw = cw_ref[...]
        off = 8 - (CONV_WIDTH - 1)
        y = cw[0:1, :] * xc_ref[off:off + tt, :]
        for i in range(1, CONV_WIDTH):
            y = y + cw[i:i + 1, :] * xc_ref[off + i:off + i + tt, :]
        return _silu(y)

    q_all = conv_silu(q_ref, hq_ref, cwq_ref)
    k_all = conv_silu(k_ref, hk_ref, cwk_ref)
    v_all = conv_silu(v_ref, hv_ref, cwv_ref)
    ri = lax.broadcasted_iota(jnp.int32, (tt, tt), 0)
    ci = lax.broadcasted_iota(jnp.int32, (tt, tt), 1)
    lower = jnp.where(((ri // GDN_CHUNK) == (ci // GDN_CHUNK)) & (ci <= ri), 1.0, 0.0).astype(BF16)
    for hh in range(hg):
        cols = slice(hh * HEAD_DIM, (hh + 1) * HEAD_DIM)
        o = _gdn_head(q_all[:, cols], k_all[:, cols], v_all[:, cols], small_ref[...], lower,
                      pl.program_id(1) * hg + hh, alog_ref[...], dtb_ref[...], s_ref.at[hh], tt)
        o = o * lax.rsqrt(jnp.mean(o * o, axis=-1, keepdims=True) + EPS) * nw_ref[...]
        o_ref[:, cols] = (o * _silu(gate_ref[:, cols])).astype(o_ref.dtype)


def _gdn_head(q, k, v, small, lower, h, alog, dtb, s_ref, tt):
    c = GDN_CHUNK
    nch = tt // c
    q = q * lax.rsqrt(jnp.sum(q * q, axis=-1, keepdims=True) + EPS) * (HEAD_DIM ** -0.5)
    k = k * lax.rsqrt(jnp.sum(k * k, axis=-1, keepdims=True) + EPS)

    beta = jax.nn.sigmoid(_lane_pick(small, SMALL_BETA + h))
    z = _lane_pick(small, SMALL_A + h) + _lane_pick(dtb, h)
    softplus = jnp.maximum(z, 0.0) + jnp.log(1.0 + jnp.exp(-jnp.abs(z)))
    g = -jnp.exp(_lane_pick(alog, h)) * softplus

    kpos = lax.broadcasted_iota(jnp.int32, (tt, c), 0) % c
    jj = lax.broadcasted_iota(jnp.int32, (tt, c), 1)
    g_c = jnp.broadcast_to(g, (tt, c))
    rhs = jnp.concatenate([g_c, jnp.where(kpos > jj, g_c, 0.0)], axis=1)
    r_hi = rhs.astype(BF16)
    rem = rhs - r_hi.astype(F32)
    r_mid = rem.astype(BF16)
    r_lo = (rem - r_mid.astype(F32)).astype(BF16)
    sums = _dot(lower, r_hi) + (_dot(lower, r_mid) + _dot(lower, r_lo))
    gcum = jnp.broadcast_to(sums[:, 0:1], (tt, HEAD_DIM))
    gc3 = gcum.reshape(nch, c, HEAD_DIM)
    grest = (gc3[:, c - 1:c, :] - gc3).reshape(tt, HEAD_DIM)
    decay = jnp.where(kpos >= jj, jnp.exp(sums[:, c:2 * c]), 0.0).reshape(nch, c, c)
    strict = (kpos > jj).reshape(nch, c, c)
    eye = jnp.where(kpos == jj, 1.0, 0.0).reshape(nch, c, c)

    def bmm(a, b):
        return jnp.einsum('nij,njk->nik', a, b, preferred_element_type=F32)

    def bmm_nt(a, b):
        return jnp.einsum('nid,njd->nij', a, b, preferred_element_type=F32)

    def bmm_split(a, b):
        (ah, al), (bh, bl) = _split_bf16(a), _split_bf16(b)
        return bmm(ah, bh) + (bmm(ah, bl) + bmm(al, bh))

    k3 = k.reshape(nch, c, HEAD_DIM).astype(BF16)
    kb = k * beta
    a = jnp.where(strict, bmm_nt(kb.reshape(nch, c, HEAD_DIM).astype(BF16), k3) * decay, 0.0)
    inv = eye - a
    p = bmm_split(a, a)
    n_sq = (c - 1).bit_length() - 1
    for lvl in range(n_sq):
        inv = inv + bmm_split(inv, p)
        if lvl + 1 < n_sq:
            p = bmm_split(p, p)
    egc = jnp.exp(gcum)
    inv_b = inv.astype(BF16)
    u3 = bmm(inv_b, (v * beta).reshape(nch, c, HEAD_DIM).astype(BF16)).astype(BF16)
    w3 = bmm(inv_b, (kb * egc).reshape(nch, c, HEAD_DIM).astype(BF16)).astype(BF16)
    qk3 = (bmm_nt(q.reshape(nch, c, HEAD_DIM).astype(BF16), k3) * decay).astype(BF16)
    kd3 = (k * jnp.exp(grest)).reshape(nch, c, HEAD_DIM).astype(BF16)
    gl3 = egc.reshape(nch, c, HEAD_DIM)
    qe3 = ((q * egc).reshape(nch, c, HEAD_DIM) - bmm(qk3, w3)).astype(BF16)
    o03 = bmm(qk3, u3)
    kw = [_dot_tn(kd3[n], w3[n]).astype(BF16) for n in range(nch)]
    ku = [_dot_tn(kd3[n], u3[n]) for n in range(nch)]

    state = s_ref[...]
    outs = []
    for n in range(nch):
        state_b = state.astype(BF16)
        outs.append(_dot(qe3[n], state_b) + o03[n])
        state = state * gl3[n, c - 1:c, :] - _dot(kw[n], state_b) + ku[n]
    s_ref[...] = state
    return jnp.concatenate(outs, axis=0)


def _gdn(proj, conv_w, a_log, dt_bias, norm_w, tt=256, hg=4):
    bsz, t, _ = proj.shape
    wdt = hg * HEAD_DIM

    def col(base):
        return pl.BlockSpec((None, tt, wdt), lambda b, h, i: (b, i, base // hg + h))

    def halo(base):
        return pl.BlockSpec((None, 8, wdt), lambda b, h, i: (b, jnp.maximum(i * (tt // 8) - 1, 0), base // hg + h))

    def cw(base):
        return pl.BlockSpec((CONV_WIDTH, wdt), lambda b, h, i: (0, base // hg + h))

    heads = pl.BlockSpec((1, GDN_HEADS), lambda b, h, i: (0, 0))
    return pl.pallas_call(
        functools.partial(_gdn_kernel, tt=tt, hg=hg),
        grid=(bsz, GDN_HEADS // hg, t // tt),
        in_specs=[
            col(COL_GDN_Q), col(COL_GDN_K), col(COL_GDN_V), col(COL_GDN_GATE),
            pl.BlockSpec((None, tt, HEAD_DIM), lambda b, h, i: (b, i, COL_SMALL)),
            halo(COL_GDN_Q), halo(COL_GDN_K), halo(COL_GDN_V),
            cw(COL_GDN_Q), cw(COL_GDN_K), cw(COL_GDN_V),
            heads, heads,
            pl.BlockSpec((1, HEAD_DIM), lambda b, h, i: (0, 0)),
        ],
        out_specs=pl.BlockSpec((None, tt, wdt), lambda b, h, i: (b, i, h)),
        out_shape=jax.ShapeDtypeStruct((bsz, t, GDN_WIDTH), BF16),
        scratch_shapes=[pltpu.VMEM((hg, HEAD_DIM, HEAD_DIM), F32), pltpu.VMEM((tt + 8, wdt), F32)],
        compiler_params=_cparams(("arbitrary", "arbitrary", "arbitrary")),
        name="gdn",
    )(proj, proj, proj, proj, proj, proj, proj, proj, conv_w, conv_w, conv_w,
      a_log.reshape(1, GDN_HEADS), dt_bias.reshape(1, GDN_HEADS), norm_w.reshape(1, HEAD_DIM))


def _compress_kernel(x_ref, pe_ref, w1_ref, w2_ref, o_ref):
    x = x_ref[...]
    half = x.shape[1]
    a = _dot(x + pe_ref[0:1, :], w1_ref[0:half, :], HI)
    b = _dot(x + pe_ref[1:2, :], w1_ref[half:2 * half, :], HI)
    hid = _silu(a + pltpu.roll(b, b.shape[0] - 1, 0))
    o_ref[...] = _dot(hid, w2_ref[...], HI)


def _compress(x16, pe, w1, w2):
    bsz, hk, r, wdt = x16.shape
    hid = w1.shape[1]
    return pl.pallas_call(
        _compress_kernel,
        grid=(bsz, hk),
        in_specs=[
            pl.BlockSpec((None, None, r, wdt), lambda b, h: (b, h, 0, 0)),
            pl.BlockSpec((2, wdt), lambda b, h: (0, 0)),
            pl.BlockSpec((2 * wdt, hid), lambda b, h: (0, 0)),
            pl.BlockSpec((hid, HEAD_DIM), lambda b, h: (0, 0)),
        ],
        out_specs=pl.BlockSpec((None, None, r, HEAD_DIM), lambda b, h: (b, h, 0, 0)),
        out_shape=jax.ShapeDtypeStruct((bsz, hk, r, HEAD_DIM), F32),
        compiler_params=_cparams(("arbitrary", "arbitrary")),
        name="compress",
    )(x16, pe.reshape(2, wdt), w1, w2)


def _cmpattn_kernel(q_ref, kc_ref, vc_ref, o_ref, sel_ref, *, tq, nc, nsel):
    ncp = kc_ref.shape[0]
    scale = HEAD_DIM ** -0.5
    t0 = pl.program_id(2) * tq
    kc = kc_ref[...]
    vc = vc_ref[...]
    pos_r = t0 + lax.broadcasted_iota(jnp.int32, (tq, ncp), 0)
    c_r = lax.broadcasted_iota(jnp.int32, (tq, ncp), 1)
    valid_r = (c_r * CMP_STRIDE + CMP_BLOCK - 1 <= pos_r) & (c_r < nc)
    pos_c = t0 + lax.broadcasted_iota(jnp.int32, (ncp, tq), 1)
    c_c = lax.broadcasted_iota(jnp.int32, (ncp, tq), 0)
    valid_c = (c_c * CMP_STRIDE + CMP_BLOCK - 1 <= pos_c) & (c_c < nc)
    psum = jnp.zeros((ncp, tq), F32)
    for g in range(NSA_GROUP):
        qg = q_ref[:, g * HEAD_DIM:(g + 1) * HEAD_DIM]
        s = jnp.where(valid_r, _dot_nt(qg, kc, HI) * scale, NEG_BIG)
        e = jnp.where(valid_r, jnp.exp(s - jnp.max(s, axis=-1, keepdims=True)), 0.0)
        p = e * (1.0 / jnp.maximum(jnp.sum(e, axis=-1, keepdims=True), 1e-30))
        o_ref[:, g * HEAD_DIM:(g + 1) * HEAD_DIM] = _dot(p, vc, HI)
        st = jnp.where(valid_c, _dot_nt(kc, qg, HI) * scale, NEG_BIG)
        et = jnp.where(valid_c, jnp.exp(st - jnp.max(st, axis=0, keepdims=True)), 0.0)
        psum = psum + et * (1.0 / jnp.maximum(jnp.sum(et, axis=0, keepdims=True), 1e-30))
    j_o = lax.broadcasted_iota(jnp.int32, (nsel, ncp), 0) * SEL_BLOCK
    c_o = lax.broadcasted_iota(jnp.int32, (nsel, ncp), 1) * CMP_STRIDE
    overlap = jnp.where((c_o < j_o + SEL_BLOCK) & (c_o + CMP_BLOCK > j_o), 1.0, 0.0)
    imp = _dot(overlap, psum, HI)
    blk = lax.broadcasted_iota(jnp.int32, (nsel, tq), 0)
    dist = (t0 + lax.broadcasted_iota(jnp.int32, (nsel, tq), 1)) // SEL_BLOCK - blk
    forced = (blk == 0) | ((dist >= 0) & (dist < N_LOCAL_BLOCKS))
    val = jnp.where(forced, -NEG_BIG, jnp.where(dist >= 0, imp, -1.0))
    rank = jnp.zeros((nsel, tq), F32)
    for j in range(nsel):
        row = val[j:j + 1, :]
        rank = rank + jnp.where((row > val) | ((row == val) & (blk > j)), 1.0, 0.0)
    sel = jnp.where((rank < min(SEL_TOPK, nsel)) & (dist >= 0), 1.0, 0.0)
    sel_ref[...] = sel.T


def _cmpattn(proj, kc, vc, tq=256):
    bsz, t, _ = proj.shape
    ncp = kc.shape[2]
    nc = (t - CMP_BLOCK) // CMP_STRIDE + 1
    nsel = t // SEL_BLOCK
    gw = NSA_GROUP * HEAD_DIM
    return pl.pallas_call(
        functools.partial(_cmpattn_kernel, tq=tq, nc=nc, nsel=nsel),
        grid=(bsz, NSA_KV_HEADS, t // tq),
        in_specs=[
            pl.BlockSpec((None, tq, gw), lambda b, h, i: (b, i, COL_NSA_Q // NSA_GROUP + h)),
            pl.BlockSpec((None, None, ncp, HEAD_DIM), lambda b, h, i: (b, h, 0, 0)),
            pl.BlockSpec((None, None, ncp, HEAD_DIM), lambda b, h, i: (b, h, 0, 0)),
        ],
        out_specs=[
            pl.BlockSpec((None, tq, gw), lambda b, h, i: (b, i, h)),
            pl.BlockSpec((None, None, tq, nsel), lambda b, h, i: (b, h, i, 0)),
        ],
        out_shape=[
            jax.ShapeDtypeStruct((bsz, t, NSA_WIDTH), F32),
            jax.ShapeDtypeStruct((bsz, NSA_KV_HEADS, t, nsel), F32),
        ],
        compiler_params=_cparams(("arbitrary", "arbitrary", "arbitrary")),
        name="cmpattn",
    )(proj, kc, vc)


def _selattn_kernel(q_ref, k_ref, v_ref, sel_ref, o_ref, m_ref, l_ref, acc_ref, *, tq, tk, nsel):
    g = NSA_GROUP
    qi = pl.program_id(2)
    m_ref[...] = jnp.full_like(m_ref, NEG_BIG)
    l_ref[...] = jnp.zeros_like(l_ref)
    acc_ref[...] = jnp.zeros_like(acc_ref)
    q4 = jnp.concatenate([q_ref[:, i * HEAD_DIM:(i + 1) * HEAD_DIM] for i in range(g)], axis=0)
    q4 = (q4 * HEAD_DIM ** -0.5).astype(BF16)
    sel = sel_ref[...].astype(BF16)
    pos = qi * tq + lax.broadcasted_iota(jnp.int32, (tq, tk), 0)

    def kv_step(kv, carry):
        k0 = pl.multiple_of(kv * tk, tk)
        s = _dot_nt(q4, k_ref[pl.ds(k0, tk), :].astype(BF16))
        key = k0 + lax.broadcasted_iota(jnp.int32, (nsel, tk), 1)
        blk = lax.broadcasted_iota(jnp.int32, (nsel, tk), 0)
        expand = jnp.where(key // SEL_BLOCK == blk, 1.0, 0.0).astype(BF16)
        picked = _dot(sel, expand)
        kpos = k0 + lax.broadcasted_iota(jnp.int32, (tq, tk), 1)
        mask = ((picked > 0.5) & (kpos <= pos))[None]
        s = jnp.where(mask, s.reshape(g, tq, tk), NEG_BIG)
        m_prev = m_ref[...]
        m_new = jnp.maximum(m_prev, jnp.max(s, axis=-1, keepdims=True))
        alpha = jnp.exp(m_prev - m_new)
        p = jnp.where(mask, jnp.exp(s - m_new), 0.0)
        l_ref[...] = alpha * l_ref[...] + jnp.sum(p, axis=-1, keepdims=True)
        pv = _dot(p.reshape(g * tq, tk).astype(BF16), v_ref[pl.ds(k0, tk), :].astype(BF16))
        acc_ref[...] = alpha * acc_ref[...] + pv.reshape(g, tq, HEAD_DIM)
        m_ref[...] = m_new
        return carry

    lax.fori_loop(0, (qi * tq + tq - 1) // tk + 1, kv_step, 0)
    o = acc_ref[...] * (1.0 / jnp.maximum(l_ref[...], 1e-30))
    for i in range(g):
        o_ref[:, i * HEAD_DIM:(i + 1) * HEAD_DIM] = o[i]


def _selattn(proj, sel, tq=Q_BLOCK, tk=512):
    bsz, t, _ = proj.shape
    nsel = t // SEL_BLOCK
    tk = min(tk, t)
    gw = NSA_GROUP * HEAD_DIM

    def kv_spec(base):
        return pl.BlockSpec((None, t, HEAD_DIM), lambda b, h, i: (b, 0, base + h))

    return pl.pallas_call(
        functools.partial(_selattn_kernel, tq=tq, tk=tk, nsel=nsel),
        grid=(bsz, NSA_KV_HEADS, t // tq),
        in_specs=[
            pl.BlockSpec((None, tq, gw), lambda b, h, i: (b, i, COL_NSA_Q // NSA_GROUP + h)),
            kv_spec(COL_KV + 2 * NSA_KV_HEADS), kv_spec(COL_KV + 3 * NSA_KV_HEADS),
            pl.BlockSpec((None, None, tq, nsel), lambda b, h, i: (b, h, i, 0)),
        ],
        out_specs=pl.BlockSpec((None, tq, gw), lambda b, h, i: (b, i, h)),
        out_shape=jax.ShapeDtypeStruct((bsz, t, NSA_WIDTH), F32),
        scratch_shapes=[pltpu.VMEM((NSA_GROUP, tq, 1), F32), pltpu.VMEM((NSA_GROUP, tq, 1), F32),
                        pltpu.VMEM((NSA_GROUP, tq, HEAD_DIM), F32)],
        compiler_params=_cparams(("arbitrary", "arbitrary", "arbitrary")),
        name="selattn",
    )(proj, proj, proj, sel)


def _winattn_kernel(*refs, tq, nblk):
    q_ref = refs[0]
    k_refs = refs[1:1 + nblk]
    v_refs = refs[1 + nblk:1 + 2 * nblk]
    ocmp_ref, osel_ref, small_ref, o_ref = refs[1 + 2 * nblk:]
    g = NSA_GROUP
    hk = pl.program_id(1)
    qi = pl.program_id(2)
    scale = HEAD_DIM ** -0.5
    nk = nblk * tq
    keys = jnp.concatenate([r[...] for r in k_refs], axis=0).astype(BF16)
    vals = jnp.concatenate([r[...] for r in v_refs], axis=0).astype(BF16)
    q4 = jnp.concatenate([q_ref[:, i * HEAD_DIM:(i + 1) * HEAD_DIM] for i in range(g)], axis=0)
    s = (_dot_nt(q4.astype(BF16), keys) * scale).reshape(g, tq, nk)
    pos = qi * tq + lax.broadcasted_iota(jnp.int32, (tq, nk), 0)
    wpos = (qi - (nblk - 1)) * tq + lax.broadcasted_iota(jnp.int32, (tq, nk), 1)
    mask = ((wpos <= pos) & (wpos > pos - WINDOW) & (wpos >= 0))[None]
    s = jnp.where(mask, s, NEG_BIG)
    e = jnp.where(mask, jnp.exp(s - jnp.max(s, axis=-1, keepdims=True)), 0.0)
    p = e * (1.0 / jnp.maximum(jnp.sum(e, axis=-1, keepdims=True), 1e-30))
    o_win = _dot(p.reshape(g * tq, nk).astype(BF16), vals).reshape(g, tq, HEAD_DIM)
    small = small_ref[...]
    for i in range(g):
        head = hk * g + i
        gates = [jax.nn.sigmoid(_lane_pick(small, SMALL_NSA_GATE + j * NSA_HEADS + head)) for j in range(3)]
        cols = slice(i * HEAD_DIM, (i + 1) * HEAD_DIM)
        o = gates[0] * ocmp_ref[:, cols] + gates[1] * osel_ref[:, cols] + gates[2] * o_win[i]
        o_ref[:, cols] = o.astype(o_ref.dtype)


def _winattn(proj, o_cmp, o_sel, tq=Q_BLOCK):
    bsz, t, _ = proj.shape
    nblk = WINDOW // tq + 1
    gw = NSA_GROUP * HEAD_DIM

    def kv_spec(base, d):
        return pl.BlockSpec((None, tq, HEAD_DIM),
                            lambda b, h, i: (b, jnp.maximum(i - (nblk - 1) + d, 0), base + h))

    grp = pl.BlockSpec((None, tq, gw), lambda b, h, i: (b, i, h))
    return pl.pallas_call(
        functools.partial(_winattn_kernel, tq=tq, nblk=nblk),
        grid=(bsz, NSA_KV_HEADS, t // tq),
        in_specs=(
            [pl.BlockSpec((None, tq, gw), lambda b, h, i: (b, i, COL_NSA_Q // NSA_GROUP + h))]
            + [kv_spec(COL_KV + 4 * NSA_KV_HEADS, d) for d in range(nblk)]
            + [kv_spec(COL_KV + 5 * NSA_KV_HEADS, d) for d in range(nblk)]
            + [grp, grp, pl.BlockSpec((None, tq, HEAD_DIM), lambda b, h, i: (b, i, COL_SMALL))]
        ),
        out_specs=grp,
        out_shape=jax.ShapeDtypeStruct((bsz, t, NSA_WIDTH), BF16),
        compiler_params=_cparams(("arbitrary", "arbitrary", "arbitrary")),
        name="winattn",
    )(proj, *([proj] * (2 * nblk)), o_cmp, o_sel, proj)


def _ffn(h, wg, wu, wd):
    act = _silu(_dot(h, wg)) * _dot(h, wu)
    return _dot(act.astype(BF16), wd)


def _outproj_kernel(yg_ref, yn_ref, x_ref, wo_ref, gate_ref, shift_ref, scale_ref, gate_m_ref, nw_ref,
                    rw_ref, rb_ref, sg_ref, su_ref, sd_ref, x1_ref, h_ref, comb_ref):
    half = yg_ref.shape[1]
    y = _dot(yg_ref[...], wo_ref[0:half, :]) + _dot(yn_ref[...], wo_ref[half:2 * half, :])
    x1 = x_ref[...] + gate_ref[...] * y
    h = _modulated_norm(x1, nw_ref[...], scale_ref[...], shift_ref[...])
    hb = h.astype(h_ref.dtype)
    h_ref[...] = hb
    x1_ref[...] = x1 + gate_m_ref[...] * _ffn(hb, sg_ref[...], su_ref[...], sd_ref[...])
    scores = jax.nn.sigmoid(_dot(h, rw_ref[...], HI))
    n_exp = scores.shape[1]
    lane = lax.broadcasted_iota(jnp.int32, scores.shape, 1)
    cand = scores + rb_ref[...]
    picked = jnp.zeros(scores.shape, jnp.bool_)
    for _ in range(TOP_K):
        best = jnp.max(cand, axis=-1, keepdims=True)
        first = jnp.min(jnp.where(cand == best, lane, n_exp), axis=-1, keepdims=True)
        hit = lane == first
        picked = picked | hit
        cand = jnp.where(hit, NEG_BIG, cand)
    s_sel = jnp.where(picked, scores, 0.0)
    comb_ref[...] = s_sel * (ROUTED_SCALE / jnp.sum(s_sel, axis=-1, keepdims=True))


def _outproj(y_gdn, y_nsa, x, w_out_b, mod4, norm_w, router_w, router_bias, sg, su, sd, tm=256):
    bsz, t, d = x.shape
    half = y_gdn.shape[2]
    n_exp = router_w.shape[1]
    sff = sg.shape[1]

    def mod(j):
        return pl.BlockSpec((None, None, 1, d), lambda b, i: (b, j, 0, 0))

    def const(shape):
        return pl.BlockSpec(shape, lambda b, i: (0, 0))

    row = pl.BlockSpec((None, tm, d), lambda b, i: (b, i, 0))
    return pl.pallas_call(
        _outproj_kernel,
        grid=(bsz, t // tm),
        in_specs=[
            pl.BlockSpec((None, tm, half), lambda b, i: (b, i, 0)),
            pl.BlockSpec((None, tm, half), lambda b, i: (b, i, 0)),
            row,
            const((2 * half, d)),
            mod(2), mod(3), mod(4), mod(5),
            const((1, d)), const((d, n_exp)), const((1, n_exp)),
            const((d, sff)), const((d, sff)), const((sff, d)),
        ],
        out_specs=[row, row, pl.BlockSpec((None, tm, n_exp), lambda b, i: (b, i, 0))],
        out_shape=[
            jax.ShapeDtypeStruct((bsz, t, d), F32),
            jax.ShapeDtypeStruct((bsz, t, d), BF16),
            jax.ShapeDtypeStruct((bsz, t, n_exp), F32),
        ],
        compiler_params=_cparams(("arbitrary", "arbitrary")),
        name="outproj",
    )(y_gdn, y_nsa, x, w_out_b, mod4, mod4, mod4, mod4, norm_w.reshape(1, d), router_w,
      router_bias.reshape(1, n_exp), sg, su, sd)


def _moe_kernel(h_ref, comb_ref, x1_ref, gate_ref, nw_ref, wg_ref, wu_ref, wd_ref, o_ref, acc_ref):
    e = pl.program_id(2)

    @pl.when(e == 0)
    def _():
        acc_ref[...] = jnp.zeros_like(acc_ref)

    cw = _lane_pick(comb_ref[...], e)
    acc_ref[...] += cw * _ffn(h_ref[...], wg_ref[...], wu_ref[...], wd_ref[...])

    @pl.when(e == pl.num_programs(2) - 1)
    def _():
        x2 = x1_ref[...] + gate_ref[...] * acc_ref[...]
        o_ref[...] = x2 * lax.rsqrt(jnp.mean(x2 * x2, axis=-1, keepdims=True) + EPS) * nw_ref[...]


def _moe(h, comb, x1, mod4, norm_w, wg, wu, wd, tm=512):
    bsz, t, d = x1.shape
    n_exp, _, ff = wg.shape
    row = pl.BlockSpec((None, tm, d), lambda b, i, e: (b, i, 0))

    def expert(shape):
        return pl.BlockSpec((None,) + shape, lambda b, i, e: (e, 0, 0))

    return pl.pallas_call(
        _moe_kernel,
        grid=(bsz, t // tm, n_exp),
        in_specs=[
            row,
            pl.BlockSpec((None, tm, n_exp), lambda b, i, e: (b, i, 0)),
            row,
            pl.BlockSpec((None, None, 1, d), lambda b, i, e: (b, 5, 0, 0)),
            pl.BlockSpec((1, d), lambda b, i, e: (0, 0)),
            expert((d, ff)), expert((d, ff)), expert((ff, d)),
        ],
        out_specs=row,
        out_shape=jax.ShapeDtypeStruct((bsz, t, d), F32),
        scratch_shapes=[pltpu.VMEM((tm, d), F32)],
        compiler_params=_cparams(("arbitrary", "arbitrary", "arbitrary")),
        name="moe",
    )(h, comb, x1, mod4, norm_w.reshape(1, d), wg, wu, wd)


def _reorder_w_in(w_in):
    d = w_in.shape[0]
    o_beta = 3 * GDN_WIDTH
    o_a = o_beta + GDN_HEADS
    o_gate = o_a + GDN_HEADS
    o_q = o_gate + GDN_WIDTH
    o_kv = o_q + NSA_WIDTH
    o_ng = o_kv + 6 * NSA_KV_WIDTH
    n_small = 2 * GDN_HEADS + 3 * NSA_HEADS
    parts = [w_in[:, :o_beta], w_in[:, o_gate:o_q], w_in[:, o_q:o_kv], w_in[:, o_kv:o_ng],
             w_in[:, o_beta:o_gate], w_in[:, o_ng:o_ng + 3 * NSA_HEADS]]
    used = COL_SMALL * HEAD_DIM + n_small
    parts.append(jnp.zeros((d, PROJ_COLS - used), w_in.dtype))
    return jnp.concatenate(parts, axis=1).astype(BF16)


def _to_rows16(proj, col_block):
    bsz, t, _ = proj.shape
    x = proj[:, :, col_block * HEAD_DIM:(col_block + NSA_KV_HEADS) * HEAD_DIM]
    x = x.reshape(bsz, t // CMP_STRIDE, CMP_STRIDE, NSA_KV_HEADS, HEAD_DIM).transpose(0, 3, 1, 2, 4)
    return x.reshape(bsz, NSA_KV_HEADS, t // CMP_STRIDE, CMP_STRIDE * HEAD_DIM)


def _layer(x, mod, norm_attn_w, norm_ffn_w, w_in, gdn_conv_w, gdn_a_log, gdn_dt_bias, gdn_norm_w,
           cmp_pe_k, cmp_w1_k, cmp_w2_k, cmp_pe_v, cmp_w1_v, cmp_w2_v, w_out, router_w, router_bias,
           expert_w_gate, expert_w_up, expert_w_down, shared_w_gate, shared_w_up, shared_w_down, final_w):
    bsz, t, d = x.shape
    mod4 = mod.reshape(bsz, N_MOD, 1, d)
    proj = _proj(x, mod4, norm_attn_w, _reorder_w_in(w_in))
    y_gdn = _gdn(proj, gdn_conv_w, gdn_a_log, gdn_dt_bias, gdn_norm_w)
    kc = _compress(_to_rows16(proj, COL_KV), cmp_pe_k, cmp_w1_k, cmp_w2_k)
    vc = _compress(_to_rows16(proj, COL_KV + NSA_KV_HEADS), cmp_pe_v, cmp_w1_v, cmp_w2_v)
    o_cmp, sel = _cmpattn(proj, kc, vc)
    o_sel = _selattn(proj, sel)
    y_nsa = _winattn(proj, o_cmp, o_sel)
    x1, h, comb = _outproj(y_gdn, y_nsa, x, w_out.astype(BF16), mod4, norm_ffn_w, router_w, router_bias,
                           shared_w_gate.astype(BF16), shared_w_up.astype(BF16), shared_w_down.astype(BF16))
    return _moe(h, comb, x1, mod4, final_w,
                expert_w_gate.astype(BF16), expert_w_up.astype(BF16), expert_w_down.astype(BF16))


def kernel(x, c, w_ada, b_ada, norm_attn_w, norm_ffn_w, norm_final_w, w_in, gdn_conv_w, gdn_a_log, gdn_dt_bias,
           gdn_norm_w, cmp_pe_k, cmp_w1_k, cmp_w2_k, cmp_pe_v, cmp_w1_v, cmp_w2_v, w_out, router_w, router_bias,
           expert_w_gate, expert_w_up, expert_w_down, shared_w_gate, shared_w_up, shared_w_down):
    depth = w_ada.shape[0]
    assert depth == 1, "the fused final norm assumes a single layer"
    l = 0
    mod = _ada(c, w_ada[l], b_ada[l])
    return _layer(x, mod, norm_attn_w[l], norm_ffn_w[l], w_in[l], gdn_conv_w[l], gdn_a_log[l], gdn_dt_bias[l],
                  gdn_norm_w[l], cmp_pe_k[l], cmp_w1_k[l], cmp_w2_k[l], cmp_pe_v[l], cmp_w1_v[l], cmp_w2_v[l],
                  w_out[l], router_w[l], router_bias[l], expert_w_gate[l], expert_w_up[l], expert_w_down[l],
                  shared_w_gate[l], shared_w_up[l], shared_w_down[l], norm_final_w)
```

```python
import functools

import jax
import jax.numpy as jnp
from jax import lax
from jax.experimental import pallas as pl
from jax.experimental.pallas import tpu as pltpu

F32 = jnp.float32
BF16 = jnp.bfloat16
HI = lax.Precision.HIGHEST

HEAD_DIM = 128
GDN_HEADS = 8
NSA_HEADS = 8
NSA_KV_HEADS = 2
NSA_GROUP = NSA_HEADS // NSA_KV_HEADS
GDN_WIDTH = GDN_HEADS * HEAD_DIM
NSA_WIDTH = NSA_HEADS * HEAD_DIM
NSA_KV_WIDTH = NSA_KV_HEADS * HEAD_DIM
CONV_WIDTH = 4
GDN_CHUNK = 64
CMP_BLOCK = 32
CMP_STRIDE = 16
SEL_BLOCK = 64
SEL_TOPK = 16
N_LOCAL_BLOCKS = 2
WINDOW = 512
Q_BLOCK = 128
TOP_K = 8
ROUTED_SCALE = 2.5
N_MOD = 6
EPS = 1e-6

COL_GDN_Q = 0
COL_GDN_K = GDN_HEADS
COL_GDN_V = 2 * GDN_HEADS
COL_GDN_GATE = 3 * GDN_HEADS
COL_NSA_Q = 4 * GDN_HEADS
COL_KV = COL_NSA_Q + NSA_HEADS
COL_SMALL = COL_KV + 6 * NSA_KV_HEADS
N_COL_BLOCKS = COL_SMALL + 1
PROJ_TN = 2304
PROJ_COLS = 3 * PROJ_TN
SMALL_BETA = 0
SMALL_A = GDN_HEADS
SMALL_NSA_GATE = 2 * GDN_HEADS

NEG_BIG = -1e30
VMEM_LIMIT = 48 * 1024 * 1024


def _cparams(semantics):
    return pltpu.CompilerParams(dimension_semantics=semantics, vmem_limit_bytes=VMEM_LIMIT)


def _silu(x):
    return x * jax.nn.sigmoid(x)


def _dot(a, b, precision=None):
    return jnp.dot(a, b, preferred_element_type=F32, precision=precision)


def _dot_nt(a, b, precision=None):
    return lax.dot_general(a, b, (((1,), (1,)), ((), ())), preferred_element_type=F32, precision=precision)


def _dot_tn(a, b, precision=None):
    return lax.dot_general(a, b, (((0,), (0,)), ((), ())), preferred_element_type=F32, precision=precision)


def _split_bf16(x):
    hi = x.astype(BF16)
    return hi, (x - hi.astype(F32)).astype(BF16)


def _lane_pick(x, idx):
    lane = lax.broadcasted_iota(jnp.int32, x.shape, 1)
    return jnp.sum(jnp.where(lane == idx, x, 0.0), axis=-1, keepdims=True)


def _ada_kernel(ct_ref, w_ref, b_ref, o_ref):
    c = ct_ref[...]
    cond = _silu(c)
    w = w_ref[...]
    rows = [jnp.sum(w * cond[:, b:b + 1], axis=0, keepdims=True) for b in range(c.shape[1])]
    o_ref[...] = jnp.concatenate(rows, axis=0) + b_ref[...]


def _ada(c, w_ada, b_ada, tn=512):
    bsz, d = c.shape
    n = w_ada.shape[1]
    return pl.pallas_call(
        _ada_kernel,
        grid=(n // tn,),
        in_specs=[
            pl.BlockSpec((d, bsz), lambda j: (0, 0)),
            pl.BlockSpec((d, tn), lambda j: (0, j)),
            pl.BlockSpec((1, tn), lambda j: (0, j)),
        ],
        out_specs=pl.BlockSpec((bsz, tn), lambda j: (0, j)),
        out_shape=jax.ShapeDtypeStruct((bsz, n), F32),
        compiler_params=_cparams(("arbitrary",)),
        name="ada",
    )(c.T, w_ada, b_ada.reshape(1, n))


def _modulated_norm(x, nw, scale, shift):
    var = jnp.mean(x * x, axis=-1, keepdims=True)
    h = x * lax.rsqrt(var + EPS) * nw
    return h * (1.0 + scale) + shift


def _proj_kernel(x_ref, shift_ref, scale_ref, nw_ref, w_ref, o_ref):
    h = _modulated_norm(x_ref[...], nw_ref[...], scale_ref[...], shift_ref[...])
    o_ref[...] = _dot(h.astype(BF16), w_ref[...])


def _proj(x, mod4, norm_w, w_r, tm=256):
    bsz, t, d = x.shape
    n = w_r.shape[1]
    return pl.pallas_call(
        _proj_kernel,
        grid=(n // PROJ_TN, bsz, t // tm),
        in_specs=[
            pl.BlockSpec((None, tm, d), lambda j, b, i: (b, i, 0)),
            pl.BlockSpec((None, None, 1, d), lambda j, b, i: (b, 0, 0, 0)),
            pl.BlockSpec((None, None, 1, d), lambda j, b, i: (b, 1, 0, 0)),
            pl.BlockSpec((1, d), lambda j, b, i: (0, 0)),
            pl.BlockSpec((d, PROJ_TN), lambda j, b, i: (0, j)),
        ],
        out_specs=pl.BlockSpec((None, tm, PROJ_TN), lambda j, b, i: (b, i, j)),
        out_shape=jax.ShapeDtypeStruct((bsz, t, n), F32),
        compiler_params=_cparams(("arbitrary", "arbitrary", "arbitrary")),
        name="proj",
    )(x, mod4, mod4, norm_w.reshape(1, d), w_r)


def _gdn_kernel(q_ref, k_ref, v_ref, gate_ref, small_ref, hq_ref, hk_ref, hv_ref,
                cwq_ref, cwk_ref, cwv_ref, alog_ref, dtb_ref, nw_ref, o_ref, s_ref, xc_ref, *, tt, hg):
    first = pl.program_id(2) == 0

    @pl.when(first)
    def _():
        s_ref[...] = jnp.zeros_like(s_ref)

    def conv_silu(x_ref, halo_ref, cw_ref):
        xc_ref[0:8, :] = jnp.where(first, 0.0, halo_ref[...])
        xc_ref[8:tt + 8, :] = x_ref[...]
        cw = cw_ref[...]
        off = 8 - (CONV_WIDTH - 1)
        y = cw[0:1, :] * xc_ref[off:off + tt, :]
        for i in range(1, CONV_WIDTH):
            y = y + cw[i:i + 1, :] * xc_ref[off + i:off + i + tt, :]
        return _silu(y)

    q_all = conv_silu(q_ref, hq_ref, cwq_ref)
    k_all = conv_silu(k_ref, hk_ref, cwk_ref)
    v_all = conv_silu(v_ref, hv_ref, cwv_ref)
    ri = lax.broadcasted_iota(jnp.int32, (tt, tt), 0)
    ci = lax.broadcasted_iota(jnp.int32, (tt, tt), 1)
    lower = jnp.where(((ri // GDN_CHUNK) == (ci // GDN_CHUNK)) & (ci <= ri), 1.0, 0.0).astype(BF16)
    for hh in range(hg):
        cols = slice(hh * HEAD_DIM, (hh + 1) * HEAD_DIM)
        o = _gdn_head(q_all[:, cols], k_all[:, cols], v_all[:, cols], small_ref[...], lower,
                      pl.program_id(1) * hg + hh, alog_ref[...], dtb_ref[...], s_ref.at[hh], tt)
        o = o * lax.rsqrt(jnp.mean(o * o, axis=-1, keepdims=True) + EPS) * nw_ref[...]
        o_ref[:, cols] = (o * _silu(gate_ref[:, cols])).astype(o_ref.dtype)


def _gdn_head(q, k, v, small, lower, h, alog, dtb, s_ref, tt):
    c = GDN_CHUNK
    nch = tt // c
    q = q * lax.rsqrt(jnp.sum(q * q, axis=-1, keepdims=True) + EPS) * (HEAD_DIM ** -0.5)
    k = k * lax.rsqrt(jnp.sum(k * k, axis=-1, keepdims=True) + EPS)

    beta = jax.nn.sigmoid(_lane_pick(small, SMALL_BETA + h))
    z = _lane_pick(small, SMALL_A + h) + _lane_pick(dtb, h)
    softplus = jnp.maximum(z, 0.0) + jnp.log(1.0 + jnp.exp(-jnp.abs(z)))
    g = -jnp.exp(_lane_pick(alog, h)) * softplus

    kpos = lax.broadcasted_iota(jnp.int32, (tt, c), 0) % c
    jj = lax.broadcasted_iota(jnp.int32, (tt, c), 1)
    g_c = jnp.broadcast_to(g, (tt, c))
    rhs = jnp.concatenate([g_c, jnp.where(kpos > jj, g_c, 0.0)], axis=1)
    r_hi = rhs.astype(BF16)
    rem = rhs - r_hi.astype(F32)
    r_mid = rem.astype(BF16)
    r_lo = (rem - r_mid.astype(F32)).astype(BF16)
    sums = _dot(lower, r_hi) + (_dot(lower, r_mid) + _dot(lower, r_lo))
    gcum = jnp.broadcast_to(sums[:, 0:1], (tt, HEAD_DIM))
    gc3 = gcum.reshape(nch, c, HEAD_DIM)
    grest = (gc3[:, c - 1:c, :] - gc3).reshape(tt, HEAD_DIM)
    decay = jnp.where(kpos >= jj, jnp.exp(sums[:, c:2 * c]), 0.0).reshape(nch, c, c)
    strict = (kpos > jj).reshape(nch, c, c)
    eye = jnp.where(kpos == jj, 1.0, 0.0).reshape(nch, c, c)

    def bmm(a, b):
        return jnp.einsum('nij,njk->nik', a, b, preferred_element_type=F32)

    def bmm_nt(a, b):
        return jnp.einsum('nid,njd->nij', a, b, preferred_element_type=F32)

    def bmm_split(a, b):
        (ah, al), (bh, bl) = _split_bf16(a), _split_bf16(b)
        return bmm(ah, bh) + (bmm(ah, bl) + bmm(al, bh))

    k3 = k.reshape(nch, c, HEAD_DIM).astype(BF16)
    kb = k * beta
    a = jnp.where(strict, bmm_nt(kb.reshape(nch, c, HEAD_DIM).astype(BF16), k3) * decay, 0.0)
    inv = eye - a
    p = bmm_split(a, a)
    n_sq = (c - 1).bit_length() - 1
    for lvl in range(n_sq):
        inv = inv + bmm_split(inv, p)
        if lvl + 1 < n_sq:
            p = bmm_split(p, p)
    egc = jnp.exp(gcum)
    inv_b = inv.astype(BF16)
    u3 = bmm(inv_b, (v * beta).reshape(nch, c, HEAD_DIM).astype(BF16)).astype(BF16)
    w3 = bmm(inv_b, (kb * egc).reshape(nch, c, HEAD_DIM).astype(BF16)).astype(BF16)
    qk3 = (bmm_nt(q.reshape(nch, c, HEAD_DIM).astype(BF16), k3) * decay).astype(BF16)
    kd3 = (k * jnp.exp(grest)).reshape(nch, c, HEAD_DIM).astype(BF16)
    gl3 = egc.reshape(nch, c, HEAD_DIM)
    qe3 = ((q * egc).reshape(nch, c, HEAD_DIM) - bmm(qk3, w3)).astype(BF16)
    o03 = bmm(qk3, u3)
    kw = [_dot_tn(kd3[n], w3[n]).astype(BF16) for n in range(nch)]
    ku = [_dot_tn(kd3[n], u3[n]) for n in range(nch)]

    state = s_ref[...]
    outs = []
    for n in range(nch):
        state_b = state.astype(BF16)
        outs.append(_dot(qe3[n], state_b) + o03[n])
        state = state * gl3[n, c - 1:c, :] - _dot(kw[n], state_b) + ku[n]
    s_ref[...] = state
    return jnp.concatenate(outs, axis=0)


def _gdn(proj, conv_w, a_log, dt_bias, norm_w, tt=256, hg=4):
    bsz, t, _ = proj.shape
    wdt = hg * HEAD_DIM

    def col(base):
        return pl.BlockSpec((None, tt, wdt), lambda b, h, i: (b, i, base // hg + h))

    def halo(base):
        return pl.BlockSpec((None, 8, wdt), lambda b, h, i: (b, jnp.maximum(i * (tt // 8) - 1, 0), base // hg + h))

    def cw(base):
        return pl.BlockSpec((CONV_WIDTH, wdt), lambda b, h, i: (0, base // hg + h))

    heads = pl.BlockSpec((1, GDN_HEADS), lambda b, h, i: (0, 0))
    return pl.pallas_call(
        functools.partial(_gdn_kernel, tt=tt, hg=hg),
        grid=(bsz, GDN_HEADS // hg, t // tt),
        in_specs=[
            col(COL_GDN_Q), col(COL_GDN_K), col(COL_GDN_V), col(COL_GDN_GATE),
            pl.BlockSpec((None, tt, HEAD_DIM), lambda b, h, i: (b, i, COL_SMALL)),
            halo(COL_GDN_Q), halo(COL_GDN_K), halo(COL_GDN_V),
            cw(COL_GDN_Q), cw(COL_GDN_K), cw(COL_GDN_V),
            heads, heads,
            pl.BlockSpec((1, HEAD_DIM), lambda b, h, i: (0, 0)),
        ],
        out_specs=pl.BlockSpec((None, tt, wdt), lambda b, h, i: (b, i, h)),
        out_shape=jax.ShapeDtypeStruct((bsz, t, GDN_WIDTH), BF16),
        scratch_shapes=[pltpu.VMEM((hg, HEAD_DIM, HEAD_DIM), F32), pltpu.VMEM((tt + 8, wdt), F32)],
        compiler_params=_cparams(("arbitrary", "arbitrary", "arbitrary")),
        name="gdn",
    )(proj, proj, proj, proj, proj, proj, proj, proj, conv_w, conv_w, conv_w,
      a_log.reshape(1, GDN_HEADS), dt_bias.reshape(1, GDN_HEADS), norm_w.reshape(1, HEAD_DIM))


def _compress_kernel(x_ref, pe_ref, w1_ref, w2_ref, o_ref):
    x = x_ref[...]
    half = x.shape[1]
    a = _dot(x + pe_ref[0:1, :], w1_ref[0:half, :], HI)
    b = _dot(x + pe_ref[1:2, :], w1_ref[half:2 * half, :], HI)
    hid = _silu(a + pltpu.roll(b, b.shape[0] - 1, 0))
    o_ref[...] = _dot(hid, w2_ref[...], HI)


def _compress(x16, pe, w1, w2):
    bsz, hk, r, wdt = x16.shape
    hid = w1.shape[1]
    return pl.pallas_call(
        _compress_kernel,
        grid=(bsz, hk),
        in_specs=[
            pl.BlockSpec((None, None, r, wdt), lambda b, h: (b, h, 0, 0)),
            pl.BlockSpec((2, wdt), lambda b, h: (0, 0)),
            pl.BlockSpec((2 * wdt, hid), lambda b, h: (0, 0)),
            pl.BlockSpec((hid, HEAD_DIM), lambda b, h: (0, 0)),
        ],
        out_specs=pl.BlockSpec((None, None, r, HEAD_DIM), lambda b, h: (b, h, 0, 0)),
        out_shape=jax.ShapeDtypeStruct((bsz, hk, r, HEAD_DIM), F32),
        compiler_params=_cparams(("arbitrary", "arbitrary")),
        name="compress",
    )(x16, pe.reshape(2, wdt), w1, w2)


def _cmpattn_kernel(q_ref, kc_ref, vc_ref, o_ref, sel_ref, *, tq, nc, nsel):
    ncp = kc_ref.shape[0]
    scale = HEAD_DIM ** -0.5
    t0 = pl.program_id(2) * tq
    kc = kc_ref[...]
    vc = vc_ref[...]
    pos_r = t0 + lax.broadcasted_iota(jnp.int32, (tq, ncp), 0)
    c_r = lax.broadcasted_iota(jnp.int32, (tq, ncp), 1)
    valid_r = (c_r * CMP_STRIDE + CMP_BLOCK - 1 <= pos_r) & (c_r < nc)
    pos_c = t0 + lax.broadcasted_iota(jnp.int32, (ncp, tq), 1)
    c_c = lax.broadcasted_iota(jnp.int32, (ncp, tq), 0)
    valid_c = (c_c * CMP_STRIDE + CMP_BLOCK - 1 <= pos_c) & (c_c < nc)
    psum = jnp.zeros((ncp, tq), F32)
    for g in range(NSA_GROUP):
        qg = q_ref[:, g * HEAD_DIM:(g + 1) * HEAD_DIM]
        s = jnp.where(valid_r, _dot_nt(qg, kc, HI) * scale, NEG_BIG)
        e = jnp.where(valid_r, jnp.exp(s - jnp.max(s, axis=-1, keepdims=True)), 0.0)
        p = e * (1.0 / jnp.maximum(jnp.sum(e, axis=-1, keepdims=True), 1e-30))
        o_ref[:, g * HEAD_DIM:(g + 1) * HEAD_DIM] = _dot(p, vc, HI)
        st = jnp.where(valid_c, _dot_nt(kc, qg, HI) * scale, NEG_BIG)
        et = jnp.where(valid_c, jnp.exp(st - jnp.max(st, axis=0, keepdims=True)), 0.0)
        psum = psum + et * (1.0 / jnp.maximum(jnp.sum(et, axis=0, keepdims=True), 1e-30))
    j_o = lax.broadcasted_iota(jnp.int32, (nsel, ncp), 0) * SEL_BLOCK
    c_o = lax.broadcasted_iota(jnp.int32, (nsel, ncp), 1) * CMP_STRIDE
    overlap = jnp.where((c_o < j_o + SEL_BLOCK) & (c_o + CMP_BLOCK > j_o), 1.0, 0.0)
    imp = _dot(overlap, psum, HI)
    blk = lax.broadcasted_iota(jnp.int32, (nsel, tq), 0)
    dist = (t0 + lax.broadcasted_iota(jnp.int32, (nsel, tq), 1)) // SEL_BLOCK - blk
    forced = (blk == 0) | ((dist >= 0) & (dist < N_LOCAL_BLOCKS))
    val = jnp.where(forced, -NEG_BIG, jnp.where(dist >= 0, imp, -1.0))
    rank = jnp.zeros((nsel, tq), F32)
    for j in range(nsel):
        row = val[j:j + 1, :]
        rank = rank + jnp.where((row > val) | ((row == val) & (blk > j)), 1.0, 0.0)
    sel = jnp.where((rank < min(SEL_TOPK, nsel)) & (dist >= 0), 1.0, 0.0)
    sel_ref[...] = sel.T


def _cmpattn(proj, kc, vc, tq=256):
    bsz, t, _ = proj.shape
    ncp = kc.shape[2]
    nc = (t - CMP_BLOCK) // CMP_STRIDE + 1
    nsel = t // SEL_BLOCK
    gw = NSA_GROUP * HEAD_DIM
    return pl.pallas_call(
        functools.partial(_cmpattn_kernel, tq=tq, nc=nc, nsel=nsel),
        grid=(bsz, NSA_KV_HEADS, t // tq),
        in_specs=[
            pl.BlockSpec((None, tq, gw), lambda b, h, i: (b, i, COL_NSA_Q // NSA_GROUP + h)),
            pl.BlockSpec((None, None, ncp, HEAD_DIM), lambda b, h, i: (b, h, 0, 0)),
            pl.BlockSpec((None, None, ncp, HEAD_DIM), lambda b, h, i: (b, h, 0, 0)),
        ],
        out_specs=[
            pl.BlockSpec((None, tq, gw), lambda b, h, i: (b, i, h)),
            pl.BlockSpec((None, None, tq, nsel), lambda b, h, i: (b, h, i, 0)),
        ],
        out_shape=[
            jax.ShapeDtypeStruct((bsz, t, NSA_WIDTH), F32),
            jax.ShapeDtypeStruct((bsz, NSA_KV_HEADS, t, nsel), F32),
        ],
        compiler_params=_cparams(("arbitrary", "arbitrary", "arbitrary")),
        name="cmpattn",
    )(proj, kc, vc)


def _selattn_kernel(q_ref, k_ref, v_ref, sel_ref, o_ref, m_ref, l_ref, acc_ref, *, tq, tk, nsel):
    g = NSA_GROUP
    qi = pl.program_id(2)
    m_ref[...] = jnp.full_like(m_ref, NEG_BIG)
    l_ref[...] = jnp.zeros_like(l_ref)
    acc_ref[...] = jnp.zeros_like(acc_ref)
    q4 = jnp.concatenate([q_ref[:, i * HEAD_DIM:(i + 1) * HEAD_DIM] for i in range(g)], axis=0)
    q4 = (q4 * HEAD_DIM ** -0.5).astype(BF16)
    sel = sel_ref[...].astype(BF16)
    pos = qi * tq + lax.broadcasted_iota(jnp.int32, (tq, tk), 0)

    def kv_step(kv, carry):
        k0 = pl.multiple_of(kv * tk, tk)
        s = _dot_nt(q4, k_ref[pl.ds(k0, tk), :].astype(BF16))
        key = k0 + lax.broadcasted_iota(jnp.int32, (nsel, tk), 1)
        blk = lax.broadcasted_iota(jnp.int32, (nsel, tk), 0)
        expand = jnp.where(key // SEL_BLOCK == blk, 1.0, 0.0).astype(BF16)
        picked = _dot(sel, expand)
        kpos = k0 + lax.broadcasted_iota(jnp.int32, (tq, tk), 1)
        mask = ((picked > 0.5) & (kpos <= pos))[None]
        s = jnp.where(mask, s.reshape(g, tq, tk), NEG_BIG)
        m_prev = m_ref[...]
        m_new = jnp.maximum(m_prev, jnp.max(s, axis=-1, keepdims=True))
        alpha = jnp.exp(m_prev - m_new)
        p = jnp.where(mask, jnp.exp(s - m_new), 0.0)
        l_ref[...] = alpha * l_ref[...] + jnp.sum(p, axis=-1, keepdims=True)
        pv = _dot(p.reshape(g * tq, tk).astype(BF16), v_ref[pl.ds(k0, tk), :].astype(BF16))
        acc_ref[...] = alpha * acc_ref[...] + pv.reshape(g, tq, HEAD_DIM)
        m_ref[...] = m_new
        return carry

    lax.fori_loop(0, (qi * tq + tq - 1) // tk + 1, kv_step, 0)
    o = acc_ref[...] * (1.0 / jnp.maximum(l_ref[...], 1e-30))
    for i in range(g):
        o_ref[:, i * HEAD_DIM:(i + 1) * HEAD_DIM] = o[i]


def _selattn(proj, sel, tq=Q_BLOCK, tk=512):
    bsz, t, _ = proj.shape
    nsel = t // SEL_BLOCK
    tk = min(tk, t)
    gw = NSA_GROUP * HEAD_DIM

    def kv_spec(base):
        return pl.BlockSpec((None, t, HEAD_DIM), lambda b, h, i: (b, 0, base + h))

    return pl.pallas_call(
        functools.partial(_selattn_kernel, tq=tq, tk=tk, nsel=nsel),
        grid=(bsz, NSA_KV_HEADS, t // tq),
        in_specs=[
            pl.BlockSpec((None, tq, gw), lambda b, h, i: (b, i, COL_NSA_Q // NSA_GROUP + h)),
            kv_spec(COL_KV + 2 * NSA_KV_HEADS), kv_spec(COL_KV + 3 * NSA_KV_HEADS),
            pl.BlockSpec((None, None, tq, nsel), lambda b, h, i: (b, h, i, 0)),
        ],
        out_specs=pl.BlockSpec((None, tq, gw), lambda b, h, i: (b, i, h)),
        out_shape=jax.ShapeDtypeStruct((bsz, t, NSA_WIDTH), F32),
        scratch_shapes=[pltpu.VMEM((NSA_GROUP, tq, 1), F32), pltpu.VMEM((NSA_GROUP, tq, 1), F32),
                        pltpu.VMEM((NSA_GROUP, tq, HEAD_DIM), F32)],
        compiler_params=_cparams(("arbitrary", "arbitrary", "arbitrary")),
        name="selattn",
    )(proj, proj, proj, sel)


def _winattn_kernel(*refs, tq, nblk):
    q_ref = refs[0]
    k_refs = refs[1:1 + nblk]
    v_refs = refs[1 + nblk:1 + 2 * nblk]
    ocmp_ref, osel_ref, small_ref, o_ref = refs[1 + 2 * nblk:]
    g = NSA_GROUP
    hk = pl.program_id(1)
    qi = pl.program_id(2)
    scale = HEAD_DIM ** -0.5
    nk = nblk * tq
    keys = jnp.concatenate([r[...] for r in k_refs], axis=0).astype(BF16)
    vals = jnp.concatenate([r[...] for r in v_refs], axis=0).astype(BF16)
    q4 = jnp.concatenate([q_ref[:, i * HEAD_DIM:(i + 1) * HEAD_DIM] for i in range(g)], axis=0)
    s = (_dot_nt(q4.astype(BF16), keys) * scale).reshape(g, tq, nk)
    pos = qi * tq + lax.broadcasted_iota(jnp.int32, (tq, nk), 0)
    wpos = (qi - (nblk - 1)) * tq + lax.broadcasted_iota(jnp.int32, (tq, nk), 1)
    mask = ((wpos <= pos) & (wpos > pos - WINDOW) & (wpos >= 0))[None]
    s = jnp.where(mask, s, NEG_BIG)
    e = jnp.where(mask, jnp.exp(s - jnp.max(s, axis=-1, keepdims=True)), 0.0)
    p = e * (1.0 / jnp.maximum(jnp.sum(e, axis=-1, keepdims=True), 1e-30))
    o_win = _dot(p.reshape(g * tq, nk).astype(BF16), vals).reshape(g, tq, HEAD_DIM)
    small = small_ref[...]
    for i in range(g):
        head = hk * g + i
        gates = [jax.nn.sigmoid(_lane_pick(small, SMALL_NSA_GATE + j * NSA_HEADS + head)) for j in range(3)]
        cols = slice(i * HEAD_DIM, (i + 1) * HEAD_DIM)
        o = gates[0] * ocmp_ref[:, cols] + gates[1] * osel_ref[:, cols] + gates[2] * o_win[i]
        o_ref[:, cols] = o.astype(o_ref.dtype)


def _winattn(proj, o_cmp, o_sel, tq=Q_BLOCK):
    bsz, t, _ = proj.shape
    nblk = WINDOW // tq + 1
    gw = NSA_GROUP * HEAD_DIM

    def kv_spec(base, d):
        return pl.BlockSpec((None, tq, HEAD_DIM),
                            lambda b, h, i: (b, jnp.maximum(i - (nblk - 1) + d, 0), base + h))

    grp = pl.BlockSpec((None, tq, gw), lambda b, h, i: (b, i, h))
    return pl.pallas_call(
        functools.partial(_winattn_kernel, tq=tq, nblk=nblk),
        grid=(bsz, NSA_KV_HEADS, t // tq),
        in_specs=(
            [pl.BlockSpec((None, tq, gw), lambda b, h, i: (b, i, COL_NSA_Q // NSA_GROUP + h))]
            + [kv_spec(COL_KV + 4 * NSA_KV_HEADS, d) for d in range(nblk)]
            + [kv_spec(COL_KV + 5 * NSA_KV_HEADS, d) for d in range(nblk)]
            + [grp, grp, pl.BlockSpec((None, tq, HEAD_DIM), lambda b, h, i: (b, i, COL_SMALL))]
        ),
        out_specs=grp,
        out_shape=jax.ShapeDtypeStruct((bsz, t, NSA_WIDTH), BF16),
        compiler_params=_cparams(("arbitrary", "arbitrary", "arbitrary")),
        name="winattn",
    )(proj, *([proj] * (2 * nblk)), o_cmp, o_sel, proj)


def _ffn(h, wg, wu, wd):
    act = _silu(_dot(h, wg)) * _dot(h, wu)
    return _dot(act.astype(BF16), wd)


def _store_token_rows(ref, x):
    n, d = x.shape
    s_tiles = d // HEAD_DIM
    for s in range(s_tiles):
        ref[pl.ds(s, n, stride=s_tiles), :] = x[:, s * HEAD_DIM:(s + 1) * HEAD_DIM]


def _load_token_rows(ref, n, d):
    s_tiles = d // HEAD_DIM
    return jnp.concatenate([ref[pl.ds(s, n, stride=s_tiles), :] for s in range(s_tiles)], axis=1)


def _outproj_kernel(yg_ref, yn_ref, x_ref, wo_ref, gate_ref, shift_ref, scale_ref, gate_m_ref, nw_ref,
                    rw_ref, rb_ref, sg_ref, su_ref, sd_ref, x1_ref, h_ref, route_ref, count_ref, carry_ref):
    tm = x_ref.shape[0]
    half = yg_ref.shape[1]

    @pl.when((pl.program_id(0) == 0) & (pl.program_id(1) == 0))
    def _():
        carry_ref[...] = jnp.zeros_like(carry_ref)

    y = _dot(yg_ref[...], wo_ref[0:half, :]) + _dot(yn_ref[...], wo_ref[half:2 * half, :])
    x1 = x_ref[...] + gate_ref[...] * y
    h = _modulated_norm(x1, nw_ref[...], scale_ref[...], shift_ref[...])
    _store_token_rows(h_ref, h)
    x1_ref[...] = x1 + gate_m_ref[...] * _ffn(h.astype(BF16), sg_ref[...], su_ref[...], sd_ref[...])
    scores = jax.nn.sigmoid(_dot(h, rw_ref[...], HI))
    n_exp = scores.shape[1]
    lane = lax.broadcasted_iota(jnp.int32, scores.shape, 1)
    cand = scores + rb_ref[...]
    picked = jnp.zeros(scores.shape, jnp.bool_)
    firsts = []
    for _ in range(TOP_K):
        best = jnp.max(cand, axis=-1, keepdims=True)
        first = jnp.min(jnp.where(cand == best, lane, n_exp), axis=-1, keepdims=True)
        hit = lane == first
        picked = picked | hit
        cand = jnp.where(hit, NEG_BIG, cand)
        firsts.append(first)
    norm = ROUTED_SCALE / jnp.sum(jnp.where(picked, scores, 0.0), axis=-1, keepdims=True)
    onehot = jnp.where(picked, 1.0, 0.0)
    ri = lax.broadcasted_iota(jnp.int32, (tm, tm), 0)
    ci = lax.broadcasted_iota(jnp.int32, (tm, tm), 1)
    before = jnp.where(ci < ri, 1.0, 0.0).astype(BF16)
    pos = _dot(before, onehot.astype(BF16)) + carry_ref[...]
    carry_ref[...] += jnp.sum(onehot, axis=0, keepdims=True)
    count_ref[...] = carry_ref[...]
    slot = lax.broadcasted_iota(jnp.int32, (tm, 3 * TOP_K), 1)
    route = jnp.zeros((tm, 3 * TOP_K), F32)
    for j, first in enumerate(firsts):
        hit = lane == first
        route = jnp.where(slot == j, first.astype(F32), route)
        route = jnp.where(slot == TOP_K + j, _lane_pick(scores, first) * norm, route)
        route = jnp.where(slot == 2 * TOP_K + j, jnp.sum(jnp.where(hit, pos, 0.0), axis=-1, keepdims=True), route)
    route_ref[...] = route


def _outproj(y_gdn, y_nsa, x, w_out_b, mod4, norm_w, router_w, router_bias, sg, su, sd, tm=256):
    bsz, t, d = x.shape
    half = y_gdn.shape[2]
    n_exp = router_w.shape[1]
    sff = sg.shape[1]
    s_tiles = d // HEAD_DIM

    def mod(j):
        return pl.BlockSpec((None, None, 1, d), lambda b, i: (b, j, 0, 0))

    def const(shape):
        return pl.BlockSpec(shape, lambda b, i: (0, 0))

    row = pl.BlockSpec((None, tm, d), lambda b, i: (b, i, 0))
    return pl.pallas_call(
        _outproj_kernel,
        grid=(bsz, t // tm),
        in_specs=[
            pl.BlockSpec((None, tm, half), lambda b, i: (b, i, 0)),
            pl.BlockSpec((None, tm, half), lambda b, i: (b, i, 0)),
            row,
            const((2 * half, d)),
            mod(2), mod(3), mod(4), mod(5),
            const((1, d)), const((d, n_exp)), const((1, n_exp)),
            const((d, sff)), const((d, sff)), const((sff, d)),
        ],
        out_specs=[row,
                   pl.BlockSpec((tm * s_tiles, HEAD_DIM), lambda b, i: (b * (t // tm) + i, 0)),
                   pl.BlockSpec((None, tm, 3 * TOP_K), lambda b, i: (b, i, 0)),
                   const((1, n_exp))],
        out_shape=[
            jax.ShapeDtypeStruct((bsz, t, d), F32),
            jax.ShapeDtypeStruct((bsz * t * s_tiles, HEAD_DIM), F32),
            jax.ShapeDtypeStruct((bsz, t, 3 * TOP_K), F32),
            jax.ShapeDtypeStruct((1, n_exp), F32),
        ],
        scratch_shapes=[pltpu.VMEM((1, n_exp), F32)],
        compiler_params=_cparams(("arbitrary", "arbitrary")),
        name="outproj",
    )(y_gdn, y_nsa, x, w_out_b, mod4, mod4, mod4, mod4, norm_w.reshape(1, d), router_w,
      router_bias.reshape(1, n_exp), sg, su, sd)


MOE_TM = 256
MOE_TOK = 128


def _token_row(ref, r, s_tiles):
    return ref.at[pl.ds(pl.multiple_of(r * s_tiles, s_tiles), s_tiles), :]


def _for_each(n, fn):
    def body(i, carry):
        fn(i)
        return carry

    lax.fori_loop(0, n, body, 0, unroll=8)


def _moe_scatter_kernel(gstart_ref, gend_ref, idx_ref, pos_ref, h_ref, xs_hbm, zero_ref, sem, *, s_tiles):
    n_exp = gstart_ref.shape[0]

    @pl.when(pl.program_id(0) == 0)
    def _():
        zero_ref[...] = jnp.zeros_like(zero_ref)

        def tail(e):
            first = pl.multiple_of((gend_ref[e] - MOE_TM) * s_tiles, MOE_TM * s_tiles)
            return pltpu.make_async_copy(zero_ref, xs_hbm.at[pl.ds(first, MOE_TM * s_tiles), :], sem)

        def start_tail(e):
            @pl.when(gend_ref[e] > gstart_ref[e])
            def _():
                tail(e).start()

        def wait_tail(e):
            @pl.when(gend_ref[e] > gstart_ref[e])
            def _():
                tail(e).wait()

        _for_each(n_exp, start_tail)
        _for_each(n_exp, wait_tail)

    def row_copy(n):
        row = gstart_ref[idx_ref[n]] + pos_ref[n]
        return pltpu.make_async_copy(_token_row(h_ref, n // TOP_K, s_tiles), _token_row(xs_hbm, row, s_tiles), sem)

    _for_each(MOE_TOK * TOP_K, lambda n: row_copy(n).start())
    _for_each(MOE_TOK * TOP_K, lambda n: row_copy(n).wait())


def _moe_scatter(h_rows, idx, pos, gstart, gend, n_rows, s_tiles):
    n_steps = idx.shape[0] // (MOE_TOK * TOP_K)
    smem = pl.BlockSpec((MOE_TOK * TOP_K,), lambda i, gs, ge: (i,), memory_space=pltpu.SMEM)
    return pl.pallas_call(
        functools.partial(_moe_scatter_kernel, s_tiles=s_tiles),
        grid_spec=pltpu.PrefetchScalarGridSpec(
            num_scalar_prefetch=2,
            grid=(n_steps,),
            in_specs=[smem, smem, pl.BlockSpec((MOE_TOK * s_tiles, HEAD_DIM), lambda i, gs, ge: (i, 0))],
            out_specs=pl.BlockSpec(memory_space=pl.ANY),
            scratch_shapes=[pltpu.VMEM((MOE_TM * s_tiles, HEAD_DIM), F32), pltpu.SemaphoreType.DMA(())],
        ),
        out_shape=jax.ShapeDtypeStruct((n_rows * s_tiles, HEAD_DIM), F32),
        compiler_params=_cparams(("arbitrary",)),
        name="moe_scatter",
    )(gstart, gend, idx, pos, h_rows)


def _moe_expert_kernel(texp_ref, nvalid_ref, xs_ref, wg_ref, wu_ref, wd_ref, ys_ref, wgb_ref, wub_ref, wdb_ref):
    i = pl.program_id(0)
    d = wg_ref.shape[0]
    valid = i < nvalid_ref[0]
    new_expert = (i == 0) | (texp_ref[i] != texp_ref[jnp.maximum(i - 1, 0)])

    @pl.when(valid & new_expert)
    def _():
        wgb_ref[...] = wg_ref[...].astype(BF16)
        wub_ref[...] = wu_ref[...].astype(BF16)
        wdb_ref[...] = wd_ref[...].astype(BF16)

    @pl.when(valid)
    def _():
        x = _load_token_rows(xs_ref, MOE_TM, d).astype(BF16)
        _store_token_rows(ys_ref, _ffn(x, wgb_ref[...], wub_ref[...], wdb_ref[...]))

    @pl.when(jnp.logical_not(valid))
    def _():
        ys_ref[...] = jnp.zeros_like(ys_ref)


def _moe_experts(xs, tile_expert, nvalid, wg, wu, wd):
    n_exp, d, ff = wg.shape
    s_tiles = d // HEAD_DIM
    blk = MOE_TM * s_tiles

    def expert(shape):
        return pl.BlockSpec((None,) + shape, lambda i, te, nv: (te[i], 0, 0))

    return pl.pallas_call(
        _moe_expert_kernel,
        grid_spec=pltpu.PrefetchScalarGridSpec(
            num_scalar_prefetch=2,
            grid=(xs.shape[0] // blk,),
            in_specs=[
                pl.BlockSpec((blk, HEAD_DIM), lambda i, te, nv: (jnp.minimum(i, nv[0] - 1), 0)),
                expert((d, ff)), expert((d, ff)), expert((ff, d)),
            ],
            out_specs=pl.BlockSpec((blk, HEAD_DIM), lambda i, te, nv: (i, 0)),
            scratch_shapes=[pltpu.VMEM((d, ff), BF16), pltpu.VMEM((d, ff), BF16), pltpu.VMEM((ff, d), BF16)],
        ),
        out_shape=jax.ShapeDtypeStruct(xs.shape, F32),
        compiler_params=_cparams(("arbitrary",)),
        name="moe_experts",
    )(tile_expert, nvalid, xs, wg, wu, wd)


def _moe_combine_kernel(gstart_ref, idx_ref, pos_ref, w_ref, ys_hbm, x1_ref, gate_ref, nw_ref, o_ref,
                        buf_ref, acc_ref, sem, *, s_tiles):
    tm, d = x1_ref.shape

    def row_copy(n):
        row = gstart_ref[idx_ref[n]] + pos_ref[n]
        return pltpu.make_async_copy(_token_row(ys_hbm, row, s_tiles), _token_row(buf_ref, n, s_tiles), sem)

    _for_each(tm * TOP_K, lambda n: row_copy(n).start())
    _for_each(tm * TOP_K, lambda n: row_copy(n).wait())

    def token(t):
        n0 = t * TOP_K
        acc = w_ref[n0] * _token_row(buf_ref, n0, s_tiles)[...]
        for j in range(1, TOP_K):
            acc = acc + w_ref[n0 + j] * _token_row(buf_ref, n0 + j, s_tiles)[...]
        _token_row(acc_ref, t, s_tiles)[...] = acc

    _for_each(tm, token)
    x2 = x1_ref[...] + gate_ref[...] * _load_token_rows(acc_ref, tm, d)
    o_ref[...] = x2 * lax.rsqrt(jnp.mean(x2 * x2, axis=-1, keepdims=True) + EPS) * nw_ref[...]


def _moe_combine(ys, idx, pos, w, gstart, x1, mod4, norm_w):
    bsz, t, d = x1.shape
    s_tiles = d // HEAD_DIM
    tm = MOE_TOK
    nt = t // tm
    smem = pl.BlockSpec((tm * TOP_K,), lambda b, i, gs: (b * nt + i,), memory_space=pltpu.SMEM)
    row = pl.BlockSpec((None, tm, d), lambda b, i, gs: (b, i, 0))
    return pl.pallas_call(
        functools.partial(_moe_combine_kernel, s_tiles=s_tiles),
        grid_spec=pltpu.PrefetchScalarGridSpec(
            num_scalar_prefetch=1,
            grid=(bsz, nt),
            in_specs=[
                smem, smem, smem,
                pl.BlockSpec(memory_space=pl.ANY),
                row,
                pl.BlockSpec((None, None, 1, d), lambda b, i, gs: (b, 5, 0, 0)),
                pl.BlockSpec((1, d), lambda b, i, gs: (0, 0)),
            ],
            out_specs=row,
            scratch_shapes=[pltpu.VMEM((tm * TOP_K * s_tiles, HEAD_DIM), F32),
                            pltpu.VMEM((tm * s_tiles, HEAD_DIM), F32), pltpu.SemaphoreType.DMA(())],
        ),
        out_shape=jax.ShapeDtypeStruct((bsz, t, d), F32),
        compiler_params=_cparams(("arbitrary", "arbitrary")),
        name="moe_combine",
    )(gstart, idx, pos, w, ys, x1, mod4, norm_w.reshape(1, d))


def _moe(h_rows, route, counts, x1, mod4, norm_w, wg, wu, wd):
    bsz, t, d = x1.shape
    n_exp = wg.shape[0]
    s_tiles = d // HEAD_DIM
    idx = route[..., 0:TOP_K].astype(jnp.int32).reshape(-1)
    w = route[..., TOP_K:2 * TOP_K].reshape(-1)
    pos = route[..., 2 * TOP_K:3 * TOP_K].astype(jnp.int32).reshape(-1)
    padded = (counts.reshape(n_exp).astype(jnp.int32) + MOE_TM - 1) // MOE_TM * MOE_TM
    gend = jnp.cumsum(padded)
    gstart = gend - padded
    nvalid = gend[-1:] // MOE_TM
    n_tiles = bsz * t * TOP_K // MOE_TM + n_exp
    tile_start = jnp.minimum(jnp.arange(n_tiles, dtype=jnp.int32), nvalid - 1) * MOE_TM
    tile_expert = jnp.sum((gend[None, :] <= tile_start[:, None]).astype(jnp.int32), axis=1)
    xs = _moe_scatter(h_rows, idx, pos, gstart, gend, n_tiles * MOE_TM, s_tiles)
    ys = _moe_experts(xs, tile_expert, nvalid, wg, wu, wd)
    return _moe_combine(ys, idx, pos, w, gstart, x1, mod4, norm_w)


def _reorder_w_in(w_in):
    d = w_in.shape[0]
    o_beta = 3 * GDN_WIDTH
    o_a = o_beta + GDN_HEADS
    o_gate = o_a + GDN_HEADS
    o_q = o_gate + GDN_WIDTH
    o_kv = o_q + NSA_WIDTH
    o_ng = o_kv + 6 * NSA_KV_WIDTH
    n_small = 2 * GDN_HEADS + 3 * NSA_HEADS
    parts = [w_in[:, :o_beta], w_in[:, o_gate:o_q], w_in[:, o_q:o_kv], w_in[:, o_kv:o_ng],
             w_in[:, o_beta:o_gate], w_in[:, o_ng:o_ng + 3 * NSA_HEADS]]
    used = COL_SMALL * HEAD_DIM + n_small
    parts.append(jnp.zeros((d, PROJ_COLS - used), w_in.dtype))
    return jnp.concatenate(parts, axis=1).astype(BF16)


def _to_rows16(proj, col_block):
    bsz, t, _ = proj.shape
    x = proj[:, :, col_block * HEAD_DIM:(col_block + NSA_KV_HEADS) * HEAD_DIM]
    x = x.reshape(bsz, t // CMP_STRIDE, CMP_STRIDE, NSA_KV_HEADS, HEAD_DIM).transpose(0, 3, 1, 2, 4)
    return x.reshape(bsz, NSA_KV_HEADS, t // CMP_STRIDE, CMP_STRIDE * HEAD_DIM)


def _layer(x, mod, norm_attn_w, norm_ffn_w, w_in, gdn_conv_w, gdn_a_log, gdn_dt_bias, gdn_norm_w,
           cmp_pe_k, cmp_w1_k, cmp_w2_k, cmp_pe_v, cmp_w1_v, cmp_w2_v, w_out, router_w, router_bias,
           expert_w_gate, expert_w_up, expert_w_down, shared_w_gate, shared_w_up, shared_w_down, final_w):
    bsz, t, d = x.shape
    mod4 = mod.reshape(bsz, N_MOD, 1, d)
    proj = _proj(x, mod4, norm_attn_w, _reorder_w_in(w_in))
    y_gdn = _gdn(proj, gdn_conv_w, gdn_a_log, gdn_dt_bias, gdn_norm_w)
    kc = _compress(_to_rows16(proj, COL_KV), cmp_pe_k, cmp_w1_k, cmp_w2_k)
    vc = _compress(_to_rows16(proj, COL_KV + NSA_KV_HEADS), cmp_pe_v, cmp_w1_v, cmp_w2_v)
    o_cmp, sel = _cmpattn(proj, kc, vc)
    o_sel = _selattn(proj, sel)
    y_nsa = _winattn(proj, o_cmp, o_sel)
    x1, h_rows, route, counts = _outproj(
        y_gdn, y_nsa, x, w_out.astype(BF16), mod4, norm_ffn_w, router_w, router_bias,
        shared_w_gate.astype(BF16), shared_w_up.astype(BF16), shared_w_down.astype(BF16))
    return _moe(h_rows, route, counts, x1, mod4, final_w, expert_w_gate, expert_w_up, expert_w_down)


def kernel(x, c, w_ada, b_ada, norm_attn_w, norm_ffn_w, norm_final_w, w_in, gdn_conv_w, gdn_a_log, gdn_dt_bias,
           gdn_norm_w, cmp_pe_k, cmp_w1_k, cmp_w2_k, cmp_pe_v, cmp_w1_v, cmp_w2_v, w_out, router_w, router_bias,
           expert_w_gate, expert_w_up, expert_w_down, shared_w_gate, shared_w_up, shared_w_down):
    depth = w_ada.shape[0]
    assert depth == 1, "the fused final norm assumes a single layer"
    l = 0
    mod = _ada(c, w_ada[l], b_ada[l])
    return _layer(x, mod, norm_attn_w[l], norm_ffn_w[l], w_in[l], gdn_conv_w[l], gdn_a_log[l], gdn_dt_bias[l],
                  gdn_norm_w[l], cmp_pe_k[l], cmp_w1_k[l], cmp_w2_k[l], cmp_pe_v[l], cmp_w1_v[l], cmp_w2_v[l],
                  w_out[l], router_w[l], router_bias[l], expert_w_gate[l], expert_w_up[l], expert_w_down[l],
                  shared_w_gate[l], shared_w_up[l], shared_w_down[l], norm_final_w)
```

```python
import functools

import jax
import jax.numpy as jnp
from jax import lax
from jax.experimental import pallas as pl
from jax.experimental.pallas import tpu as pltpu

F32 = jnp.float32
BF16 = jnp.bfloat16
HI = lax.Precision.HIGHEST

HEAD_DIM = 128
GDN_HEADS = 8
NSA_HEADS = 8
NSA_KV_HEADS = 2
NSA_GROUP = NSA_HEADS // NSA_KV_HEADS
GDN_WIDTH = GDN_HEADS * HEAD_DIM
NSA_WIDTH = NSA_HEADS * HEAD_DIM
NSA_KV_WIDTH = NSA_KV_HEADS * HEAD_DIM
CONV_WIDTH = 4
GDN_CHUNK = 64
CMP_BLOCK = 32
CMP_STRIDE = 16
SEL_BLOCK = 64
SEL_TOPK = 16
N_LOCAL_BLOCKS = 2
WINDOW = 512
Q_BLOCK = 128
TOP_K = 8
ROUTED_SCALE = 2.5
N_MOD = 6
EPS = 1e-6

COL_GDN_Q = 0
COL_GDN_K = GDN_HEADS
COL_GDN_V = 2 * GDN_HEADS
COL_GDN_GATE = 3 * GDN_HEADS
COL_NSA_Q = 4 * GDN_HEADS
COL_KV = COL_NSA_Q + NSA_HEADS
COL_SMALL = COL_KV + 6 * NSA_KV_HEADS
N_COL_BLOCKS = COL_SMALL + 1
PROJ_TN = 2304
PROJ_COLS = 3 * PROJ_TN
SMALL_BETA = 0
SMALL_A = GDN_HEADS
SMALL_NSA_GATE = 2 * GDN_HEADS

NEG_BIG = -1e30
VMEM_LIMIT = 48 * 1024 * 1024


def _cparams(semantics):
    return pltpu.CompilerParams(dimension_semantics=semantics, vmem_limit_bytes=VMEM_LIMIT)


def _silu(x):
    return x * jax.nn.sigmoid(x)


def _dot(a, b, precision=None):
    return jnp.dot(a, b, preferred_element_type=F32, precision=precision)


def _dot_nt(a, b, precision=None):
    return lax.dot_general(a, b, (((1,), (1,)), ((), ())), preferred_element_type=F32, precision=precision)


def _dot_tn(a, b, precision=None):
    return lax.dot_general(a, b, (((0,), (0,)), ((), ())), preferred_element_type=F32, precision=precision)


def _split_bf16(x):
    hi = x.astype(BF16)
    return hi, (x - hi.astype(F32)).astype(BF16)


def _dot_split(a, b, nt=False):
    (ah, al), (bh, bl) = _split_bf16(a), _split_bf16(b)
    f = _dot_nt if nt else _dot
    return f(ah, bh) + (f(ah, bl) + f(al, bh))


def _lane_pick(x, idx):
    lane = lax.broadcasted_iota(jnp.int32, x.shape, 1)
    return jnp.sum(jnp.where(lane == idx, x, 0.0), axis=-1, keepdims=True)


def _ada_kernel(ct_ref, w_ref, b_ref, o_ref):
    c = ct_ref[...]
    cond = _silu(c)
    w = w_ref[...]
    rows = [jnp.sum(w * cond[:, b:b + 1], axis=0, keepdims=True) for b in range(c.shape[1])]
    o_ref[...] = jnp.concatenate(rows, axis=0) + b_ref[...]


def _ada(c, w_ada, b_ada, tn=512):
    bsz, d = c.shape
    n = w_ada.shape[1]
    return pl.pallas_call(
        _ada_kernel,
        grid=(n // tn,),
        in_specs=[
            pl.BlockSpec((d, bsz), lambda j: (0, 0)),
            pl.BlockSpec((d, tn), lambda j: (0, j)),
            pl.BlockSpec((1, tn), lambda j: (0, j)),
        ],
        out_specs=pl.BlockSpec((bsz, tn), lambda j: (0, j)),
        out_shape=jax.ShapeDtypeStruct((bsz, n), F32),
        compiler_params=_cparams(("arbitrary",)),
        name="ada",
    )(c.T, w_ada, b_ada.reshape(1, n))


def _modulated_norm(x, nw, scale, shift):
    var = jnp.mean(x * x, axis=-1, keepdims=True)
    h = x * lax.rsqrt(var + EPS) * nw
    return h * (1.0 + scale) + shift


def _proj_kernel(x_ref, shift_ref, scale_ref, nw_ref, w_ref, o_ref):
    h = _modulated_norm(x_ref[...], nw_ref[...], scale_ref[...], shift_ref[...])
    o_ref[...] = _dot(h.astype(BF16), w_ref[...])


def _proj(x, mod4, norm_w, w_r, tm=256):
    bsz, t, d = x.shape
    n = w_r.shape[1]
    return pl.pallas_call(
        _proj_kernel,
        grid=(n // PROJ_TN, bsz, t // tm),
        in_specs=[
            pl.BlockSpec((None, tm, d), lambda j, b, i: (b, i, 0)),
            pl.BlockSpec((None, None, 1, d), lambda j, b, i: (b, 0, 0, 0)),
            pl.BlockSpec((None, None, 1, d), lambda j, b, i: (b, 1, 0, 0)),
            pl.BlockSpec((1, d), lambda j, b, i: (0, 0)),
            pl.BlockSpec((d, PROJ_TN), lambda j, b, i: (0, j)),
        ],
        out_specs=pl.BlockSpec((None, tm, PROJ_TN), lambda j, b, i: (b, i, j)),
        out_shape=jax.ShapeDtypeStruct((bsz, t, n), F32),
        compiler_params=_cparams(("arbitrary", "arbitrary", "arbitrary")),
        name="proj",
    )(x, mod4, mod4, norm_w.reshape(1, d), w_r)


def _gdn_kernel(q_ref, k_ref, v_ref, gate_ref, small_ref, hq_ref, hk_ref, hv_ref,
                cwq_ref, cwk_ref, cwv_ref, alog_ref, dtb_ref, nw_ref, o_ref, s_ref, xc_ref, *, tt, hg):
    first = pl.program_id(2) == 0

    @pl.when(first)
    def _():
        s_ref[...] = jnp.zeros_like(s_ref)

    def conv_silu(x_ref, halo_ref, cw_ref):
        xc_ref[0:8, :] = jnp.where(first, 0.0, halo_ref[...])
        xc_ref[8:tt + 8, :] = x_ref[...]
        cw = cw_ref[...]
        off = 8 - (CONV_WIDTH - 1)
        y = cw[0:1, :] * xc_ref[off:off + tt, :]
        for i in range(1, CONV_WIDTH):
            y = y + cw[i:i + 1, :] * xc_ref[off + i:off + i + tt, :]
        return _silu(y)

    q_all = conv_silu(q_ref, hq_ref, cwq_ref)
    k_all = conv_silu(k_ref, hk_ref, cwk_ref)
    v_all = conv_silu(v_ref, hv_ref, cwv_ref)
    ri = lax.broadcasted_iota(jnp.int32, (tt, tt), 0)
    ci = lax.broadcasted_iota(jnp.int32, (tt, tt), 1)
    lower = jnp.where(((ri // GDN_CHUNK) == (ci // GDN_CHUNK)) & (ci <= ri), 1.0, 0.0).astype(BF16)
    for hh in range(hg):
        cols = slice(hh * HEAD_DIM, (hh + 1) * HEAD_DIM)
        o = _gdn_head(q_all[:, cols], k_all[:, cols], v_all[:, cols], small_ref[...], lower,
                      pl.program_id(1) * hg + hh, alog_ref[...], dtb_ref[...], s_ref.at[hh], tt)
        o = o * lax.rsqrt(jnp.mean(o * o, axis=-1, keepdims=True) + EPS) * nw_ref[...]
        o_ref[:, cols] = (o * _silu(gate_ref[:, cols])).astype(o_ref.dtype)


def _gdn_head(q, k, v, small, lower, h, alog, dtb, s_ref, tt):
    c = GDN_CHUNK
    nch = tt // c
    q = q * lax.rsqrt(jnp.sum(q * q, axis=-1, keepdims=True) + EPS) * (HEAD_DIM ** -0.5)
    k = k * lax.rsqrt(jnp.sum(k * k, axis=-1, keepdims=True) + EPS)

    beta = jax.nn.sigmoid(_lane_pick(small, SMALL_BETA + h))
    z = _lane_pick(small, SMALL_A + h) + _lane_pick(dtb, h)
    softplus = jnp.maximum(z, 0.0) + jnp.log(1.0 + jnp.exp(-jnp.abs(z)))
    g = -jnp.exp(_lane_pick(alog, h)) * softplus

    kpos = lax.broadcasted_iota(jnp.int32, (tt, c), 0) % c
    jj = lax.broadcasted_iota(jnp.int32, (tt, c), 1)
    g_c = jnp.broadcast_to(g, (tt, c))
    rhs = jnp.concatenate([g_c, jnp.where(kpos > jj, g_c, 0.0)], axis=1)
    r_hi = rhs.astype(BF16)
    rem = rhs - r_hi.astype(F32)
    r_mid = rem.astype(BF16)
    r_lo = (rem - r_mid.astype(F32)).astype(BF16)
    sums = _dot(lower, r_hi) + (_dot(lower, r_mid) + _dot(lower, r_lo))
    gcum = jnp.broadcast_to(sums[:, 0:1], (tt, HEAD_DIM))
    gc3 = gcum.reshape(nch, c, HEAD_DIM)
    grest = (gc3[:, c - 1:c, :] - gc3).reshape(tt, HEAD_DIM)
    decay = jnp.where(kpos >= jj, jnp.exp(sums[:, c:2 * c]), 0.0).reshape(nch, c, c)
    strict = (kpos > jj).reshape(nch, c, c)
    eye = jnp.where(kpos == jj, 1.0, 0.0).reshape(nch, c, c)

    def bmm(a, b):
        return jnp.einsum('nij,njk->nik', a, b, preferred_element_type=F32)

    def bmm_nt(a, b):
        return jnp.einsum('nid,njd->nij', a, b, preferred_element_type=F32)

    def bmm_split(a, b):
        (ah, al), (bh, bl) = _split_bf16(a), _split_bf16(b)
        return bmm(ah, bh) + (bmm(ah, bl) + bmm(al, bh))

    k3 = k.reshape(nch, c, HEAD_DIM).astype(BF16)
    kb = k * beta
    a = jnp.where(strict, bmm_nt(kb.reshape(nch, c, HEAD_DIM).astype(BF16), k3) * decay, 0.0)
    inv = eye - a
    p = bmm_split(a, a)
    n_sq = (c - 1).bit_length() - 1
    for lvl in range(n_sq):
        inv = inv + bmm_split(inv, p)
        if lvl + 1 < n_sq:
            p = bmm_split(p, p)
    egc = jnp.exp(gcum)
    inv_b = inv.astype(BF16)
    u3 = bmm(inv_b, (v * beta).reshape(nch, c, HEAD_DIM).astype(BF16)).astype(BF16)
    w3 = bmm(inv_b, (kb * egc).reshape(nch, c, HEAD_DIM).astype(BF16)).astype(BF16)
    qk3 = (bmm_nt(q.reshape(nch, c, HEAD_DIM).astype(BF16), k3) * decay).astype(BF16)
    kd3 = (k * jnp.exp(grest)).reshape(nch, c, HEAD_DIM).astype(BF16)
    gl3 = egc.reshape(nch, c, HEAD_DIM)
    qe3 = ((q * egc).reshape(nch, c, HEAD_DIM) - bmm(qk3, w3)).astype(BF16)
    o03 = bmm(qk3, u3)
    kw = [_dot_tn(kd3[n], w3[n]).astype(BF16) for n in range(nch)]
    ku = [_dot_tn(kd3[n], u3[n]) for n in range(nch)]

    state = s_ref[...]
    outs = []
    for n in range(nch):
        state_b = state.astype(BF16)
        outs.append(_dot(qe3[n], state_b) + o03[n])
        state = state * gl3[n, c - 1:c, :] - _dot(kw[n], state_b) + ku[n]
    s_ref[...] = state
    return jnp.concatenate(outs, axis=0)


def _gdn(proj, conv_w, a_log, dt_bias, norm_w, tt=256, hg=4):
    bsz, t, _ = proj.shape
    wdt = hg * HEAD_DIM

    def col(base):
        return pl.BlockSpec((None, tt, wdt), lambda b, h, i: (b, i, base // hg + h))

    def halo(base):
        return pl.BlockSpec((None, 8, wdt), lambda b, h, i: (b, jnp.maximum(i * (tt // 8) - 1, 0), base // hg + h))

    def cw(base):
        return pl.BlockSpec((CONV_WIDTH, wdt), lambda b, h, i: (0, base // hg + h))

    heads = pl.BlockSpec((1, GDN_HEADS), lambda b, h, i: (0, 0))
    return pl.pallas_call(
        functools.partial(_gdn_kernel, tt=tt, hg=hg),
        grid=(bsz, GDN_HEADS // hg, t // tt),
        in_specs=[
            col(COL_GDN_Q), col(COL_GDN_K), col(COL_GDN_V), col(COL_GDN_GATE),
            pl.BlockSpec((None, tt, HEAD_DIM), lambda b, h, i: (b, i, COL_SMALL)),
            halo(COL_GDN_Q), halo(COL_GDN_K), halo(COL_GDN_V),
            cw(COL_GDN_Q), cw(COL_GDN_K), cw(COL_GDN_V),
            heads, heads,
            pl.BlockSpec((1, HEAD_DIM), lambda b, h, i: (0, 0)),
        ],
        out_specs=pl.BlockSpec((None, tt, wdt), lambda b, h, i: (b, i, h)),
        out_shape=jax.ShapeDtypeStruct((bsz, t, GDN_WIDTH), BF16),
        scratch_shapes=[pltpu.VMEM((hg, HEAD_DIM, HEAD_DIM), F32), pltpu.VMEM((tt + 8, wdt), F32)],
        compiler_params=_cparams(("arbitrary", "arbitrary", "arbitrary")),
        name="gdn",
    )(proj, proj, proj, proj, proj, proj, proj, proj, conv_w, conv_w, conv_w,
      a_log.reshape(1, GDN_HEADS), dt_bias.reshape(1, GDN_HEADS), norm_w.reshape(1, HEAD_DIM))


def _compress_kernel(x_ref, pe_ref, w1_ref, w2_ref, o_ref):
    x = x_ref[...]
    half = x.shape[1]
    a = _dot(x + pe_ref[0:1, :], w1_ref[0:half, :], HI)
    b = _dot(x + pe_ref[1:2, :], w1_ref[half:2 * half, :], HI)
    hid = _silu(a + pltpu.roll(b, b.shape[0] - 1, 0))
    o_ref[...] = _dot(hid, w2_ref[...], HI)


def _compress(x16, pe, w1, w2):
    bsz, hk, r, wdt = x16.shape
    hid = w1.shape[1]
    return pl.pallas_call(
        _compress_kernel,
        grid=(bsz, hk),
        in_specs=[
            pl.BlockSpec((None, None, r, wdt), lambda b, h: (b, h, 0, 0)),
            pl.BlockSpec((2, wdt), lambda b, h: (0, 0)),
            pl.BlockSpec((2 * wdt, hid), lambda b, h: (0, 0)),
            pl.BlockSpec((hid, HEAD_DIM), lambda b, h: (0, 0)),
        ],
        out_specs=pl.BlockSpec((None, None, r, HEAD_DIM), lambda b, h: (b, h, 0, 0)),
        out_shape=jax.ShapeDtypeStruct((bsz, hk, r, HEAD_DIM), F32),
        compiler_params=_cparams(("arbitrary", "arbitrary")),
        name="compress",
    )(x16, pe.reshape(2, wdt), w1, w2)


def _cmpattn_kernel(q_ref, kc_ref, vc_ref, o_ref, sel_ref, *, tq, nc, nsel):
    ncp = kc_ref.shape[0]
    scale = HEAD_DIM ** -0.5
    t0 = pl.program_id(2) * tq
    kc = kc_ref[...]
    vc = vc_ref[...]
    pos_r = t0 + lax.broadcasted_iota(jnp.int32, (tq, ncp), 0)
    c_r = lax.broadcasted_iota(jnp.int32, (tq, ncp), 1)
    valid_r = (c_r * CMP_STRIDE + CMP_BLOCK - 1 <= pos_r) & (c_r < nc)
    pos_c = t0 + lax.broadcasted_iota(jnp.int32, (ncp, tq), 1)
    c_c = lax.broadcasted_iota(jnp.int32, (ncp, tq), 0)
    valid_c = (c_c * CMP_STRIDE + CMP_BLOCK - 1 <= pos_c) & (c_c < nc)
    psum = jnp.zeros((ncp, tq), F32)
    kc_hi, kc_lo = _split_bf16(kc)
    vc_b = vc.astype(BF16)
    for g in range(NSA_GROUP):
        q_hi, q_lo = _split_bf16(q_ref[:, g * HEAD_DIM:(g + 1) * HEAD_DIM])
        s = _dot_nt(q_hi, kc_hi) + (_dot_nt(q_hi, kc_lo) + _dot_nt(q_lo, kc_hi))
        s = jnp.where(valid_r, s * scale, NEG_BIG)
        e = jnp.where(valid_r, jnp.exp(s - jnp.max(s, axis=-1, keepdims=True)), 0.0)
        p = e * (1.0 / jnp.maximum(jnp.sum(e, axis=-1, keepdims=True), 1e-30))
        o_ref[:, g * HEAD_DIM:(g + 1) * HEAD_DIM] = _dot(p.astype(BF16), vc_b)
        st = _dot_nt(kc_hi, q_hi) + (_dot_nt(kc_lo, q_hi) + _dot_nt(kc_hi, q_lo))
        st = jnp.where(valid_c, st * scale, NEG_BIG)
        et = jnp.where(valid_c, jnp.exp(st - jnp.max(st, axis=0, keepdims=True)), 0.0)
        psum = psum + et * (1.0 / jnp.maximum(jnp.sum(et, axis=0, keepdims=True), 1e-30))
    j_o = lax.broadcasted_iota(jnp.int32, (nsel, ncp), 0) * SEL_BLOCK
    c_o = lax.broadcasted_iota(jnp.int32, (nsel, ncp), 1) * CMP_STRIDE
    overlap = jnp.where((c_o < j_o + SEL_BLOCK) & (c_o + CMP_BLOCK > j_o), 1.0, 0.0).astype(BF16)
    p_hi, p_lo = _split_bf16(psum)
    imp = _dot(overlap, p_hi) + _dot(overlap, p_lo)
    blk = lax.broadcasted_iota(jnp.int32, (nsel, tq), 0)
    dist = (t0 + lax.broadcasted_iota(jnp.int32, (nsel, tq), 1)) // SEL_BLOCK - blk
    forced = (blk == 0) | ((dist >= 0) & (dist < N_LOCAL_BLOCKS))
    val = jnp.where(forced, -NEG_BIG, jnp.where(dist >= 0, imp, -1.0))
    rank = jnp.zeros((nsel, tq), F32)
    for j in range(nsel):
        row = val[j:j + 1, :]
        rank = rank + jnp.where((row > val) | ((row == val) & (blk > j)), 1.0, 0.0)
    sel = jnp.where((rank < min(SEL_TOPK, nsel)) & (dist >= 0), 1.0, 0.0)
    sel_ref[...] = sel.T


def _cmpattn(proj, kc, vc, tq=256):
    bsz, t, _ = proj.shape
    ncp = kc.shape[2]
    nc = (t - CMP_BLOCK) // CMP_STRIDE + 1
    nsel = t // SEL_BLOCK
    gw = NSA_GROUP * HEAD_DIM
    return pl.pallas_call(
        functools.partial(_cmpattn_kernel, tq=tq, nc=nc, nsel=nsel),
        grid=(bsz, NSA_KV_HEADS, t // tq),
        in_specs=[
            pl.BlockSpec((None, tq, gw), lambda b, h, i: (b, i, COL_NSA_Q // NSA_GROUP + h)),
            pl.BlockSpec((None, None, ncp, HEAD_DIM), lambda b, h, i: (b, h, 0, 0)),
            pl.BlockSpec((None, None, ncp, HEAD_DIM), lambda b, h, i: (b, h, 0, 0)),
        ],
        out_specs=[
            pl.BlockSpec((None, tq, gw), lambda b, h, i: (b, i, h)),
            pl.BlockSpec((None, None, tq, nsel), lambda b, h, i: (b, h, i, 0)),
        ],
        out_shape=[
            jax.ShapeDtypeStruct((bsz, t, NSA_WIDTH), F32),
            jax.ShapeDtypeStruct((bsz, NSA_KV_HEADS, t, nsel), F32),
        ],
        compiler_params=_cparams(("arbitrary", "arbitrary", "arbitrary")),
        name="cmpattn",
    )(proj, kc, vc)


def _selattn_kernel(q_ref, k_ref, v_ref, sel_ref, o_ref, m_ref, l_ref, acc_ref, *, tq, tk, nsel):
    g = NSA_GROUP
    qi = pl.program_id(2)
    m_ref[...] = jnp.full_like(m_ref, NEG_BIG)
    l_ref[...] = jnp.zeros_like(l_ref)
    acc_ref[...] = jnp.zeros_like(acc_ref)
    q4 = jnp.concatenate([q_ref[:, i * HEAD_DIM:(i + 1) * HEAD_DIM] for i in range(g)], axis=0)
    q4 = (q4 * HEAD_DIM ** -0.5).astype(BF16)
    sel = sel_ref[...].astype(BF16)
    pos = qi * tq + lax.broadcasted_iota(jnp.int32, (tq, tk), 0)

    def kv_step(kv, carry):
        k0 = pl.multiple_of(kv * tk, tk)
        s = _dot_nt(q4, k_ref[pl.ds(k0, tk), :].astype(BF16))
        key = k0 + lax.broadcasted_iota(jnp.int32, (nsel, tk), 1)
        blk = lax.broadcasted_iota(jnp.int32, (nsel, tk), 0)
        expand = jnp.where(key // SEL_BLOCK == blk, 1.0, 0.0).astype(BF16)
        picked = _dot(sel, expand)
        kpos = k0 + lax.broadcasted_iota(jnp.int32, (tq, tk), 1)
        mask = ((picked > 0.5) & (kpos <= pos))[None]
        s = jnp.where(mask, s.reshape(g, tq, tk), NEG_BIG)
        m_prev = m_ref[...]
        m_new = jnp.maximum(m_prev, jnp.max(s, axis=-1, keepdims=True))
        alpha = jnp.exp(m_prev - m_new)
        p = jnp.where(mask, jnp.exp(s - m_new), 0.0)
        l_ref[...] = alpha * l_ref[...] + jnp.sum(p, axis=-1, keepdims=True)
        pv = _dot(p.reshape(g * tq, tk).astype(BF16), v_ref[pl.ds(k0, tk), :].astype(BF16))
        acc_ref[...] = alpha * acc_ref[...] + pv.reshape(g, tq, HEAD_DIM)
        m_ref[...] = m_new
        return carry

    lax.fori_loop(0, (qi * tq + tq - 1) // tk + 1, kv_step, 0)
    o = acc_ref[...] * (1.0 / jnp.maximum(l_ref[...], 1e-30))
    for i in range(g):
        o_ref[:, i * HEAD_DIM:(i + 1) * HEAD_DIM] = o[i]


def _selattn(proj, sel, tq=Q_BLOCK, tk=512):
    bsz, t, _ = proj.shape
    nsel = t // SEL_BLOCK
    tk = min(tk, t)
    gw = NSA_GROUP * HEAD_DIM

    def kv_spec(base):
        return pl.BlockSpec((None, t, HEAD_DIM), lambda b, h, i: (b, 0, base + h))

    return pl.pallas_call(
        functools.partial(_selattn_kernel, tq=tq, tk=tk, nsel=nsel),
        grid=(bsz, NSA_KV_HEADS, t // tq),
        in_specs=[
            pl.BlockSpec((None, tq, gw), lambda b, h, i: (b, i, COL_NSA_Q // NSA_GROUP + h)),
            kv_spec(COL_KV + 2 * NSA_KV_HEADS), kv_spec(COL_KV + 3 * NSA_KV_HEADS),
            pl.BlockSpec((None, None, tq, nsel), lambda b, h, i: (b, h, i, 0)),
        ],
        out_specs=pl.BlockSpec((None, tq, gw), lambda b, h, i: (b, i, h)),
        out_shape=jax.ShapeDtypeStruct((bsz, t, NSA_WIDTH), F32),
        scratch_shapes=[pltpu.VMEM((NSA_GROUP, tq, 1), F32), pltpu.VMEM((NSA_GROUP, tq, 1), F32),
                        pltpu.VMEM((NSA_GROUP, tq, HEAD_DIM), F32)],
        compiler_params=_cparams(("arbitrary", "arbitrary", "arbitrary")),
        name="selattn",
    )(proj, proj, proj, sel)


def _winattn_kernel(*refs, tq, nblk):
    q_ref = refs[0]
    k_refs = refs[1:1 + nblk]
    v_refs = refs[1 + nblk:1 + 2 * nblk]
    ocmp_ref, osel_ref, small_ref, o_ref = refs[1 + 2 * nblk:]
    g = NSA_GROUP
    hk = pl.program_id(1)
    qi = pl.program_id(2)
    scale = HEAD_DIM ** -0.5
    nk = nblk * tq
    keys = jnp.concatenate([r[...] for r in k_refs], axis=0).astype(BF16)
    vals = jnp.concatenate([r[...] for r in v_refs], axis=0).astype(BF16)
    q4 = jnp.concatenate([q_ref[:, i * HEAD_DIM:(i + 1) * HEAD_DIM] for i in range(g)], axis=0)
    s = (_dot_nt(q4.astype(BF16), keys) * scale).reshape(g, tq, nk)
    pos = qi * tq + lax.broadcasted_iota(jnp.int32, (tq, nk), 0)
    wpos = (qi - (nblk - 1)) * tq + lax.broadcasted_iota(jnp.int32, (tq, nk), 1)
    mask = ((wpos <= pos) & (wpos > pos - WINDOW) & (wpos >= 0))[None]
    s = jnp.where(mask, s, NEG_BIG)
    e = jnp.where(mask, jnp.exp(s - jnp.max(s, axis=-1, keepdims=True)), 0.0)
    p = e * (1.0 / jnp.maximum(jnp.sum(e, axis=-1, keepdims=True), 1e-30))
    o_win = _dot(p.reshape(g * tq, nk).astype(BF16), vals).reshape(g, tq, HEAD_DIM)
    small = small_ref[...]
    for i in range(g):
        head = hk * g + i
        gates = [jax.nn.sigmoid(_lane_pick(small, SMALL_NSA_GATE + j * NSA_HEADS + head)) for j in range(3)]
        cols = slice(i * HEAD_DIM, (i + 1) * HEAD_DIM)
        o = gates[0] * ocmp_ref[:, cols] + gates[1] * osel_ref[:, cols] + gates[2] * o_win[i]
        o_ref[:, cols] = o.astype(o_ref.dtype)


def _winattn(proj, o_cmp, o_sel, tq=Q_BLOCK):
    bsz, t, _ = proj.shape
    nblk = WINDOW // tq + 1
    gw = NSA_GROUP * HEAD_DIM

    def kv_spec(base, d):
        return pl.BlockSpec((None, tq, HEAD_DIM),
                            lambda b, h, i: (b, jnp.maximum(i - (nblk - 1) + d, 0), base + h))

    grp = pl.BlockSpec((None, tq, gw), lambda b, h, i: (b, i, h))
    return pl.pallas_call(
        functools.partial(_winattn_kernel, tq=tq, nblk=nblk),
        grid=(bsz, NSA_KV_HEADS, t // tq),
        in_specs=(
            [pl.BlockSpec((None, tq, gw), lambda b, h, i: (b, i, COL_NSA_Q // NSA_GROUP + h))]
            + [kv_spec(COL_KV + 4 * NSA_KV_HEADS, d) for d in range(nblk)]
            + [kv_spec(COL_KV + 5 * NSA_KV_HEADS, d) for d in range(nblk)]
            + [grp, grp, pl.BlockSpec((None, tq, HEAD_DIM), lambda b, h, i: (b, i, COL_SMALL))]
        ),
        out_specs=grp,
        out_shape=jax.ShapeDtypeStruct((bsz, t, NSA_WIDTH), BF16),
        compiler_params=_cparams(("arbitrary", "arbitrary", "arbitrary")),
        name="winattn",
    )(proj, *([proj] * (2 * nblk)), o_cmp, o_sel, proj)


def _ffn(h, wg, wu, wd):
    act = _silu(_dot(h, wg)) * _dot(h, wu)
    return _dot(act.astype(BF16), wd)


def _pack_bf16_pairs(x):
    half = x.shape[1] // 2
    lo = pltpu.bitcast(x[:, :half].astype(BF16).astype(F32), jnp.uint32)
    hi = pltpu.bitcast(x[:, half:].astype(BF16).astype(F32), jnp.uint32)
    return hi | (lo >> 16)


def _unpack_bf16_pairs(p):
    return pltpu.bitcast(p << 16, F32), pltpu.bitcast(p & jnp.uint32(0xFFFF0000), F32)


def _store_token_rows(ref, x):
    n, d = x.shape
    s_tiles = d // HEAD_DIM
    for s in range(s_tiles):
        ref[pl.ds(s, n, stride=s_tiles), :] = x[:, s * HEAD_DIM:(s + 1) * HEAD_DIM]


def _load_token_rows(ref, n, d):
    s_tiles = d // HEAD_DIM
    return jnp.concatenate([ref[pl.ds(s, n, stride=s_tiles), :] for s in range(s_tiles)], axis=1)


def _outproj_kernel(yg_ref, yn_ref, x_ref, wo_ref, gate_ref, shift_ref, scale_ref, gate_m_ref, nw_ref,
                    rw_ref, rb_ref, sg_ref, su_ref, sd_ref, x1_ref, h_ref, route_ref, count_ref, carry_ref):
    tm = x_ref.shape[0]
    half = yg_ref.shape[1]

    @pl.when((pl.program_id(0) == 0) & (pl.program_id(1) == 0))
    def _():
        carry_ref[...] = jnp.zeros_like(carry_ref)

    y = _dot(yg_ref[...], wo_ref[0:half, :]) + _dot(yn_ref[...], wo_ref[half:2 * half, :])
    x1 = x_ref[...] + gate_ref[...] * y
    h = _modulated_norm(x1, nw_ref[...], scale_ref[...], shift_ref[...])
    _store_token_rows(h_ref, _pack_bf16_pairs(h))
    x1_ref[...] = x1 + gate_m_ref[...] * _ffn(h.astype(BF16), sg_ref[...], su_ref[...], sd_ref[...])
    scores = jax.nn.sigmoid(_dot_split(h, rw_ref[...]))
    n_exp = scores.shape[1]
    lane = lax.broadcasted_iota(jnp.int32, scores.shape, 1)
    cand = scores + rb_ref[...]
    picked = jnp.zeros(scores.shape, jnp.bool_)
    firsts = []
    for _ in range(TOP_K):
        best = jnp.max(cand, axis=-1, keepdims=True)
        first = jnp.min(jnp.where(cand == best, lane, n_exp), axis=-1, keepdims=True)
        hit = lane == first
        picked = picked | hit
        cand = jnp.where(hit, NEG_BIG, cand)
        firsts.append(first)
    norm = ROUTED_SCALE / jnp.sum(jnp.where(picked, scores, 0.0), axis=-1, keepdims=True)
    onehot = jnp.where(picked, 1.0, 0.0)
    ri = lax.broadcasted_iota(jnp.int32, (tm, tm), 0)
    ci = lax.broadcasted_iota(jnp.int32, (tm, tm), 1)
    before = jnp.where(ci < ri, 1.0, 0.0).astype(BF16)
    pos = _dot(before, onehot.astype(BF16)) + carry_ref[...]
    carry_ref[...] += jnp.sum(onehot, axis=0, keepdims=True)
    count_ref[...] = carry_ref[...]
    slot = lax.broadcasted_iota(jnp.int32, (tm, 3 * TOP_K), 1)
    route = jnp.zeros((tm, 3 * TOP_K), F32)
    for j, first in enumerate(firsts):
        hit = lane == first
        route = jnp.where(slot == j, first.astype(F32), route)
        route = jnp.where(slot == TOP_K + j, _lane_pick(scores, first) * norm, route)
        route = jnp.where(slot == 2 * TOP_K + j, jnp.sum(jnp.where(hit, pos, 0.0), axis=-1, keepdims=True), route)
    route_ref[...] = route


def _outproj(y_gdn, y_nsa, x, w_out_b, mod4, norm_w, router_w, router_bias, sg, su, sd, tm=256):
    bsz, t, d = x.shape
    half = y_gdn.shape[2]
    n_exp = router_w.shape[1]
    sff = sg.shape[1]
    s_tiles = d // 2 // HEAD_DIM

    def mod(j):
        return pl.BlockSpec((None, None, 1, d), lambda b, i: (b, j, 0, 0))

    def const(shape):
        return pl.BlockSpec(shape, lambda b, i: (0, 0))

    row = pl.BlockSpec((None, tm, d), lambda b, i: (b, i, 0))
    return pl.pallas_call(
        _outproj_kernel,
        grid=(bsz, t // tm),
        in_specs=[
            pl.BlockSpec((None, tm, half), lambda b, i: (b, i, 0)),
            pl.BlockSpec((None, tm, half), lambda b, i: (b, i, 0)),
            row,
            const((2 * half, d)),
            mod(2), mod(3), mod(4), mod(5),
            const((1, d)), const((d, n_exp)), const((1, n_exp)),
            const((d, sff)), const((d, sff)), const((sff, d)),
        ],
        out_specs=[row,
                   pl.BlockSpec((tm * s_tiles, HEAD_DIM), lambda b, i: (b * (t // tm) + i, 0)),
                   pl.BlockSpec((None, tm, 3 * TOP_K), lambda b, i: (b, i, 0)),
                   const((1, n_exp))],
        out_shape=[
            jax.ShapeDtypeStruct((bsz, t, d), F32),
            jax.ShapeDtypeStruct((bsz * t * s_tiles, HEAD_DIM), jnp.uint32),
            jax.ShapeDtypeStruct((bsz, t, 3 * TOP_K), F32),
            jax.ShapeDtypeStruct((1, n_exp), F32),
        ],
        scratch_shapes=[pltpu.VMEM((1, n_exp), F32)],
        compiler_params=_cparams(("arbitrary", "arbitrary")),
        name="outproj",
    )(y_gdn, y_nsa, x, w_out_b, mod4, mod4, mod4, mod4, norm_w.reshape(1, d), router_w,
      router_bias.reshape(1, n_exp), sg, su, sd)


MOE_TM = 256
MOE_TOK = 128


def _token_row(ref, r, s_tiles):
    return ref.at[pl.ds(pl.multiple_of(r * s_tiles, s_tiles), s_tiles), :]


def _for_each(n, fn):
    def body(i, carry):
        fn(i)
        return carry

    lax.fori_loop(0, n, body, 0, unroll=8)


def _moe_scatter_kernel(gstart_ref, gend_ref, idx_ref, pos_ref, h_ref, xs_hbm, zero_ref, sem, *, s_tiles):
    n_exp = gstart_ref.shape[0]

    @pl.when(pl.program_id(0) == 0)
    def _():
        zero_ref[...] = jnp.zeros_like(zero_ref)

        def tail(e):
            first = pl.multiple_of((gend_ref[e] - MOE_TM) * s_tiles, MOE_TM * s_tiles)
            return pltpu.make_async_copy(zero_ref, xs_hbm.at[pl.ds(first, MOE_TM * s_tiles), :], sem)

        def start_tail(e):
            @pl.when(gend_ref[e] > gstart_ref[e])
            def _():
                tail(e).start()

        def wait_tail(e):
            @pl.when(gend_ref[e] > gstart_ref[e])
            def _():
                tail(e).wait()

        _for_each(n_exp, start_tail)
        _for_each(n_exp, wait_tail)

    def row_copy(n):
        row = gstart_ref[idx_ref[n]] + pos_ref[n]
        return pltpu.make_async_copy(_token_row(h_ref, n // TOP_K, s_tiles), _token_row(xs_hbm, row, s_tiles), sem)

    _for_each(MOE_TOK * TOP_K, lambda n: row_copy(n).start())
    _for_each(MOE_TOK * TOP_K, lambda n: row_copy(n).wait())


def _moe_scatter(h_rows, idx, pos, gstart, gend, n_rows, s_tiles):
    n_steps = idx.shape[0] // (MOE_TOK * TOP_K)
    smem = pl.BlockSpec((MOE_TOK * TOP_K,), lambda i, gs, ge: (i,), memory_space=pltpu.SMEM)
    return pl.pallas_call(
        functools.partial(_moe_scatter_kernel, s_tiles=s_tiles),
        grid_spec=pltpu.PrefetchScalarGridSpec(
            num_scalar_prefetch=2,
            grid=(n_steps,),
            in_specs=[smem, smem, pl.BlockSpec((MOE_TOK * s_tiles, HEAD_DIM), lambda i, gs, ge: (i, 0))],
            out_specs=pl.BlockSpec(memory_space=pl.ANY),
            scratch_shapes=[pltpu.VMEM((MOE_TM * s_tiles, HEAD_DIM), h_rows.dtype), pltpu.SemaphoreType.DMA(())],
        ),
        out_shape=jax.ShapeDtypeStruct((n_rows * s_tiles, HEAD_DIM), h_rows.dtype),
        compiler_params=_cparams(("arbitrary",)),
        name="moe_scatter",
    )(gstart, gend, idx, pos, h_rows)


def _moe_expert_kernel(texp_ref, nvalid_ref, xs_ref, wg_ref, wu_ref, wd_ref, ys_ref, wgb_ref, wub_ref, wdb_ref):
    i = pl.program_id(0)
    d = wg_ref.shape[0]
    valid = i < nvalid_ref[0]
    new_expert = (i == 0) | (texp_ref[i] != texp_ref[jnp.maximum(i - 1, 0)])

    @pl.when(valid & new_expert)
    def _():
        wgb_ref[...] = wg_ref[...].astype(BF16)
        wub_ref[...] = wu_ref[...].astype(BF16)
        wdb_ref[...] = wd_ref[...].astype(BF16)

    @pl.when(valid)
    def _():
        half = d // 2
        lo, hi = _unpack_bf16_pairs(_load_token_rows(xs_ref, MOE_TM, half))
        lo, hi = lo.astype(BF16), hi.astype(BF16)
        gate = _dot(lo, wgb_ref[0:half, :]) + _dot(hi, wgb_ref[half:d, :])
        up = _dot(lo, wub_ref[0:half, :]) + _dot(hi, wub_ref[half:d, :])
        y = _dot((_silu(gate) * up).astype(BF16), wdb_ref[...])
        _store_token_rows(ys_ref, _pack_bf16_pairs(y))

    @pl.when(jnp.logical_not(valid))
    def _():
        ys_ref[...] = jnp.zeros_like(ys_ref)


def _moe_experts(xs, tile_expert, nvalid, wg, wu, wd):
    n_exp, d, ff = wg.shape
    blk = xs.shape[0] // tile_expert.shape[0]

    def expert(shape):
        return pl.BlockSpec((None,) + shape, lambda i, te, nv: (te[i], 0, 0))

    return pl.pallas_call(
        _moe_expert_kernel,
        grid_spec=pltpu.PrefetchScalarGridSpec(
            num_scalar_prefetch=2,
            grid=(xs.shape[0] // blk,),
            in_specs=[
                pl.BlockSpec((blk, HEAD_DIM), lambda i, te, nv: (jnp.minimum(i, nv[0] - 1), 0)),
                expert((d, ff)), expert((d, ff)), expert((ff, d)),
            ],
            out_specs=pl.BlockSpec((blk, HEAD_DIM), lambda i, te, nv: (i, 0)),
            scratch_shapes=[pltpu.VMEM((d, ff), BF16), pltpu.VMEM((d, ff), BF16), pltpu.VMEM((ff, d), BF16)],
        ),
        out_shape=jax.ShapeDtypeStruct(xs.shape, xs.dtype),
        compiler_params=_cparams(("arbitrary",)),
        name="moe_experts",
    )(tile_expert, nvalid, xs, wg, wu, wd)


def _moe_combine_kernel(gstart_ref, idx_ref, pos_ref, w_ref, ys_hbm, x1_ref, gate_ref, nw_ref, o_ref,
                        buf_ref, lo_ref, hi_ref, sem, *, s_tiles):
    tm, d = x1_ref.shape

    def row_copy(n):
        row = gstart_ref[idx_ref[n]] + pos_ref[n]
        return pltpu.make_async_copy(_token_row(ys_hbm, row, s_tiles), _token_row(buf_ref, n, s_tiles), sem)

    _for_each(tm * TOP_K, lambda n: row_copy(n).start())
    _for_each(tm * TOP_K, lambda n: row_copy(n).wait())

    def token(t):
        n0 = t * TOP_K
        lo, hi = _unpack_bf16_pairs(_token_row(buf_ref, n0, s_tiles)[...])
        acc_lo, acc_hi = w_ref[n0] * lo, w_ref[n0] * hi
        for j in range(1, TOP_K):
            lo, hi = _unpack_bf16_pairs(_token_row(buf_ref, n0 + j, s_tiles)[...])
            acc_lo, acc_hi = acc_lo + w_ref[n0 + j] * lo, acc_hi + w_ref[n0 + j] * hi
        _token_row(lo_ref, t, s_tiles)[...] = acc_lo
        _token_row(hi_ref, t, s_tiles)[...] = acc_hi

    _for_each(tm, token)
    routed = jnp.concatenate([_load_token_rows(lo_ref, tm, d // 2), _load_token_rows(hi_ref, tm, d // 2)], axis=1)
    x2 = x1_ref[...] + gate_ref[...] * routed
    o_ref[...] = x2 * lax.rsqrt(jnp.mean(x2 * x2, axis=-1, keepdims=True) + EPS) * nw_ref[...]


def _moe_combine(ys, idx, pos, w, gstart, x1, mod4, norm_w):
    bsz, t, d = x1.shape
    s_tiles = d // 2 // HEAD_DIM
    tm = MOE_TOK
    nt = t // tm
    smem = pl.BlockSpec((tm * TOP_K,), lambda b, i, gs: (b * nt + i,), memory_space=pltpu.SMEM)
    row = pl.BlockSpec((None, tm, d), lambda b, i, gs: (b, i, 0))
    return pl.pallas_call(
        functools.partial(_moe_combine_kernel, s_tiles=s_tiles),
        grid_spec=pltpu.PrefetchScalarGridSpec(
            num_scalar_prefetch=1,
            grid=(bsz, nt),
            in_specs=[
                smem, smem, smem,
                pl.BlockSpec(memory_space=pl.ANY),
                row,
                pl.BlockSpec((None, None, 1, d), lambda b, i, gs: (b, 5, 0, 0)),
                pl.BlockSpec((1, d), lambda b, i, gs: (0, 0)),
            ],
            out_specs=row,
            scratch_shapes=[pltpu.VMEM((tm * TOP_K * s_tiles, HEAD_DIM), ys.dtype),
                            pltpu.VMEM((tm * s_tiles, HEAD_DIM), F32), pltpu.VMEM((tm * s_tiles, HEAD_DIM), F32),
                            pltpu.SemaphoreType.DMA(())],
        ),
        out_shape=jax.ShapeDtypeStruct((bsz, t, d), F32),
        compiler_params=_cparams(("arbitrary", "arbitrary")),
        name="moe_combine",
    )(gstart, idx, pos, w, ys, x1, mod4, norm_w.reshape(1, d))


def _moe(h_rows, route, counts, x1, mod4, norm_w, wg, wu, wd):
    bsz, t, d = x1.shape
    n_exp = wg.shape[0]
    s_tiles = h_rows.shape[0] // (bsz * t)
    idx = route[..., 0:TOP_K].astype(jnp.int32).reshape(-1)
    w = route[..., TOP_K:2 * TOP_K].reshape(-1)
    pos = route[..., 2 * TOP_K:3 * TOP_K].astype(jnp.int32).reshape(-1)
    padded = (counts.reshape(n_exp).astype(jnp.int32) + MOE_TM - 1) // MOE_TM * MOE_TM
    gend = jnp.cumsum(padded)
    gstart = gend - padded
    nvalid = gend[-1:] // MOE_TM
    n_tiles = bsz * t * TOP_K // MOE_TM + n_exp
    tile_start = jnp.minimum(jnp.arange(n_tiles, dtype=jnp.int32), nvalid - 1) * MOE_TM
    tile_expert = jnp.sum((gend[None, :] <= tile_start[:, None]).astype(jnp.int32), axis=1)
    xs = _moe_scatter(h_rows, idx, pos, gstart, gend, n_tiles * MOE_TM, s_tiles)
    ys = _moe_experts(xs, tile_expert, nvalid, wg, wu, wd)
    return _moe_combine(ys, idx, pos, w, gstart, x1, mod4, norm_w)


def _reorder_w_in(w_in):
    d = w_in.shape[0]
    o_beta = 3 * GDN_WIDTH
    o_a = o_beta + GDN_HEADS
    o_gate = o_a + GDN_HEADS
    o_q = o_gate + GDN_WIDTH
    o_kv = o_q + NSA_WIDTH
    o_ng = o_kv + 6 * NSA_KV_WIDTH
    n_small = 2 * GDN_HEADS + 3 * NSA_HEADS
    parts = [w_in[:, :o_beta], w_in[:, o_gate:o_q], w_in[:, o_q:o_kv], w_in[:, o_kv:o_ng],
             w_in[:, o_beta:o_gate], w_in[:, o_ng:o_ng + 3 * NSA_HEADS]]
    used = COL_SMALL * HEAD_DIM + n_small
    parts.append(jnp.zeros((d, PROJ_COLS - used), w_in.dtype))
    return jnp.concatenate(parts, axis=1).astype(BF16)


def _to_rows16(proj, col_block):
    bsz, t, _ = proj.shape
    x = proj[:, :, col_block * HEAD_DIM:(col_block + NSA_KV_HEADS) * HEAD_DIM]
    x = x.reshape(bsz, t // CMP_STRIDE, CMP_STRIDE, NSA_KV_HEADS, HEAD_DIM).transpose(0, 3, 1, 2, 4)
    return x.reshape(bsz, NSA_KV_HEADS, t // CMP_STRIDE, CMP_STRIDE * HEAD_DIM)


def _layer(x, mod, norm_attn_w, norm_ffn_w, w_in, gdn_conv_w, gdn_a_log, gdn_dt_bias, gdn_norm_w,
           cmp_pe_k, cmp_w1_k, cmp_w2_k, cmp_pe_v, cmp_w1_v, cmp_w2_v, w_out, router_w, router_bias,
           expert_w_gate, expert_w_up, expert_w_down, shared_w_gate, shared_w_up, shared_w_down, final_w):
    bsz, t, d = x.shape
    mod4 = mod.reshape(bsz, N_MOD, 1, d)
    proj = _proj(x, mod4, norm_attn_w, _reorder_w_in(w_in))
    y_gdn = _gdn(proj, gdn_conv_w, gdn_a_log, gdn_dt_bias, gdn_norm_w)
    kc = _compress(_to_rows16(proj, COL_KV), cmp_pe_k, cmp_w1_k, cmp_w2_k)
    vc = _compress(_to_rows16(proj, COL_KV + NSA_KV_HEADS), cmp_pe_v, cmp_w1_v, cmp_w2_v)
    o_cmp, sel = _cmpattn(proj, kc, vc)
    o_sel = _selattn(proj, sel)
    y_nsa = _winattn(proj, o_cmp, o_sel)
    x1, h_rows, route, counts = _outproj(
        y_gdn, y_nsa, x, w_out.astype(BF16), mod4, norm_ffn_w, router_w, router_bias,
        shared_w_gate.astype(BF16), shared_w_up.astype(BF16), shared_w_down.astype(BF16))
    return _moe(h_rows, route, counts, x1, mod4, final_w, expert_w_gate, expert_w_up, expert_w_down)


def kernel(x, c, w_ada, b_ada, norm_attn_w, norm_ffn_w, norm_final_w, w_in, gdn_conv_w, gdn_a_log, gdn_dt_bias,
           gdn_norm_w, cmp_pe_k, cmp_w1_k, cmp_w2_k, cmp_pe_v, cmp_w1_v, cmp_w2_v, w_out, router_w, router_bias,
           expert_w_gate, expert_w_up, expert_w_down, shared_w_gate, shared_w_up, shared_w_down):
    depth = w_ada.shape[0]
    assert depth == 1, "the fused final norm assumes a single layer"
    l = 0
    mod = _ada(c, w_ada[l], b_ada[l])
    return _layer(x, mod, norm_attn_w[l], norm_ffn_w[l], w_in[l], gdn_conv_w[l], gdn_a_log[l], gdn_dt_bias[l],
                  gdn_norm_w[l], cmp_pe_k[l], cmp_w1_k[l], cmp_w2_k[l], cmp_pe_v[l], cmp_w1_v[l], cmp_w2_v[l],
                  w_out[l], router_w[l], router_bias[l], expert_w_gate[l], expert_w_up[l], expert_w_down[l],
                  shared_w_gate[l], shared_w_up[l], shared_w_down[l], norm_final_w)
```

```python
import functools

import jax
import jax.numpy as jnp
from jax import lax
from jax.experimental import pallas as pl
from jax.experimental.pallas import tpu as pltpu

F32 = jnp.float32
BF16 = jnp.bfloat16
HI = lax.Precision.HIGHEST

HEAD_DIM = 128
GDN_HEADS = 8
NSA_HEADS = 8
NSA_KV_HEADS = 2
NSA_GROUP = NSA_HEADS // NSA_KV_HEADS
GDN_WIDTH = GDN_HEADS * HEAD_DIM
NSA_WIDTH = NSA_HEADS * HEAD_DIM
NSA_KV_WIDTH = NSA_KV_HEADS * HEAD_DIM
CONV_WIDTH = 4
GDN_CHUNK = 64
CMP_BLOCK = 32
CMP_STRIDE = 16
SEL_BLOCK = 64
SEL_TOPK = 16
N_LOCAL_BLOCKS = 2
WINDOW = 512
Q_BLOCK = 128
TOP_K = 8
ROUTED_SCALE = 2.5
N_MOD = 6
EPS = 1e-6

COL_GDN_Q = 0
COL_GDN_K = GDN_HEADS
COL_GDN_V = 2 * GDN_HEADS
COL_GDN_GATE = 3 * GDN_HEADS
COL_NSA_Q = 4 * GDN_HEADS
COL_KV = COL_NSA_Q + NSA_HEADS
COL_SMALL = COL_KV + 6 * NSA_KV_HEADS
N_COL_BLOCKS = COL_SMALL + 1
PROJ_TN = 2304
PROJ_COLS = 3 * PROJ_TN
SMALL_BETA = 0
SMALL_A = GDN_HEADS
SMALL_NSA_GATE = 2 * GDN_HEADS

NEG_BIG = -1e30
VMEM_LIMIT = 48 * 1024 * 1024


def _cparams(semantics):
    return pltpu.CompilerParams(dimension_semantics=semantics, vmem_limit_bytes=VMEM_LIMIT)


def _silu(x):
    return x * jax.nn.sigmoid(x)


def _dot(a, b, precision=None):
    return jnp.dot(a, b, preferred_element_type=F32, precision=precision)


def _dot_nt(a, b, precision=None):
    return lax.dot_general(a, b, (((1,), (1,)), ((), ())), preferred_element_type=F32, precision=precision)


def _dot_tn(a, b, precision=None):
    return lax.dot_general(a, b, (((0,), (0,)), ((), ())), preferred_element_type=F32, precision=precision)


def _split_bf16(x):
    hi = x.astype(BF16)
    return hi, (x - hi.astype(F32)).astype(BF16)


def _dot_split(a, b, nt=False):
    (ah, al), (bh, bl) = _split_bf16(a), _split_bf16(b)
    f = _dot_nt if nt else _dot
    return f(ah, bh) + (f(ah, bl) + f(al, bh))


def _lane_pick(x, idx):
    lane = lax.broadcasted_iota(jnp.int32, x.shape, 1)
    return jnp.sum(jnp.where(lane == idx, x, 0.0), axis=-1, keepdims=True)


def _ada_kernel(ct_ref, w_ref, b_ref, o_ref):
    c = ct_ref[...]
    cond = _silu(c)
    w = w_ref[...]
    rows = [jnp.sum(w * cond[:, b:b + 1], axis=0, keepdims=True) for b in range(c.shape[1])]
    o_ref[...] = jnp.concatenate(rows, axis=0) + b_ref[...]


def _ada(c, w_ada, b_ada, tn=512):
    bsz, d = c.shape
    n = w_ada.shape[1]
    return pl.pallas_call(
        _ada_kernel,
        grid=(n // tn,),
        in_specs=[
            pl.BlockSpec((d, bsz), lambda j: (0, 0)),
            pl.BlockSpec((d, tn), lambda j: (0, j)),
            pl.BlockSpec((1, tn), lambda j: (0, j)),
        ],
        out_specs=pl.BlockSpec((bsz, tn), lambda j: (0, j)),
        out_shape=jax.ShapeDtypeStruct((bsz, n), F32),
        compiler_params=_cparams(("arbitrary",)),
        name="ada",
    )(c.T, w_ada, b_ada.reshape(1, n))


def _modulated_norm(x, nw, scale, shift):
    var = jnp.mean(x * x, axis=-1, keepdims=True)
    h = x * lax.rsqrt(var + EPS) * nw
    return h * (1.0 + scale) + shift


def _proj_kernel(x_ref, shift_ref, scale_ref, nw_ref, w_ref, o_ref):
    h = _modulated_norm(x_ref[...], nw_ref[...], scale_ref[...], shift_ref[...])
    o_ref[...] = _dot(h.astype(BF16), w_ref[...])


def _proj(x, mod4, norm_w, w_r, tm=256):
    bsz, t, d = x.shape
    n = w_r.shape[1]
    return pl.pallas_call(
        _proj_kernel,
        grid=(n // PROJ_TN, bsz, t // tm),
        in_specs=[
            pl.BlockSpec((None, tm, d), lambda j, b, i: (b, i, 0)),
            pl.BlockSpec((None, None, 1, d), lambda j, b, i: (b, 0, 0, 0)),
            pl.BlockSpec((None, None, 1, d), lambda j, b, i: (b, 1, 0, 0)),
            pl.BlockSpec((1, d), lambda j, b, i: (0, 0)),
            pl.BlockSpec((d, PROJ_TN), lambda j, b, i: (0, j)),
        ],
        out_specs=pl.BlockSpec((None, tm, PROJ_TN), lambda j, b, i: (b, i, j)),
        out_shape=jax.ShapeDtypeStruct((bsz, t, n), F32),
        compiler_params=_cparams(("arbitrary", "arbitrary", "arbitrary")),
        name="proj",
    )(x, mod4, mod4, norm_w.reshape(1, d), w_r)


def _gdn_kernel(q_ref, k_ref, v_ref, gate_ref, small_ref, hq_ref, hk_ref, hv_ref,
                cwq_ref, cwk_ref, cwv_ref, alog_ref, dtb_ref, nw_ref, o_ref, s_ref, xc_ref, *, tt, hg):
    first = pl.program_id(2) == 0

    @pl.when(first)
    def _():
        s_ref[...] = jnp.zeros_like(s_ref)

    def conv_silu(x_ref, halo_ref, cw_ref):
        xc_ref[0:8, :] = jnp.where(first, 0.0, halo_ref[...])
        xc_ref[8:tt + 8, :] = x_ref[...]
        cw = cw_ref[...]
        off = 8 - (CONV_WIDTH - 1)
        y = cw[0:1, :] * xc_ref[off:off + tt, :]
        for i in range(1, CONV_WIDTH):
            y = y + cw[i:i + 1, :] * xc_ref[off + i:off + i + tt, :]
        return _silu(y)

    q_all = conv_silu(q_ref, hq_ref, cwq_ref)
    k_all = conv_silu(k_ref, hk_ref, cwk_ref)
    v_all = conv_silu(v_ref, hv_ref, cwv_ref)
    ri = lax.broadcasted_iota(jnp.int32, (tt, tt), 0)
    ci = lax.broadcasted_iota(jnp.int32, (tt, tt), 1)
    lower = jnp.where(((ri // GDN_CHUNK) == (ci // GDN_CHUNK)) & (ci <= ri), 1.0, 0.0).astype(BF16)
    for hh in range(hg):
        cols = slice(hh * HEAD_DIM, (hh + 1) * HEAD_DIM)
        o = _gdn_head(q_all[:, cols], k_all[:, cols], v_all[:, cols], small_ref[...], lower,
                      pl.program_id(1) * hg + hh, alog_ref[...], dtb_ref[...], s_ref.at[hh], tt)
        o = o * lax.rsqrt(jnp.mean(o * o, axis=-1, keepdims=True) + EPS) * nw_ref[...]
        o_ref[:, cols] = (o * _silu(gate_ref[:, cols])).astype(o_ref.dtype)


def _gdn_head(q, k, v, small, lower, h, alog, dtb, s_ref, tt):
    c = GDN_CHUNK
    nch = tt // c
    q = q * lax.rsqrt(jnp.sum(q * q, axis=-1, keepdims=True) + EPS) * (HEAD_DIM ** -0.5)
    k = k * lax.rsqrt(jnp.sum(k * k, axis=-1, keepdims=True) + EPS)

    beta = jax.nn.sigmoid(_lane_pick(small, SMALL_BETA + h))
    z = _lane_pick(small, SMALL_A + h) + _lane_pick(dtb, h)
    softplus = jnp.maximum(z, 0.0) + jnp.log(1.0 + jnp.exp(-jnp.abs(z)))
    g = -jnp.exp(_lane_pick(alog, h)) * softplus

    kpos = lax.broadcasted_iota(jnp.int32, (tt, c), 0) % c
    jj = lax.broadcasted_iota(jnp.int32, (tt, c), 1)
    g_c = jnp.broadcast_to(g, (tt, c))
    rhs = jnp.concatenate([g_c, jnp.where(kpos > jj, g_c, 0.0)], axis=1)
    r_hi = rhs.astype(BF16)
    rem = rhs - r_hi.astype(F32)
    r_mid = rem.astype(BF16)
    r_lo = (rem - r_mid.astype(F32)).astype(BF16)
    sums = _dot(lower, r_hi) + (_dot(lower, r_mid) + _dot(lower, r_lo))
    gcum = jnp.broadcast_to(sums[:, 0:1], (tt, HEAD_DIM))
    gc3 = gcum.reshape(nch, c, HEAD_DIM)
    grest = (gc3[:, c - 1:c, :] - gc3).reshape(tt, HEAD_DIM)
    decay = jnp.where(kpos >= jj, jnp.exp(sums[:, c:2 * c]), 0.0).reshape(nch, c, c)
    strict = (kpos > jj).reshape(nch, c, c)
    eye = jnp.where(kpos == jj, 1.0, 0.0).reshape(nch, c, c)

    def bmm(a, b):
        return jnp.einsum('nij,njk->nik', a, b, preferred_element_type=F32)

    def bmm_nt(a, b):
        return jnp.einsum('nid,njd->nij', a, b, preferred_element_type=F32)

    def bmm_split(a, b):
        (ah, al), (bh, bl) = _split_bf16(a), _split_bf16(b)
        return bmm(ah, bh) + (bmm(ah, bl) + bmm(al, bh))

    k3 = k.reshape(nch, c, HEAD_DIM).astype(BF16)
    kb = k * beta
    a = jnp.where(strict, bmm_nt(kb.reshape(nch, c, HEAD_DIM).astype(BF16), k3) * decay, 0.0)
    inv = eye - a
    p = bmm_split(a, a)
    n_sq = (c - 1).bit_length() - 1
    for lvl in range(n_sq):
        inv = inv + bmm_split(inv, p)
        if lvl + 1 < n_sq:
            p = bmm_split(p, p)
    egc = jnp.exp(gcum)
    inv_b = inv.astype(BF16)
    u3 = bmm(inv_b, (v * beta).reshape(nch, c, HEAD_DIM).astype(BF16)).astype(BF16)
    w3 = bmm(inv_b, (kb * egc).reshape(nch, c, HEAD_DIM).astype(BF16)).astype(BF16)
    qk3 = (bmm_nt(q.reshape(nch, c, HEAD_DIM).astype(BF16), k3) * decay).astype(BF16)
    kd3 = (k * jnp.exp(grest)).reshape(nch, c, HEAD_DIM).astype(BF16)
    gl3 = egc.reshape(nch, c, HEAD_DIM)
    qe3 = ((q * egc).reshape(nch, c, HEAD_DIM) - bmm(qk3, w3)).astype(BF16)
    o03 = bmm(qk3, u3)
    kw = [_dot_tn(kd3[n], w3[n]).astype(BF16) for n in range(nch)]
    ku = [_dot_tn(kd3[n], u3[n]) for n in range(nch)]

    state = s_ref[...]
    outs = []
    for n in range(nch):
        state_b = state.astype(BF16)
        outs.append(_dot(qe3[n], state_b) + o03[n])
        state = state * gl3[n, c - 1:c, :] - _dot(kw[n], state_b) + ku[n]
    s_ref[...] = state
    return jnp.concatenate(outs, axis=0)


def _gdn(proj, conv_w, a_log, dt_bias, norm_w, tt=256, hg=4):
    bsz, t, _ = proj.shape
    wdt = hg * HEAD_DIM

    def col(base):
        return pl.BlockSpec((None, tt, wdt), lambda b, h, i: (b, i, base // hg + h))

    def halo(base):
        return pl.BlockSpec((None, 8, wdt), lambda b, h, i: (b, jnp.maximum(i * (tt // 8) - 1, 0), base // hg + h))

    def cw(base):
        return pl.BlockSpec((CONV_WIDTH, wdt), lambda b, h, i: (0, base // hg + h))

    heads = pl.BlockSpec((1, GDN_HEADS), lambda b, h, i: (0, 0))
    return pl.pallas_call(
        functools.partial(_gdn_kernel, tt=tt, hg=hg),
        grid=(bsz, GDN_HEADS // hg, t // tt),
        in_specs=[
            col(COL_GDN_Q), col(COL_GDN_K), col(COL_GDN_V), col(COL_GDN_GATE),
            pl.BlockSpec((None, tt, HEAD_DIM), lambda b, h, i: (b, i, COL_SMALL)),
            halo(COL_GDN_Q), halo(COL_GDN_K), halo(COL_GDN_V),
            cw(COL_GDN_Q), cw(COL_GDN_K), cw(COL_GDN_V),
            heads, heads,
            pl.BlockSpec((1, HEAD_DIM), lambda b, h, i: (0, 0)),
        ],
        out_specs=pl.BlockSpec((None, tt, wdt), lambda b, h, i: (b, i, h)),
        out_shape=jax.ShapeDtypeStruct((bsz, t, GDN_WIDTH), BF16),
        scratch_shapes=[pltpu.VMEM((hg, HEAD_DIM, HEAD_DIM), F32), pltpu.VMEM((tt + 8, wdt), F32)],
        compiler_params=_cparams(("arbitrary", "arbitrary", "arbitrary")),
        name="gdn",
    )(proj, proj, proj, proj, proj, proj, proj, proj, conv_w, conv_w, conv_w,
      a_log.reshape(1, GDN_HEADS), dt_bias.reshape(1, GDN_HEADS), norm_w.reshape(1, HEAD_DIM))


def _compress_kernel(x_ref, pe_ref, w1_ref, w2_ref, o_ref):
    r = x_ref.shape[0] // CMP_STRIDE
    d = x_ref.shape[1]
    a = b = None
    for l in range(CMP_STRIDE):
        x_l = x_ref[pl.ds(l, r, stride=CMP_STRIDE), :]
        a_l = _dot_split(x_l + pe_ref[l:l + 1, :], w1_ref[l * d:(l + 1) * d, :])
        b_l = _dot_split(x_l + pe_ref[CMP_STRIDE + l:CMP_STRIDE + l + 1, :],
                         w1_ref[(CMP_STRIDE + l) * d:(CMP_STRIDE + l + 1) * d, :])
        a, b = (a_l, b_l) if a is None else (a + a_l, b + b_l)
    hid = _silu(a + pltpu.roll(b, r - 1, 0))
    o_ref[...] = _dot_split(hid, w2_ref[...])


def _compress(proj, col_block, pe, w1, w2):
    bsz, t, _ = proj.shape
    r = t // CMP_STRIDE
    hid = w1.shape[1]
    return pl.pallas_call(
        _compress_kernel,
        grid=(bsz, NSA_KV_HEADS),
        in_specs=[
            pl.BlockSpec((None, t, HEAD_DIM), lambda b, h: (b, 0, col_block + h)),
            pl.BlockSpec((CMP_BLOCK, HEAD_DIM), lambda b, h: (0, 0)),
            pl.BlockSpec((CMP_BLOCK * HEAD_DIM, hid), lambda b, h: (0, 0)),
            pl.BlockSpec((hid, HEAD_DIM), lambda b, h: (0, 0)),
        ],
        out_specs=pl.BlockSpec((None, None, r, HEAD_DIM), lambda b, h: (b, h, 0, 0)),
        out_shape=jax.ShapeDtypeStruct((bsz, NSA_KV_HEADS, r, HEAD_DIM), F32),
        compiler_params=_cparams(("arbitrary", "arbitrary")),
        name="compress",
    )(proj, pe, w1, w2)


def _cmpattn_kernel(q_ref, kc_ref, vc_ref, o_ref, sel_ref, *, tq, nc, nsel):
    ncp = kc_ref.shape[0]
    scale = HEAD_DIM ** -0.5
    t0 = pl.program_id(2) * tq
    kc = kc_ref[...]
    vc = vc_ref[...]
    pos_r = t0 + lax.broadcasted_iota(jnp.int32, (tq, ncp), 0)
    c_r = lax.broadcasted_iota(jnp.int32, (tq, ncp), 1)
    valid_r = (c_r * CMP_STRIDE + CMP_BLOCK - 1 <= pos_r) & (c_r < nc)
    pos_c = t0 + lax.broadcasted_iota(jnp.int32, (ncp, tq), 1)
    c_c = lax.broadcasted_iota(jnp.int32, (ncp, tq), 0)
    valid_c = (c_c * CMP_STRIDE + CMP_BLOCK - 1 <= pos_c) & (c_c < nc)
    psum = jnp.zeros((ncp, tq), F32)
    kc_hi, kc_lo = _split_bf16(kc)
    vc_b = vc.astype(BF16)
    for g in range(NSA_GROUP):
        q_hi, q_lo = _split_bf16(q_ref[:, g * HEAD_DIM:(g + 1) * HEAD_DIM])
        s = _dot_nt(q_hi, kc_hi) + (_dot_nt(q_hi, kc_lo) + _dot_nt(q_lo, kc_hi))
        s = jnp.where(valid_r, s * scale, NEG_BIG)
        e = jnp.where(valid_r, jnp.exp(s - jnp.max(s, axis=-1, keepdims=True)), 0.0)
        p = e * (1.0 / jnp.maximum(jnp.sum(e, axis=-1, keepdims=True), 1e-30))
        o_ref[:, g * HEAD_DIM:(g + 1) * HEAD_DIM] = _dot(p.astype(BF16), vc_b)
        st = _dot_nt(kc_hi, q_hi) + (_dot_nt(kc_lo, q_hi) + _dot_nt(kc_hi, q_lo))
        st = jnp.where(valid_c, st * scale, NEG_BIG)
        et = jnp.where(valid_c, jnp.exp(st - jnp.max(st, axis=0, keepdims=True)), 0.0)
        psum = psum + et * (1.0 / jnp.maximum(jnp.sum(et, axis=0, keepdims=True), 1e-30))
    j_o = lax.broadcasted_iota(jnp.int32, (nsel, ncp), 0) * SEL_BLOCK
    c_o = lax.broadcasted_iota(jnp.int32, (nsel, ncp), 1) * CMP_STRIDE
    overlap = jnp.where((c_o < j_o + SEL_BLOCK) & (c_o + CMP_BLOCK > j_o), 1.0, 0.0).astype(BF16)
    p_hi, p_lo = _split_bf16(psum)
    imp = _dot(overlap, p_hi) + _dot(overlap, p_lo)
    blk = lax.broadcasted_iota(jnp.int32, (nsel, tq), 0)
    dist = (t0 + lax.broadcasted_iota(jnp.int32, (nsel, tq), 1)) // SEL_BLOCK - blk
    forced = (blk == 0) | ((dist >= 0) & (dist < N_LOCAL_BLOCKS))
    val = jnp.where(forced, -NEG_BIG, jnp.where(dist >= 0, imp, -1.0))
    rank = jnp.zeros((nsel, tq), F32)
    for j in range(nsel):
        row = val[j:j + 1, :]
        rank = rank + jnp.where((row > val) | ((row == val) & (blk > j)), 1.0, 0.0)
    sel = jnp.where((rank < min(SEL_TOPK, nsel)) & (dist >= 0), 1.0, 0.0)
    sel_ref[...] = sel.T


def _cmpattn(proj, kc, vc, tq=256):
    bsz, t, _ = proj.shape
    ncp = kc.shape[2]
    nc = (t - CMP_BLOCK) // CMP_STRIDE + 1
    nsel = t // SEL_BLOCK
    gw = NSA_GROUP * HEAD_DIM
    return pl.pallas_call(
        functools.partial(_cmpattn_kernel, tq=tq, nc=nc, nsel=nsel),
        grid=(bsz, NSA_KV_HEADS, t // tq),
        in_specs=[
            pl.BlockSpec((None, tq, gw), lambda b, h, i: (b, i, COL_NSA_Q // NSA_GROUP + h)),
            pl.BlockSpec((None, None, ncp, HEAD_DIM), lambda b, h, i: (b, h, 0, 0)),
            pl.BlockSpec((None, None, ncp, HEAD_DIM), lambda b, h, i: (b, h, 0, 0)),
        ],
        out_specs=[
            pl.BlockSpec((None, tq, gw), lambda b, h, i: (b, i, h)),
            pl.BlockSpec((None, None, tq, nsel), lambda b, h, i: (b, h, i, 0)),
        ],
        out_shape=[
            jax.ShapeDtypeStruct((bsz, t, NSA_WIDTH), F32),
            jax.ShapeDtypeStruct((bsz, NSA_KV_HEADS, t, nsel), F32),
        ],
        compiler_params=_cparams(("arbitrary", "arbitrary", "arbitrary")),
        name="cmpattn",
    )(proj, kc, vc)


def _selattn_kernel(q_ref, k_ref, v_ref, sel_ref, o_ref, m_ref, acc_ref, *, tq, tk, nsel):
    g = NSA_GROUP
    qi = pl.program_id(2)
    m_ref[...] = jnp.full_like(m_ref, NEG_BIG)
    acc_ref[...] = jnp.zeros_like(acc_ref)
    q4 = jnp.concatenate([q_ref[:, i * HEAD_DIM:(i + 1) * HEAD_DIM] for i in range(g)], axis=0)
    q4 = (q4 * HEAD_DIM ** -0.5).astype(BF16)
    sel = sel_ref[...].astype(BF16)
    pos = qi * tq + lax.broadcasted_iota(jnp.int32, (tq, tk), 0)
    ones = jnp.ones((tk, HEAD_DIM), BF16)

    def kv_step(kv, carry):
        k0 = pl.multiple_of(kv * tk, tk)
        s = _dot_nt(q4, k_ref[pl.ds(k0, tk), :].astype(BF16))
        key = k0 + lax.broadcasted_iota(jnp.int32, (nsel, tk), 1)
        blk = lax.broadcasted_iota(jnp.int32, (nsel, tk), 0)
        expand = jnp.where(key // SEL_BLOCK == blk, 1.0, 0.0).astype(BF16)
        picked = _dot(sel, expand)
        kpos = k0 + lax.broadcasted_iota(jnp.int32, (tq, tk), 1)
        mask = ((picked > 0.5) & (kpos <= pos))[None]
        s = jnp.where(mask, s.reshape(g, tq, tk), NEG_BIG)
        m_prev = m_ref[...]
        m_new = jnp.maximum(m_prev, jnp.max(s, axis=-1, keepdims=True))
        p = jnp.exp(s - m_new).astype(BF16)
        v_ext = jnp.concatenate([v_ref[pl.ds(k0, tk), :].astype(BF16), ones], axis=1)
        pv = _dot(p.reshape(g * tq, tk), v_ext)
        acc_ref[...] = jnp.exp(m_prev - m_new) * acc_ref[...] + pv.reshape(g, tq, 2 * HEAD_DIM)
        m_ref[...] = m_new
        return carry

    lax.fori_loop(0, (qi * tq + tq - 1) // tk + 1, kv_step, 0)
    acc = acc_ref[...]
    o = acc[:, :, 0:HEAD_DIM] * (1.0 / jnp.maximum(acc[:, :, HEAD_DIM:2 * HEAD_DIM], 1e-30))
    for i in range(g):
        o_ref[:, i * HEAD_DIM:(i + 1) * HEAD_DIM] = o[i]


def _selattn(proj, sel, tq=Q_BLOCK, tk=512):
    bsz, t, _ = proj.shape
    nsel = t // SEL_BLOCK
    tk = min(tk, t)
    gw = NSA_GROUP * HEAD_DIM

    def kv_spec(base):
        return pl.BlockSpec((None, t, HEAD_DIM), lambda b, h, i: (b, 0, base + h))

    return pl.pallas_call(
        functools.partial(_selattn_kernel, tq=tq, tk=tk, nsel=nsel),
        grid=(bsz, NSA_KV_HEADS, t // tq),
        in_specs=[
            pl.BlockSpec((None, tq, gw), lambda b, h, i: (b, i, COL_NSA_Q // NSA_GROUP + h)),
            kv_spec(COL_KV + 2 * NSA_KV_HEADS), kv_spec(COL_KV + 3 * NSA_KV_HEADS),
            pl.BlockSpec((None, None, tq, nsel), lambda b, h, i: (b, h, i, 0)),
        ],
        out_specs=pl.BlockSpec((None, tq, gw), lambda b, h, i: (b, i, h)),
        out_shape=jax.ShapeDtypeStruct((bsz, t, NSA_WIDTH), F32),
        scratch_shapes=[pltpu.VMEM((NSA_GROUP, tq, 1), F32), pltpu.VMEM((NSA_GROUP, tq, 2 * HEAD_DIM), F32)],
        compiler_params=_cparams(("arbitrary", "arbitrary", "arbitrary")),
        name="selattn",
    )(proj, proj, proj, sel)


def _winattn_kernel(*refs, tq, nblk):
    q_ref = refs[0]
    k_refs = refs[1:1 + nblk]
    v_refs = refs[1 + nblk:1 + 2 * nblk]
    ocmp_ref, osel_ref, small_ref, o_ref = refs[1 + 2 * nblk:]
    g = NSA_GROUP
    hk = pl.program_id(1)
    qi = pl.program_id(2)
    scale = HEAD_DIM ** -0.5
    nk = nblk * tq
    keys = jnp.concatenate([r[...] for r in k_refs], axis=0).astype(BF16)
    vals = jnp.concatenate([r[...] for r in v_refs], axis=0).astype(BF16)
    q4 = jnp.concatenate([q_ref[:, i * HEAD_DIM:(i + 1) * HEAD_DIM] for i in range(g)], axis=0)
    s = (_dot_nt(q4.astype(BF16), keys) * scale).reshape(g, tq, nk)
    pos = qi * tq + lax.broadcasted_iota(jnp.int32, (tq, nk), 0)
    wpos = (qi - (nblk - 1)) * tq + lax.broadcasted_iota(jnp.int32, (tq, nk), 1)
    mask = ((wpos <= pos) & (wpos > pos - WINDOW) & (wpos >= 0))[None]
    s = jnp.where(mask, s, NEG_BIG)
    e = jnp.exp(s - jnp.max(s, axis=-1, keepdims=True)).astype(BF16)
    ov = _dot(e.reshape(g * tq, nk), jnp.concatenate([vals, jnp.ones_like(vals)], axis=1))
    o_win = (ov[:, 0:HEAD_DIM] * (1.0 / jnp.maximum(ov[:, HEAD_DIM:2 * HEAD_DIM], 1e-30))).reshape(g, tq, HEAD_DIM)
    small = small_ref[...]
    for i in range(g):
        head = hk * g + i
        gates = [jax.nn.sigmoid(_lane_pick(small, SMALL_NSA_GATE + j * NSA_HEADS + head)) for j in range(3)]
        cols = slice(i * HEAD_DIM, (i + 1) * HEAD_DIM)
        o = gates[0] * ocmp_ref[:, cols] + gates[1] * osel_ref[:, cols] + gates[2] * o_win[i]
        o_ref[:, cols] = o.astype(o_ref.dtype)


def _winattn(proj, o_cmp, o_sel, tq=Q_BLOCK):
    bsz, t, _ = proj.shape
    nblk = WINDOW // tq + 1
    gw = NSA_GROUP * HEAD_DIM

    def kv_spec(base, d):
        return pl.BlockSpec((None, tq, HEAD_DIM),
                            lambda b, h, i: (b, jnp.maximum(i - (nblk - 1) + d, 0), base + h))

    grp = pl.BlockSpec((None, tq, gw), lambda b, h, i: (b, i, h))
    return pl.pallas_call(
        functools.partial(_winattn_kernel, tq=tq, nblk=nblk),
        grid=(bsz, NSA_KV_HEADS, t // tq),
        in_specs=(
            [pl.BlockSpec((None, tq, gw), lambda b, h, i: (b, i, COL_NSA_Q // NSA_GROUP + h))]
            + [kv_spec(COL_KV + 4 * NSA_KV_HEADS, d) for d in range(nblk)]
            + [kv_spec(COL_KV + 5 * NSA_KV_HEADS, d) for d in range(nblk)]
            + [grp, grp, pl.BlockSpec((None, tq, HEAD_DIM), lambda b, h, i: (b, i, COL_SMALL))]
        ),
        out_specs=grp,
        out_shape=jax.ShapeDtypeStruct((bsz, t, NSA_WIDTH), BF16),
        compiler_params=_cparams(("arbitrary", "arbitrary", "arbitrary")),
        name="winattn",
    )(proj, *([proj] * (2 * nblk)), o_cmp, o_sel, proj)


def _ffn(h, wg, wu, wd):
    act = _silu(_dot(h, wg)) * _dot(h, wu)
    return _dot(act.astype(BF16), wd)


def _pack_bf16_pairs(x):
    half = x.shape[1] // 2
    lo = pltpu.bitcast(x[:, :half].astype(BF16).astype(F32), jnp.uint32)
    hi = pltpu.bitcast(x[:, half:].astype(BF16).astype(F32), jnp.uint32)
    return hi | (lo >> 16)


def _unpack_bf16_pairs(p):
    return pltpu.bitcast(p << 16, F32), pltpu.bitcast(p & jnp.uint32(0xFFFF0000), F32)


def _store_token_rows(ref, x):
    n, d = x.shape
    s_tiles = d // HEAD_DIM
    for s in range(s_tiles):
        ref[pl.ds(s, n, stride=s_tiles), :] = x[:, s * HEAD_DIM:(s + 1) * HEAD_DIM]


def _load_token_rows(ref, n, d):
    s_tiles = d // HEAD_DIM
    return jnp.concatenate([ref[pl.ds(s, n, stride=s_tiles), :] for s in range(s_tiles)], axis=1)


def _outproj_kernel(yg_ref, yn_ref, x_ref, wo_ref, gate_ref, shift_ref, scale_ref, gate_m_ref, nw_ref,
                    rw_ref, rb_ref, sg_ref, su_ref, sd_ref, x1_ref, h_ref, route_ref, count_ref, carry_ref):
    tm = x_ref.shape[0]
    half = yg_ref.shape[1]

    @pl.when((pl.program_id(0) == 0) & (pl.program_id(1) == 0))
    def _():
        carry_ref[...] = jnp.zeros_like(carry_ref)

    y = _dot(yg_ref[...], wo_ref[0:half, :]) + _dot(yn_ref[...], wo_ref[half:2 * half, :])
    x1 = x_ref[...] + gate_ref[...] * y
    h = _modulated_norm(x1, nw_ref[...], scale_ref[...], shift_ref[...])
    _store_token_rows(h_ref, _pack_bf16_pairs(h))
    x1_ref[...] = x1 + gate_m_ref[...] * _ffn(h.astype(BF16), sg_ref[...], su_ref[...], sd_ref[...])
    scores = jax.nn.sigmoid(_dot_split(h, rw_ref[...]))
    n_exp = scores.shape[1]
    lane = lax.broadcasted_iota(jnp.int32, scores.shape, 1)
    cand = scores + rb_ref[...]
    picked = jnp.zeros(scores.shape, jnp.bool_)
    firsts = []
    for _ in range(TOP_K):
        best = jnp.max(cand, axis=-1, keepdims=True)
        first = jnp.min(jnp.where(cand == best, lane, n_exp), axis=-1, keepdims=True)
        hit = lane == first
        picked = picked | hit
        cand = jnp.where(hit, NEG_BIG, cand)
        firsts.append(first)
    norm = ROUTED_SCALE / jnp.sum(jnp.where(picked, scores, 0.0), axis=-1, keepdims=True)
    onehot = jnp.where(picked, 1.0, 0.0)
    ri = lax.broadcasted_iota(jnp.int32, (tm, tm), 0)
    ci = lax.broadcasted_iota(jnp.int32, (tm, tm), 1)
    before = jnp.where(ci < ri, 1.0, 0.0).astype(BF16)
    pos = _dot(before, onehot.astype(BF16)) + carry_ref[...]
    carry_ref[...] += jnp.sum(onehot, axis=0, keepdims=True)
    count_ref[...] = carry_ref[...]
    slot = lax.broadcasted_iota(jnp.int32, (tm, 3 * TOP_K), 1)
    route = jnp.zeros((tm, 3 * TOP_K), F32)
    for j, first in enumerate(firsts):
        hit = lane == first
        route = jnp.where(slot == j, first.astype(F32), route)
        route = jnp.where(slot == TOP_K + j, _lane_pick(scores, first) * norm, route)
        route = jnp.where(slot == 2 * TOP_K + j, jnp.sum(jnp.where(hit, pos, 0.0), axis=-1, keepdims=True), route)
    route_ref[...] = route


def _outproj(y_gdn, y_nsa, x, w_out_b, mod4, norm_w, router_w, router_bias, sg, su, sd, tm=256):
    bsz, t, d = x.shape
    half = y_gdn.shape[2]
    n_exp = router_w.shape[1]
    sff = sg.shape[1]
    s_tiles = d // 2 // HEAD_DIM

    def mod(j):
        return pl.BlockSpec((None, None, 1, d), lambda b, i: (b, j, 0, 0))

    def const(shape):
        return pl.BlockSpec(shape, lambda b, i: (0, 0))

    row = pl.BlockSpec((None, tm, d), lambda b, i: (b, i, 0))
    return pl.pallas_call(
        _outproj_kernel,
        grid=(bsz, t // tm),
        in_specs=[
            pl.BlockSpec((None, tm, half), lambda b, i: (b, i, 0)),
            pl.BlockSpec((None, tm, half), lambda b, i: (b, i, 0)),
            row,
            const((2 * half, d)),
            mod(2), mod(3), mod(4), mod(5),
            const((1, d)), const((d, n_exp)), const((1, n_exp)),
            const((d, sff)), const((d, sff)), const((sff, d)),
        ],
        out_specs=[row,
                   pl.BlockSpec((tm * s_tiles, HEAD_DIM), lambda b, i: (b * (t // tm) + i, 0)),
                   pl.BlockSpec((None, tm, 3 * TOP_K), lambda b, i: (b, i, 0)),
                   const((1, n_exp))],
        out_shape=[
            jax.ShapeDtypeStruct((bsz, t, d), F32),
            jax.ShapeDtypeStruct((bsz * t * s_tiles, HEAD_DIM), jnp.uint32),
            jax.ShapeDtypeStruct((bsz, t, 3 * TOP_K), F32),
            jax.ShapeDtypeStruct((1, n_exp), F32),
        ],
        scratch_shapes=[pltpu.VMEM((1, n_exp), F32)],
        compiler_params=_cparams(("arbitrary", "arbitrary")),
        name="outproj",
    )(y_gdn, y_nsa, x, w_out_b, mod4, mod4, mod4, mod4, norm_w.reshape(1, d), router_w,
      router_bias.reshape(1, n_exp), sg, su, sd)


MOE_TM = 256
MOE_TOK = 128


def _token_row(ref, r, s_tiles):
    return ref.at[pl.ds(pl.multiple_of(r * s_tiles, s_tiles), s_tiles), :]


def _for_each(n, fn, unroll=8):
    def body(i, carry):
        fn(i)
        return carry

    lax.fori_loop(0, n, body, 0, unroll=unroll)


def _moe_scatter_kernel(gstart_ref, gend_ref, idx_ref, pos_ref, h_ref, xs_hbm, zero_ref, sem, *, s_tiles):
    n_exp = gstart_ref.shape[0]

    @pl.when(pl.program_id(0) == 0)
    def _():
        zero_ref[...] = jnp.zeros_like(zero_ref)

        def tail(e):
            first = pl.multiple_of((gend_ref[e] - MOE_TM) * s_tiles, MOE_TM * s_tiles)
            return pltpu.make_async_copy(zero_ref, xs_hbm.at[pl.ds(first, MOE_TM * s_tiles), :], sem)

        def start_tail(e):
            @pl.when(gend_ref[e] > gstart_ref[e])
            def _():
                tail(e).start()

        def wait_tail(e):
            @pl.when(gend_ref[e] > gstart_ref[e])
            def _():
                tail(e).wait()

        _for_each(n_exp, start_tail)
        _for_each(n_exp, wait_tail)

    def row_copy(t, j):
        n = t * TOP_K + j
        row = gstart_ref[idx_ref[n]] + pos_ref[n]
        return pltpu.make_async_copy(_token_row(h_ref, t, s_tiles), _token_row(xs_hbm, row, s_tiles), sem)

    def start_token(t):
        for j in range(TOP_K):
            row_copy(t, j).start()

    def wait_token(t):
        for j in range(TOP_K):
            row_copy(t, j).wait()

    _for_each(MOE_TOK, start_token, unroll=2)
    _for_each(MOE_TOK, wait_token, unroll=2)


def _moe_scatter(h_rows, idx, pos, gstart, gend, n_rows, s_tiles):
    n_steps = idx.shape[0] // (MOE_TOK * TOP_K)
    smem = pl.BlockSpec((MOE_TOK * TOP_K,), lambda i, gs, ge: (i,), memory_space=pltpu.SMEM)
    return pl.pallas_call(
        functools.partial(_moe_scatter_kernel, s_tiles=s_tiles),
        grid_spec=pltpu.PrefetchScalarGridSpec(
            num_scalar_prefetch=2,
            grid=(n_steps,),
            in_specs=[smem, smem, pl.BlockSpec((MOE_TOK * s_tiles, HEAD_DIM), lambda i, gs, ge: (i, 0))],
            out_specs=pl.BlockSpec(memory_space=pl.ANY),
            scratch_shapes=[pltpu.VMEM((MOE_TM * s_tiles, HEAD_DIM), h_rows.dtype), pltpu.SemaphoreType.DMA(())],
        ),
        out_shape=jax.ShapeDtypeStruct((n_rows * s_tiles, HEAD_DIM), h_rows.dtype),
        compiler_params=_cparams(("arbitrary",)),
        name="moe_scatter",
    )(gstart, gend, idx, pos, h_rows)


def _moe_expert_kernel(texp_ref, nvalid_ref, xs_ref, wg_ref, wu_ref, wd_ref, ys_ref, wgb_ref, wub_ref, wdb_ref):
    i = pl.program_id(0)
    d = wg_ref.shape[0]
    valid = i < nvalid_ref[0]
    new_expert = (i == 0) | (texp_ref[i] != texp_ref[jnp.maximum(i - 1, 0)])

    @pl.when(valid & new_expert)
    def _():
        wgb_ref[...] = wg_ref[...].astype(BF16)
        wub_ref[...] = wu_ref[...].astype(BF16)
        wdb_ref[...] = wd_ref[...].astype(BF16)

    @pl.when(valid)
    def _():
        half = d // 2
        lo, hi = _unpack_bf16_pairs(_load_token_rows(xs_ref, MOE_TM, half))
        lo, hi = lo.astype(BF16), hi.astype(BF16)
        gate = _dot(lo, wgb_ref[0:half, :]) + _dot(hi, wgb_ref[half:d, :])
        up = _dot(lo, wub_ref[0:half, :]) + _dot(hi, wub_ref[half:d, :])
        y = _dot((_silu(gate) * up).astype(BF16), wdb_ref[...])
        _store_token_rows(ys_ref, _pack_bf16_pairs(y))

    @pl.when(jnp.logical_not(valid))
    def _():
        ys_ref[...] = jnp.zeros_like(ys_ref)


def _moe_experts(xs, tile_expert, nvalid, wg, wu, wd):
    n_exp, d, ff = wg.shape
    blk = xs.shape[0] // tile_expert.shape[0]

    def expert(shape):
        return pl.BlockSpec((None,) + shape, lambda i, te, nv: (te[i], 0, 0))

    return pl.pallas_call(
        _moe_expert_kernel,
        grid_spec=pltpu.PrefetchScalarGridSpec(
            num_scalar_prefetch=2,
            grid=(xs.shape[0] // blk,),
            in_specs=[
                pl.BlockSpec((blk, HEAD_DIM), lambda i, te, nv: (jnp.minimum(i, nv[0] - 1), 0)),
                expert((d, ff)), expert((d, ff)), expert((ff, d)),
            ],
            out_specs=pl.BlockSpec((blk, HEAD_DIM), lambda i, te, nv: (i, 0)),
            scratch_shapes=[pltpu.VMEM((d, ff), BF16), pltpu.VMEM((d, ff), BF16), pltpu.VMEM((ff, d), BF16)],
        ),
        out_shape=jax.ShapeDtypeStruct(xs.shape, xs.dtype),
        compiler_params=_cparams(("arbitrary",)),
        name="moe_experts",
    )(tile_expert, nvalid, xs, wg, wu, wd)


def _moe_combine_kernel(gstart_ref, idx_ref, pos_ref, w_ref, ys_hbm, x1_ref, gate_ref, nw_ref, o_ref,
                        buf_ref, lo_ref, hi_ref, sem, *, s_tiles):
    tm, d = x1_ref.shape

    def row_copy(n):
        row = gstart_ref[idx_ref[n]] + pos_ref[n]
        return pltpu.make_async_copy(_token_row(ys_hbm, row, s_tiles), _token_row(buf_ref, n, s_tiles), sem)

    _for_each(tm * TOP_K, lambda n: row_copy(n).start())
    _for_each(tm * TOP_K, lambda n: row_copy(n).wait())

    def token(t):
        n0 = t * TOP_K
        lo, hi = _unpack_bf16_pairs(_token_row(buf_ref, n0, s_tiles)[...])
        acc_lo, acc_hi = w_ref[n0] * lo, w_ref[n0] * hi
        for j in range(1, TOP_K):
            lo, hi = _unpack_bf16_pairs(_token_row(buf_ref, n0 + j, s_tiles)[...])
            acc_lo, acc_hi = acc_lo + w_ref[n0 + j] * lo, acc_hi + w_ref[n0 + j] * hi
        _token_row(lo_ref, t, s_tiles)[...] = acc_lo
        _token_row(hi_ref, t, s_tiles)[...] = acc_hi

    _for_each(tm, token)
    routed = jnp.concatenate([_load_token_rows(lo_ref, tm, d // 2), _load_token_rows(hi_ref, tm, d // 2)], axis=1)
    x2 = x1_ref[...] + gate_ref[...] * routed
    o_ref[...] = x2 * lax.rsqrt(jnp.mean(x2 * x2, axis=-1, keepdims=True) + EPS) * nw_ref[...]


def _moe_combine(ys, idx, pos, w, gstart, x1, mod4, norm_w):
    bsz, t, d = x1.shape
    s_tiles = d // 2 // HEAD_DIM
    tm = MOE_TOK
    nt = t // tm
    smem = pl.BlockSpec((tm * TOP_K,), lambda b, i, gs: (b * nt + i,), memory_space=pltpu.SMEM)
    row = pl.BlockSpec((None, tm, d), lambda b, i, gs: (b, i, 0))
    return pl.pallas_call(
        functools.partial(_moe_combine_kernel, s_tiles=s_tiles),
        grid_spec=pltpu.PrefetchScalarGridSpec(
            num_scalar_prefetch=1,
            grid=(bsz, nt),
            in_specs=[
                smem, smem, smem,
                pl.BlockSpec(memory_space=pl.ANY),
                row,
                pl.BlockSpec((None, None, 1, d), lambda b, i, gs: (b, 5, 0, 0)),
                pl.BlockSpec((1, d), lambda b, i, gs: (0, 0)),
            ],
            out_specs=row,
            scratch_shapes=[pltpu.VMEM((tm * TOP_K * s_tiles, HEAD_DIM), ys.dtype),
                            pltpu.VMEM((tm * s_tiles, HEAD_DIM), F32), pltpu.VMEM((tm * s_tiles, HEAD_DIM), F32),
                            pltpu.SemaphoreType.DMA(())],
        ),
        out_shape=jax.ShapeDtypeStruct((bsz, t, d), F32),
        compiler_params=_cparams(("arbitrary", "arbitrary")),
        name="moe_combine",
    )(gstart, idx, pos, w, ys, x1, mod4, norm_w.reshape(1, d))


def _moe(h_rows, route, counts, x1, mod4, norm_w, wg, wu, wd):
    bsz, t, d = x1.shape
    n_exp = wg.shape[0]
    s_tiles = h_rows.shape[0] // (bsz * t)
    idx = route[..., 0:TOP_K].astype(jnp.int32).reshape(-1)
    w = route[..., TOP_K:2 * TOP_K].reshape(-1)
    pos = route[..., 2 * TOP_K:3 * TOP_K].astype(jnp.int32).reshape(-1)
    padded = (counts.reshape(n_exp).astype(jnp.int32) + MOE_TM - 1) // MOE_TM * MOE_TM
    gend = jnp.cumsum(padded)
    gstart = gend - padded
    nvalid = gend[-1:] // MOE_TM
    n_tiles = bsz * t * TOP_K // MOE_TM + n_exp
    tile_start = jnp.minimum(jnp.arange(n_tiles, dtype=jnp.int32), nvalid - 1) * MOE_TM
    tile_expert = jnp.sum((gend[None, :] <= tile_start[:, None]).astype(jnp.int32), axis=1)
    xs = _moe_scatter(h_rows, idx, pos, gstart, gend, n_tiles * MOE_TM, s_tiles)
    ys = _moe_experts(xs, tile_expert, nvalid, wg, wu, wd)
    return _moe_combine(ys, idx, pos, w, gstart, x1, mod4, norm_w)


def _reorder_w_in(w_in):
    d = w_in.shape[0]
    o_beta = 3 * GDN_WIDTH
    o_a = o_beta + GDN_HEADS
    o_gate = o_a + GDN_HEADS
    o_q = o_gate + GDN_WIDTH
    o_kv = o_q + NSA_WIDTH
    o_ng = o_kv + 6 * NSA_KV_WIDTH
    n_small = 2 * GDN_HEADS + 3 * NSA_HEADS
    parts = [w_in[:, :o_beta], w_in[:, o_gate:o_q], w_in[:, o_q:o_kv], w_in[:, o_kv:o_ng],
             w_in[:, o_beta:o_gate], w_in[:, o_ng:o_ng + 3 * NSA_HEADS]]
    used = COL_SMALL * HEAD_DIM + n_small
    parts.append(jnp.zeros((d, PROJ_COLS - used), w_in.dtype))
    return jnp.concatenate(parts, axis=1).astype(BF16)


def _layer(x, mod, norm_attn_w, norm_ffn_w, w_in, gdn_conv_w, gdn_a_log, gdn_dt_bias, gdn_norm_w,
           cmp_pe_k, cmp_w1_k, cmp_w2_k, cmp_pe_v, cmp_w1_v, cmp_w2_v, w_out, router_w, router_bias,
           expert_w_gate, expert_w_up, expert_w_down, shared_w_gate, shared_w_up, shared_w_down, final_w):
    bsz, t, d = x.shape
    mod4 = mod.reshape(bsz, N_MOD, 1, d)
    proj = _proj(x, mod4, norm_attn_w, _reorder_w_in(w_in))
    y_gdn = _gdn(proj, gdn_conv_w, gdn_a_log, gdn_dt_bias, gdn_norm_w)
    kc = _compress(proj, COL_KV, cmp_pe_k, cmp_w1_k, cmp_w2_k)
    vc = _compress(proj, COL_KV + NSA_KV_HEADS, cmp_pe_v, cmp_w1_v, cmp_w2_v)
    o_cmp, sel = _cmpattn(proj, kc, vc)
    o_sel = _selattn(proj, sel)
    y_nsa = _winattn(proj, o_cmp, o_sel)
    x1, h_rows, route, counts = _outproj(
        y_gdn, y_nsa, x, w_out.astype(BF16), mod4, norm_ffn_w, router_w, router_bias,
        shared_w_gate.astype(BF16), shared_w_up.astype(BF16), shared_w_down.astype(BF16))
    return _moe(h_rows, route, counts, x1, mod4, final_w, expert_w_gate, expert_w_up, expert_w_down)


def kernel(x, c, w_ada, b_ada, norm_attn_w, norm_ffn_w, norm_final_w, w_in, gdn_conv_w, gdn_a_log, gdn_dt_bias,
           gdn_norm_w, cmp_pe_k, cmp_w1_k, cmp_w2_k, cmp_pe_v, cmp_w1_v, cmp_w2_v, w_out, router_w, router_bias,
           expert_w_gate, expert_w_up, expert_w_down, shared_w_gate, shared_w_up, shared_w_down):
    depth = w_ada.shape[0]
    assert depth == 1, "the fused final norm assumes a single layer"
    l = 0
    mod = _ada(c, w_ada[l], b_ada[l])
    return _layer(x, mod, norm_attn_w[l], norm_ffn_w[l], w_in[l], gdn_conv_w[l], gdn_a_log[l], gdn_dt_bias[l],
                  gdn_norm_w[l], cmp_pe_k[l], cmp_w1_k[l], cmp_w2_k[l], cmp_pe_v[l], cmp_w1_v[l], cmp_w2_v[l],
                  w_out[l], router_w[l], router_bias[l], expert_w_gate[l], expert_w_up[l], expert_w_down[l],
                  shared_w_gate[l], shared_w_up[l], shared_w_down[l], norm_final_w)
```

```python
import functools

import jax
import jax.numpy as jnp
from jax import lax
from jax.experimental import pallas as pl
from jax.experimental.pallas import tpu as pltpu

F32 = jnp.float32
BF16 = jnp.bfloat16
HI = lax.Precision.HIGHEST

HEAD_DIM = 128
GDN_HEADS = 8
NSA_HEADS = 8
NSA_KV_HEADS = 2
NSA_GROUP = NSA_HEADS // NSA_KV_HEADS
GDN_WIDTH = GDN_HEADS * HEAD_DIM
NSA_WIDTH = NSA_HEADS * HEAD_DIM
NSA_KV_WIDTH = NSA_KV_HEADS * HEAD_DIM
CONV_WIDTH = 4
GDN_CHUNK = 64
CMP_BLOCK = 32
CMP_STRIDE = 16
SEL_BLOCK = 64
SEL_TOPK = 16
N_LOCAL_BLOCKS = 2
WINDOW = 512
Q_BLOCK = 128
TOP_K = 8
ROUTED_SCALE = 2.5
N_MOD = 6
EPS = 1e-6

COL_GDN_Q = 0
COL_GDN_K = GDN_HEADS
COL_GDN_V = 2 * GDN_HEADS
COL_GDN_GATE = 3 * GDN_HEADS
COL_NSA_Q = 4 * GDN_HEADS
COL_KV = COL_NSA_Q + NSA_HEADS
COL_SMALL = COL_KV + 6 * NSA_KV_HEADS
N_COL_BLOCKS = COL_SMALL + 1
PROJ_TN = 2304
PROJ_COLS = 3 * PROJ_TN
SMALL_BETA = 0
SMALL_A = GDN_HEADS
SMALL_NSA_GATE = 2 * GDN_HEADS

NEG_BIG = -1e30
VMEM_LIMIT = 48 * 1024 * 1024


def _cparams(semantics, vmem_limit=VMEM_LIMIT):
    return pltpu.CompilerParams(dimension_semantics=semantics, vmem_limit_bytes=vmem_limit)


def _silu(x):
    return (0.5 * x) * (1.0 + jnp.tanh(0.5 * x))


def _dot(a, b, precision=None):
    return jnp.dot(a, b, preferred_element_type=F32, precision=precision)


def _dot_nt(a, b, precision=None):
    return lax.dot_general(a, b, (((1,), (1,)), ((), ())), preferred_element_type=F32, precision=precision)


def _dot_tn(a, b, precision=None):
    return lax.dot_general(a, b, (((0,), (0,)), ((), ())), preferred_element_type=F32, precision=precision)


def _split_bf16(x):
    hi = x.astype(BF16)
    return hi, (x - hi.astype(F32)).astype(BF16)


def _dot_split(a, b, nt=False):
    (ah, al), (bh, bl) = _split_bf16(a), _split_bf16(b)
    f = _dot_nt if nt else _dot
    return f(ah, bh) + (f(ah, bl) + f(al, bh))


def _lane_pick(x, idx):
    lane = lax.broadcasted_iota(jnp.int32, x.shape, 1)
    return jnp.sum(jnp.where(lane == idx, x, 0.0), axis=-1, keepdims=True)


def _ada_kernel(ct_ref, w_ref, b_ref, o_ref):
    c = ct_ref[...]
    cond = _silu(c)
    w = w_ref[...]
    rows = [jnp.sum(w * cond[:, b:b + 1], axis=0, keepdims=True) for b in range(c.shape[1])]
    o_ref[...] = jnp.concatenate(rows, axis=0) + b_ref[...]


def _ada(c, w_ada, b_ada, tn=512):
    bsz, d = c.shape
    n = w_ada.shape[1]
    return pl.pallas_call(
        _ada_kernel,
        grid=(n // tn,),
        in_specs=[
            pl.BlockSpec((d, bsz), lambda j: (0, 0)),
            pl.BlockSpec((d, tn), lambda j: (0, j)),
            pl.BlockSpec((1, tn), lambda j: (0, j)),
        ],
        out_specs=pl.BlockSpec((bsz, tn), lambda j: (0, j)),
        out_shape=jax.ShapeDtypeStruct((bsz, n), F32),
        compiler_params=_cparams(("arbitrary",)),
        name="ada",
    )(c.T, w_ada, b_ada.reshape(1, n))


def _modulated_norm(x, nw, scale, shift):
    var = jnp.mean(x * x, axis=-1, keepdims=True)
    h = x * lax.rsqrt(var + EPS) * nw
    return h * (1.0 + scale) + shift


def _proj_kernel(x_ref, shift_ref, scale_ref, nw_ref, w_ref, o_ref):
    h = _modulated_norm(x_ref[...], nw_ref[...], scale_ref[...], shift_ref[...])
    o_ref[...] = _dot(h.astype(BF16), w_ref[...])


def _proj(x, mod4, norm_w, w_r, tm=256):
    bsz, t, d = x.shape
    n = w_r.shape[1]
    return pl.pallas_call(
        _proj_kernel,
        grid=(n // PROJ_TN, bsz, t // tm),
        in_specs=[
            pl.BlockSpec((None, tm, d), lambda j, b, i: (b, i, 0)),
            pl.BlockSpec((None, None, 1, d), lambda j, b, i: (b, 0, 0, 0)),
            pl.BlockSpec((None, None, 1, d), lambda j, b, i: (b, 1, 0, 0)),
            pl.BlockSpec((1, d), lambda j, b, i: (0, 0)),
            pl.BlockSpec((d, PROJ_TN), lambda j, b, i: (0, j)),
        ],
        out_specs=pl.BlockSpec((None, tm, PROJ_TN), lambda j, b, i: (b, i, j)),
        out_shape=jax.ShapeDtypeStruct((bsz, t, n), F32),
        compiler_params=_cparams(("arbitrary", "arbitrary", "arbitrary")),
        name="proj",
    )(x, mod4, mod4, norm_w.reshape(1, d), w_r)


def _gdn_kernel(q_ref, k_ref, v_ref, gate_ref, small_ref, hq_ref, hk_ref, hv_ref,
                cwq_ref, cwk_ref, cwv_ref, alog_ref, dtb_ref, nw_ref, o_ref, s_ref, xc_ref, *, tt, hg):
    first = pl.program_id(2) == 0

    @pl.when(first)
    def _():
        s_ref[...] = jnp.zeros_like(s_ref)

    def conv_silu(x_ref, halo_ref, cw_ref):
        xc_ref[0:8, :] = jnp.where(first, 0.0, halo_ref[...])
        xc_ref[8:tt + 8, :] = x_ref[...]
        cw = cw_ref[...]
        off = 8 - (CONV_WIDTH - 1)
        y = cw[0:1, :] * xc_ref[off:off + tt, :]
        for i in range(1, CONV_WIDTH):
            y = y + cw[i:i + 1, :] * xc_ref[off + i:off + i + tt, :]
        return _silu(y)

    q_all = conv_silu(q_ref, hq_ref, cwq_ref)
    k_all = conv_silu(k_ref, hk_ref, cwk_ref)
    v_all = conv_silu(v_ref, hv_ref, cwv_ref)
    ri = lax.broadcasted_iota(jnp.int32, (tt, tt), 0)
    ci = lax.broadcasted_iota(jnp.int32, (tt, tt), 1)
    lower = jnp.where(((ri // GDN_CHUNK) == (ci // GDN_CHUNK)) & (ci <= ri), 1.0, 0.0).astype(BF16)
    for hh in range(hg):
        cols = slice(hh * HEAD_DIM, (hh + 1) * HEAD_DIM)
        o = _gdn_head(q_all[:, cols], k_all[:, cols], v_all[:, cols], small_ref[...], lower,
                      pl.program_id(1) * hg + hh, alog_ref[...], dtb_ref[...], s_ref.at[hh], tt)
        o = o * lax.rsqrt(jnp.mean(o * o, axis=-1, keepdims=True) + EPS) * nw_ref[...]
        o_ref[:, cols] = (o * _silu(gate_ref[:, cols])).astype(o_ref.dtype)


def _gdn_head(q, k, v, small, lower, h, alog, dtb, s_ref, tt):
    c = GDN_CHUNK
    nch = tt // c
    q = q * lax.rsqrt(jnp.sum(q * q, axis=-1, keepdims=True) + EPS) * (HEAD_DIM ** -0.5)
    k = k * lax.rsqrt(jnp.sum(k * k, axis=-1, keepdims=True) + EPS)

    beta = jax.nn.sigmoid(_lane_pick(small, SMALL_BETA + h))
    z = _lane_pick(small, SMALL_A + h) + _lane_pick(dtb, h)
    softplus = jnp.maximum(z, 0.0) + jnp.log(1.0 + jnp.exp(-jnp.abs(z)))
    g = -jnp.exp(_lane_pick(alog, h)) * softplus

    kpos = lax.broadcasted_iota(jnp.int32, (tt, c), 0) % c
    jj = lax.broadcasted_iota(jnp.int32, (tt, c), 1)
    g_c = jnp.broadcast_to(g, (tt, c))
    rhs = jnp.concatenate([g_c, jnp.where(kpos > jj, g_c, 0.0)], axis=1)
    r_hi = rhs.astype(BF16)
    rem = rhs - r_hi.astype(F32)
    r_mid = rem.astype(BF16)
    r_lo = (rem - r_mid.astype(F32)).astype(BF16)
    sums = _dot(lower, r_hi) + (_dot(lower, r_mid) + _dot(lower, r_lo))
    gcum = jnp.broadcast_to(sums[:, 0:1], (tt, HEAD_DIM))
    gc3 = gcum.reshape(nch, c, HEAD_DIM)
    grest = (gc3[:, c - 1:c, :] - gc3).reshape(tt, HEAD_DIM)
    decay = jnp.where(kpos >= jj, jnp.exp(sums[:, c:2 * c]), 0.0).reshape(nch, c, c)
    strict = (kpos > jj).reshape(nch, c, c)
    eye = jnp.where(kpos == jj, 1.0, 0.0).reshape(nch, c, c)

    def bmm(a, b):
        return jnp.einsum('nij,njk->nik', a, b, preferred_element_type=F32)

    def bmm_nt(a, b):
        return jnp.einsum('nid,njd->nij', a, b, preferred_element_type=F32)

    def bmm_split(a, b):
        (ah, al), (bh, bl) = _split_bf16(a), _split_bf16(b)
        return bmm(ah, bh) + (bmm(ah, bl) + bmm(al, bh))

    k3 = k.reshape(nch, c, HEAD_DIM).astype(BF16)
    kb = k * beta
    a = jnp.where(strict, bmm_nt(kb.reshape(nch, c, HEAD_DIM).astype(BF16), k3) * decay, 0.0)
    inv = eye - a
    p = bmm_split(a, a)
    n_sq = (c - 1).bit_length() - 1
    for lvl in range(n_sq):
        inv = inv + bmm_split(inv, p)
        if lvl + 1 < n_sq:
            p = bmm_split(p, p)
    egc = jnp.exp(gcum)
    inv_b = inv.astype(BF16)
    u3 = bmm(inv_b, (v * beta).reshape(nch, c, HEAD_DIM).astype(BF16)).astype(BF16)
    w3 = bmm(inv_b, (kb * egc).reshape(nch, c, HEAD_DIM).astype(BF16)).astype(BF16)
    qk3 = (bmm_nt(q.reshape(nch, c, HEAD_DIM).astype(BF16), k3) * decay).astype(BF16)
    kd3 = (k * jnp.exp(grest)).reshape(nch, c, HEAD_DIM).astype(BF16)
    gl3 = egc.reshape(nch, c, HEAD_DIM)
    qe3 = ((q * egc).reshape(nch, c, HEAD_DIM) - bmm(qk3, w3)).astype(BF16)
    o03 = bmm(qk3, u3)
    kw = [_dot_tn(kd3[n], w3[n]).astype(BF16) for n in range(nch)]
    ku = [_dot_tn(kd3[n], u3[n]) for n in range(nch)]

    state = s_ref[...]
    outs = []
    for n in range(nch):
        state_b = state.astype(BF16)
        outs.append(_dot(qe3[n], state_b) + o03[n])
        state = state * gl3[n, c - 1:c, :] - _dot(kw[n], state_b) + ku[n]
    s_ref[...] = state
    return jnp.concatenate(outs, axis=0)


def _gdn(proj, conv_w, a_log, dt_bias, norm_w, tt=256, hg=4):
    bsz, t, _ = proj.shape
    wdt = hg * HEAD_DIM

    def col(base):
        return pl.BlockSpec((None, tt, wdt), lambda b, h, i: (b, i, base // hg + h))

    def halo(base):
        return pl.BlockSpec((None, 8, wdt), lambda b, h, i: (b, jnp.maximum(i * (tt // 8) - 1, 0), base // hg + h))

    def cw(base):
        return pl.BlockSpec((CONV_WIDTH, wdt), lambda b, h, i: (0, base // hg + h))

    heads = pl.BlockSpec((1, GDN_HEADS), lambda b, h, i: (0, 0))
    return pl.pallas_call(
        functools.partial(_gdn_kernel, tt=tt, hg=hg),
        grid=(bsz, GDN_HEADS // hg, t // tt),
        in_specs=[
            col(COL_GDN_Q), col(COL_GDN_K), col(COL_GDN_V), col(COL_GDN_GATE),
            pl.BlockSpec((None, tt, HEAD_DIM), lambda b, h, i: (b, i, COL_SMALL)),
            halo(COL_GDN_Q), halo(COL_GDN_K), halo(COL_GDN_V),
            cw(COL_GDN_Q), cw(COL_GDN_K), cw(COL_GDN_V),
            heads, heads,
            pl.BlockSpec((1, HEAD_DIM), lambda b, h, i: (0, 0)),
        ],
        out_specs=pl.BlockSpec((None, tt, wdt), lambda b, h, i: (b, i, h)),
        out_shape=jax.ShapeDtypeStruct((bsz, t, GDN_WIDTH), BF16),
        scratch_shapes=[pltpu.VMEM((hg, HEAD_DIM, HEAD_DIM), F32), pltpu.VMEM((tt + 8, wdt), F32)],
        compiler_params=_cparams(("arbitrary", "arbitrary", "arbitrary")),
        name="gdn",
    )(proj, proj, proj, proj, proj, proj, proj, proj, conv_w, conv_w, conv_w,
      a_log.reshape(1, GDN_HEADS), dt_bias.reshape(1, GDN_HEADS), norm_w.reshape(1, HEAD_DIM))


def _compress_kernel(x_ref, pe_ref, w1_ref, w2_ref, o_ref):
    r = x_ref.shape[0] // CMP_STRIDE
    d = x_ref.shape[1]
    a = b = None
    for l in range(CMP_STRIDE):
        x_l = x_ref[pl.ds(l, r, stride=CMP_STRIDE), :]
        a_l = _dot_split(x_l + pe_ref[l:l + 1, :], w1_ref[l * d:(l + 1) * d, :])
        b_l = _dot_split(x_l + pe_ref[CMP_STRIDE + l:CMP_STRIDE + l + 1, :],
                         w1_ref[(CMP_STRIDE + l) * d:(CMP_STRIDE + l + 1) * d, :])
        a, b = (a_l, b_l) if a is None else (a + a_l, b + b_l)
    hid = _silu(a + pltpu.roll(b, r - 1, 0))
    o_ref[...] = _dot_split(hid, w2_ref[...])


def _compress(proj, col_block, pe, w1, w2):
    bsz, t, _ = proj.shape
    r = t // CMP_STRIDE
    hid = w1.shape[1]
    return pl.pallas_call(
        _compress_kernel,
        grid=(bsz, NSA_KV_HEADS),
        in_specs=[
            pl.BlockSpec((None, t, HEAD_DIM), lambda b, h: (b, 0, col_block + h)),
            pl.BlockSpec((CMP_BLOCK, HEAD_DIM), lambda b, h: (0, 0)),
            pl.BlockSpec((CMP_BLOCK * HEAD_DIM, hid), lambda b, h: (0, 0)),
            pl.BlockSpec((hid, HEAD_DIM), lambda b, h: (0, 0)),
        ],
        out_specs=pl.BlockSpec((None, None, r, HEAD_DIM), lambda b, h: (b, h, 0, 0)),
        out_shape=jax.ShapeDtypeStruct((bsz, NSA_KV_HEADS, r, HEAD_DIM), F32),
        compiler_params=_cparams(("arbitrary", "arbitrary")),
        name="compress",
    )(proj, pe, w1, w2)


def _cmpattn_kernel(q_ref, kc_ref, vc_ref, o_ref, sel_ref, *, tq, nc, nsel):
    ncp = kc_ref.shape[0]
    scale = HEAD_DIM ** -0.5
    t0 = pl.program_id(2) * tq
    kc = kc_ref[...]
    vc = vc_ref[...]
    pos_r = t0 + lax.broadcasted_iota(jnp.int32, (tq, ncp), 0)
    c_r = lax.broadcasted_iota(jnp.int32, (tq, ncp), 1)
    valid_r = (c_r * CMP_STRIDE + CMP_BLOCK - 1 <= pos_r) & (c_r < nc)
    pos_c = t0 + lax.broadcasted_iota(jnp.int32, (ncp, tq), 1)
    c_c = lax.broadcasted_iota(jnp.int32, (ncp, tq), 0)
    valid_c = (c_c * CMP_STRIDE + CMP_BLOCK - 1 <= pos_c) & (c_c < nc)
    psum = jnp.zeros((ncp, tq), F32)
    kc_hi, kc_lo = _split_bf16(kc)
    vc_b = vc.astype(BF16)
    for g in range(NSA_GROUP):
        q_hi, q_lo = _split_bf16(q_ref[:, g * HEAD_DIM:(g + 1) * HEAD_DIM])
        s = _dot_nt(q_hi, kc_hi) + (_dot_nt(q_hi, kc_lo) + _dot_nt(q_lo, kc_hi))
        s = jnp.where(valid_r, s * scale, NEG_BIG)
        e = jnp.where(valid_r, jnp.exp(s - jnp.max(s, axis=-1, keepdims=True)), 0.0)
        p = e * (1.0 / jnp.maximum(jnp.sum(e, axis=-1, keepdims=True), 1e-30))
        o_ref[:, g * HEAD_DIM:(g + 1) * HEAD_DIM] = _dot(p.astype(BF16), vc_b)
        st = _dot_nt(kc_hi, q_hi) + (_dot_nt(kc_lo, q_hi) + _dot_nt(kc_hi, q_lo))
        st = jnp.where(valid_c, st * scale, NEG_BIG)
        et = jnp.where(valid_c, jnp.exp(st - jnp.max(st, axis=0, keepdims=True)), 0.0)
        psum = psum + et * (1.0 / jnp.maximum(jnp.sum(et, axis=0, keepdims=True), 1e-30))
    j_o = lax.broadcasted_iota(jnp.int32, (nsel, ncp), 0) * SEL_BLOCK
    c_o = lax.broadcasted_iota(jnp.int32, (nsel, ncp), 1) * CMP_STRIDE
    overlap = jnp.where((c_o < j_o + SEL_BLOCK) & (c_o + CMP_BLOCK > j_o), 1.0, 0.0).astype(BF16)
    p_hi, p_lo = _split_bf16(psum)
    imp = _dot(overlap, p_hi) + _dot(overlap, p_lo)
    blk = lax.broadcasted_iota(jnp.int32, (nsel, tq), 0)
    dist = (t0 + lax.broadcasted_iota(jnp.int32, (nsel, tq), 1)) // SEL_BLOCK - blk
    forced = (blk == 0) | ((dist >= 0) & (dist < N_LOCAL_BLOCKS))
    val = jnp.where(forced, -NEG_BIG, jnp.where(dist >= 0, imp, -1.0))
    rank = jnp.zeros((nsel, tq), F32)
    for j in range(nsel):
        row = val[j:j + 1, :]
        rank = rank + jnp.where((row > val) | ((row == val) & (blk > j)), 1.0, 0.0)
    sel = jnp.where((rank < min(SEL_TOPK, nsel)) & (dist >= 0), 1.0, 0.0)
    sel_ref[...] = sel.T


def _cmpattn(proj, kc, vc, tq=256):
    bsz, t, _ = proj.shape
    ncp = kc.shape[2]
    nc = (t - CMP_BLOCK) // CMP_STRIDE + 1
    nsel = t // SEL_BLOCK
    gw = NSA_GROUP * HEAD_DIM
    return pl.pallas_call(
        functools.partial(_cmpattn_kernel, tq=tq, nc=nc, nsel=nsel),
        grid=(bsz, NSA_KV_HEADS, t // tq),
        in_specs=[
            pl.BlockSpec((None, tq, gw), lambda b, h, i: (b, i, COL_NSA_Q // NSA_GROUP + h)),
            pl.BlockSpec((None, None, ncp, HEAD_DIM), lambda b, h, i: (b, h, 0, 0)),
            pl.BlockSpec((None, None, ncp, HEAD_DIM), lambda b, h, i: (b, h, 0, 0)),
        ],
        out_specs=[
            pl.BlockSpec((None, tq, gw), lambda b, h, i: (b, i, h)),
            pl.BlockSpec((None, None, tq, nsel), lambda b, h, i: (b, h, i, 0)),
        ],
        out_shape=[
            jax.ShapeDtypeStruct((bsz, t, NSA_WIDTH), F32),
            jax.ShapeDtypeStruct((bsz, NSA_KV_HEADS, t, nsel), F32),
        ],
        compiler_params=_cparams(("arbitrary", "arbitrary", "arbitrary")),
        name="cmpattn",
    )(proj, kc, vc)


def _selattn_kernel(q_ref, k_ref, v_ref, sel_ref, o_ref, m_ref, l_ref, acc_ref, *, tq, tk, nsel):
    g = NSA_GROUP
    qi = pl.program_id(2)
    m_ref[...] = jnp.full_like(m_ref, NEG_BIG)
    l_ref[...] = jnp.zeros_like(l_ref)
    acc_ref[...] = jnp.zeros_like(acc_ref)
    q4 = jnp.concatenate([q_ref[:, i * HEAD_DIM:(i + 1) * HEAD_DIM] for i in range(g)], axis=0)
    q4 = (q4 * HEAD_DIM ** -0.5).astype(BF16)
    sel = sel_ref[...].astype(BF16)
    pos = qi * tq + lax.broadcasted_iota(jnp.int32, (tq, tk), 0)

    def kv_step(kv, carry):
        k0 = pl.multiple_of(kv * tk, tk)
        s = _dot_nt(q4, k_ref[pl.ds(k0, tk), :].astype(BF16))
        key = k0 + lax.broadcasted_iota(jnp.int32, (nsel, tk), 1)
        blk = lax.broadcasted_iota(jnp.int32, (nsel, tk), 0)
        expand = jnp.where(key // SEL_BLOCK == blk, 1.0, 0.0).astype(BF16)
        picked = _dot(sel, expand)
        kpos = k0 + lax.broadcasted_iota(jnp.int32, (tq, tk), 1)
        mask = ((picked > 0.5) & (kpos <= pos))[None]
        s = jnp.where(mask, s.reshape(g, tq, tk), NEG_BIG)
        m_prev = m_ref[...]
        m_new = jnp.maximum(m_prev, jnp.max(s, axis=-1, keepdims=True))
        p = jnp.exp(s - m_new)
        alpha = jnp.exp(m_prev - m_new)
        l_ref[...] = alpha * l_ref[...] + jnp.sum(p, axis=-1, keepdims=True)
        pv = _dot(p.reshape(g * tq, tk).astype(BF16), v_ref[pl.ds(k0, tk), :].astype(BF16))
        acc_ref[...] = alpha * acc_ref[...] + pv.reshape(g, tq, HEAD_DIM)
        m_ref[...] = m_new
        return carry

    lax.fori_loop(0, (qi * tq + tq - 1) // tk + 1, kv_step, 0)
    o = acc_ref[...] * (1.0 / jnp.maximum(l_ref[...], 1e-30))
    for i in range(g):
        o_ref[:, i * HEAD_DIM:(i + 1) * HEAD_DIM] = o[i]


def _selattn(proj, sel, tq=Q_BLOCK, tk=512):
    bsz, t, _ = proj.shape
    nsel = t // SEL_BLOCK
    tk = min(tk, t)
    gw = NSA_GROUP * HEAD_DIM

    def kv_spec(base):
        return pl.BlockSpec((None, t, HEAD_DIM), lambda b, h, i: (b, 0, base + h))

    return pl.pallas_call(
        functools.partial(_selattn_kernel, tq=tq, tk=tk, nsel=nsel),
        grid=(bsz, NSA_KV_HEADS, t // tq),
        in_specs=[
            pl.BlockSpec((None, tq, gw), lambda b, h, i: (b, i, COL_NSA_Q // NSA_GROUP + h)),
            kv_spec(COL_KV + 2 * NSA_KV_HEADS), kv_spec(COL_KV + 3 * NSA_KV_HEADS),
            pl.BlockSpec((None, None, tq, nsel), lambda b, h, i: (b, h, i, 0)),
        ],
        out_specs=pl.BlockSpec((None, tq, gw), lambda b, h, i: (b, i, h)),
        out_shape=jax.ShapeDtypeStruct((bsz, t, NSA_WIDTH), F32),
        scratch_shapes=[pltpu.VMEM((NSA_GROUP, tq, 1), F32), pltpu.VMEM((NSA_GROUP, tq, 1), F32),
                        pltpu.VMEM((NSA_GROUP, tq, HEAD_DIM), F32)],
        compiler_params=_cparams(("arbitrary", "arbitrary", "arbitrary")),
        name="selattn",
    )(proj, proj, proj, sel)


def _winattn_kernel(*refs, tq, nblk):
    q_ref = refs[0]
    k_refs = refs[1:1 + nblk]
    v_refs = refs[1 + nblk:1 + 2 * nblk]
    ocmp_ref, osel_ref, small_ref, o_ref = refs[1 + 2 * nblk:]
    g = NSA_GROUP
    hk = pl.program_id(1)
    qi = pl.program_id(2)
    scale = HEAD_DIM ** -0.5
    nk = nblk * tq
    keys = jnp.concatenate([r[...] for r in k_refs], axis=0).astype(BF16)
    vals = jnp.concatenate([r[...] for r in v_refs], axis=0).astype(BF16)
    q4 = jnp.concatenate([q_ref[:, i * HEAD_DIM:(i + 1) * HEAD_DIM] for i in range(g)], axis=0)
    s = (_dot_nt(q4.astype(BF16), keys) * scale).reshape(g, tq, nk)
    pos = qi * tq + lax.broadcasted_iota(jnp.int32, (tq, nk), 0)
    wpos = (qi - (nblk - 1)) * tq + lax.broadcasted_iota(jnp.int32, (tq, nk), 1)
    mask = ((wpos <= pos) & (wpos > pos - WINDOW) & (wpos >= 0))[None]
    s = jnp.where(mask, s, NEG_BIG)
    e = jnp.exp(s - jnp.max(s, axis=-1, keepdims=True)).astype(BF16)
    ov = _dot(e.reshape(g * tq, nk), jnp.concatenate([vals, jnp.ones_like(vals)], axis=1))
    o_win = (ov[:, 0:HEAD_DIM] * (1.0 / jnp.maximum(ov[:, HEAD_DIM:2 * HEAD_DIM], 1e-30))).reshape(g, tq, HEAD_DIM)
    small = small_ref[...]
    for i in range(g):
        head = hk * g + i
        gates = [jax.nn.sigmoid(_lane_pick(small, SMALL_NSA_GATE + j * NSA_HEADS + head)) for j in range(3)]
        cols = slice(i * HEAD_DIM, (i + 1) * HEAD_DIM)
        o = gates[0] * ocmp_ref[:, cols] + gates[1] * osel_ref[:, cols] + gates[2] * o_win[i]
        o_ref[:, cols] = o.astype(o_ref.dtype)


def _winattn(proj, o_cmp, o_sel, tq=Q_BLOCK):
    bsz, t, _ = proj.shape
    nblk = WINDOW // tq + 1
    gw = NSA_GROUP * HEAD_DIM

    def kv_spec(base, d):
        return pl.BlockSpec((None, tq, HEAD_DIM),
                            lambda b, h, i: (b, jnp.maximum(i - (nblk - 1) + d, 0), base + h))

    grp = pl.BlockSpec((None, tq, gw), lambda b, h, i: (b, i, h))
    return pl.pallas_call(
        functools.partial(_winattn_kernel, tq=tq, nblk=nblk),
        grid=(bsz, NSA_KV_HEADS, t // tq),
        in_specs=(
            [pl.BlockSpec((None, tq, gw), lambda b, h, i: (b, i, COL_NSA_Q // NSA_GROUP + h))]
            + [kv_spec(COL_KV + 4 * NSA_KV_HEADS, d) for d in range(nblk)]
            + [kv_spec(COL_KV + 5 * NSA_KV_HEADS, d) for d in range(nblk)]
            + [grp, grp, pl.BlockSpec((None, tq, HEAD_DIM), lambda b, h, i: (b, i, COL_SMALL))]
        ),
        out_specs=grp,
        out_shape=jax.ShapeDtypeStruct((bsz, t, NSA_WIDTH), BF16),
        compiler_params=_cparams(("arbitrary", "arbitrary", "arbitrary")),
        name="winattn",
    )(proj, *([proj] * (2 * nblk)), o_cmp, o_sel, proj)


def _ffn(h, wg, wu, wd):
    act = _silu(_dot(h, wg)) * _dot(h, wu)
    return _dot(act.astype(BF16), wd)


def _pack_bf16_pairs(x):
    half = x.shape[1] // 2
    lo = pltpu.bitcast(x[:, :half].astype(BF16).astype(F32), jnp.uint32)
    hi = pltpu.bitcast(x[:, half:].astype(BF16).astype(F32), jnp.uint32)
    return hi | (lo >> 16)


def _unpack_bf16_pairs(p):
    return pltpu.bitcast(p << 16, F32), pltpu.bitcast(p & jnp.uint32(0xFFFF0000), F32)


def _store_token_rows(ref, x):
    n, d = x.shape
    s_tiles = d // HEAD_DIM
    for s in range(s_tiles):
        ref[pl.ds(s, n, stride=s_tiles), :] = x[:, s * HEAD_DIM:(s + 1) * HEAD_DIM]


def _load_token_rows(ref, n, d):
    s_tiles = d // HEAD_DIM
    return jnp.concatenate([ref[pl.ds(s, n, stride=s_tiles), :] for s in range(s_tiles)], axis=1)


def _outproj_kernel(yg_ref, yn_ref, x_ref, wo_ref, gate_ref, shift_ref, scale_ref, gate_m_ref, nw_ref,
                    rw_ref, rb_ref, sg_ref, su_ref, sd_ref, x1_ref, h_ref, route_ref, count_ref, carry_ref):
    tm = x_ref.shape[0]
    half = yg_ref.shape[1]

    @pl.when((pl.program_id(0) == 0) & (pl.program_id(1) == 0))
    def _():
        carry_ref[...] = jnp.zeros_like(carry_ref)

    y = _dot(yg_ref[...], wo_ref[0:half, :]) + _dot(yn_ref[...], wo_ref[half:2 * half, :])
    x1 = x_ref[...] + gate_ref[...] * y
    h = _modulated_norm(x1, nw_ref[...], scale_ref[...], shift_ref[...])
    _store_token_rows(h_ref, _pack_bf16_pairs(h))
    x1_ref[...] = x1 + gate_m_ref[...] * _ffn(h.astype(BF16), sg_ref[...], su_ref[...], sd_ref[...])
    scores = jax.nn.sigmoid(_dot_split(h, rw_ref[...]))
    n_exp = scores.shape[1]
    lane = lax.broadcasted_iota(jnp.int32, scores.shape, 1)
    cand = scores + rb_ref[...]
    picked = jnp.zeros(scores.shape, jnp.bool_)
    firsts = []
    for _ in range(TOP_K):
        best = jnp.max(cand, axis=-1, keepdims=True)
        first = jnp.min(jnp.where(cand == best, lane, n_exp), axis=-1, keepdims=True)
        hit = lane == first
        picked = picked | hit
        cand = jnp.where(hit, NEG_BIG, cand)
        firsts.append(first)
    norm = ROUTED_SCALE / jnp.sum(jnp.where(picked, scores, 0.0), axis=-1, keepdims=True)
    onehot = jnp.where(picked, 1.0, 0.0)
    ri = lax.broadcasted_iota(jnp.int32, (tm, tm), 0)
    ci = lax.broadcasted_iota(jnp.int32, (tm, tm), 1)
    before = jnp.where(ci < ri, 1.0, 0.0).astype(BF16)
    pos = _dot(before, onehot.astype(BF16)) + carry_ref[...]
    carry_ref[...] += jnp.sum(onehot, axis=0, keepdims=True)
    count_ref[...] = carry_ref[...]
    slot = lax.broadcasted_iota(jnp.int32, (tm, 3 * TOP_K), 1)
    route = jnp.zeros((tm, 3 * TOP_K), F32)
    for j, first in enumerate(firsts):
        hit = lane == first
        route = jnp.where(slot == j, first.astype(F32), route)
        route = jnp.where(slot == TOP_K + j, _lane_pick(scores, first) * norm, route)
        route = jnp.where(slot == 2 * TOP_K + j, jnp.sum(jnp.where(hit, pos, 0.0), axis=-1, keepdims=True), route)
    route_ref[...] = route


def _outproj(y_gdn, y_nsa, x, w_out_b, mod4, norm_w, router_w, router_bias, sg, su, sd, tm=256):
    bsz, t, d = x.shape
    half = y_gdn.shape[2]
    n_exp = router_w.shape[1]
    sff = sg.shape[1]
    s_tiles = d // 2 // HEAD_DIM

    def mod(j):
        return pl.BlockSpec((None, None, 1, d), lambda b, i: (b, j, 0, 0))

    def const(shape):
        return pl.BlockSpec(shape, lambda b, i: (0, 0))

    row = pl.BlockSpec((None, tm, d), lambda b, i: (b, i, 0))
    return pl.pallas_call(
        _outproj_kernel,
        grid=(bsz, t // tm),
        in_specs=[
            pl.BlockSpec((None, tm, half), lambda b, i: (b, i, 0)),
            pl.BlockSpec((None, tm, half), lambda b, i: (b, i, 0)),
            row,
            const((2 * half, d)),
            mod(2), mod(3), mod(4), mod(5),
            const((1, d)), const((d, n_exp)), const((1, n_exp)),
            const((d, sff)), const((d, sff)), const((sff, d)),
        ],
        out_specs=[row,
                   pl.BlockSpec((tm * s_tiles, HEAD_DIM), lambda b, i: (b * (t // tm) + i, 0)),
                   pl.BlockSpec((None, tm, 3 * TOP_K), lambda b, i: (b, i, 0)),
                   const((1, n_exp))],
        out_shape=[
            jax.ShapeDtypeStruct((bsz, t, d), F32),
            jax.ShapeDtypeStruct((bsz * t * s_tiles, HEAD_DIM), jnp.uint32),
            jax.ShapeDtypeStruct((bsz, t, 3 * TOP_K), F32),
            jax.ShapeDtypeStruct((1, n_exp), F32),
        ],
        scratch_shapes=[pltpu.VMEM((1, n_exp), F32)],
        compiler_params=_cparams(("arbitrary", "arbitrary")),
        name="outproj",
    )(y_gdn, y_nsa, x, w_out_b, mod4, mod4, mod4, mod4, norm_w.reshape(1, d), router_w,
      router_bias.reshape(1, n_exp), sg, su, sd)


MOE_TM = 256
MOE_TOK = 128


def _token_row(ref, r, s_tiles):
    return ref.at[pl.ds(pl.multiple_of(r * s_tiles, s_tiles), s_tiles), :]


def _for_each(n, fn, unroll=8):
    def body(i, carry):
        fn(i)
        return carry

    lax.fori_loop(0, n, body, 0, unroll=unroll)


def _moe_scatter_kernel(gstart_ref, gend_ref, idx_ref, pos_ref, h_ref, xs_hbm, zero_ref, sem, *, s_tiles):
    n_exp = gstart_ref.shape[0]

    @pl.when(pl.program_id(0) == 0)
    def _():
        zero_ref[...] = jnp.zeros_like(zero_ref)

        def tail(e):
            first = pl.multiple_of((gend_ref[e] - MOE_TM) * s_tiles, MOE_TM * s_tiles)
            return pltpu.make_async_copy(zero_ref, xs_hbm.at[pl.ds(first, MOE_TM * s_tiles), :], sem)

        def start_tail(e):
            @pl.when(gend_ref[e] > gstart_ref[e])
            def _():
                tail(e).start()

        def wait_tail(e):
            @pl.when(gend_ref[e] > gstart_ref[e])
            def _():
                tail(e).wait()

        _for_each(n_exp, start_tail)
        _for_each(n_exp, wait_tail)

    def row_copy(t, j):
        n = t * TOP_K + j
        row = gstart_ref[idx_ref[n]] + pos_ref[n]
        return pltpu.make_async_copy(_token_row(h_ref, t, s_tiles), _token_row(xs_hbm, row, s_tiles), sem)

    def start_token(t):
        for j in range(TOP_K):
            row_copy(t, j).start()

    def wait_token(t):
        for j in range(TOP_K):
            row_copy(t, j).wait()

    _for_each(MOE_TOK, start_token, unroll=2)
    _for_each(MOE_TOK, wait_token, unroll=2)


def _moe_scatter(h_rows, idx, pos, gstart, gend, n_rows, s_tiles):
    n_steps = idx.shape[0] // (MOE_TOK * TOP_K)
    smem = pl.BlockSpec((MOE_TOK * TOP_K,), lambda i, gs, ge: (i,), memory_space=pltpu.SMEM)
    return pl.pallas_call(
        functools.partial(_moe_scatter_kernel, s_tiles=s_tiles),
        grid_spec=pltpu.PrefetchScalarGridSpec(
            num_scalar_prefetch=2,
            grid=(n_steps,),
            in_specs=[smem, smem, pl.BlockSpec((MOE_TOK * s_tiles, HEAD_DIM), lambda i, gs, ge: (i, 0))],
            out_specs=pl.BlockSpec(memory_space=pl.ANY),
            scratch_shapes=[pltpu.VMEM((MOE_TM * s_tiles, HEAD_DIM), h_rows.dtype), pltpu.SemaphoreType.DMA(())],
        ),
        out_shape=jax.ShapeDtypeStruct((n_rows * s_tiles, HEAD_DIM), h_rows.dtype),
        compiler_params=_cparams(("arbitrary",)),
        name="moe_scatter",
    )(gstart, gend, idx, pos, h_rows)


def _moe_expert_kernel(texp_ref, tslot_ref, tnext_ref, nvalid_ref, xs_ref, wg_hbm, wu_hbm, wd_hbm, ys_ref,
                       wg_buf, wu_buf, wd_buf, wgb_ref, wub_ref, wdb_ref, sems):
    i = pl.program_id(0)
    d = wgb_ref.shape[0]
    valid = i < nvalid_ref[0]
    new_expert = (i == 0) | (texp_ref[i] != texp_ref[jnp.maximum(i - 1, 0)])
    slot = tslot_ref[i]

    def fetch(expert, s):
        return [pltpu.make_async_copy(src.at[expert], dst.at[s], sems.at[s, n])
                for n, (src, dst) in enumerate(((wg_hbm, wg_buf), (wu_hbm, wu_buf), (wd_hbm, wd_buf)))]

    @pl.when(i == 0)
    def _():
        for cp in fetch(texp_ref[0], slot):
            cp.start()

    @pl.when(valid & new_expert)
    def _():
        nxt = tnext_ref[i]

        @pl.when(nxt < nvalid_ref[0])
        def _():
            for cp in fetch(texp_ref[nxt], 1 - slot):
                cp.start()

        for cp in fetch(texp_ref[i], slot):
            cp.wait()
        wgb_ref[...] = wg_buf[slot].astype(BF16)
        wub_ref[...] = wu_buf[slot].astype(BF16)
        wdb_ref[...] = wd_buf[slot].astype(BF16)

    @pl.when(valid)
    def _():
        half = d // 2
        lo, hi = _unpack_bf16_pairs(_load_token_rows(xs_ref, MOE_TM, half))
        lo, hi = lo.astype(BF16), hi.astype(BF16)
        gate = _dot(lo, wgb_ref[0:half, :]) + _dot(hi, wgb_ref[half:d, :])
        up = _dot(lo, wub_ref[0:half, :]) + _dot(hi, wub_ref[half:d, :])
        y = _dot((_silu(gate) * up).astype(BF16), wdb_ref[...])
        _store_token_rows(ys_ref, _pack_bf16_pairs(y))

    @pl.when(jnp.logical_not(valid))
    def _():
        ys_ref[...] = jnp.zeros_like(ys_ref)


def _moe_experts(xs, tile_expert, tile_slot, tile_next, nvalid, wg, wu, wd):
    n_exp, d, ff = wg.shape
    blk = xs.shape[0] // tile_expert.shape[0]
    hbm = pl.BlockSpec(memory_space=pl.ANY)
    return pl.pallas_call(
        _moe_expert_kernel,
        grid_spec=pltpu.PrefetchScalarGridSpec(
            num_scalar_prefetch=4,
            grid=(xs.shape[0] // blk,),
            in_specs=[
                pl.BlockSpec((blk, HEAD_DIM), lambda i, te, ts, tn, nv: (jnp.minimum(i, nv[0] - 1), 0)),
                hbm, hbm, hbm,
            ],
            out_specs=pl.BlockSpec((blk, HEAD_DIM), lambda i, te, ts, tn, nv: (i, 0)),
            scratch_shapes=[pltpu.VMEM((2, d, ff), wg.dtype), pltpu.VMEM((2, d, ff), wu.dtype),
                            pltpu.VMEM((2, ff, d), wd.dtype),
                            pltpu.VMEM((d, ff), BF16), pltpu.VMEM((d, ff), BF16), pltpu.VMEM((ff, d), BF16),
                            pltpu.SemaphoreType.DMA((2, 3))],
        ),
        out_shape=jax.ShapeDtypeStruct(xs.shape, xs.dtype),
        compiler_params=_cparams(("arbitrary",)),
        name="moe_experts",
    )(tile_expert, tile_slot, tile_next, nvalid, xs, wg, wu, wd)


def _moe_combine_kernel(gstart_ref, idx_ref, pos_ref, w_ref, ys_hbm, x1_ref, gate_ref, nw_ref, o_ref,
                        buf_ref, lo_ref, hi_ref, sem, *, s_tiles):
    tm, d = x1_ref.shape

    def row_copy(n):
        row = gstart_ref[idx_ref[n]] + pos_ref[n]
        return pltpu.make_async_copy(_token_row(ys_hbm, row, s_tiles), _token_row(buf_ref, n, s_tiles), sem)

    _for_each(tm * TOP_K, lambda n: row_copy(n).start())
    _for_each(tm * TOP_K, lambda n: row_copy(n).wait())

    def token(t):
        n0 = t * TOP_K
        lo, hi = _unpack_bf16_pairs(_token_row(buf_ref, n0, s_tiles)[...])
        acc_lo, acc_hi = w_ref[n0] * lo, w_ref[n0] * hi
        for j in range(1, TOP_K):
            lo, hi = _unpack_bf16_pairs(_token_row(buf_ref, n0 + j, s_tiles)[...])
            acc_lo, acc_hi = acc_lo + w_ref[n0 + j] * lo, acc_hi + w_ref[n0 + j] * hi
        _token_row(lo_ref, t, s_tiles)[...] = acc_lo
        _token_row(hi_ref, t, s_tiles)[...] = acc_hi

    _for_each(tm, token)
    routed = jnp.concatenate([_load_token_rows(lo_ref, tm, d // 2), _load_token_rows(hi_ref, tm, d // 2)], axis=1)
    x2 = x1_ref[...] + gate_ref[...] * routed
    o_ref[...] = x2 * lax.rsqrt(jnp.mean(x2 * x2, axis=-1, keepdims=True) + EPS) * nw_ref[...]


def _moe_combine(ys, idx, pos, w, gstart, x1, mod4, norm_w):
    bsz, t, d = x1.shape
    s_tiles = d // 2 // HEAD_DIM
    tm = MOE_TOK
    nt = t // tm
    smem = pl.BlockSpec((tm * TOP_K,), lambda b, i, gs: (b * nt + i,), memory_space=pltpu.SMEM)
    row = pl.BlockSpec((None, tm, d), lambda b, i, gs: (b, i, 0))
    return pl.pallas_call(
        functools.partial(_moe_combine_kernel, s_tiles=s_tiles),
        grid_spec=pltpu.PrefetchScalarGridSpec(
            num_scalar_prefetch=1,
            grid=(bsz, nt),
            in_specs=[
                smem, smem, smem,
                pl.BlockSpec(memory_space=pl.ANY),
                row,
                pl.BlockSpec((None, None, 1, d), lambda b, i, gs: (b, 5, 0, 0)),
                pl.BlockSpec((1, d), lambda b, i, gs: (0, 0)),
            ],
            out_specs=row,
            scratch_shapes=[pltpu.VMEM((tm * TOP_K * s_tiles, HEAD_DIM), ys.dtype),
                            pltpu.VMEM((tm * s_tiles, HEAD_DIM), F32), pltpu.VMEM((tm * s_tiles, HEAD_DIM), F32),
                            pltpu.SemaphoreType.DMA(())],
        ),
        out_shape=jax.ShapeDtypeStruct((bsz, t, d), F32),
        compiler_params=_cparams(("arbitrary", "arbitrary")),
        name="moe_combine",
    )(gstart, idx, pos, w, ys, x1, mod4, norm_w.reshape(1, d))


def _moe(h_rows, route, counts, x1, mod4, norm_w, wg, wu, wd):
    bsz, t, d = x1.shape
    n_exp = wg.shape[0]
    s_tiles = h_rows.shape[0] // (bsz * t)
    idx = route[..., 0:TOP_K].astype(jnp.int32).reshape(-1)
    w = route[..., TOP_K:2 * TOP_K].reshape(-1)
    pos = route[..., 2 * TOP_K:3 * TOP_K].astype(jnp.int32).reshape(-1)
    padded = (counts.reshape(n_exp).astype(jnp.int32) + MOE_TM - 1) // MOE_TM * MOE_TM
    gend = jnp.cumsum(padded)
    gstart = gend - padded
    nvalid = gend[-1:] // MOE_TM
    n_tiles = bsz * t * TOP_K // MOE_TM + n_exp
    tile_start = jnp.minimum(jnp.arange(n_tiles, dtype=jnp.int32), nvalid - 1) * MOE_TM
    tile_expert = jnp.sum((gend[None, :] <= tile_start[:, None]).astype(jnp.int32), axis=1)
    first_of_group = jnp.concatenate([jnp.ones((1,), jnp.int32),
                                      (tile_expert[1:] != tile_expert[:-1]).astype(jnp.int32)])
    tile_slot = (jnp.cumsum(first_of_group) - 1) % 2
    experts = jnp.arange(n_exp, dtype=jnp.int32)
    tile_next = jnp.sum(jnp.where(tile_expert[:, None] == experts[None, :], gend[None, :], 0), axis=1) // MOE_TM
    xs = _moe_scatter(h_rows, idx, pos, gstart, gend, n_tiles * MOE_TM, s_tiles)
    ys = _moe_experts(xs, tile_expert, tile_slot, tile_next, nvalid, wg, wu, wd)
    return _moe_combine(ys, idx, pos, w, gstart, x1, mod4, norm_w)


def _reorder_w_in(w_in):
    d = w_in.shape[0]
    o_beta = 3 * GDN_WIDTH
    o_a = o_beta + GDN_HEADS
    o_gate = o_a + GDN_HEADS
    o_q = o_gate + GDN_WIDTH
    o_kv = o_q + NSA_WIDTH
    o_ng = o_kv + 6 * NSA_KV_WIDTH
    n_small = 2 * GDN_HEADS + 3 * NSA_HEADS
    parts = [w_in[:, :o_beta], w_in[:, o_gate:o_q], w_in[:, o_q:o_kv], w_in[:, o_kv:o_ng],
             w_in[:, o_beta:o_gate], w_in[:, o_ng:o_ng + 3 * NSA_HEADS]]
    used = COL_SMALL * HEAD_DIM + n_small
    parts.append(jnp.zeros((d, PROJ_COLS - used), w_in.dtype))
    return jnp.concatenate(parts, axis=1).astype(BF16)


def _layer(x, mod, norm_attn_w, norm_ffn_w, w_in, gdn_conv_w, gdn_a_log, gdn_dt_bias, gdn_norm_w,
           cmp_pe_k, cmp_w1_k, cmp_w2_k, cmp_pe_v, cmp_w1_v, cmp_w2_v, w_out, router_w, router_bias,
           expert_w_gate, expert_w_up, expert_w_down, shared_w_gate, shared_w_up, shared_w_down, final_w):
    bsz, t, d = x.shape
    mod4 = mod.reshape(bsz, N_MOD, 1, d)
    proj = _proj(x, mod4, norm_attn_w, _reorder_w_in(w_in))
    y_gdn = _gdn(proj, gdn_conv_w, gdn_a_log, gdn_dt_bias, gdn_norm_w)
    kc = _compress(proj, COL_KV, cmp_pe_k, cmp_w1_k, cmp_w2_k)
    vc = _compress(proj, COL_KV + NSA_KV_HEADS, cmp_pe_v, cmp_w1_v, cmp_w2_v)
    o_cmp, sel = _cmpattn(proj, kc, vc)
    o_sel = _selattn(proj, sel)
    y_nsa = _winattn(proj, o_cmp, o_sel)
    x1, h_rows, route, counts = _outproj(
        y_gdn, y_nsa, x, w_out.astype(BF16), mod4, norm_ffn_w, router_w, router_bias,
        shared_w_gate.astype(BF16), shared_w_up.astype(BF16), shared_w_down.astype(BF16))
    return _moe(h_rows, route, counts, x1, mod4, final_w, expert_w_gate, expert_w_up, expert_w_down)


def kernel(x, c, w_ada, b_ada, norm_attn_w, norm_ffn_w, norm_final_w, w_in, gdn_conv_w, gdn_a_log, gdn_dt_bias,
           gdn_norm_w, cmp_pe_k, cmp_w1_k, cmp_w2_k, cmp_pe_v, cmp_w1_v, cmp_w2_v, w_out, router_w, router_bias,
           expert_w_gate, expert_w_up, expert_w_down, shared_w_gate, shared_w_up, shared_w_down):
    depth = w_ada.shape[0]
    assert depth == 1, "the fused final norm assumes a single layer"
    l = 0
    mod = _ada(c, w_ada[l], b_ada[l])
    return _layer(x, mod, norm_attn_w[l], norm_ffn_w[l], w_in[l], gdn_conv_w[l], gdn_a_log[l], gdn_dt_bias[l],
                  gdn_norm_w[l], cmp_pe_k[l], cmp_w1_k[l], cmp_w2_k[l], cmp_pe_v[l], cmp_w1_v[l], cmp_w2_v[l],
                  w_out[l], router_w[l], router_bias[l], expert_w_gate[l], expert_w_up[l], expert_w_down[l],
                  shared_w_gate[l], shared_w_up[l], shared_w_down[l], norm_final_w)
```

```python
import functools

import jax
import jax.numpy as jnp
from jax import lax
from jax.experimental import pallas as pl
from jax.experimental.pallas import tpu as pltpu

F32 = jnp.float32
BF16 = jnp.bfloat16
HI = lax.Precision.HIGHEST

HEAD_DIM = 128
GDN_HEADS = 8
NSA_HEADS = 8
NSA_KV_HEADS = 2
NSA_GROUP = NSA_HEADS // NSA_KV_HEADS
GDN_WIDTH = GDN_HEADS * HEAD_DIM
NSA_WIDTH = NSA_HEADS * HEAD_DIM
NSA_KV_WIDTH = NSA_KV_HEADS * HEAD_DIM
CONV_WIDTH = 4
GDN_CHUNK = 64
CMP_BLOCK = 32
CMP_STRIDE = 16
SEL_BLOCK = 64
SEL_TOPK = 16
N_LOCAL_BLOCKS = 2
WINDOW = 512
Q_BLOCK = 128
TOP_K = 8
ROUTED_SCALE = 2.5
N_MOD = 6
EPS = 1e-6

COL_GDN_Q = 0
COL_GDN_K = GDN_HEADS
COL_GDN_V = 2 * GDN_HEADS
COL_GDN_GATE = 3 * GDN_HEADS
COL_NSA_Q = 4 * GDN_HEADS
COL_KV = COL_NSA_Q + NSA_HEADS
COL_SMALL = COL_KV + 6 * NSA_KV_HEADS
N_COL_BLOCKS = COL_SMALL + 1
PROJ_TN = 2304
PROJ_COLS = 3 * PROJ_TN
SMALL_BETA = 0
SMALL_A = GDN_HEADS
SMALL_NSA_GATE = 2 * GDN_HEADS

NEG_BIG = -1e30
VMEM_LIMIT = 48 * 1024 * 1024


def _cparams(semantics, vmem_limit=VMEM_LIMIT):
    return pltpu.CompilerParams(dimension_semantics=semantics, vmem_limit_bytes=vmem_limit)


def _silu(x):
    return (0.5 * x) * (1.0 + jnp.tanh(0.5 * x))


def _dot(a, b, precision=None):
    return jnp.dot(a, b, preferred_element_type=F32, precision=precision)


def _dot_nt(a, b, precision=None):
    return lax.dot_general(a, b, (((1,), (1,)), ((), ())), preferred_element_type=F32, precision=precision)


def _dot_tn(a, b, precision=None):
    return lax.dot_general(a, b, (((0,), (0,)), ((), ())), preferred_element_type=F32, precision=precision)


def _split_bf16(x):
    hi = x.astype(BF16)
    return hi, (x - hi.astype(F32)).astype(BF16)


def _dot_split(a, b, nt=False):
    (ah, al), (bh, bl) = _split_bf16(a), _split_bf16(b)
    f = _dot_nt if nt else _dot
    return f(ah, bh) + (f(ah, bl) + f(al, bh))


def _lane_pick(x, idx):
    lane = lax.broadcasted_iota(jnp.int32, x.shape, 1)
    return jnp.sum(jnp.where(lane == idx, x, 0.0), axis=-1, keepdims=True)


def _ada_kernel(ct_ref, w_ref, b_ref, o_ref):
    c = ct_ref[...]
    cond = _silu(c)
    w = w_ref[...]
    rows = [jnp.sum(w * cond[:, b:b + 1], axis=0, keepdims=True) for b in range(c.shape[1])]
    o_ref[...] = jnp.concatenate(rows, axis=0) + b_ref[...]


def _ada(c, w_ada, b_ada, tn=512):
    bsz, d = c.shape
    n = w_ada.shape[1]
    return pl.pallas_call(
        _ada_kernel,
        grid=(n // tn,),
        in_specs=[
            pl.BlockSpec((d, bsz), lambda j: (0, 0)),
            pl.BlockSpec((d, tn), lambda j: (0, j)),
            pl.BlockSpec((1, tn), lambda j: (0, j)),
        ],
        out_specs=pl.BlockSpec((bsz, tn), lambda j: (0, j)),
        out_shape=jax.ShapeDtypeStruct((bsz, n), F32),
        compiler_params=_cparams(("arbitrary",)),
        name="ada",
    )(c.T, w_ada, b_ada.reshape(1, n))


def _modulated_norm(x, nw, scale, shift):
    var = jnp.mean(x * x, axis=-1, keepdims=True)
    h = x * lax.rsqrt(var + EPS) * nw
    return h * (1.0 + scale) + shift


def _proj_kernel(x_ref, shift_ref, scale_ref, nw_ref, w_ref, o_ref):
    h = _modulated_norm(x_ref[...], nw_ref[...], scale_ref[...], shift_ref[...])
    o_ref[...] = _dot(h.astype(BF16), w_ref[...])


def _proj(x, mod4, norm_w, w_r, tm=256):
    bsz, t, d = x.shape
    n = w_r.shape[1]
    return pl.pallas_call(
        _proj_kernel,
        grid=(n // PROJ_TN, bsz, t // tm),
        in_specs=[
            pl.BlockSpec((None, tm, d), lambda j, b, i: (b, i, 0)),
            pl.BlockSpec((None, None, 1, d), lambda j, b, i: (b, 0, 0, 0)),
            pl.BlockSpec((None, None, 1, d), lambda j, b, i: (b, 1, 0, 0)),
            pl.BlockSpec((1, d), lambda j, b, i: (0, 0)),
            pl.BlockSpec((d, PROJ_TN), lambda j, b, i: (0, j)),
        ],
        out_specs=pl.BlockSpec((None, tm, PROJ_TN), lambda j, b, i: (b, i, j)),
        out_shape=jax.ShapeDtypeStruct((bsz, t, n), F32),
        compiler_params=_cparams(("arbitrary", "arbitrary", "arbitrary")),
        name="proj",
    )(x, mod4, mod4, norm_w.reshape(1, d), w_r)


def _gdn_kernel(q_ref, k_ref, v_ref, gate_ref, small_ref, hq_ref, hk_ref, hv_ref,
                cwq_ref, cwk_ref, cwv_ref, alog_ref, dtb_ref, nw_ref, o_ref, s_ref, xc_ref, *, tt, hg):
    first = pl.program_id(2) == 0

    @pl.when(first)
    def _():
        s_ref[...] = jnp.zeros_like(s_ref)

    def conv_silu(x_ref, halo_ref, cw_ref):
        xc_ref[0:8, :] = jnp.where(first, 0.0, halo_ref[...])
        xc_ref[8:tt + 8, :] = x_ref[...]
        cw = cw_ref[...]
        off = 8 - (CONV_WIDTH - 1)
        y = cw[0:1, :] * xc_ref[off:off + tt, :]
        for i in range(1, CONV_WIDTH):
            y = y + cw[i:i + 1, :] * xc_ref[off + i:off + i + tt, :]
        return _silu(y)

    q_all = conv_silu(q_ref, hq_ref, cwq_ref)
    k_all = conv_silu(k_ref, hk_ref, cwk_ref)
    v_all = conv_silu(v_ref, hv_ref, cwv_ref)
    ri = lax.broadcasted_iota(jnp.int32, (tt, tt), 0)
    ci = lax.broadcasted_iota(jnp.int32, (tt, tt), 1)
    lower = jnp.where(((ri // GDN_CHUNK) == (ci // GDN_CHUNK)) & (ci <= ri), 1.0, 0.0).astype(BF16)
    heads = [pl.program_id(1) * hg + hh for hh in range(hg)]
    outs = _gdn_heads(q_all, k_all, v_all, small_ref[...], lower, heads, alog_ref[...], dtb_ref[...], s_ref, tt)
    for hh, o in enumerate(outs):
        cols = slice(hh * HEAD_DIM, (hh + 1) * HEAD_DIM)
        o = o * lax.rsqrt(jnp.mean(o * o, axis=-1, keepdims=True) + EPS) * nw_ref[...]
        o_ref[:, cols] = (o * _silu(gate_ref[:, cols])).astype(o_ref.dtype)


def _gdn_heads(q_all, k_all, v_all, small, lower, heads, alog, dtb, s_ref, tt):
    c = GDN_CHUNK
    nch = tt // c
    hg = len(heads)
    kpos = lax.broadcasted_iota(jnp.int32, (tt, c), 0) % c
    jj = lax.broadcasted_iota(jnp.int32, (tt, c), 1)
    per_head = []
    for hh, h in enumerate(heads):
        cols = slice(hh * HEAD_DIM, (hh + 1) * HEAD_DIM)
        q, k, v = q_all[:, cols], k_all[:, cols], v_all[:, cols]
        q = q * lax.rsqrt(jnp.sum(q * q, axis=-1, keepdims=True) + EPS) * (HEAD_DIM ** -0.5)
        k = k * lax.rsqrt(jnp.sum(k * k, axis=-1, keepdims=True) + EPS)
        beta = jax.nn.sigmoid(_lane_pick(small, SMALL_BETA + h))
        z = _lane_pick(small, SMALL_A + h) + _lane_pick(dtb, h)
        softplus = jnp.maximum(z, 0.0) + jnp.log(1.0 + jnp.exp(-jnp.abs(z)))
        g = -jnp.exp(_lane_pick(alog, h)) * softplus
        g_c = jnp.broadcast_to(g, (tt, c))
        rhs = jnp.concatenate([g_c, jnp.where(kpos > jj, g_c, 0.0)], axis=1)
        r_hi = rhs.astype(BF16)
        rem = rhs - r_hi.astype(F32)
        r_mid = rem.astype(BF16)
        r_lo = (rem - r_mid.astype(F32)).astype(BF16)
        sums = _dot(lower, r_hi) + (_dot(lower, r_mid) + _dot(lower, r_lo))
        gcum = jnp.broadcast_to(sums[:, 0:1], (tt, HEAD_DIM))
        gc3 = gcum.reshape(nch, c, HEAD_DIM)
        grest = (gc3[:, c - 1:c, :] - gc3).reshape(tt, HEAD_DIM)
        decay_h = jnp.where(kpos >= jj, jnp.exp(sums[:, c:2 * c]), 0.0)
        per_head.append((q, k, v * beta, k * beta, jnp.exp(gcum), jnp.exp(grest), decay_h))

    def stack(j):
        x = jnp.concatenate([ph[j] for ph in per_head], axis=0)
        return x.reshape(hg * nch, c, x.shape[-1])

    q, k, vb, kb, egc, egr, decay = (stack(j) for j in range(7))
    kpos_all = lax.broadcasted_iota(jnp.int32, (hg * tt, c), 0) % c
    jj_all = lax.broadcasted_iota(jnp.int32, (hg * tt, c), 1)
    strict = (kpos_all > jj_all).reshape(hg * nch, c, c)
    eye = jnp.where(kpos_all == jj_all, 1.0, 0.0).reshape(hg * nch, c, c)

    def bmm(a, b):
        return jnp.einsum('nij,njk->nik', a, b, preferred_element_type=F32)

    def bmm_nt(a, b):
        return jnp.einsum('nid,njd->nij', a, b, preferred_element_type=F32)

    def bmm_split(a, b):
        (ah, al), (bh, bl) = _split_bf16(a), _split_bf16(b)
        return bmm(ah, bh) + (bmm(ah, bl) + bmm(al, bh))

    k3 = k.astype(BF16)
    a = jnp.where(strict, bmm_nt(kb.astype(BF16), k3) * decay, 0.0)
    inv = eye - a
    p = bmm_split(a, a)
    n_sq = (c - 1).bit_length() - 1
    for lvl in range(n_sq):
        inv = inv + bmm_split(inv, p)
        if lvl + 1 < n_sq:
            p = bmm_split(p, p)
    inv_b = inv.astype(BF16)
    u3 = bmm(inv_b, vb.astype(BF16)).astype(BF16)
    w3 = bmm(inv_b, (kb * egc).astype(BF16)).astype(BF16)
    qk3 = (bmm_nt(q.astype(BF16), k3) * decay).astype(BF16)
    kd3 = (k * egr).astype(BF16)
    qe3 = (q * egc - bmm(qk3, w3)).astype(BF16)
    o03 = bmm(qk3, u3)
    kw = [_dot_tn(kd3[i], w3[i]).astype(BF16) for i in range(hg * nch)]
    ku = [_dot_tn(kd3[i], u3[i]) for i in range(hg * nch)]

    states = [s_ref[hh] for hh in range(hg)]
    outs = [[] for _ in range(hg)]
    for n in range(nch):
        for hh in range(hg):
            i = hh * nch + n
            state_b = states[hh].astype(BF16)
            outs[hh].append(_dot(qe3[i], state_b) + o03[i])
            states[hh] = states[hh] * egc[i, c - 1:c, :] - _dot(kw[i], state_b) + ku[i]
    for hh in range(hg):
        s_ref[hh] = states[hh]
    return [jnp.concatenate(o, axis=0) for o in outs]


def _gdn(proj, conv_w, a_log, dt_bias, norm_w, tt=256, hg=4):
    bsz, t, _ = proj.shape
    wdt = hg * HEAD_DIM

    def col(base):
        return pl.BlockSpec((None, tt, wdt), lambda b, h, i: (b, i, base // hg + h))

    def halo(base):
        return pl.BlockSpec((None, 8, wdt), lambda b, h, i: (b, jnp.maximum(i * (tt // 8) - 1, 0), base // hg + h))

    def cw(base):
        return pl.BlockSpec((CONV_WIDTH, wdt), lambda b, h, i: (0, base // hg + h))

    heads = pl.BlockSpec((1, GDN_HEADS), lambda b, h, i: (0, 0))
    return pl.pallas_call(
        functools.partial(_gdn_kernel, tt=tt, hg=hg),
        grid=(bsz, GDN_HEADS // hg, t // tt),
        in_specs=[
            col(COL_GDN_Q), col(COL_GDN_K), col(COL_GDN_V), col(COL_GDN_GATE),
            pl.BlockSpec((None, tt, HEAD_DIM), lambda b, h, i: (b, i, COL_SMALL)),
            halo(COL_GDN_Q), halo(COL_GDN_K), halo(COL_GDN_V),
            cw(COL_GDN_Q), cw(COL_GDN_K), cw(COL_GDN_V),
            heads, heads,
            pl.BlockSpec((1, HEAD_DIM), lambda b, h, i: (0, 0)),
        ],
        out_specs=pl.BlockSpec((None, tt, wdt), lambda b, h, i: (b, i, h)),
        out_shape=jax.ShapeDtypeStruct((bsz, t, GDN_WIDTH), BF16),
        scratch_shapes=[pltpu.VMEM((hg, HEAD_DIM, HEAD_DIM), F32), pltpu.VMEM((tt + 8, wdt), F32)],
        compiler_params=_cparams(("arbitrary", "arbitrary", "arbitrary")),
        name="gdn",
    )(proj, proj, proj, proj, proj, proj, proj, proj, conv_w, conv_w, conv_w,
      a_log.reshape(1, GDN_HEADS), dt_bias.reshape(1, GDN_HEADS), norm_w.reshape(1, HEAD_DIM))


def _compress_kernel(x_ref, pe_ref, w1_ref, w2_ref, o_ref):
    r = x_ref.shape[0] // CMP_STRIDE
    d = x_ref.shape[1]
    a = b = None
    for l in range(CMP_STRIDE):
        x_l = x_ref[pl.ds(l, r, stride=CMP_STRIDE), :]
        a_l = _dot_split(x_l + pe_ref[l:l + 1, :], w1_ref[l * d:(l + 1) * d, :])
        b_l = _dot_split(x_l + pe_ref[CMP_STRIDE + l:CMP_STRIDE + l + 1, :],
                         w1_ref[(CMP_STRIDE + l) * d:(CMP_STRIDE + l + 1) * d, :])
        a, b = (a_l, b_l) if a is None else (a + a_l, b + b_l)
    hid = _silu(a + pltpu.roll(b, r - 1, 0))
    o_ref[...] = _dot_split(hid, w2_ref[...])


def _compress(proj, col_block, pe, w1, w2):
    bsz, t, _ = proj.shape
    r = t // CMP_STRIDE
    hid = w1.shape[1]
    return pl.pallas_call(
        _compress_kernel,
        grid=(bsz, NSA_KV_HEADS),
        in_specs=[
            pl.BlockSpec((None, t, HEAD_DIM), lambda b, h: (b, 0, col_block + h)),
            pl.BlockSpec((CMP_BLOCK, HEAD_DIM), lambda b, h: (0, 0)),
            pl.BlockSpec((CMP_BLOCK * HEAD_DIM, hid), lambda b, h: (0, 0)),
            pl.BlockSpec((hid, HEAD_DIM), lambda b, h: (0, 0)),
        ],
        out_specs=pl.BlockSpec((None, None, r, HEAD_DIM), lambda b, h: (b, h, 0, 0)),
        out_shape=jax.ShapeDtypeStruct((bsz, NSA_KV_HEADS, r, HEAD_DIM), F32),
        compiler_params=_cparams(("arbitrary", "arbitrary")),
        name="compress",
    )(proj, pe, w1, w2)


def _cmpattn_kernel(q_ref, kc_ref, vc_ref, o_ref, sel_ref, *, tq, nc, nsel):
    ncp = kc_ref.shape[0]
    scale = HEAD_DIM ** -0.5
    t0 = pl.program_id(2) * tq
    kc = kc_ref[...]
    vc = vc_ref[...]
    pos_r = t0 + lax.broadcasted_iota(jnp.int32, (tq, ncp), 0)
    c_r = lax.broadcasted_iota(jnp.int32, (tq, ncp), 1)
    valid_r = (c_r * CMP_STRIDE + CMP_BLOCK - 1 <= pos_r) & (c_r < nc)
    pos_c = t0 + lax.broadcasted_iota(jnp.int32, (ncp, tq), 1)
    c_c = lax.broadcasted_iota(jnp.int32, (ncp, tq), 0)
    valid_c = (c_c * CMP_STRIDE + CMP_BLOCK - 1 <= pos_c) & (c_c < nc)
    psum = jnp.zeros((ncp, tq), F32)
    kc_hi, kc_lo = _split_bf16(kc)
    vc_b = vc.astype(BF16)
    for g in range(NSA_GROUP):
        q_hi, q_lo = _split_bf16(q_ref[:, g * HEAD_DIM:(g + 1) * HEAD_DIM])
        s = _dot_nt(q_hi, kc_hi) + (_dot_nt(q_hi, kc_lo) + _dot_nt(q_lo, kc_hi))
        s = jnp.where(valid_r, s * scale, NEG_BIG)
        e = jnp.where(valid_r, jnp.exp(s - jnp.max(s, axis=-1, keepdims=True)), 0.0)
        p = e * (1.0 / jnp.maximum(jnp.sum(e, axis=-1, keepdims=True), 1e-30))
        o_ref[:, g * HEAD_DIM:(g + 1) * HEAD_DIM] = _dot(p.astype(BF16), vc_b)
        st = _dot_nt(kc_hi, q_hi) + (_dot_nt(kc_lo, q_hi) + _dot_nt(kc_hi, q_lo))
        st = jnp.where(valid_c, st * scale, NEG_BIG)
        et = jnp.where(valid_c, jnp.exp(st - jnp.max(st, axis=0, keepdims=True)), 0.0)
        psum = psum + et * (1.0 / jnp.maximum(jnp.sum(et, axis=0, keepdims=True), 1e-30))
    j_o = lax.broadcasted_iota(jnp.int32, (nsel, ncp), 0) * SEL_BLOCK
    c_o = lax.broadcasted_iota(jnp.int32, (nsel, ncp), 1) * CMP_STRIDE
    overlap = jnp.where((c_o < j_o + SEL_BLOCK) & (c_o + CMP_BLOCK > j_o), 1.0, 0.0).astype(BF16)
    p_hi, p_lo = _split_bf16(psum)
    imp = _dot(overlap, p_hi) + _dot(overlap, p_lo)
    blk = lax.broadcasted_iota(jnp.int32, (nsel, tq), 0)
    dist = (t0 + lax.broadcasted_iota(jnp.int32, (nsel, tq), 1)) // SEL_BLOCK - blk
    forced = (blk == 0) | ((dist >= 0) & (dist < N_LOCAL_BLOCKS))
    val = jnp.where(forced, -NEG_BIG, jnp.where(dist >= 0, imp, -1.0))
    rank = jnp.zeros((nsel, tq), F32)
    for j in range(nsel):
        row = val[j:j + 1, :]
        rank = rank + jnp.where((row > val) | ((row == val) & (blk > j)), 1.0, 0.0)
    sel = jnp.where((rank < min(SEL_TOPK, nsel)) & (dist >= 0), 1.0, 0.0)
    sel_ref[...] = sel.T


def _cmpattn(proj, kc, vc, tq=256):
    bsz, t, _ = proj.shape
    ncp = kc.shape[2]
    nc = (t - CMP_BLOCK) // CMP_STRIDE + 1
    nsel = t // SEL_BLOCK
    gw = NSA_GROUP * HEAD_DIM
    return pl.pallas_call(
        functools.partial(_cmpattn_kernel, tq=tq, nc=nc, nsel=nsel),
        grid=(bsz, NSA_KV_HEADS, t // tq),
        in_specs=[
            pl.BlockSpec((None, tq, gw), lambda b, h, i: (b, i, COL_NSA_Q // NSA_GROUP + h)),
            pl.BlockSpec((None, None, ncp, HEAD_DIM), lambda b, h, i: (b, h, 0, 0)),
            pl.BlockSpec((None, None, ncp, HEAD_DIM), lambda b, h, i: (b, h, 0, 0)),
        ],
        out_specs=[
            pl.BlockSpec((None, tq, gw), lambda b, h, i: (b, i, h)),
            pl.BlockSpec((None, None, tq, nsel), lambda b, h, i: (b, h, i, 0)),
        ],
        out_shape=[
            jax.ShapeDtypeStruct((bsz, t, NSA_WIDTH), F32),
            jax.ShapeDtypeStruct((bsz, NSA_KV_HEADS, t, nsel), F32),
        ],
        compiler_params=_cparams(("arbitrary", "arbitrary", "arbitrary")),
        name="cmpattn",
    )(proj, kc, vc)


def _selattn_kernel(q_ref, k_ref, v_ref, sel_ref, o_ref, m_ref, l_ref, acc_ref, *, tq, tk, nsel):
    g = NSA_GROUP
    qi = pl.program_id(2)
    m_ref[...] = jnp.full_like(m_ref, NEG_BIG)
    l_ref[...] = jnp.zeros_like(l_ref)
    acc_ref[...] = jnp.zeros_like(acc_ref)
    q4 = jnp.concatenate([q_ref[:, i * HEAD_DIM:(i + 1) * HEAD_DIM] for i in range(g)], axis=0)
    q4 = (q4 * HEAD_DIM ** -0.5).astype(BF16)
    sel = sel_ref[...].astype(BF16)
    pos = qi * tq + lax.broadcasted_iota(jnp.int32, (tq, tk), 0)

    def kv_step(kv, carry):
        k0 = pl.multiple_of(kv * tk, tk)
        s = _dot_nt(q4, k_ref[pl.ds(k0, tk), :].astype(BF16))
        key = k0 + lax.broadcasted_iota(jnp.int32, (nsel, tk), 1)
        blk = lax.broadcasted_iota(jnp.int32, (nsel, tk), 0)
        expand = jnp.where(key // SEL_BLOCK == blk, 1.0, 0.0).astype(BF16)
        picked = _dot(sel, expand)
        kpos = k0 + lax.broadcasted_iota(jnp.int32, (tq, tk), 1)
        mask = ((picked > 0.5) & (kpos <= pos))[None]
        s = jnp.where(mask, s.reshape(g, tq, tk), NEG_BIG)
        m_prev = m_ref[...]
        m_new = jnp.maximum(m_prev, jnp.max(s, axis=-1, keepdims=True))
        p = jnp.exp(s - m_new)
        alpha = jnp.exp(m_prev - m_new)
        l_ref[...] = alpha * l_ref[...] + jnp.sum(p, axis=-1, keepdims=True)
        pv = _dot(p.reshape(g * tq, tk).astype(BF16), v_ref[pl.ds(k0, tk), :].astype(BF16))
        acc_ref[...] = alpha * acc_ref[...] + pv.reshape(g, tq, HEAD_DIM)
        m_ref[...] = m_new
        return carry

    lax.fori_loop(0, (qi * tq + tq - 1) // tk + 1, kv_step, 0)
    o = acc_ref[...] * (1.0 / jnp.maximum(l_ref[...], 1e-30))
    for i in range(g):
        o_ref[:, i * HEAD_DIM:(i + 1) * HEAD_DIM] = o[i]


def _selattn(proj, sel, tq=Q_BLOCK, tk=512):
    bsz, t, _ = proj.shape
    nsel = t // SEL_BLOCK
    tk = min(tk, t)
    gw = NSA_GROUP * HEAD_DIM

    def kv_spec(base):
        return pl.BlockSpec((None, t, HEAD_DIM), lambda b, h, i: (b, 0, base + h))

    return pl.pallas_call(
        functools.partial(_selattn_kernel, tq=tq, tk=tk, nsel=nsel),
        grid=(bsz, NSA_KV_HEADS, t // tq),
        in_specs=[
            pl.BlockSpec((None, tq, gw), lambda b, h, i: (b, i, COL_NSA_Q // NSA_GROUP + h)),
            kv_spec(COL_KV + 2 * NSA_KV_HEADS), kv_spec(COL_KV + 3 * NSA_KV_HEADS),
            pl.BlockSpec((None, None, tq, nsel), lambda b, h, i: (b, h, i, 0)),
        ],
        out_specs=pl.BlockSpec((None, tq, gw), lambda b, h, i: (b, i, h)),
        out_shape=jax.ShapeDtypeStruct((bsz, t, NSA_WIDTH), F32),
        scratch_shapes=[pltpu.VMEM((NSA_GROUP, tq, 1), F32), pltpu.VMEM((NSA_GROUP, tq, 1), F32),
                        pltpu.VMEM((NSA_GROUP, tq, HEAD_DIM), F32)],
        compiler_params=_cparams(("arbitrary", "arbitrary", "arbitrary")),
        name="selattn",
    )(proj, proj, proj, sel)


def _winattn_kernel(*refs, tq, nblk):
    q_ref = refs[0]
    k_refs = refs[1:1 + nblk]
    v_refs = refs[1 + nblk:1 + 2 * nblk]
    ocmp_ref, osel_ref, small_ref, o_ref = refs[1 + 2 * nblk:]
    g = NSA_GROUP
    hk = pl.program_id(1)
    qi = pl.program_id(2)
    scale = HEAD_DIM ** -0.5
    nk = nblk * tq
    keys = jnp.concatenate([r[...] for r in k_refs], axis=0).astype(BF16)
    vals = jnp.concatenate([r[...] for r in v_refs], axis=0).astype(BF16)
    q4 = jnp.concatenate([q_ref[:, i * HEAD_DIM:(i + 1) * HEAD_DIM] for i in range(g)], axis=0)
    s = (_dot_nt(q4.astype(BF16), keys) * scale).reshape(g, tq, nk)
    pos = qi * tq + lax.broadcasted_iota(jnp.int32, (tq, nk), 0)
    wpos = (qi - (nblk - 1)) * tq + lax.broadcasted_iota(jnp.int32, (tq, nk), 1)
    mask = ((wpos <= pos) & (wpos > pos - WINDOW) & (wpos >= 0))[None]
    s = jnp.where(mask, s, NEG_BIG)
    e = jnp.exp(s - jnp.max(s, axis=-1, keepdims=True)).astype(BF16)
    ov = _dot(e.reshape(g * tq, nk), jnp.concatenate([vals, jnp.ones_like(vals)], axis=1))
    o_win = (ov[:, 0:HEAD_DIM] * (1.0 / jnp.maximum(ov[:, HEAD_DIM:2 * HEAD_DIM], 1e-30))).reshape(g, tq, HEAD_DIM)
    small = small_ref[...]
    for i in range(g):
        head = hk * g + i
        gates = [jax.nn.sigmoid(_lane_pick(small, SMALL_NSA_GATE + j * NSA_HEADS + head)) for j in range(3)]
        cols = slice(i * HEAD_DIM, (i + 1) * HEAD_DIM)
        o = gates[0] * ocmp_ref[:, cols] + gates[1] * osel_ref[:, cols] + gates[2] * o_win[i]
        o_ref[:, cols] = o.astype(o_ref.dtype)


def _winattn(proj, o_cmp, o_sel, tq=Q_BLOCK):
    bsz, t, _ = proj.shape
    nblk = WINDOW // tq + 1
    gw = NSA_GROUP * HEAD_DIM

    def kv_spec(base, d):
        return pl.BlockSpec((None, tq, HEAD_DIM),
                            lambda b, h, i: (b, jnp.maximum(i - (nblk - 1) + d, 0), base + h))

    grp = pl.BlockSpec((None, tq, gw), lambda b, h, i: (b, i, h))
    return pl.pallas_call(
        functools.partial(_winattn_kernel, tq=tq, nblk=nblk),
        grid=(bsz, NSA_KV_HEADS, t // tq),
        in_specs=(
            [pl.BlockSpec((None, tq, gw), lambda b, h, i: (b, i, COL_NSA_Q // NSA_GROUP + h))]
            + [kv_spec(COL_KV + 4 * NSA_KV_HEADS, d) for d in range(nblk)]
            + [kv_spec(COL_KV + 5 * NSA_KV_HEADS, d) for d in range(nblk)]
            + [grp, grp, pl.BlockSpec((None, tq, HEAD_DIM), lambda b, h, i: (b, i, COL_SMALL))]
        ),
        out_specs=grp,
        out_shape=jax.ShapeDtypeStruct((bsz, t, NSA_WIDTH), BF16),
        compiler_params=_cparams(("arbitrary", "arbitrary", "arbitrary")),
        name="winattn",
    )(proj, *([proj] * (2 * nblk)), o_cmp, o_sel, proj)


def _ffn(h, wg, wu, wd):
    act = _silu(_dot(h, wg)) * _dot(h, wu)
    return _dot(act.astype(BF16), wd)


def _pack_bf16_pairs(x):
    half = x.shape[1] // 2
    lo = pltpu.bitcast(x[:, :half].astype(BF16).astype(F32), jnp.uint32)
    hi = pltpu.bitcast(x[:, half:].astype(BF16).astype(F32), jnp.uint32)
    return hi | (lo >> 16)


def _unpack_bf16_pairs(p):
    return pltpu.bitcast(p << 16, F32), pltpu.bitcast(p & jnp.uint32(0xFFFF0000), F32)


def _store_token_rows(ref, x):
    n, d = x.shape
    s_tiles = d // HEAD_DIM
    for s in range(s_tiles):
        ref[pl.ds(s, n, stride=s_tiles), :] = x[:, s * HEAD_DIM:(s + 1) * HEAD_DIM]


def _load_token_rows(ref, n, d):
    s_tiles = d // HEAD_DIM
    return jnp.concatenate([ref[pl.ds(s, n, stride=s_tiles), :] for s in range(s_tiles)], axis=1)


def _outproj_kernel(yg_ref, yn_ref, x_ref, wo_ref, gate_ref, shift_ref, scale_ref, gate_m_ref, nw_ref,
                    rw_ref, rb_ref, sg_ref, su_ref, sd_ref, x1_ref, h_ref, route_ref, count_ref, carry_ref):
    tm = x_ref.shape[0]
    half = yg_ref.shape[1]

    @pl.when((pl.program_id(0) == 0) & (pl.program_id(1) == 0))
    def _():
        carry_ref[...] = jnp.zeros_like(carry_ref)

    y = _dot(yg_ref[...], wo_ref[0:half, :]) + _dot(yn_ref[...], wo_ref[half:2 * half, :])
    x1 = x_ref[...] + gate_ref[...] * y
    h = _modulated_norm(x1, nw_ref[...], scale_ref[...], shift_ref[...])
    _store_token_rows(h_ref, _pack_bf16_pairs(h))
    x1_ref[...] = x1 + gate_m_ref[...] * _ffn(h.astype(BF16), sg_ref[...], su_ref[...], sd_ref[...])
    scores = jax.nn.sigmoid(_dot_split(h, rw_ref[...]))
    n_exp = scores.shape[1]
    lane = lax.broadcasted_iota(jnp.int32, scores.shape, 1)
    cand = scores + rb_ref[...]
    picked = jnp.zeros(scores.shape, jnp.bool_)
    firsts = []
    for _ in range(TOP_K):
        best = jnp.max(cand, axis=-1, keepdims=True)
        first = jnp.min(jnp.where(cand == best, lane, n_exp), axis=-1, keepdims=True)
        hit = lane == first
        picked = picked | hit
        cand = jnp.where(hit, NEG_BIG, cand)
        firsts.append(first)
    norm = ROUTED_SCALE / jnp.sum(jnp.where(picked, scores, 0.0), axis=-1, keepdims=True)
    onehot = jnp.where(picked, 1.0, 0.0)
    ri = lax.broadcasted_iota(jnp.int32, (tm, tm), 0)
    ci = lax.broadcasted_iota(jnp.int32, (tm, tm), 1)
    before = jnp.where(ci < ri, 1.0, 0.0).astype(BF16)
    pos = _dot(before, onehot.astype(BF16)) + carry_ref[...]
    carry_ref[...] += jnp.sum(onehot, axis=0, keepdims=True)
    count_ref[...] = carry_ref[...]
    slot = lax.broadcasted_iota(jnp.int32, (tm, 3 * TOP_K), 1)
    route = jnp.zeros((tm, 3 * TOP_K), F32)
    for j, first in enumerate(firsts):
        hit = lane == first
        route = jnp.where(slot == j, first.astype(F32), route)
        route = jnp.where(slot == TOP_K + j, _lane_pick(scores, first) * norm, route)
        route = jnp.where(slot == 2 * TOP_K + j, jnp.sum(jnp.where(hit, pos, 0.0), axis=-1, keepdims=True), route)
    route_ref[...] = route


def _outproj(y_gdn, y_nsa, x, w_out_b, mod4, norm_w, router_w, router_bias, sg, su, sd, tm=256):
    bsz, t, d = x.shape
    half = y_gdn.shape[2]
    n_exp = router_w.shape[1]
    sff = sg.shape[1]
    s_tiles = d // 2 // HEAD_DIM

    def mod(j):
        return pl.BlockSpec((None, None, 1, d), lambda b, i: (b, j, 0, 0))

    def const(shape):
        return pl.BlockSpec(shape, lambda b, i: (0, 0))

    row = pl.BlockSpec((None, tm, d), lambda b, i: (b, i, 0))
    return pl.pallas_call(
        _outproj_kernel,
        grid=(bsz, t // tm),
        in_specs=[
            pl.BlockSpec((None, tm, half), lambda b, i: (b, i, 0)),
            pl.BlockSpec((None, tm, half), lambda b, i: (b, i, 0)),
            row,
            const((2 * half, d)),
            mod(2), mod(3), mod(4), mod(5),
            const((1, d)), const((d, n_exp)), const((1, n_exp)),
            const((d, sff)), const((d, sff)), const((sff, d)),
        ],
        out_specs=[row,
                   pl.BlockSpec((tm * s_tiles, HEAD_DIM), lambda b, i: (b * (t // tm) + i, 0)),
                   pl.BlockSpec((None, tm, 3 * TOP_K), lambda b, i: (b, i, 0)),
                   const((1, n_exp))],
        out_shape=[
            jax.ShapeDtypeStruct((bsz, t, d), F32),
            jax.ShapeDtypeStruct((bsz * t * s_tiles, HEAD_DIM), jnp.uint32),
            jax.ShapeDtypeStruct((bsz, t, 3 * TOP_K), F32),
            jax.ShapeDtypeStruct((1, n_exp), F32),
        ],
        scratch_shapes=[pltpu.VMEM((1, n_exp), F32)],
        compiler_params=_cparams(("arbitrary", "arbitrary")),
        name="outproj",
    )(y_gdn, y_nsa, x, w_out_b, mod4, mod4, mod4, mod4, norm_w.reshape(1, d), router_w,
      router_bias.reshape(1, n_exp), sg, su, sd)


MOE_TM = 256
MOE_TOK = 128


def _token_row(ref, r, s_tiles):
    return ref.at[pl.ds(pl.multiple_of(r * s_tiles, s_tiles), s_tiles), :]


def _for_each(n, fn, unroll=8):
    def body(i, carry):
        fn(i)
        return carry

    lax.fori_loop(0, n, body, 0, unroll=unroll)


def _moe_scatter_kernel(gstart_ref, gend_ref, idx_ref, pos_ref, h_ref, xs_hbm, zero_ref, sem, *, s_tiles):
    n_exp = gstart_ref.shape[0]

    @pl.when(pl.program_id(0) == 0)
    def _():
        zero_ref[...] = jnp.zeros_like(zero_ref)

        def tail(e):
            first = pl.multiple_of((gend_ref[e] - MOE_TM) * s_tiles, MOE_TM * s_tiles)
            return pltpu.make_async_copy(zero_ref, xs_hbm.at[pl.ds(first, MOE_TM * s_tiles), :], sem)

        def start_tail(e):
            @pl.when(gend_ref[e] > gstart_ref[e])
            def _():
                tail(e).start()

        def wait_tail(e):
            @pl.when(gend_ref[e] > gstart_ref[e])
            def _():
                tail(e).wait()

        _for_each(n_exp, start_tail)
        _for_each(n_exp, wait_tail)

    def row_copy(t, j):
        n = t * TOP_K + j
        row = gstart_ref[idx_ref[n]] + pos_ref[n]
        return pltpu.make_async_copy(_token_row(h_ref, t, s_tiles), _token_row(xs_hbm, row, s_tiles), sem)

    def start_token(t):
        for j in range(TOP_K):
            row_copy(t, j).start()

    def wait_token(t):
        for j in range(TOP_K):
            row_copy(t, j).wait()

    _for_each(MOE_TOK, start_token, unroll=2)
    _for_each(MOE_TOK, wait_token, unroll=2)


def _moe_scatter(h_rows, idx, pos, gstart, gend, n_rows, s_tiles):
    n_steps = idx.shape[0] // (MOE_TOK * TOP_K)
    smem = pl.BlockSpec((MOE_TOK * TOP_K,), lambda i, gs, ge: (i,), memory_space=pltpu.SMEM)
    return pl.pallas_call(
        functools.partial(_moe_scatter_kernel, s_tiles=s_tiles),
        grid_spec=pltpu.PrefetchScalarGridSpec(
            num_scalar_prefetch=2,
            grid=(n_steps,),
            in_specs=[smem, smem, pl.BlockSpec((MOE_TOK * s_tiles, HEAD_DIM), lambda i, gs, ge: (i, 0))],
            out_specs=pl.BlockSpec(memory_space=pl.ANY),
            scratch_shapes=[pltpu.VMEM((MOE_TM * s_tiles, HEAD_DIM), h_rows.dtype), pltpu.SemaphoreType.DMA(())],
        ),
        out_shape=jax.ShapeDtypeStruct((n_rows * s_tiles, HEAD_DIM), h_rows.dtype),
        compiler_params=_cparams(("arbitrary",)),
        name="moe_scatter",
    )(gstart, gend, idx, pos, h_rows)


def _moe_expert_kernel(texp_ref, tslot_ref, tnext_ref, nvalid_ref, xs_ref, wg_hbm, wu_hbm, wd_hbm, ys_ref,
                       wg_buf, wu_buf, wd_buf, wgb_ref, wub_ref, wdb_ref, sems):
    i = pl.program_id(0)
    d = wgb_ref.shape[0]
    valid = i < nvalid_ref[0]
    new_expert = (i == 0) | (texp_ref[i] != texp_ref[jnp.maximum(i - 1, 0)])
    slot = tslot_ref[i]

    def fetch(expert, s):
        return [pltpu.make_async_copy(src.at[expert], dst.at[s], sems.at[s, n])
                for n, (src, dst) in enumerate(((wg_hbm, wg_buf), (wu_hbm, wu_buf), (wd_hbm, wd_buf)))]

    @pl.when(i == 0)
    def _():
        for cp in fetch(texp_ref[0], slot):
            cp.start()

    @pl.when(valid & new_expert)
    def _():
        nxt = tnext_ref[i]

        @pl.when(nxt < nvalid_ref[0])
        def _():
            for cp in fetch(texp_ref[nxt], 1 - slot):
                cp.start()

        for cp in fetch(texp_ref[i], slot):
            cp.wait()
        wgb_ref[...] = wg_buf[slot].astype(BF16)
        wub_ref[...] = wu_buf[slot].astype(BF16)
        wdb_ref[...] = wd_buf[slot].astype(BF16)

    @pl.when(valid)
    def _():
        half = d // 2
        lo, hi = _unpack_bf16_pairs(_load_token_rows(xs_ref, MOE_TM, half))
        lo, hi = lo.astype(BF16), hi.astype(BF16)
        gate = _dot(lo, wgb_ref[0:half, :]) + _dot(hi, wgb_ref[half:d, :])
        up = _dot(lo, wub_ref[0:half, :]) + _dot(hi, wub_ref[half:d, :])
        y = _dot((_silu(gate) * up).astype(BF16), wdb_ref[...])
        _store_token_rows(ys_ref, _pack_bf16_pairs(y))

    @pl.when(jnp.logical_not(valid))
    def _():
        ys_ref[...] = jnp.zeros_like(ys_ref)


def _moe_experts(xs, tile_expert, tile_slot, tile_next, nvalid, wg, wu, wd):
    n_exp, d, ff = wg.shape
    blk = xs.shape[0] // tile_expert.shape[0]
    hbm = pl.BlockSpec(memory_space=pl.ANY)
    return pl.pallas_call(
        _moe_expert_kernel,
        grid_spec=pltpu.PrefetchScalarGridSpec(
            num_scalar_prefetch=4,
            grid=(xs.shape[0] // blk,),
            in_specs=[
                pl.BlockSpec((blk, HEAD_DIM), lambda i, te, ts, tn, nv: (jnp.minimum(i, nv[0] - 1), 0)),
                hbm, hbm, hbm,
            ],
            out_specs=pl.BlockSpec((blk, HEAD_DIM), lambda i, te, ts, tn, nv: (i, 0)),
            scratch_shapes=[pltpu.VMEM((2, d, ff), wg.dtype), pltpu.VMEM((2, d, ff), wu.dtype),
                            pltpu.VMEM((2, ff, d), wd.dtype),
                            pltpu.VMEM((d, ff), BF16), pltpu.VMEM((d, ff), BF16), pltpu.VMEM((ff, d), BF16),
                            pltpu.SemaphoreType.DMA((2, 3))],
        ),
        out_shape=jax.ShapeDtypeStruct(xs.shape, xs.dtype),
        compiler_params=_cparams(("arbitrary",)),
        name="moe_experts",
    )(tile_expert, tile_slot, tile_next, nvalid, xs, wg, wu, wd)


def _moe_combine_kernel(gstart_ref, idx_ref, pos_ref, w_ref, ys_hbm, x1_ref, gate_ref, nw_ref, o_ref,
                        buf_ref, lo_ref, hi_ref, sem, *, s_tiles):
    tm, d = x1_ref.shape

    def row_copy(n):
        row = gstart_ref[idx_ref[n]] + pos_ref[n]
        return pltpu.make_async_copy(_token_row(ys_hbm, row, s_tiles), _token_row(buf_ref, n, s_tiles), sem)

    _for_each(tm * TOP_K, lambda n: row_copy(n).start())
    _for_each(tm * TOP_K, lambda n: row_copy(n).wait())

    def token(t):
        n0 = t * TOP_K
        lo, hi = _unpack_bf16_pairs(_token_row(buf_ref, n0, s_tiles)[...])
        acc_lo, acc_hi = w_ref[n0] * lo, w_ref[n0] * hi
        for j in range(1, TOP_K):
            lo, hi = _unpack_bf16_pairs(_token_row(buf_ref, n0 + j, s_tiles)[...])
            acc_lo, acc_hi = acc_lo + w_ref[n0 + j] * lo, acc_hi + w_ref[n0 + j] * hi
        _token_row(lo_ref, t, s_tiles)[...] = acc_lo
        _token_row(hi_ref, t, s_tiles)[...] = acc_hi

    _for_each(tm, token)
    routed = jnp.concatenate([_load_token_rows(lo_ref, tm, d // 2), _load_token_rows(hi_ref, tm, d // 2)], axis=1)
    x2 = x1_ref[...] + gate_ref[...] * routed
    o_ref[...] = x2 * lax.rsqrt(jnp.mean(x2 * x2, axis=-1, keepdims=True) + EPS) * nw_ref[...]


def _moe_combine(ys, idx, pos, w, gstart, x1, mod4, norm_w):
    bsz, t, d = x1.shape
    s_tiles = d // 2 // HEAD_DIM
    tm = MOE_TOK
    nt = t // tm
    smem = pl.BlockSpec((tm * TOP_K,), lambda b, i, gs: (b * nt + i,), memory_space=pltpu.SMEM)
    row = pl.BlockSpec((None, tm, d), lambda b, i, gs: (b, i, 0))
    return pl.pallas_call(
        functools.partial(_moe_combine_kernel, s_tiles=s_tiles),
        grid_spec=pltpu.PrefetchScalarGridSpec(
            num_scalar_prefetch=1,
            grid=(bsz, nt),
            in_specs=[
                smem, smem, smem,
                pl.BlockSpec(memory_space=pl.ANY),
                row,
                pl.BlockSpec((None, None, 1, d), lambda b, i, gs: (b, 5, 0, 0)),
                pl.BlockSpec((1, d), lambda b, i, gs: (0, 0)),
            ],
            out_specs=row,
            scratch_shapes=[pltpu.VMEM((tm * TOP_K * s_tiles, HEAD_DIM), ys.dtype),
                            pltpu.VMEM((tm * s_tiles, HEAD_DIM), F32), pltpu.VMEM((tm * s_tiles, HEAD_DIM), F32),
                            pltpu.SemaphoreType.DMA(())],
        ),
        out_shape=jax.ShapeDtypeStruct((bsz, t, d), F32),
        compiler_params=_cparams(("arbitrary", "arbitrary")),
        name="moe_combine",
    )(gstart, idx, pos, w, ys, x1, mod4, norm_w.reshape(1, d))


def _moe(h_rows, route, counts, x1, mod4, norm_w, wg, wu, wd):
    bsz, t, d = x1.shape
    n_exp = wg.shape[0]
    s_tiles = h_rows.shape[0] // (bsz * t)
    idx = route[..., 0:TOP_K].astype(jnp.int32).reshape(-1)
    w = route[..., TOP_K:2 * TOP_K].reshape(-1)
    pos = route[..., 2 * TOP_K:3 * TOP_K].astype(jnp.int32).reshape(-1)
    padded = (counts.reshape(n_exp).astype(jnp.int32) + MOE_TM - 1) // MOE_TM * MOE_TM
    gend = jnp.cumsum(padded)
    gstart = gend - padded
    nvalid = gend[-1:] // MOE_TM
    n_tiles = bsz * t * TOP_K // MOE_TM + n_exp
    tile_start = jnp.minimum(jnp.arange(n_tiles, dtype=jnp.int32), nvalid - 1) * MOE_TM
    tile_expert = jnp.sum((gend[None, :] <= tile_start[:, None]).astype(jnp.int32), axis=1)
    first_of_group = jnp.concatenate([jnp.ones((1,), jnp.int32),
                                      (tile_expert[1:] != tile_expert[:-1]).astype(jnp.int32)])
    tile_slot = (jnp.cumsum(first_of_group) - 1) % 2
    experts = jnp.arange(n_exp, dtype=jnp.int32)
    tile_next = jnp.sum(jnp.where(tile_expert[:, None] == experts[None, :], gend[None, :], 0), axis=1) // MOE_TM
    xs = _moe_scatter(h_rows, idx, pos, gstart, gend, n_tiles * MOE_TM, s_tiles)
    ys = _moe_experts(xs, tile_expert, tile_slot, tile_next, nvalid, wg, wu, wd)
    return _moe_combine(ys, idx, pos, w, gstart, x1, mod4, norm_w)


def _reorder_w_in(w_in):
    d = w_in.shape[0]
    o_beta = 3 * GDN_WIDTH
    o_a = o_beta + GDN_HEADS
    o_gate = o_a + GDN_HEADS
    o_q = o_gate + GDN_WIDTH
    o_kv = o_q + NSA_WIDTH
    o_ng = o_kv + 6 * NSA_KV_WIDTH
    n_small = 2 * GDN_HEADS + 3 * NSA_HEADS
    parts = [w_in[:, :o_beta], w_in[:, o_gate:o_q], w_in[:, o_q:o_kv], w_in[:, o_kv:o_ng],
             w_in[:, o_beta:o_gate], w_in[:, o_ng:o_ng + 3 * NSA_HEADS]]
    used = COL_SMALL * HEAD_DIM + n_small
    parts.append(jnp.zeros((d, PROJ_COLS - used), w_in.dtype))
    return jnp.concatenate(parts, axis=1).astype(BF16)


def _layer(x, mod, norm_attn_w, norm_ffn_w, w_in, gdn_conv_w, gdn_a_log, gdn_dt_bias, gdn_norm_w,
           cmp_pe_k, cmp_w1_k, cmp_w2_k, cmp_pe_v, cmp_w1_v, cmp_w2_v, w_out, router_w, router_bias,
           expert_w_gate, expert_w_up, expert_w_down, shared_w_gate, shared_w_up, shared_w_down, final_w):
    bsz, t, d = x.shape
    mod4 = mod.reshape(bsz, N_MOD, 1, d)
    proj = _proj(x, mod4, norm_attn_w, _reorder_w_in(w_in))
    y_gdn = _gdn(proj, gdn_conv_w, gdn_a_log, gdn_dt_bias, gdn_norm_w)
    kc = _compress(proj, COL_KV, cmp_pe_k, cmp_w1_k, cmp_w2_k)
    vc = _compress(proj, COL_KV + NSA_KV_HEADS, cmp_pe_v, cmp_w1_v, cmp_w2_v)
    o_cmp, sel = _cmpattn(proj, kc, vc)
    o_sel = _selattn(proj, sel)
    y_nsa = _winattn(proj, o_cmp, o_sel)
    x1, h_rows, route, counts = _outproj(
        y_gdn, y_nsa, x, w_out.astype(BF16), mod4, norm_ffn_w, router_w, router_bias,
        shared_w_gate.astype(BF16), shared_w_up.astype(BF16), shared_w_down.astype(BF16))
    return _moe(h_rows, route, counts, x1, mod4, final_w, expert_w_gate, expert_w_up, expert_w_down)


def kernel(x, c, w_ada, b_ada, norm_attn_w, norm_ffn_w, norm_final_w, w_in, gdn_conv_w, gdn_a_log, gdn_dt_bias,
           gdn_norm_w, cmp_pe_k, cmp_w1_k, cmp_w2_k, cmp_pe_v, cmp_w1_v, cmp_w2_v, w_out, router_w, router_bias,
           expert_w_gate, expert_w_up, expert_w_down, shared_w_gate, shared_w_up, shared_w_down):
    depth = w_ada.shape[0]
    assert depth == 1, "the fused final norm assumes a single layer"
    l = 0
    mod = _ada(c, w_ada[l], b_ada[l])
    return _layer(x, mod, norm_attn_w[l], norm_ffn_w[l], w_in[l], gdn_conv_w[l], gdn_a_log[l], gdn_dt_bias[l],
                  gdn_norm_w[l], cmp_pe_k[l], cmp_w1_k[l], cmp_w2_k[l], cmp_pe_v[l], cmp_w1_v[l], cmp_w2_v[l],
                  w_out[l], router_w[l], router_bias[l], expert_w_gate[l], expert_w_up[l], expert_w_down[l],
                  shared_w_gate[l], shared_w_up[l], shared_w_down[l], norm_final_w)
```

```python
import functools

import jax
import jax.numpy as jnp
from jax import lax
from jax.experimental import pallas as pl
from jax.experimental.pallas import tpu as pltpu

F32 = jnp.float32
BF16 = jnp.bfloat16
HI = lax.Precision.HIGHEST

HEAD_DIM = 128
GDN_HEADS = 8
NSA_HEADS = 8
NSA_KV_HEADS = 2
NSA_GROUP = NSA_HEADS // NSA_KV_HEADS
GDN_WIDTH = GDN_HEADS * HEAD_DIM
NSA_WIDTH = NSA_HEADS * HEAD_DIM
NSA_KV_WIDTH = NSA_KV_HEADS * HEAD_DIM
CONV_WIDTH = 4
GDN_CHUNK = 64
CMP_BLOCK = 32
CMP_STRIDE = 16
SEL_BLOCK = 64
SEL_TOPK = 16
N_LOCAL_BLOCKS = 2
WINDOW = 512
Q_BLOCK = 128
TOP_K = 8
ROUTED_SCALE = 2.5
N_MOD = 6
EPS = 1e-6

COL_GDN_Q = 0
COL_GDN_K = GDN_HEADS
COL_GDN_V = 2 * GDN_HEADS
COL_GDN_GATE = 3 * GDN_HEADS
COL_NSA_Q = 4 * GDN_HEADS
COL_KV = COL_NSA_Q + NSA_HEADS
COL_SMALL = COL_KV + 6 * NSA_KV_HEADS
N_COL_BLOCKS = COL_SMALL + 1
PROJ_TN = 2304
PROJ_COLS = 3 * PROJ_TN
SMALL_BETA = 0
SMALL_A = GDN_HEADS
SMALL_NSA_GATE = 2 * GDN_HEADS

NEG_BIG = -1e30
VMEM_LIMIT = 48 * 1024 * 1024


def _cparams(semantics, vmem_limit=VMEM_LIMIT):
    return pltpu.CompilerParams(dimension_semantics=semantics, vmem_limit_bytes=vmem_limit)


def _silu(x):
    return (0.5 * x) * (1.0 + jnp.tanh(0.5 * x))


def _dot(a, b, precision=None):
    return jnp.dot(a, b, preferred_element_type=F32, precision=precision)


def _dot_nt(a, b, precision=None):
    return lax.dot_general(a, b, (((1,), (1,)), ((), ())), preferred_element_type=F32, precision=precision)


def _dot_tn(a, b, precision=None):
    return lax.dot_general(a, b, (((0,), (0,)), ((), ())), preferred_element_type=F32, precision=precision)


def _split_bf16(x):
    hi = x.astype(BF16)
    return hi, (x - hi.astype(F32)).astype(BF16)


def _dot_split(a, b, nt=False):
    (ah, al), (bh, bl) = _split_bf16(a), _split_bf16(b)
    f = _dot_nt if nt else _dot
    return f(ah, bh) + (f(ah, bl) + f(al, bh))


def _lane_pick(x, idx):
    lane = lax.broadcasted_iota(jnp.int32, x.shape, 1)
    return jnp.sum(jnp.where(lane == idx, x, 0.0), axis=-1, keepdims=True)


def _ada_kernel(ct_ref, w_ref, b_ref, o_ref):
    c = ct_ref[...]
    cond = _silu(c)
    w = w_ref[...]
    rows = [jnp.sum(w * cond[:, b:b + 1], axis=0, keepdims=True) for b in range(c.shape[1])]
    o_ref[...] = jnp.concatenate(rows, axis=0) + b_ref[...]


def _ada(c, w_ada, b_ada, tn=512):
    bsz, d = c.shape
    n = w_ada.shape[1]
    return pl.pallas_call(
        _ada_kernel,
        grid=(n // tn,),
        in_specs=[
            pl.BlockSpec((d, bsz), lambda j: (0, 0)),
            pl.BlockSpec((d, tn), lambda j: (0, j)),
            pl.BlockSpec((1, tn), lambda j: (0, j)),
        ],
        out_specs=pl.BlockSpec((bsz, tn), lambda j: (0, j)),
        out_shape=jax.ShapeDtypeStruct((bsz, n), F32),
        compiler_params=_cparams(("arbitrary",)),
        name="ada",
    )(c.T, w_ada, b_ada.reshape(1, n))


def _modulated_norm(x, nw, scale, shift):
    var = jnp.mean(x * x, axis=-1, keepdims=True)
    h = x * lax.rsqrt(var + EPS) * nw
    return h * (1.0 + scale) + shift


def _proj_kernel(x_ref, shift_ref, scale_ref, nw_ref, w_ref, o_ref):
    h = _modulated_norm(x_ref[...], nw_ref[...], scale_ref[...], shift_ref[...])
    o_ref[...] = _dot(h.astype(BF16), w_ref[...])


def _proj(x, mod4, norm_w, w_r, tm=256):
    bsz, t, d = x.shape
    n = w_r.shape[1]
    return pl.pallas_call(
        _proj_kernel,
        grid=(n // PROJ_TN, bsz, t // tm),
        in_specs=[
            pl.BlockSpec((None, tm, d), lambda j, b, i: (b, i, 0)),
            pl.BlockSpec((None, None, 1, d), lambda j, b, i: (b, 0, 0, 0)),
            pl.BlockSpec((None, None, 1, d), lambda j, b, i: (b, 1, 0, 0)),
            pl.BlockSpec((1, d), lambda j, b, i: (0, 0)),
            pl.BlockSpec((d, PROJ_TN), lambda j, b, i: (0, j)),
        ],
        out_specs=pl.BlockSpec((None, tm, PROJ_TN), lambda j, b, i: (b, i, j)),
        out_shape=jax.ShapeDtypeStruct((bsz, t, n), F32),
        compiler_params=_cparams(("arbitrary", "arbitrary", "arbitrary")),
        name="proj",
    )(x, mod4, mod4, norm_w.reshape(1, d), w_r)


def _gdn_kernel(q_ref, k_ref, v_ref, gate_ref, small_ref, hq_ref, hk_ref, hv_ref,
                cwq_ref, cwk_ref, cwv_ref, alog_ref, dtb_ref, nw_ref, o_ref, s_ref, xc_ref, *, tt, hg):
    first = pl.program_id(2) == 0

    @pl.when(first)
    def _():
        s_ref[...] = jnp.zeros_like(s_ref)

    def conv_silu(x_ref, halo_ref, cw_ref):
        xc_ref[0:8, :] = jnp.where(first, 0.0, halo_ref[...])
        xc_ref[8:tt + 8, :] = x_ref[...]
        cw = cw_ref[...]
        off = 8 - (CONV_WIDTH - 1)
        y = cw[0:1, :] * xc_ref[off:off + tt, :]
        for i in range(1, CONV_WIDTH):
            y = y + cw[i:i + 1, :] * xc_ref[off + i:off + i + tt, :]
        return _silu(y)

    q_all = conv_silu(q_ref, hq_ref, cwq_ref)
    k_all = conv_silu(k_ref, hk_ref, cwk_ref)
    v_all = conv_silu(v_ref, hv_ref, cwv_ref)
    ri = lax.broadcasted_iota(jnp.int32, (tt, tt), 0)
    ci = lax.broadcasted_iota(jnp.int32, (tt, tt), 1)
    lower = jnp.where(((ri // GDN_CHUNK) == (ci // GDN_CHUNK)) & (ci <= ri), 1.0, 0.0).astype(BF16)
    heads = [pl.program_id(1) * hg + hh for hh in range(hg)]
    outs = _gdn_heads(q_all, k_all, v_all, small_ref[...], lower, heads, alog_ref[...], dtb_ref[...], s_ref, tt)
    for hh, o in enumerate(outs):
        cols = slice(hh * HEAD_DIM, (hh + 1) * HEAD_DIM)
        o = o * lax.rsqrt(jnp.mean(o * o, axis=-1, keepdims=True) + EPS) * nw_ref[...]
        o_ref[:, cols] = (o * _silu(gate_ref[:, cols])).astype(o_ref.dtype)


def _gdn_heads(q_all, k_all, v_all, small, lower, heads, alog, dtb, s_ref, tt):
    c = GDN_CHUNK
    nch = tt // c
    hg = len(heads)
    kpos = lax.broadcasted_iota(jnp.int32, (tt, c), 0) % c
    jj = lax.broadcasted_iota(jnp.int32, (tt, c), 1)
    per_head = []
    for hh, h in enumerate(heads):
        cols = slice(hh * HEAD_DIM, (hh + 1) * HEAD_DIM)
        q, k, v = q_all[:, cols], k_all[:, cols], v_all[:, cols]
        q = q * lax.rsqrt(jnp.sum(q * q, axis=-1, keepdims=True) + EPS) * (HEAD_DIM ** -0.5)
        k = k * lax.rsqrt(jnp.sum(k * k, axis=-1, keepdims=True) + EPS)
        beta = jax.nn.sigmoid(_lane_pick(small, SMALL_BETA + h))
        z = _lane_pick(small, SMALL_A + h) + _lane_pick(dtb, h)
        softplus = jnp.maximum(z, 0.0) + jnp.log(1.0 + jnp.exp(-jnp.abs(z)))
        g = -jnp.exp(_lane_pick(alog, h)) * softplus
        g_c = jnp.broadcast_to(g, (tt, c))
        rhs = jnp.concatenate([g_c, jnp.where(kpos > jj, g_c, 0.0)], axis=1)
        r_hi = rhs.astype(BF16)
        rem = rhs - r_hi.astype(F32)
        r_mid = rem.astype(BF16)
        r_lo = (rem - r_mid.astype(F32)).astype(BF16)
        sums = _dot(lower, r_hi) + (_dot(lower, r_mid) + _dot(lower, r_lo))
        gcum = jnp.broadcast_to(sums[:, 0:1], (tt, HEAD_DIM))
        gc3 = gcum.reshape(nch, c, HEAD_DIM)
        grest = (gc3[:, c - 1:c, :] - gc3).reshape(tt, HEAD_DIM)
        decay_h = jnp.where(kpos >= jj, jnp.exp(sums[:, c:2 * c]), 0.0)
        per_head.append((q, k, v * beta, k * beta, jnp.exp(gcum), jnp.exp(grest), decay_h))

    def stack(j):
        x = jnp.concatenate([ph[j] for ph in per_head], axis=0)
        return x.reshape(hg * nch, c, x.shape[-1])

    q, k, vb, kb, egc, egr, decay = (stack(j) for j in range(7))
    kpos_all = lax.broadcasted_iota(jnp.int32, (hg * tt, c), 0) % c
    jj_all = lax.broadcasted_iota(jnp.int32, (hg * tt, c), 1)
    strict = (kpos_all > jj_all).reshape(hg * nch, c, c)
    eye = jnp.where(kpos_all == jj_all, 1.0, 0.0).reshape(hg * nch, c, c)

    def bmm(a, b):
        return jnp.einsum('nij,njk->nik', a, b, preferred_element_type=F32)

    def bmm_nt(a, b):
        return jnp.einsum('nid,njd->nij', a, b, preferred_element_type=F32)

    def bmm_split(a, b):
        (ah, al), (bh, bl) = _split_bf16(a), _split_bf16(b)
        return bmm(ah, bh) + (bmm(ah, bl) + bmm(al, bh))

    k3 = k.astype(BF16)
    a = jnp.where(strict, bmm_nt(kb.astype(BF16), k3) * decay, 0.0)
    inv = eye - a
    p = bmm_split(a, a)
    n_sq = (c - 1).bit_length() - 1
    for lvl in range(n_sq):
        inv = inv + bmm_split(inv, p)
        if lvl + 1 < n_sq:
            p = bmm_split(p, p)
    inv_b = inv.astype(BF16)
    u3 = bmm(inv_b, vb.astype(BF16)).astype(BF16)
    w3 = bmm(inv_b, (kb * egc).astype(BF16)).astype(BF16)
    qk3 = (bmm_nt(q.astype(BF16), k3) * decay).astype(BF16)
    kd3 = (k * egr).astype(BF16)
    qe3 = (q * egc - bmm(qk3, w3)).astype(BF16)
    o03 = bmm(qk3, u3)
    kw = [_dot_tn(kd3[i], w3[i]).astype(BF16) for i in range(hg * nch)]
    ku = [_dot_tn(kd3[i], u3[i]) for i in range(hg * nch)]

    states = [s_ref[hh] for hh in range(hg)]
    outs = [[] for _ in range(hg)]
    for n in range(nch):
        for hh in range(hg):
            i = hh * nch + n
            state_b = states[hh].astype(BF16)
            outs[hh].append(_dot(qe3[i], state_b) + o03[i])
            states[hh] = states[hh] * egc[i, c - 1:c, :] - _dot(kw[i], state_b) + ku[i]
    for hh in range(hg):
        s_ref[hh] = states[hh]
    return [jnp.concatenate(o, axis=0) for o in outs]


def _gdn(proj, conv_w, a_log, dt_bias, norm_w, tt=128, hg=8):
    bsz, t, _ = proj.shape
    wdt = hg * HEAD_DIM

    def col(base):
        return pl.BlockSpec((None, tt, wdt), lambda b, h, i: (b, i, base // hg + h))

    def halo(base):
        return pl.BlockSpec((None, 8, wdt), lambda b, h, i: (b, jnp.maximum(i * (tt // 8) - 1, 0), base // hg + h))

    def cw(base):
        return pl.BlockSpec((CONV_WIDTH, wdt), lambda b, h, i: (0, base // hg + h))

    heads = pl.BlockSpec((1, GDN_HEADS), lambda b, h, i: (0, 0))
    return pl.pallas_call(
        functools.partial(_gdn_kernel, tt=tt, hg=hg),
        grid=(bsz, GDN_HEADS // hg, t // tt),
        in_specs=[
            col(COL_GDN_Q), col(COL_GDN_K), col(COL_GDN_V), col(COL_GDN_GATE),
            pl.BlockSpec((None, tt, HEAD_DIM), lambda b, h, i: (b, i, COL_SMALL)),
            halo(COL_GDN_Q), halo(COL_GDN_K), halo(COL_GDN_V),
            cw(COL_GDN_Q), cw(COL_GDN_K), cw(COL_GDN_V),
            heads, heads,
            pl.BlockSpec((1, HEAD_DIM), lambda b, h, i: (0, 0)),
        ],
        out_specs=pl.BlockSpec((None, tt, wdt), lambda b, h, i: (b, i, h)),
        out_shape=jax.ShapeDtypeStruct((bsz, t, GDN_WIDTH), BF16),
        scratch_shapes=[pltpu.VMEM((hg, HEAD_DIM, HEAD_DIM), F32), pltpu.VMEM((tt + 8, wdt), F32)],
        compiler_params=_cparams(("arbitrary", "arbitrary", "arbitrary")),
        name="gdn",
    )(proj, proj, proj, proj, proj, proj, proj, proj, conv_w, conv_w, conv_w,
      a_log.reshape(1, GDN_HEADS), dt_bias.reshape(1, GDN_HEADS), norm_w.reshape(1, HEAD_DIM))


def _compress_kernel(x_ref, pe_ref, w1_ref, w2_ref, o_ref):
    r = x_ref.shape[0] // CMP_STRIDE
    d = x_ref.shape[1]
    a = b = None
    for l in range(CMP_STRIDE):
        x_l = x_ref[pl.ds(l, r, stride=CMP_STRIDE), :]
        a_l = _dot_split(x_l + pe_ref[l:l + 1, :], w1_ref[l * d:(l + 1) * d, :])
        b_l = _dot_split(x_l + pe_ref[CMP_STRIDE + l:CMP_STRIDE + l + 1, :],
                         w1_ref[(CMP_STRIDE + l) * d:(CMP_STRIDE + l + 1) * d, :])
        a, b = (a_l, b_l) if a is None else (a + a_l, b + b_l)
    hid = _silu(a + pltpu.roll(b, r - 1, 0))
    o_ref[...] = _dot_split(hid, w2_ref[...])


def _compress(proj, col_block, pe, w1, w2):
    bsz, t, _ = proj.shape
    r = t // CMP_STRIDE
    hid = w1.shape[1]
    return pl.pallas_call(
        _compress_kernel,
        grid=(bsz, NSA_KV_HEADS),
        in_specs=[
            pl.BlockSpec((None, t, HEAD_DIM), lambda b, h: (b, 0, col_block + h)),
            pl.BlockSpec((CMP_BLOCK, HEAD_DIM), lambda b, h: (0, 0)),
            pl.BlockSpec((CMP_BLOCK * HEAD_DIM, hid), lambda b, h: (0, 0)),
            pl.BlockSpec((hid, HEAD_DIM), lambda b, h: (0, 0)),
        ],
        out_specs=pl.BlockSpec((None, None, r, HEAD_DIM), lambda b, h: (b, h, 0, 0)),
        out_shape=jax.ShapeDtypeStruct((bsz, NSA_KV_HEADS, r, HEAD_DIM), F32),
        compiler_params=_cparams(("arbitrary", "arbitrary")),
        name="compress",
    )(proj, pe, w1, w2)


def _cmpattn_kernel(q_ref, kc_ref, vc_ref, o_ref, sel_ref, *, tq, nc, nsel):
    ncp = kc_ref.shape[0]
    scale = HEAD_DIM ** -0.5
    t0 = pl.program_id(2) * tq
    kc = kc_ref[...]
    vc = vc_ref[...]
    pos_r = t0 + lax.broadcasted_iota(jnp.int32, (tq, ncp), 0)
    c_r = lax.broadcasted_iota(jnp.int32, (tq, ncp), 1)
    valid_r = (c_r * CMP_STRIDE + CMP_BLOCK - 1 <= pos_r) & (c_r < nc)
    pos_c = t0 + lax.broadcasted_iota(jnp.int32, (ncp, tq), 1)
    c_c = lax.broadcasted_iota(jnp.int32, (ncp, tq), 0)
    valid_c = (c_c * CMP_STRIDE + CMP_BLOCK - 1 <= pos_c) & (c_c < nc)
    psum = jnp.zeros((ncp, tq), F32)
    kc_hi, kc_lo = _split_bf16(kc)
    vc_b = vc.astype(BF16)
    for g in range(NSA_GROUP):
        q_hi, q_lo = _split_bf16(q_ref[:, g * HEAD_DIM:(g + 1) * HEAD_DIM])
        s = _dot_nt(q_hi, kc_hi) + (_dot_nt(q_hi, kc_lo) + _dot_nt(q_lo, kc_hi))
        s = jnp.where(valid_r, s * scale, NEG_BIG)
        e = jnp.where(valid_r, jnp.exp(s - jnp.max(s, axis=-1, keepdims=True)), 0.0)
        p = e * (1.0 / jnp.maximum(jnp.sum(e, axis=-1, keepdims=True), 1e-30))
        o_ref[:, g * HEAD_DIM:(g + 1) * HEAD_DIM] = _dot(p.astype(BF16), vc_b)
        st = _dot_nt(kc_hi, q_hi) + (_dot_nt(kc_lo, q_hi) + _dot_nt(kc_hi, q_lo))
        st = jnp.where(valid_c, st * scale, NEG_BIG)
        et = jnp.where(valid_c, jnp.exp(st - jnp.max(st, axis=0, keepdims=True)), 0.0)
        psum = psum + et * (1.0 / jnp.maximum(jnp.sum(et, axis=0, keepdims=True), 1e-30))
    j_o = lax.broadcasted_iota(jnp.int32, (nsel, ncp), 0) * SEL_BLOCK
    c_o = lax.broadcasted_iota(jnp.int32, (nsel, ncp), 1) * CMP_STRIDE
    overlap = jnp.where((c_o < j_o + SEL_BLOCK) & (c_o + CMP_BLOCK > j_o), 1.0, 0.0).astype(BF16)
    p_hi, p_lo = _split_bf16(psum)
    imp = _dot(overlap, p_hi) + _dot(overlap, p_lo)
    blk = lax.broadcasted_iota(jnp.int32, (nsel, tq), 0)
    dist = (t0 + lax.broadcasted_iota(jnp.int32, (nsel, tq), 1)) // SEL_BLOCK - blk
    forced = (blk == 0) | ((dist >= 0) & (dist < N_LOCAL_BLOCKS))
    val = jnp.where(forced, -NEG_BIG, jnp.where(dist >= 0, imp, -1.0))
    rank = jnp.zeros((nsel, tq), F32)
    for j in range(nsel):
        row = val[j:j + 1, :]
        rank = rank + jnp.where((row > val) | ((row == val) & (blk > j)), 1.0, 0.0)
    sel = jnp.where((rank < min(SEL_TOPK, nsel)) & (dist >= 0), 1.0, 0.0)
    sel_ref[...] = sel.T


def _cmpattn(proj, kc, vc, tq=256):
    bsz, t, _ = proj.shape
    ncp = kc.shape[2]
    nc = (t - CMP_BLOCK) // CMP_STRIDE + 1
    nsel = t // SEL_BLOCK
    gw = NSA_GROUP * HEAD_DIM
    return pl.pallas_call(
        functools.partial(_cmpattn_kernel, tq=tq, nc=nc, nsel=nsel),
        grid=(bsz, NSA_KV_HEADS, t // tq),
        in_specs=[
            pl.BlockSpec((None, tq, gw), lambda b, h, i: (b, i, COL_NSA_Q // NSA_GROUP + h)),
            pl.BlockSpec((None, None, ncp, HEAD_DIM), lambda b, h, i: (b, h, 0, 0)),
            pl.BlockSpec((None, None, ncp, HEAD_DIM), lambda b, h, i: (b, h, 0, 0)),
        ],
        out_specs=[
            pl.BlockSpec((None, tq, gw), lambda b, h, i: (b, i, h)),
            pl.BlockSpec((None, None, tq, nsel), lambda b, h, i: (b, h, i, 0)),
        ],
        out_shape=[
            jax.ShapeDtypeStruct((bsz, t, NSA_WIDTH), F32),
            jax.ShapeDtypeStruct((bsz, NSA_KV_HEADS, t, nsel), F32),
        ],
        compiler_params=_cparams(("arbitrary", "arbitrary", "arbitrary")),
        name="cmpattn",
    )(proj, kc, vc)


def _selattn_kernel(q_ref, k_ref, v_ref, sel_ref, o_ref, m_ref, l_ref, acc_ref, *, tq, tk, nsel):
    g = NSA_GROUP
    qi = pl.program_id(2)
    m_ref[...] = jnp.full_like(m_ref, NEG_BIG)
    l_ref[...] = jnp.zeros_like(l_ref)
    acc_ref[...] = jnp.zeros_like(acc_ref)
    q4 = jnp.concatenate([q_ref[:, i * HEAD_DIM:(i + 1) * HEAD_DIM] for i in range(g)], axis=0)
    q4 = (q4 * HEAD_DIM ** -0.5).astype(BF16)
    sel = sel_ref[...].astype(BF16)
    pos = qi * tq + lax.broadcasted_iota(jnp.int32, (tq, tk), 0)

    def kv_step(kv, carry):
        k0 = pl.multiple_of(kv * tk, tk)
        s = _dot_nt(q4, k_ref[pl.ds(k0, tk), :].astype(BF16))
        key = k0 + lax.broadcasted_iota(jnp.int32, (nsel, tk), 1)
        blk = lax.broadcasted_iota(jnp.int32, (nsel, tk), 0)
        expand = jnp.where(key // SEL_BLOCK == blk, 1.0, 0.0).astype(BF16)
        picked = _dot(sel, expand)
        kpos = k0 + lax.broadcasted_iota(jnp.int32, (tq, tk), 1)
        mask = ((picked > 0.5) & (kpos <= pos))[None]
        s = jnp.where(mask, s.reshape(g, tq, tk), NEG_BIG)
        m_prev = m_ref[...]
        m_new = jnp.maximum(m_prev, jnp.max(s, axis=-1, keepdims=True))
        p = jnp.exp(s - m_new)
        alpha = jnp.exp(m_prev - m_new)
        l_ref[...] = alpha * l_ref[...] + jnp.sum(p, axis=-1, keepdims=True)
        pv = _dot(p.reshape(g * tq, tk).astype(BF16), v_ref[pl.ds(k0, tk), :].astype(BF16))
        acc_ref[...] = alpha * acc_ref[...] + pv.reshape(g, tq, HEAD_DIM)
        m_ref[...] = m_new
        return carry

    lax.fori_loop(0, (qi * tq + tq - 1) // tk + 1, kv_step, 0)
    o = acc_ref[...] * (1.0 / jnp.maximum(l_ref[...], 1e-30))
    for i in range(g):
        o_ref[:, i * HEAD_DIM:(i + 1) * HEAD_DIM] = o[i]


def _selattn(proj, sel, tq=2 * Q_BLOCK, tk=512):
    bsz, t, _ = proj.shape
    nsel = t // SEL_BLOCK
    tk = min(tk, t)
    gw = NSA_GROUP * HEAD_DIM

    def kv_spec(base):
        return pl.BlockSpec((None, t, HEAD_DIM), lambda b, h, i: (b, 0, base + h))

    return pl.pallas_call(
        functools.partial(_selattn_kernel, tq=tq, tk=tk, nsel=nsel),
        grid=(bsz, NSA_KV_HEADS, t // tq),
        in_specs=[
            pl.BlockSpec((None, tq, gw), lambda b, h, i: (b, i, COL_NSA_Q // NSA_GROUP + h)),
            kv_spec(COL_KV + 2 * NSA_KV_HEADS), kv_spec(COL_KV + 3 * NSA_KV_HEADS),
            pl.BlockSpec((None, None, tq, nsel), lambda b, h, i: (b, h, i, 0)),
        ],
        out_specs=pl.BlockSpec((None, tq, gw), lambda b, h, i: (b, i, h)),
        out_shape=jax.ShapeDtypeStruct((bsz, t, NSA_WIDTH), F32),
        scratch_shapes=[pltpu.VMEM((NSA_GROUP, tq, 1), F32), pltpu.VMEM((NSA_GROUP, tq, 1), F32),
                        pltpu.VMEM((NSA_GROUP, tq, HEAD_DIM), F32)],
        compiler_params=_cparams(("arbitrary", "arbitrary", "arbitrary")),
        name="selattn",
    )(proj, proj, proj, sel)


def _winattn_kernel(*refs, tq, nblk):
    q_ref = refs[0]
    k_refs = refs[1:1 + nblk]
    v_refs = refs[1 + nblk:1 + 2 * nblk]
    ocmp_ref, osel_ref, small_ref, o_ref = refs[1 + 2 * nblk:]
    g = NSA_GROUP
    hk = pl.program_id(1)
    qi = pl.program_id(2)
    scale = HEAD_DIM ** -0.5
    nk = nblk * tq
    keys = jnp.concatenate([r[...] for r in k_refs], axis=0).astype(BF16)
    vals = jnp.concatenate([r[...] for r in v_refs], axis=0).astype(BF16)
    q4 = jnp.concatenate([q_ref[:, i * HEAD_DIM:(i + 1) * HEAD_DIM] for i in range(g)], axis=0)
    s = (_dot_nt(q4.astype(BF16), keys) * scale).reshape(g, tq, nk)
    pos = qi * tq + lax.broadcasted_iota(jnp.int32, (tq, nk), 0)
    wpos = (qi - (nblk - 1)) * tq + lax.broadcasted_iota(jnp.int32, (tq, nk), 1)
    mask = ((wpos <= pos) & (wpos > pos - WINDOW) & (wpos >= 0))[None]
    s = jnp.where(mask, s, NEG_BIG)
    e = jnp.exp(s - jnp.max(s, axis=-1, keepdims=True)).astype(BF16)
    ov = _dot(e.reshape(g * tq, nk), jnp.concatenate([vals, jnp.ones_like(vals)], axis=1))
    o_win = (ov[:, 0:HEAD_DIM] * (1.0 / jnp.maximum(ov[:, HEAD_DIM:2 * HEAD_DIM], 1e-30))).reshape(g, tq, HEAD_DIM)
    small = small_ref[...]
    for i in range(g):
        head = hk * g + i
        gates = [jax.nn.sigmoid(_lane_pick(small, SMALL_NSA_GATE + j * NSA_HEADS + head)) for j in range(3)]
        cols = slice(i * HEAD_DIM, (i + 1) * HEAD_DIM)
        o = gates[0] * ocmp_ref[:, cols] + gates[1] * osel_ref[:, cols] + gates[2] * o_win[i]
        o_ref[:, cols] = o.astype(o_ref.dtype)


def _winattn(proj, o_cmp, o_sel, tq=Q_BLOCK):
    bsz, t, _ = proj.shape
    nblk = WINDOW // tq + 1
    gw = NSA_GROUP * HEAD_DIM

    def kv_spec(base, d):
        return pl.BlockSpec((None, tq, HEAD_DIM),
                            lambda b, h, i: (b, jnp.maximum(i - (nblk - 1) + d, 0), base + h))

    grp = pl.BlockSpec((None, tq, gw), lambda b, h, i: (b, i, h))
    return pl.pallas_call(
        functools.partial(_winattn_kernel, tq=tq, nblk=nblk),
        grid=(bsz, NSA_KV_HEADS, t // tq),
        in_specs=(
            [pl.BlockSpec((None, tq, gw), lambda b, h, i: (b, i, COL_NSA_Q // NSA_GROUP + h))]
            + [kv_spec(COL_KV + 4 * NSA_KV_HEADS, d) for d in range(nblk)]
            + [kv_spec(COL_KV + 5 * NSA_KV_HEADS, d) for d in range(nblk)]
            + [grp, grp, pl.BlockSpec((None, tq, HEAD_DIM), lambda b, h, i: (b, i, COL_SMALL))]
        ),
        out_specs=grp,
        out_shape=jax.ShapeDtypeStruct((bsz, t, NSA_WIDTH), BF16),
        compiler_params=_cparams(("arbitrary", "arbitrary", "arbitrary")),
        name="winattn",
    )(proj, *([proj] * (2 * nblk)), o_cmp, o_sel, proj)


def _ffn(h, wg, wu, wd):
    act = _silu(_dot(h, wg)) * _dot(h, wu)
    return _dot(act.astype(BF16), wd)


def _pack_bf16_pairs(x):
    half = x.shape[1] // 2
    lo = pltpu.bitcast(x[:, :half].astype(BF16).astype(F32), jnp.uint32)
    hi = pltpu.bitcast(x[:, half:].astype(BF16).astype(F32), jnp.uint32)
    return hi | (lo >> 16)


def _unpack_bf16_pairs(p):
    return pltpu.bitcast(p << 16, F32), pltpu.bitcast(p & jnp.uint32(0xFFFF0000), F32)


def _store_token_rows(ref, x):
    n, d = x.shape
    s_tiles = d // HEAD_DIM
    for s in range(s_tiles):
        ref[pl.ds(s, n, stride=s_tiles), :] = x[:, s * HEAD_DIM:(s + 1) * HEAD_DIM]


def _load_token_rows(ref, n, d):
    s_tiles = d // HEAD_DIM
    return jnp.concatenate([ref[pl.ds(s, n, stride=s_tiles), :] for s in range(s_tiles)], axis=1)


def _outproj_kernel(yg_ref, yn_ref, x_ref, wo_ref, gate_ref, shift_ref, scale_ref, gate_m_ref, nw_ref,
                    rw_ref, rb_ref, sg_ref, su_ref, sd_ref, x1_ref, h_ref, route_ref, count_ref, carry_ref):
    tm = x_ref.shape[0]
    half = yg_ref.shape[1]

    @pl.when((pl.program_id(0) == 0) & (pl.program_id(1) == 0))
    def _():
        carry_ref[...] = jnp.zeros_like(carry_ref)

    y = _dot(yg_ref[...], wo_ref[0:half, :]) + _dot(yn_ref[...], wo_ref[half:2 * half, :])
    x1 = x_ref[...] + gate_ref[...] * y
    h = _modulated_norm(x1, nw_ref[...], scale_ref[...], shift_ref[...])
    _store_token_rows(h_ref, _pack_bf16_pairs(h))
    x1_ref[...] = x1 + gate_m_ref[...] * _ffn(h.astype(BF16), sg_ref[...], su_ref[...], sd_ref[...])
    scores = jax.nn.sigmoid(_dot_split(h, rw_ref[...]))
    n_exp = scores.shape[1]
    lane = lax.broadcasted_iota(jnp.int32, scores.shape, 1)
    cand = scores + rb_ref[...]
    picked = jnp.zeros(scores.shape, jnp.bool_)
    firsts = []
    for _ in range(TOP_K):
        best = jnp.max(cand, axis=-1, keepdims=True)
        first = jnp.min(jnp.where(cand == best, lane, n_exp), axis=-1, keepdims=True)
        hit = lane == first
        picked = picked | hit
        cand = jnp.where(hit, NEG_BIG, cand)
        firsts.append(first)
    norm = ROUTED_SCALE / jnp.sum(jnp.where(picked, scores, 0.0), axis=-1, keepdims=True)
    onehot = jnp.where(picked, 1.0, 0.0)
    ri = lax.broadcasted_iota(jnp.int32, (tm, tm), 0)
    ci = lax.broadcasted_iota(jnp.int32, (tm, tm), 1)
    before = jnp.where(ci < ri, 1.0, 0.0).astype(BF16)
    pos = _dot(before, onehot.astype(BF16)) + carry_ref[...]
    carry_ref[...] += jnp.sum(onehot, axis=0, keepdims=True)
    count_ref[...] = carry_ref[...]
    slot = lax.broadcasted_iota(jnp.int32, (tm, 3 * TOP_K), 1)
    route = jnp.zeros((tm, 3 * TOP_K), F32)
    for j, first in enumerate(firsts):
        hit = lane == first
        route = jnp.where(slot == j, first.astype(F32), route)
        route = jnp.where(slot == TOP_K + j, _lane_pick(scores, first) * norm, route)
        route = jnp.where(slot == 2 * TOP_K + j, jnp.sum(jnp.where(hit, pos, 0.0), axis=-1, keepdims=True), route)
    route_ref[...] = route


def _outproj(y_gdn, y_nsa, x, w_out_b, mod4, norm_w, router_w, router_bias, sg, su, sd, tm=256):
    bsz, t, d = x.shape
    half = y_gdn.shape[2]
    n_exp = router_w.shape[1]
    sff = sg.shape[1]
    s_tiles = d // 2 // HEAD_DIM

    def mod(j):
        return pl.BlockSpec((None, None, 1, d), lambda b, i: (b, j, 0, 0))

    def const(shape):
        return pl.BlockSpec(shape, lambda b, i: (0, 0))

    row = pl.BlockSpec((None, tm, d), lambda b, i: (b, i, 0))
    return pl.pallas_call(
        _outproj_kernel,
        grid=(bsz, t // tm),
        in_specs=[
            pl.BlockSpec((None, tm, half), lambda b, i: (b, i, 0)),
            pl.BlockSpec((None, tm, half), lambda b, i: (b, i, 0)),
            row,
            const((2 * half, d)),
            mod(2), mod(3), mod(4), mod(5),
            const((1, d)), const((d, n_exp)), const((1, n_exp)),
            const((d, sff)), const((d, sff)), const((sff, d)),
        ],
        out_specs=[row,
                   pl.BlockSpec((tm * s_tiles, HEAD_DIM), lambda b, i: (b * (t // tm) + i, 0)),
                   pl.BlockSpec((None, tm, 3 * TOP_K), lambda b, i: (b, i, 0)),
                   const((1, n_exp))],
        out_shape=[
            jax.ShapeDtypeStruct((bsz, t, d), F32),
            jax.ShapeDtypeStruct((bsz * t * s_tiles, HEAD_DIM), jnp.uint32),
            jax.ShapeDtypeStruct((bsz, t, 3 * TOP_K), F32),
            jax.ShapeDtypeStruct((1, n_exp), F32),
        ],
        scratch_shapes=[pltpu.VMEM((1, n_exp), F32)],
        compiler_params=_cparams(("arbitrary", "arbitrary")),
        name="outproj",
    )(y_gdn, y_nsa, x, w_out_b, mod4, mod4, mod4, mod4, norm_w.reshape(1, d), router_w,
      router_bias.reshape(1, n_exp), sg, su, sd)


MOE_TM = 256
MOE_TOK = 128


def _token_row(ref, r, s_tiles):
    return ref.at[pl.ds(pl.multiple_of(r * s_tiles, s_tiles), s_tiles), :]


def _for_each(n, fn, unroll=8):
    def body(i, carry):
        fn(i)
        return carry

    lax.fori_loop(0, n, body, 0, unroll=unroll)


def _moe_scatter_kernel(gstart_ref, gend_ref, idx_ref, pos_ref, h_ref, xs_hbm, zero_ref, sem, *, s_tiles):
    n_exp = gstart_ref.shape[0]

    @pl.when(pl.program_id(0) == 0)
    def _():
        zero_ref[...] = jnp.zeros_like(zero_ref)

        def tail(e):
            first = pl.multiple_of((gend_ref[e] - MOE_TM) * s_tiles, MOE_TM * s_tiles)
            return pltpu.make_async_copy(zero_ref, xs_hbm.at[pl.ds(first, MOE_TM * s_tiles), :], sem)

        def start_tail(e):
            @pl.when(gend_ref[e] > gstart_ref[e])
            def _():
                tail(e).start()

        def wait_tail(e):
            @pl.when(gend_ref[e] > gstart_ref[e])
            def _():
                tail(e).wait()

        _for_each(n_exp, start_tail)
        _for_each(n_exp, wait_tail)

    def row_copy(t, j):
        n = t * TOP_K + j
        row = gstart_ref[idx_ref[n]] + pos_ref[n]
        return pltpu.make_async_copy(_token_row(h_ref, t, s_tiles), _token_row(xs_hbm, row, s_tiles), sem)

    def start_token(t):
        for j in range(TOP_K):
            row_copy(t, j).start()

    def wait_token(t):
        for j in range(TOP_K):
            row_copy(t, j).wait()

    _for_each(MOE_TOK, start_token, unroll=2)
    _for_each(MOE_TOK, wait_token, unroll=2)


def _moe_scatter(h_rows, idx, pos, gstart, gend, n_rows, s_tiles):
    n_steps = idx.shape[0] // (MOE_TOK * TOP_K)
    smem = pl.BlockSpec((MOE_TOK * TOP_K,), lambda i, gs, ge: (i,), memory_space=pltpu.SMEM)
    return pl.pallas_call(
        functools.partial(_moe_scatter_kernel, s_tiles=s_tiles),
        grid_spec=pltpu.PrefetchScalarGridSpec(
            num_scalar_prefetch=2,
            grid=(n_steps,),
            in_specs=[smem, smem, pl.BlockSpec((MOE_TOK * s_tiles, HEAD_DIM), lambda i, gs, ge: (i, 0))],
            out_specs=pl.BlockSpec(memory_space=pl.ANY),
            scratch_shapes=[pltpu.VMEM((MOE_TM * s_tiles, HEAD_DIM), h_rows.dtype), pltpu.SemaphoreType.DMA(())],
        ),
        out_shape=jax.ShapeDtypeStruct((n_rows * s_tiles, HEAD_DIM), h_rows.dtype),
        compiler_params=_cparams(("arbitrary",)),
        name="moe_scatter",
    )(gstart, gend, idx, pos, h_rows)


def _moe_expert_kernel(texp_ref, tslot_ref, tnext_ref, nvalid_ref, xs_ref, wg_hbm, wu_hbm, wd_hbm, ys_ref,
                       wg_buf, wu_buf, wd_buf, wgb_ref, wub_ref, wdb_ref, sems):
    i = pl.program_id(0)
    d = wgb_ref.shape[0]
    valid = i < nvalid_ref[0]
    new_expert = (i == 0) | (texp_ref[i] != texp_ref[jnp.maximum(i - 1, 0)])
    slot = tslot_ref[i]

    def fetch(expert, s):
        return [pltpu.make_async_copy(src.at[expert], dst.at[s], sems.at[s, n])
                for n, (src, dst) in enumerate(((wg_hbm, wg_buf), (wu_hbm, wu_buf), (wd_hbm, wd_buf)))]

    @pl.when(i == 0)
    def _():
        for cp in fetch(texp_ref[0], slot):
            cp.start()

    @pl.when(valid & new_expert)
    def _():
        nxt = tnext_ref[i]

        @pl.when(nxt < nvalid_ref[0])
        def _():
            for cp in fetch(texp_ref[nxt], 1 - slot):
                cp.start()

        for cp in fetch(texp_ref[i], slot):
            cp.wait()
        wgb_ref[...] = wg_buf[slot].astype(BF16)
        wub_ref[...] = wu_buf[slot].astype(BF16)
        wdb_ref[...] = wd_buf[slot].astype(BF16)

    @pl.when(valid)
    def _():
        half = d // 2
        lo, hi = _unpack_bf16_pairs(_load_token_rows(xs_ref, MOE_TM, half))
        lo, hi = lo.astype(BF16), hi.astype(BF16)
        gate = _dot(lo, wgb_ref[0:half, :]) + _dot(hi, wgb_ref[half:d, :])
        up = _dot(lo, wub_ref[0:half, :]) + _dot(hi, wub_ref[half:d, :])
        y = _dot((_silu(gate) * up).astype(BF16), wdb_ref[...])
        _store_token_rows(ys_ref, _pack_bf16_pairs(y))

    @pl.when(jnp.logical_not(valid))
    def _():
        ys_ref[...] = jnp.zeros_like(ys_ref)


def _moe_experts(xs, tile_expert, tile_slot, tile_next, nvalid, wg, wu, wd):
    n_exp, d, ff = wg.shape
    blk = xs.shape[0] // tile_expert.shape[0]
    hbm = pl.BlockSpec(memory_space=pl.ANY)
    return pl.pallas_call(
        _moe_expert_kernel,
        grid_spec=pltpu.PrefetchScalarGridSpec(
            num_scalar_prefetch=4,
            grid=(xs.shape[0] // blk,),
            in_specs=[
                pl.BlockSpec((blk, HEAD_DIM), lambda i, te, ts, tn, nv: (jnp.minimum(i, nv[0] - 1), 0)),
                hbm, hbm, hbm,
            ],
            out_specs=pl.BlockSpec((blk, HEAD_DIM), lambda i, te, ts, tn, nv: (i, 0)),
            scratch_shapes=[pltpu.VMEM((2, d, ff), wg.dtype), pltpu.VMEM((2, d, ff), wu.dtype),
                            pltpu.VMEM((2, ff, d), wd.dtype),
                            pltpu.VMEM((d, ff), BF16), pltpu.VMEM((d, ff), BF16), pltpu.VMEM((ff, d), BF16),
                            pltpu.SemaphoreType.DMA((2, 3))],
        ),
        out_shape=jax.ShapeDtypeStruct(xs.shape, xs.dtype),
        compiler_params=_cparams(("arbitrary",)),
        name="moe_experts",
    )(tile_expert, tile_slot, tile_next, nvalid, xs, wg, wu, wd)


def _moe_combine_kernel(gstart_ref, idx_ref, pos_ref, w_ref, ys_hbm, x1_ref, gate_ref, nw_ref, o_ref,
                        buf_ref, lo_ref, hi_ref, sem, *, s_tiles):
    tm, d = x1_ref.shape

    def row_copy(n):
        row = gstart_ref[idx_ref[n]] + pos_ref[n]
        return pltpu.make_async_copy(_token_row(ys_hbm, row, s_tiles), _token_row(buf_ref, n, s_tiles), sem)

    _for_each(tm * TOP_K, lambda n: row_copy(n).start())
    _for_each(tm * TOP_K, lambda n: row_copy(n).wait())

    def token(t):
        n0 = t * TOP_K
        lo, hi = _unpack_bf16_pairs(_token_row(buf_ref, n0, s_tiles)[...])
        acc_lo, acc_hi = w_ref[n0] * lo, w_ref[n0] * hi
        for j in range(1, TOP_K):
            lo, hi = _unpack_bf16_pairs(_token_row(buf_ref, n0 + j, s_tiles)[...])
            acc_lo, acc_hi = acc_lo + w_ref[n0 + j] * lo, acc_hi + w_ref[n0 + j] * hi
        _token_row(lo_ref, t, s_tiles)[...] = acc_lo
        _token_row(hi_ref, t, s_tiles)[...] = acc_hi

    _for_each(tm, token)
    routed = jnp.concatenate([_load_token_rows(lo_ref, tm, d // 2), _load_token_rows(hi_ref, tm, d // 2)], axis=1)
    x2 = x1_ref[...] + gate_ref[...] * routed
    o_ref[...] = x2 * lax.rsqrt(jnp.mean(x2 * x2, axis=-1, keepdims=True) + EPS) * nw_ref[...]


def _moe_combine(ys, idx, pos, w, gstart, x1, mod4, norm_w):
    bsz, t, d = x1.shape
    s_tiles = d // 2 // HEAD_DIM
    tm = MOE_TOK
    nt = t // tm
    smem = pl.BlockSpec((tm * TOP_K,), lambda b, i, gs: (b * nt + i,), memory_space=pltpu.SMEM)
    row = pl.BlockSpec((None, tm, d), lambda b, i, gs: (b, i, 0))
    return pl.pallas_call(
        functools.partial(_moe_combine_kernel, s_tiles=s_tiles),
        grid_spec=pltpu.PrefetchScalarGridSpec(
            num_scalar_prefetch=1,
            grid=(bsz, nt),
            in_specs=[
                smem, smem, smem,
                pl.BlockSpec(memory_space=pl.ANY),
                row,
                pl.BlockSpec((None, None, 1, d), lambda b, i, gs: (b, 5, 0, 0)),
                pl.BlockSpec((1, d), lambda b, i, gs: (0, 0)),
            ],
            out_specs=row,
            scratch_shapes=[pltpu.VMEM((tm * TOP_K * s_tiles, HEAD_DIM), ys.dtype),
                            pltpu.VMEM((tm * s_tiles, HEAD_DIM), F32), pltpu.VMEM((tm * s_tiles, HEAD_DIM), F32),
                            pltpu.SemaphoreType.DMA(())],
        ),
        out_shape=jax.ShapeDtypeStruct((bsz, t, d), F32),
        compiler_params=_cparams(("arbitrary", "arbitrary")),
        name="moe_combine",
    )(gstart, idx, pos, w, ys, x1, mod4, norm_w.reshape(1, d))


def _moe(h_rows, route, counts, x1, mod4, norm_w, wg, wu, wd):
    bsz, t, d = x1.shape
    n_exp = wg.shape[0]
    s_tiles = h_rows.shape[0] // (bsz * t)
    idx = route[..., 0:TOP_K].astype(jnp.int32).reshape(-1)
    w = route[..., TOP_K:2 * TOP_K].reshape(-1)
    pos = route[..., 2 * TOP_K:3 * TOP_K].astype(jnp.int32).reshape(-1)
    padded = (counts.reshape(n_exp).astype(jnp.int32) + MOE_TM - 1) // MOE_TM * MOE_TM
    gend = jnp.cumsum(padded)
    gstart = gend - padded
    nvalid = gend[-1:] // MOE_TM
    n_tiles = bsz * t * TOP_K // MOE_TM + n_exp
    tile_start = jnp.minimum(jnp.arange(n_tiles, dtype=jnp.int32), nvalid - 1) * MOE_TM
    tile_expert = jnp.sum((gend[None, :] <= tile_start[:, None]).astype(jnp.int32), axis=1)
    first_of_group = jnp.concatenate([jnp.ones((1,), jnp.int32),
                                      (tile_expert[1:] != tile_expert[:-1]).astype(jnp.int32)])
    tile_slot = (jnp.cumsum(first_of_group) - 1) % 2
    experts = jnp.arange(n_exp, dtype=jnp.int32)
    tile_next = jnp.sum(jnp.where(tile_expert[:, None] == experts[None, :], gend[None, :], 0), axis=1) // MOE_TM
    xs = _moe_scatter(h_rows, idx, pos, gstart, gend, n_tiles * MOE_TM, s_tiles)
    ys = _moe_experts(xs, tile_expert, tile_slot, tile_next, nvalid, wg, wu, wd)
    return _moe_combine(ys, idx, pos, w, gstart, x1, mod4, norm_w)


def _reorder_w_in(w_in):
    d = w_in.shape[0]
    o_beta = 3 * GDN_WIDTH
    o_a = o_beta + GDN_HEADS
    o_gate = o_a + GDN_HEADS
    o_q = o_gate + GDN_WIDTH
    o_kv = o_q + NSA_WIDTH
    o_ng = o_kv + 6 * NSA_KV_WIDTH
    n_small = 2 * GDN_HEADS + 3 * NSA_HEADS
    parts = [w_in[:, :o_beta], w_in[:, o_gate:o_q], w_in[:, o_q:o_kv], w_in[:, o_kv:o_ng],
             w_in[:, o_beta:o_gate], w_in[:, o_ng:o_ng + 3 * NSA_HEADS]]
    used = COL_SMALL * HEAD_DIM + n_small
    parts.append(jnp.zeros((d, PROJ_COLS - used), w_in.dtype))
    return jnp.concatenate([p.astype(BF16) for p in parts], axis=1)


def _layer(x, mod, norm_attn_w, norm_ffn_w, w_in, gdn_conv_w, gdn_a_log, gdn_dt_bias, gdn_norm_w,
           cmp_pe_k, cmp_w1_k, cmp_w2_k, cmp_pe_v, cmp_w1_v, cmp_w2_v, w_out, router_w, router_bias,
           expert_w_gate, expert_w_up, expert_w_down, shared_w_gate, shared_w_up, shared_w_down, final_w):
    bsz, t, d = x.shape
    mod4 = mod.reshape(bsz, N_MOD, 1, d)
    proj = _proj(x, mod4, norm_attn_w, _reorder_w_in(w_in))
    y_gdn = _gdn(proj, gdn_conv_w, gdn_a_log, gdn_dt_bias, gdn_norm_w)
    kc = _compress(proj, COL_KV, cmp_pe_k, cmp_w1_k, cmp_w2_k)
    vc = _compress(proj, COL_KV + NSA_KV_HEADS, cmp_pe_v, cmp_w1_v, cmp_w2_v)
    o_cmp, sel = _cmpattn(proj, kc, vc)
    o_sel = _selattn(proj, sel)
    y_nsa = _winattn(proj, o_cmp, o_sel)
    x1, h_rows, route, counts = _outproj(
        y_gdn, y_nsa, x, w_out.astype(BF16), mod4, norm_ffn_w, router_w, router_bias,
        shared_w_gate.astype(BF16), shared_w_up.astype(BF16), shared_w_down.astype(BF16))
    return _moe(h_rows, route, counts, x1, mod4, final_w, expert_w_gate, expert_w_up, expert_w_down)


def kernel(x, c, w_ada, b_ada, norm_attn_w, norm_ffn_w, norm_final_w, w_in, gdn_conv_w, gdn_a_log, gdn_dt_bias,
           gdn_norm_w, cmp_pe_k, cmp_w1_k, cmp_w2_k, cmp_pe_v, cmp_w1_v, cmp_w2_v, w_out, router_w, router_bias,
           expert_w_gate, expert_w_up, expert_w_down, shared_w_gate, shared_w_up, shared_w_down):
    depth = w_ada.shape[0]
    assert depth == 1, "the fused final norm assumes a single layer"
    l = 0
    mod = _ada(c, w_ada[l], b_ada[l])
    return _layer(x, mod, norm_attn_w[l], norm_ffn_w[l], w_in[l], gdn_conv_w[l], gdn_a_log[l], gdn_dt_bias[l],
                  gdn_norm_w[l], cmp_pe_k[l], cmp_w1_k[l], cmp_w2_k[l], cmp_pe_v[l], cmp_w1_v[l], cmp_w2_v[l],
                  w_out[l], router_w[l], router_bias[l], expert_w_gate[l], expert_w_up[l], expert_w_down[l],
                  shared_w_gate[l], shared_w_up[l], shared_w_down[l], norm_final_w)
```

```python
import functools

import jax
import jax.numpy as jnp
from jax import lax
from jax.experimental import pallas as pl
from jax.experimental.pallas import tpu as pltpu

F32 = jnp.float32
BF16 = jnp.bfloat16
HI = lax.Precision.HIGHEST

HEAD_DIM = 128
GDN_HEADS = 8
NSA_HEADS = 8
NSA_KV_HEADS = 2
NSA_GROUP = NSA_HEADS // NSA_KV_HEADS
GDN_WIDTH = GDN_HEADS * HEAD_DIM
NSA_WIDTH = NSA_HEADS * HEAD_DIM
NSA_KV_WIDTH = NSA_KV_HEADS * HEAD_DIM
CONV_WIDTH = 4
GDN_CHUNK = 64
CMP_BLOCK = 32
CMP_STRIDE = 16
SEL_BLOCK = 64
SEL_TOPK = 16
N_LOCAL_BLOCKS = 2
WINDOW = 512
Q_BLOCK = 128
TOP_K = 8
ROUTED_SCALE = 2.5
N_MOD = 6
EPS = 1e-6

COL_GDN_Q = 0
COL_GDN_K = GDN_HEADS
COL_GDN_V = 2 * GDN_HEADS
COL_GDN_GATE = 3 * GDN_HEADS
COL_NSA_Q = 4 * GDN_HEADS
COL_KV = COL_NSA_Q + NSA_HEADS
COL_SMALL = COL_KV + 6 * NSA_KV_HEADS
N_COL_BLOCKS = COL_SMALL + 1
PROJ_TN = 2304
PROJ_COLS = 3 * PROJ_TN
SMALL_BETA = 0
SMALL_A = GDN_HEADS
SMALL_NSA_GATE = 2 * GDN_HEADS

NEG_BIG = -1e30
VMEM_LIMIT = 48 * 1024 * 1024


def _cparams(semantics, vmem_limit=VMEM_LIMIT):
    return pltpu.CompilerParams(dimension_semantics=semantics, vmem_limit_bytes=vmem_limit)


def _silu(x):
    return (0.5 * x) * (1.0 + jnp.tanh(0.5 * x))


def _dot(a, b, precision=None):
    return jnp.dot(a, b, preferred_element_type=F32, precision=precision)


def _dot_nt(a, b, precision=None):
    return lax.dot_general(a, b, (((1,), (1,)), ((), ())), preferred_element_type=F32, precision=precision)


def _dot_tn(a, b, precision=None):
    return lax.dot_general(a, b, (((0,), (0,)), ((), ())), preferred_element_type=F32, precision=precision)


def _split_bf16(x):
    hi = x.astype(BF16)
    return hi, (x - hi.astype(F32)).astype(BF16)


def _dot_split(a, b, nt=False):
    (ah, al), (bh, bl) = _split_bf16(a), _split_bf16(b)
    f = _dot_nt if nt else _dot
    return f(ah, bh) + (f(ah, bl) + f(al, bh))


def _lane_pick(x, idx):
    lane = lax.broadcasted_iota(jnp.int32, x.shape, 1)
    return jnp.sum(jnp.where(lane == idx, x, 0.0), axis=-1, keepdims=True)


def _ada_kernel(ct_ref, w_ref, b_ref, o_ref):
    c = ct_ref[...]
    cond = _silu(c)
    w = w_ref[...]
    rows = [jnp.sum(w * cond[:, b:b + 1], axis=0, keepdims=True) for b in range(c.shape[1])]
    o_ref[...] = jnp.concatenate(rows, axis=0) + b_ref[...]


def _ada(c, w_ada, b_ada, tn=512):
    bsz, d = c.shape
    n = w_ada.shape[1]
    return pl.pallas_call(
        _ada_kernel,
        grid=(n // tn,),
        in_specs=[
            pl.BlockSpec((d, bsz), lambda j: (0, 0)),
            pl.BlockSpec((d, tn), lambda j: (0, j)),
            pl.BlockSpec((1, tn), lambda j: (0, j)),
        ],
        out_specs=pl.BlockSpec((bsz, tn), lambda j: (0, j)),
        out_shape=jax.ShapeDtypeStruct((bsz, n), F32),
        compiler_params=_cparams(("arbitrary",)),
        name="ada",
    )(c.T, w_ada, b_ada.reshape(1, n))


def _modulated_norm(x, nw, scale, shift):
    var = jnp.mean(x * x, axis=-1, keepdims=True)
    h = x * lax.rsqrt(var + EPS) * nw
    return h * (1.0 + scale) + shift


def _proj_kernel(x_ref, shift_ref, scale_ref, nw_ref, w_ref, o_ref):
    h = _modulated_norm(x_ref[...], nw_ref[...], scale_ref[...], shift_ref[...])
    o_ref[...] = _dot(h.astype(BF16), w_ref[...])


def _proj(x, mod4, norm_w, w_r, tm=256):
    bsz, t, d = x.shape
    n = w_r.shape[1]
    return pl.pallas_call(
        _proj_kernel,
        grid=(n // PROJ_TN, bsz, t // tm),
        in_specs=[
            pl.BlockSpec((None, tm, d), lambda j, b, i: (b, i, 0)),
            pl.BlockSpec((None, None, 1, d), lambda j, b, i: (b, 0, 0, 0)),
            pl.BlockSpec((None, None, 1, d), lambda j, b, i: (b, 1, 0, 0)),
            pl.BlockSpec((1, d), lambda j, b, i: (0, 0)),
            pl.BlockSpec((d, PROJ_TN), lambda j, b, i: (0, j)),
        ],
        out_specs=pl.BlockSpec((None, tm, PROJ_TN), lambda j, b, i: (b, i, j)),
        out_shape=jax.ShapeDtypeStruct((bsz, t, n), F32),
        compiler_params=_cparams(("arbitrary", "arbitrary", "arbitrary")),
        name="proj",
    )(x, mod4, mod4, norm_w.reshape(1, d), w_r)


def _gdn_kernel(q_ref, k_ref, v_ref, gate_ref, small_ref, hq_ref, hk_ref, hv_ref,
                cwq_ref, cwk_ref, cwv_ref, alog_ref, dtb_ref, nw_ref, o_ref, s_ref, xc_ref, *, tt, hg):
    first = pl.program_id(2) == 0

    @pl.when(first)
    def _():
        s_ref[...] = jnp.zeros_like(s_ref)

    def conv_silu(x_ref, halo_ref, cw_ref):
        xc_ref[0:8, :] = jnp.where(first, 0.0, halo_ref[...])
        xc_ref[8:tt + 8, :] = x_ref[...]
        cw = cw_ref[...]
        off = 8 - (CONV_WIDTH - 1)
        y = cw[0:1, :] * xc_ref[off:off + tt, :]
        for i in range(1, CONV_WIDTH):
            y = y + cw[i:i + 1, :] * xc_ref[off + i:off + i + tt, :]
        return _silu(y)

    q_all = conv_silu(q_ref, hq_ref, cwq_ref)
    k_all = conv_silu(k_ref, hk_ref, cwk_ref)
    v_all = conv_silu(v_ref, hv_ref, cwv_ref)
    ri = lax.broadcasted_iota(jnp.int32, (tt, tt), 0)
    ci = lax.broadcasted_iota(jnp.int32, (tt, tt), 1)
    lower = jnp.where(((ri // GDN_CHUNK) == (ci // GDN_CHUNK)) & (ci <= ri), 1.0, 0.0).astype(BF16)
    heads = [pl.program_id(1) * hg + hh for hh in range(hg)]
    outs = _gdn_heads(q_all, k_all, v_all, small_ref[...], lower, heads, alog_ref[...], dtb_ref[...], s_ref, tt)
    for hh, o in enumerate(outs):
        cols = slice(hh * HEAD_DIM, (hh + 1) * HEAD_DIM)
        o = o * lax.rsqrt(jnp.mean(o * o, axis=-1, keepdims=True) + EPS) * nw_ref[...]
        o_ref[:, cols] = (o * _silu(gate_ref[:, cols])).astype(o_ref.dtype)


def _gdn_heads(q_all, k_all, v_all, small, lower, heads, alog, dtb, s_ref, tt):
    c = GDN_CHUNK
    nch = tt // c
    hg = len(heads)
    kpos = lax.broadcasted_iota(jnp.int32, (tt, c), 0) % c
    jj = lax.broadcasted_iota(jnp.int32, (tt, c), 1)
    per_head = []
    for hh, h in enumerate(heads):
        cols = slice(hh * HEAD_DIM, (hh + 1) * HEAD_DIM)
        q, k, v = q_all[:, cols], k_all[:, cols], v_all[:, cols]
        q = q * lax.rsqrt(jnp.sum(q * q, axis=-1, keepdims=True) + EPS) * (HEAD_DIM ** -0.5)
        k = k * lax.rsqrt(jnp.sum(k * k, axis=-1, keepdims=True) + EPS)
        beta = jax.nn.sigmoid(_lane_pick(small, SMALL_BETA + h))
        z = _lane_pick(small, SMALL_A + h) + _lane_pick(dtb, h)
        softplus = jnp.maximum(z, 0.0) + jnp.log(1.0 + jnp.exp(-jnp.abs(z)))
        g = -jnp.exp(_lane_pick(alog, h)) * softplus
        g_c = jnp.broadcast_to(g, (tt, c))
        rhs = jnp.concatenate([g_c, jnp.where(kpos > jj, g_c, 0.0)], axis=1)
        r_hi = rhs.astype(BF16)
        rem = rhs - r_hi.astype(F32)
        r_mid = rem.astype(BF16)
        r_lo = (rem - r_mid.astype(F32)).astype(BF16)
        sums = _dot(lower, r_hi) + (_dot(lower, r_mid) + _dot(lower, r_lo))
        gcum = jnp.broadcast_to(sums[:, 0:1], (tt, HEAD_DIM))
        gc3 = gcum.reshape(nch, c, HEAD_DIM)
        grest = (gc3[:, c - 1:c, :] - gc3).reshape(tt, HEAD_DIM)
        decay_h = jnp.where(kpos >= jj, jnp.exp(sums[:, c:2 * c]), 0.0)
        per_head.append((q, k, v * beta, k * beta, jnp.exp(gcum), jnp.exp(grest), decay_h))

    def stack(j):
        x = jnp.concatenate([ph[j] for ph in per_head], axis=0)
        return x.reshape(hg * nch, c, x.shape[-1])

    q, k, vb, kb, egc, egr, decay = (stack(j) for j in range(7))
    kpos_all = lax.broadcasted_iota(jnp.int32, (hg * tt, c), 0) % c
    jj_all = lax.broadcasted_iota(jnp.int32, (hg * tt, c), 1)
    strict = (kpos_all > jj_all).reshape(hg * nch, c, c)
    eye = jnp.where(kpos_all == jj_all, 1.0, 0.0).reshape(hg * nch, c, c)

    def bmm(a, b):
        return jnp.einsum('nij,njk->nik', a, b, preferred_element_type=F32)

    def bmm_nt(a, b):
        return jnp.einsum('nid,njd->nij', a, b, preferred_element_type=F32)

    def bmm_split(a, b):
        (ah, al), (bh, bl) = _split_bf16(a), _split_bf16(b)
        return bmm(ah, bh) + (bmm(ah, bl) + bmm(al, bh))

    k3 = k.astype(BF16)
    a = jnp.where(strict, bmm_nt(kb.astype(BF16), k3) * decay, 0.0)
    inv = eye - a
    p = bmm_split(a, a)
    n_sq = (c - 1).bit_length() - 1
    for lvl in range(n_sq):
        inv = inv + bmm_split(inv, p)
        if lvl + 1 < n_sq:
            p = bmm_split(p, p)
    inv_b = inv.astype(BF16)
    u3 = bmm(inv_b, vb.astype(BF16)).astype(BF16)
    w3 = bmm(inv_b, (kb * egc).astype(BF16)).astype(BF16)
    qk3 = (bmm_nt(q.astype(BF16), k3) * decay).astype(BF16)
    kd3 = (k * egr).astype(BF16)
    qe3 = (q * egc - bmm(qk3, w3)).astype(BF16)
    o03 = bmm(qk3, u3)
    kw = [_dot_tn(kd3[i], w3[i]).astype(BF16) for i in range(hg * nch)]
    ku = [_dot_tn(kd3[i], u3[i]) for i in range(hg * nch)]

    states = [s_ref[hh] for hh in range(hg)]
    outs = [[] for _ in range(hg)]
    for n in range(nch):
        for hh in range(hg):
            i = hh * nch + n
            state_b = states[hh].astype(BF16)
            outs[hh].append(_dot(qe3[i], state_b) + o03[i])
            states[hh] = states[hh] * egc[i, c - 1:c, :] - _dot(kw[i], state_b) + ku[i]
    for hh in range(hg):
        s_ref[hh] = states[hh]
    return [jnp.concatenate(o, axis=0) for o in outs]


def _gdn(proj, conv_w, a_log, dt_bias, norm_w, tt=128, hg=8):
    bsz, t, _ = proj.shape
    wdt = hg * HEAD_DIM

    def col(base):
        return pl.BlockSpec((None, tt, wdt), lambda b, h, i: (b, i, base // hg + h))

    def halo(base):
        return pl.BlockSpec((None, 8, wdt), lambda b, h, i: (b, jnp.maximum(i * (tt // 8) - 1, 0), base // hg + h))

    def cw(base):
        return pl.BlockSpec((CONV_WIDTH, wdt), lambda b, h, i: (0, base // hg + h))

    heads = pl.BlockSpec((1, GDN_HEADS), lambda b, h, i: (0, 0))
    return pl.pallas_call(
        functools.partial(_gdn_kernel, tt=tt, hg=hg),
        grid=(bsz, GDN_HEADS // hg, t // tt),
        in_specs=[
            col(COL_GDN_Q), col(COL_GDN_K), col(COL_GDN_V), col(COL_GDN_GATE),
            pl.BlockSpec((None, tt, HEAD_DIM), lambda b, h, i: (b, i, COL_SMALL)),
            halo(COL_GDN_Q), halo(COL_GDN_K), halo(COL_GDN_V),
            cw(COL_GDN_Q), cw(COL_GDN_K), cw(COL_GDN_V),
            heads, heads,
            pl.BlockSpec((1, HEAD_DIM), lambda b, h, i: (0, 0)),
        ],
        out_specs=pl.BlockSpec((None, tt, wdt), lambda b, h, i: (b, i, h)),
        out_shape=jax.ShapeDtypeStruct((bsz, t, GDN_WIDTH), BF16),
        scratch_shapes=[pltpu.VMEM((hg, HEAD_DIM, HEAD_DIM), F32), pltpu.VMEM((tt + 8, wdt), F32)],
        compiler_params=_cparams(("arbitrary", "arbitrary", "arbitrary")),
        name="gdn",
    )(proj, proj, proj, proj, proj, proj, proj, proj, conv_w, conv_w, conv_w,
      a_log.reshape(1, GDN_HEADS), dt_bias.reshape(1, GDN_HEADS), norm_w.reshape(1, HEAD_DIM))


def _compress_kernel(x_ref, pe_ref, w1_ref, w2_ref, o_ref):
    r = x_ref.shape[0] // CMP_STRIDE
    d = x_ref.shape[1]
    a = b = None
    for l in range(CMP_STRIDE):
        x_l = x_ref[pl.ds(l, r, stride=CMP_STRIDE), :]
        a_l = _dot_split(x_l + pe_ref[l:l + 1, :], w1_ref[l * d:(l + 1) * d, :])
        b_l = _dot_split(x_l + pe_ref[CMP_STRIDE + l:CMP_STRIDE + l + 1, :],
                         w1_ref[(CMP_STRIDE + l) * d:(CMP_STRIDE + l + 1) * d, :])
        a, b = (a_l, b_l) if a is None else (a + a_l, b + b_l)
    hid = _silu(a + pltpu.roll(b, r - 1, 0))
    o_ref[...] = _dot_split(hid, w2_ref[...])


def _compress(proj, col_block, pe, w1, w2):
    bsz, t, _ = proj.shape
    r = t // CMP_STRIDE
    hid = w1.shape[1]
    return pl.pallas_call(
        _compress_kernel,
        grid=(bsz, NSA_KV_HEADS),
        in_specs=[
            pl.BlockSpec((None, t, HEAD_DIM), lambda b, h: (b, 0, col_block + h)),
            pl.BlockSpec((CMP_BLOCK, HEAD_DIM), lambda b, h: (0, 0)),
            pl.BlockSpec((CMP_BLOCK * HEAD_DIM, hid), lambda b, h: (0, 0)),
            pl.BlockSpec((hid, HEAD_DIM), lambda b, h: (0, 0)),
        ],
        out_specs=pl.BlockSpec((None, None, r, HEAD_DIM), lambda b, h: (b, h, 0, 0)),
        out_shape=jax.ShapeDtypeStruct((bsz, NSA_KV_HEADS, r, HEAD_DIM), F32),
        compiler_params=_cparams(("arbitrary", "arbitrary")),
        name="compress",
    )(proj, pe, w1, w2)


def _cmpattn_kernel(q_ref, kc_ref, vc_ref, o_ref, sel_ref, *, tq, nc, nsel):
    ncp = kc_ref.shape[0]
    scale = HEAD_DIM ** -0.5
    t0 = pl.program_id(2) * tq
    kc = kc_ref[...]
    vc = vc_ref[...]
    pos_r = t0 + lax.broadcasted_iota(jnp.int32, (tq, ncp), 0)
    c_r = lax.broadcasted_iota(jnp.int32, (tq, ncp), 1)
    valid_r = (c_r * CMP_STRIDE + CMP_BLOCK - 1 <= pos_r) & (c_r < nc)
    pos_c = t0 + lax.broadcasted_iota(jnp.int32, (ncp, tq), 1)
    c_c = lax.broadcasted_iota(jnp.int32, (ncp, tq), 0)
    valid_c = (c_c * CMP_STRIDE + CMP_BLOCK - 1 <= pos_c) & (c_c < nc)
    psum = jnp.zeros((ncp, tq), F32)
    kc_hi, kc_lo = _split_bf16(kc)
    vc_b = vc.astype(BF16)
    for g in range(NSA_GROUP):
        q_hi, q_lo = _split_bf16(q_ref[:, g * HEAD_DIM:(g + 1) * HEAD_DIM])
        s = _dot_nt(q_hi, kc_hi) + (_dot_nt(q_hi, kc_lo) + _dot_nt(q_lo, kc_hi))
        s = jnp.where(valid_r, s * scale, NEG_BIG)
        e = jnp.where(valid_r, jnp.exp(s - jnp.max(s, axis=-1, keepdims=True)), 0.0)
        p = e * (1.0 / jnp.maximum(jnp.sum(e, axis=-1, keepdims=True), 1e-30))
        o_ref[:, g * HEAD_DIM:(g + 1) * HEAD_DIM] = _dot(p.astype(BF16), vc_b)
        st = _dot_nt(kc_hi, q_hi) + (_dot_nt(kc_lo, q_hi) + _dot_nt(kc_hi, q_lo))
        st = jnp.where(valid_c, st * scale, NEG_BIG)
        et = jnp.where(valid_c, jnp.exp(st - jnp.max(st, axis=0, keepdims=True)), 0.0)
        psum = psum + et * (1.0 / jnp.maximum(jnp.sum(et, axis=0, keepdims=True), 1e-30))
    j_o = lax.broadcasted_iota(jnp.int32, (nsel, ncp), 0) * SEL_BLOCK
    c_o = lax.broadcasted_iota(jnp.int32, (nsel, ncp), 1) * CMP_STRIDE
    overlap = jnp.where((c_o < j_o + SEL_BLOCK) & (c_o + CMP_BLOCK > j_o), 1.0, 0.0).astype(BF16)
    p_hi, p_lo = _split_bf16(psum)
    imp = _dot(overlap, p_hi) + _dot(overlap, p_lo)
    blk = lax.broadcasted_iota(jnp.int32, (nsel, tq), 0)
    dist = (t0 + lax.broadcasted_iota(jnp.int32, (nsel, tq), 1)) // SEL_BLOCK - blk
    forced = (blk == 0) | ((dist >= 0) & (dist < N_LOCAL_BLOCKS))
    val = jnp.where(forced, -NEG_BIG, jnp.where(dist >= 0, imp, -1.0))
    rank = jnp.zeros((nsel, tq), F32)
    for j in range(nsel):
        row = val[j:j + 1, :]
        rank = rank + jnp.where((row > val) | ((row == val) & (blk > j)), 1.0, 0.0)
    sel = jnp.where((rank < min(SEL_TOPK, nsel)) & (dist >= 0), 1.0, 0.0)
    sel_ref[...] = sel.T


def _cmpattn(proj, kc, vc, tq=256):
    bsz, t, _ = proj.shape
    ncp = kc.shape[2]
    nc = (t - CMP_BLOCK) // CMP_STRIDE + 1
    nsel = t // SEL_BLOCK
    gw = NSA_GROUP * HEAD_DIM
    return pl.pallas_call(
        functools.partial(_cmpattn_kernel, tq=tq, nc=nc, nsel=nsel),
        grid=(bsz, NSA_KV_HEADS, t // tq),
        in_specs=[
            pl.BlockSpec((None, tq, gw), lambda b, h, i: (b, i, COL_NSA_Q // NSA_GROUP + h)),
            pl.BlockSpec((None, None, ncp, HEAD_DIM), lambda b, h, i: (b, h, 0, 0)),
            pl.BlockSpec((None, None, ncp, HEAD_DIM), lambda b, h, i: (b, h, 0, 0)),
        ],
        out_specs=[
            pl.BlockSpec((None, tq, gw), lambda b, h, i: (b, i, h)),
            pl.BlockSpec((None, None, tq, nsel), lambda b, h, i: (b, h, i, 0)),
        ],
        out_shape=[
            jax.ShapeDtypeStruct((bsz, t, NSA_WIDTH), F32),
            jax.ShapeDtypeStruct((bsz, NSA_KV_HEADS, t, nsel), F32),
        ],
        compiler_params=_cparams(("arbitrary", "arbitrary", "arbitrary")),
        name="cmpattn",
    )(proj, kc, vc)


def _selattn_kernel(q_ref, k_ref, v_ref, sel_ref, o_ref, m_ref, l_ref, acc_ref, *, tq, tk, nsel):
    g = NSA_GROUP
    qi = pl.program_id(2)
    m_ref[...] = jnp.full_like(m_ref, NEG_BIG)
    l_ref[...] = jnp.zeros_like(l_ref)
    acc_ref[...] = jnp.zeros_like(acc_ref)
    q4 = jnp.concatenate([q_ref[:, i * HEAD_DIM:(i + 1) * HEAD_DIM] for i in range(g)], axis=0)
    q4 = (q4 * HEAD_DIM ** -0.5).astype(BF16)
    sel = sel_ref[...].astype(BF16)
    pos = qi * tq + lax.broadcasted_iota(jnp.int32, (tq, tk), 0)

    def kv_step(kv, carry):
        k0 = pl.multiple_of(kv * tk, tk)
        s = _dot_nt(q4, k_ref[pl.ds(k0, tk), :].astype(BF16))
        key = k0 + lax.broadcasted_iota(jnp.int32, (nsel, tk), 1)
        blk = lax.broadcasted_iota(jnp.int32, (nsel, tk), 0)
        expand = jnp.where(key // SEL_BLOCK == blk, 1.0, 0.0).astype(BF16)
        picked = _dot(sel, expand)
        kpos = k0 + lax.broadcasted_iota(jnp.int32, (tq, tk), 1)
        mask = ((picked > 0.5) & (kpos <= pos))[None]
        s = jnp.where(mask, s.reshape(g, tq, tk), NEG_BIG)
        m_prev = m_ref[...]
        m_new = jnp.maximum(m_prev, jnp.max(s, axis=-1, keepdims=True))
        p = jnp.exp(s - m_new)
        alpha = jnp.exp(m_prev - m_new)
        l_ref[...] = alpha * l_ref[...] + jnp.sum(p, axis=-1, keepdims=True)
        pv = _dot(p.reshape(g * tq, tk).astype(BF16), v_ref[pl.ds(k0, tk), :].astype(BF16))
        acc_ref[...] = alpha * acc_ref[...] + pv.reshape(g, tq, HEAD_DIM)
        m_ref[...] = m_new
        return carry

    lax.fori_loop(0, (qi * tq + tq - 1) // tk + 1, kv_step, 0)
    o = acc_ref[...] * (1.0 / jnp.maximum(l_ref[...], 1e-30))
    for i in range(g):
        o_ref[:, i * HEAD_DIM:(i + 1) * HEAD_DIM] = o[i]


def _selattn(proj, sel, tq=2 * Q_BLOCK, tk=512):
    bsz, t, _ = proj.shape
    nsel = t // SEL_BLOCK
    tk = min(tk, t)
    gw = NSA_GROUP * HEAD_DIM

    def kv_spec(base):
        return pl.BlockSpec((None, t, HEAD_DIM), lambda b, h, i: (b, 0, base + h))

    return pl.pallas_call(
        functools.partial(_selattn_kernel, tq=tq, tk=tk, nsel=nsel),
        grid=(bsz, NSA_KV_HEADS, t // tq),
        in_specs=[
            pl.BlockSpec((None, tq, gw), lambda b, h, i: (b, i, COL_NSA_Q // NSA_GROUP + h)),
            kv_spec(COL_KV + 2 * NSA_KV_HEADS), kv_spec(COL_KV + 3 * NSA_KV_HEADS),
            pl.BlockSpec((None, None, tq, nsel), lambda b, h, i: (b, h, i, 0)),
        ],
        out_specs=pl.BlockSpec((None, tq, gw), lambda b, h, i: (b, i, h)),
        out_shape=jax.ShapeDtypeStruct((bsz, t, NSA_WIDTH), F32),
        scratch_shapes=[pltpu.VMEM((NSA_GROUP, tq, 1), F32), pltpu.VMEM((NSA_GROUP, tq, 1), F32),
                        pltpu.VMEM((NSA_GROUP, tq, HEAD_DIM), F32)],
        compiler_params=_cparams(("arbitrary", "arbitrary", "arbitrary")),
        name="selattn",
    )(proj, proj, proj, sel)


def _winattn_kernel(*refs, tq, nblk):
    q_ref = refs[0]
    k_refs = refs[1:1 + nblk]
    v_refs = refs[1 + nblk:1 + 2 * nblk]
    ocmp_ref, osel_ref, small_ref, o_ref = refs[1 + 2 * nblk:]
    g = NSA_GROUP
    hk = pl.program_id(1)
    qi = pl.program_id(2)
    scale = HEAD_DIM ** -0.5
    nk = nblk * tq
    keys = jnp.concatenate([r[...] for r in k_refs], axis=0).astype(BF16)
    vals = jnp.concatenate([r[...] for r in v_refs], axis=0).astype(BF16)
    q4 = jnp.concatenate([q_ref[:, i * HEAD_DIM:(i + 1) * HEAD_DIM] for i in range(g)], axis=0)
    s = (_dot_nt(q4.astype(BF16), keys) * scale).reshape(g, tq, nk)
    pos = qi * tq + lax.broadcasted_iota(jnp.int32, (tq, nk), 0)
    wpos = (qi - (nblk - 1)) * tq + lax.broadcasted_iota(jnp.int32, (tq, nk), 1)
    mask = ((wpos <= pos) & (wpos > pos - WINDOW) & (wpos >= 0))[None]
    s = jnp.where(mask, s, NEG_BIG)
    e = jnp.exp(s - jnp.max(s, axis=-1, keepdims=True)).astype(BF16)
    ov = _dot(e.reshape(g * tq, nk), jnp.concatenate([vals, jnp.ones_like(vals)], axis=1))
    o_win = (ov[:, 0:HEAD_DIM] * (1.0 / jnp.maximum(ov[:, HEAD_DIM:2 * HEAD_DIM], 1e-30))).reshape(g, tq, HEAD_DIM)
    small = small_ref[...]
    for i in range(g):
        head = hk * g + i
        gates = [jax.nn.sigmoid(_lane_pick(small, SMALL_NSA_GATE + j * NSA_HEADS + head)) for j in range(3)]
        cols = slice(i * HEAD_DIM, (i + 1) * HEAD_DIM)
        o = gates[0] * ocmp_ref[:, cols] + gates[1] * osel_ref[:, cols] + gates[2] * o_win[i]
        o_ref[:, cols] = o.astype(o_ref.dtype)


def _winattn(proj, o_cmp, o_sel, tq=2 * Q_BLOCK):
    bsz, t, _ = proj.shape
    nblk = WINDOW // tq + 1
    gw = NSA_GROUP * HEAD_DIM

    def kv_spec(base, d):
        return pl.BlockSpec((None, tq, HEAD_DIM),
                            lambda b, h, i: (b, jnp.maximum(i - (nblk - 1) + d, 0), base + h))

    grp = pl.BlockSpec((None, tq, gw), lambda b, h, i: (b, i, h))
    return pl.pallas_call(
        functools.partial(_winattn_kernel, tq=tq, nblk=nblk),
        grid=(bsz, NSA_KV_HEADS, t // tq),
        in_specs=(
            [pl.BlockSpec((None, tq, gw), lambda b, h, i: (b, i, COL_NSA_Q // NSA_GROUP + h))]
            + [kv_spec(COL_KV + 4 * NSA_KV_HEADS, d) for d in range(nblk)]
            + [kv_spec(COL_KV + 5 * NSA_KV_HEADS, d) for d in range(nblk)]
            + [grp, grp, pl.BlockSpec((None, tq, HEAD_DIM), lambda b, h, i: (b, i, COL_SMALL))]
        ),
        out_specs=grp,
        out_shape=jax.ShapeDtypeStruct((bsz, t, NSA_WIDTH), BF16),
        compiler_params=_cparams(("arbitrary", "arbitrary", "arbitrary")),
        name="winattn",
    )(proj, *([proj] * (2 * nblk)), o_cmp, o_sel, proj)


def _ffn(h, wg, wu, wd):
    act = _silu(_dot(h, wg)) * _dot(h, wu)
    return _dot(act.astype(BF16), wd)


def _pack_bf16_pairs(x):
    half = x.shape[1] // 2
    lo = pltpu.bitcast(x[:, :half].astype(BF16).astype(F32), jnp.uint32)
    hi = pltpu.bitcast(x[:, half:].astype(BF16).astype(F32), jnp.uint32)
    return hi | (lo >> 16)


def _unpack_bf16_pairs(p):
    return pltpu.bitcast(p << 16, F32), pltpu.bitcast(p & jnp.uint32(0xFFFF0000), F32)


def _store_token_rows(ref, x, first=0):
    n, d = x.shape
    s_tiles = d // HEAD_DIM
    for s in range(s_tiles):
        ref[pl.ds(first * s_tiles + s, n, stride=s_tiles), :] = x[:, s * HEAD_DIM:(s + 1) * HEAD_DIM]


def _load_token_rows(ref, n, d, first=0):
    s_tiles = d // HEAD_DIM
    return jnp.concatenate([ref[pl.ds(first * s_tiles + s, n, stride=s_tiles), :] for s in range(s_tiles)], axis=1)


def _outproj_kernel(yg_ref, yn_ref, x_ref, wo_ref, gate_ref, shift_ref, scale_ref, gate_m_ref, nw_ref,
                    rw_ref, rb_ref, sg_ref, su_ref, sd_ref, x1_ref, h_ref, route_ref, count_ref, carry_ref):
    tm = x_ref.shape[0]
    half = yg_ref.shape[1]

    @pl.when((pl.program_id(0) == 0) & (pl.program_id(1) == 0))
    def _():
        carry_ref[...] = jnp.zeros_like(carry_ref)

    y = _dot(yg_ref[...], wo_ref[0:half, :]) + _dot(yn_ref[...], wo_ref[half:2 * half, :])
    x1 = x_ref[...] + gate_ref[...] * y
    h = _modulated_norm(x1, nw_ref[...], scale_ref[...], shift_ref[...])
    _store_token_rows(h_ref, _pack_bf16_pairs(h))
    x1_ref[...] = x1 + gate_m_ref[...] * _ffn(h.astype(BF16), sg_ref[...], su_ref[...], sd_ref[...])
    scores = jax.nn.sigmoid(_dot_split(h, rw_ref[...]))
    n_exp = scores.shape[1]
    lane = lax.broadcasted_iota(jnp.int32, scores.shape, 1)
    cand = scores + rb_ref[...]
    picked = jnp.zeros(scores.shape, jnp.bool_)
    firsts = []
    for _ in range(TOP_K):
        best = jnp.max(cand, axis=-1, keepdims=True)
        first = jnp.min(jnp.where(cand == best, lane, n_exp), axis=-1, keepdims=True)
        hit = lane == first
        picked = picked | hit
        cand = jnp.where(hit, NEG_BIG, cand)
        firsts.append(first)
    norm = ROUTED_SCALE / jnp.sum(jnp.where(picked, scores, 0.0), axis=-1, keepdims=True)
    onehot = jnp.where(picked, 1.0, 0.0)
    ri = lax.broadcasted_iota(jnp.int32, (tm, tm), 0)
    ci = lax.broadcasted_iota(jnp.int32, (tm, tm), 1)
    before = jnp.where(ci < ri, 1.0, 0.0).astype(BF16)
    pos = _dot(before, onehot.astype(BF16)) + carry_ref[...]
    carry_ref[...] += jnp.sum(onehot, axis=0, keepdims=True)
    count_ref[...] = carry_ref[...]
    slot = lax.broadcasted_iota(jnp.int32, (tm, 3 * TOP_K), 1)
    route = jnp.zeros((tm, 3 * TOP_K), F32)
    for j, first in enumerate(firsts):
        hit = lane == first
        route = jnp.where(slot == j, first.astype(F32), route)
        route = jnp.where(slot == TOP_K + j, _lane_pick(scores, first) * norm, route)
        route = jnp.where(slot == 2 * TOP_K + j, jnp.sum(jnp.where(hit, pos, 0.0), axis=-1, keepdims=True), route)
    route_ref[...] = route


def _outproj(y_gdn, y_nsa, x, w_out_b, mod4, norm_w, router_w, router_bias, sg, su, sd, tm=256):
    bsz, t, d = x.shape
    half = y_gdn.shape[2]
    n_exp = router_w.shape[1]
    sff = sg.shape[1]
    s_tiles = d // 2 // HEAD_DIM

    def mod(j):
        return pl.BlockSpec((None, None, 1, d), lambda b, i: (b, j, 0, 0))

    def const(shape):
        return pl.BlockSpec(shape, lambda b, i: (0, 0))

    row = pl.BlockSpec((None, tm, d), lambda b, i: (b, i, 0))
    return pl.pallas_call(
        _outproj_kernel,
        grid=(bsz, t // tm),
        in_specs=[
            pl.BlockSpec((None, tm, half), lambda b, i: (b, i, 0)),
            pl.BlockSpec((None, tm, half), lambda b, i: (b, i, 0)),
            row,
            const((2 * half, d)),
            mod(2), mod(3), mod(4), mod(5),
            const((1, d)), const((d, n_exp)), const((1, n_exp)),
            const((d, sff)), const((d, sff)), const((sff, d)),
        ],
        out_specs=[row,
                   pl.BlockSpec((tm * s_tiles, HEAD_DIM), lambda b, i: (b * (t // tm) + i, 0)),
                   pl.BlockSpec((None, tm, 3 * TOP_K), lambda b, i: (b, i, 0)),
                   const((1, n_exp))],
        out_shape=[
            jax.ShapeDtypeStruct((bsz, t, d), F32),
            jax.ShapeDtypeStruct((bsz * t * s_tiles, HEAD_DIM), jnp.uint32),
            jax.ShapeDtypeStruct((bsz, t, 3 * TOP_K), F32),
            jax.ShapeDtypeStruct((1, n_exp), F32),
        ],
        scratch_shapes=[pltpu.VMEM((1, n_exp), F32)],
        compiler_params=_cparams(("arbitrary", "arbitrary")),
        name="outproj",
    )(y_gdn, y_nsa, x, w_out_b, mod4, mod4, mod4, mod4, norm_w.reshape(1, d), router_w,
      router_bias.reshape(1, n_exp), sg, su, sd)


MOE_TM = 256
MOE_TOK = 128
MOE_TILES_PER_STEP = 2


def _token_row(ref, r, s_tiles):
    return ref.at[pl.ds(pl.multiple_of(r * s_tiles, s_tiles), s_tiles), :]


def _for_each(n, fn, unroll=8):
    def body(i, carry):
        fn(i)
        return carry

    lax.fori_loop(0, n, body, 0, unroll=unroll)


def _moe_scatter_kernel(gstart_ref, gend_ref, idx_ref, pos_ref, h_ref, xs_hbm, zero_ref, sem, *, s_tiles):
    n_exp = gstart_ref.shape[0]

    @pl.when(pl.program_id(0) == 0)
    def _():
        zero_ref[...] = jnp.zeros_like(zero_ref)

        def tail(e):
            first = pl.multiple_of((gend_ref[e] - MOE_TM) * s_tiles, MOE_TM * s_tiles)
            return pltpu.make_async_copy(zero_ref, xs_hbm.at[pl.ds(first, MOE_TM * s_tiles), :], sem)

        def start_tail(e):
            @pl.when(gend_ref[e] > gstart_ref[e])
            def _():
                tail(e).start()

        def wait_tail(e):
            @pl.when(gend_ref[e] > gstart_ref[e])
            def _():
                tail(e).wait()

        _for_each(n_exp, start_tail)
        _for_each(n_exp, wait_tail)

    def row_copy(t, j):
        n = t * TOP_K + j
        row = gstart_ref[idx_ref[n]] + pos_ref[n]
        return pltpu.make_async_copy(_token_row(h_ref, t, s_tiles), _token_row(xs_hbm, row, s_tiles), sem)

    def start_token(t):
        for j in range(TOP_K):
            row_copy(t, j).start()

    def wait_token(t):
        for j in range(TOP_K):
            row_copy(t, j).wait()

    _for_each(MOE_TOK, start_token, unroll=2)
    _for_each(MOE_TOK, wait_token, unroll=2)


def _moe_scatter(h_rows, idx, pos, gstart, gend, n_rows, s_tiles):
    n_steps = idx.shape[0] // (MOE_TOK * TOP_K)
    smem = pl.BlockSpec((MOE_TOK * TOP_K,), lambda i, gs, ge: (i,), memory_space=pltpu.SMEM)
    return pl.pallas_call(
        functools.partial(_moe_scatter_kernel, s_tiles=s_tiles),
        grid_spec=pltpu.PrefetchScalarGridSpec(
            num_scalar_prefetch=2,
            grid=(n_steps,),
            in_specs=[smem, smem, pl.BlockSpec((MOE_TOK * s_tiles, HEAD_DIM), lambda i, gs, ge: (i, 0))],
            out_specs=pl.BlockSpec(memory_space=pl.ANY),
            scratch_shapes=[pltpu.VMEM((MOE_TM * s_tiles, HEAD_DIM), h_rows.dtype), pltpu.SemaphoreType.DMA(())],
        ),
        out_shape=jax.ShapeDtypeStruct((n_rows * s_tiles, HEAD_DIM), h_rows.dtype),
        compiler_params=_cparams(("arbitrary",)),
        name="moe_scatter",
    )(gstart, gend, idx, pos, h_rows)


def _moe_expert_kernel(texp_ref, tslot_ref, tnext_ref, nvalid_ref, xs_ref, wg_hbm, wu_hbm, wd_hbm, ys_ref,
                       wg_buf, wu_buf, wd_buf, wgb_ref, wub_ref, wdb_ref, sems):
    d = wgb_ref.shape[0]
    half = d // 2
    rows_per_tile = xs_ref.shape[0] // MOE_TILES_PER_STEP

    def fetch(expert, s):
        return [pltpu.make_async_copy(src.at[expert], dst.at[s], sems.at[s, n])
                for n, (src, dst) in enumerate(((wg_hbm, wg_buf), (wu_hbm, wu_buf), (wd_hbm, wd_buf)))]

    def tile(i, sub):
        valid = i < nvalid_ref[0]
        new_expert = (i == 0) | (texp_ref[i] != texp_ref[jnp.maximum(i - 1, 0)])
        slot = tslot_ref[i]

        @pl.when(i == 0)
        def _():
            for cp in fetch(texp_ref[0], slot):
                cp.start()

        @pl.when(valid & new_expert)
        def _():
            nxt = tnext_ref[i]

            @pl.when(nxt < nvalid_ref[0])
            def _():
                for cp in fetch(texp_ref[nxt], 1 - slot):
                    cp.start()

            for cp in fetch(texp_ref[i], slot):
                cp.wait()
            wgb_ref[...] = wg_buf[slot].astype(BF16)
            wub_ref[...] = wu_buf[slot].astype(BF16)
            wdb_ref[...] = wd_buf[slot].astype(BF16)

        @pl.when(valid)
        def _():
            lo, hi = _unpack_bf16_pairs(_load_token_rows(xs_ref, MOE_TM, half, first=sub * MOE_TM))
            lo, hi = lo.astype(BF16), hi.astype(BF16)
            gate = _dot(lo, wgb_ref[0:half, :]) + _dot(hi, wgb_ref[half:d, :])
            up = _dot(lo, wub_ref[0:half, :]) + _dot(hi, wub_ref[half:d, :])
            y = _dot((_silu(gate) * up).astype(BF16), wdb_ref[...])
            _store_token_rows(ys_ref, _pack_bf16_pairs(y), first=sub * MOE_TM)

        @pl.when(jnp.logical_not(valid))
        def _():
            ys_ref[sub * rows_per_tile:(sub + 1) * rows_per_tile, :] = jnp.zeros((rows_per_tile, HEAD_DIM), ys_ref.dtype)

    for sub in range(MOE_TILES_PER_STEP):
        tile(pl.program_id(0) * MOE_TILES_PER_STEP + sub, sub)


def _moe_experts(xs, tile_expert, tile_slot, tile_next, nvalid, wg, wu, wd):
    n_exp, d, ff = wg.shape
    n_steps = tile_expert.shape[0] // MOE_TILES_PER_STEP
    blk = xs.shape[0] // n_steps
    hbm = pl.BlockSpec(memory_space=pl.ANY)
    return pl.pallas_call(
        _moe_expert_kernel,
        grid_spec=pltpu.PrefetchScalarGridSpec(
            num_scalar_prefetch=4,
            grid=(n_steps,),
            in_specs=[
                pl.BlockSpec((blk, HEAD_DIM),
                             lambda i, te, ts, tn, nv: (jnp.minimum(i, (nv[0] - 1) // MOE_TILES_PER_STEP), 0)),
                hbm, hbm, hbm,
            ],
            out_specs=pl.BlockSpec((blk, HEAD_DIM), lambda i, te, ts, tn, nv: (i, 0)),
            scratch_shapes=[pltpu.VMEM((2, d, ff), wg.dtype), pltpu.VMEM((2, d, ff), wu.dtype),
                            pltpu.VMEM((2, ff, d), wd.dtype),
                            pltpu.VMEM((d, ff), BF16), pltpu.VMEM((d, ff), BF16), pltpu.VMEM((ff, d), BF16),
                            pltpu.SemaphoreType.DMA((2, 3))],
        ),
        out_shape=jax.ShapeDtypeStruct(xs.shape, xs.dtype),
        compiler_params=_cparams(("arbitrary",)),
        name="moe_experts",
    )(tile_expert, tile_slot, tile_next, nvalid, xs, wg, wu, wd)


def _moe_combine_kernel(gstart_ref, idx_ref, pos_ref, w_ref, ys_hbm, x1_ref, gate_ref, nw_ref, o_ref,
                        buf_ref, lo_ref, hi_ref, sem, *, s_tiles):
    tm, d = x1_ref.shape

    def row_copy(n):
        row = gstart_ref[idx_ref[n]] + pos_ref[n]
        return pltpu.make_async_copy(_token_row(ys_hbm, row, s_tiles), _token_row(buf_ref, n, s_tiles), sem)

    _for_each(tm * TOP_K, lambda n: row_copy(n).start())
    _for_each(tm * TOP_K, lambda n: row_copy(n).wait())

    def token(t):
        n0 = t * TOP_K
        lo, hi = _unpack_bf16_pairs(_token_row(buf_ref, n0, s_tiles)[...])
        acc_lo, acc_hi = w_ref[n0] * lo, w_ref[n0] * hi
        for j in range(1, TOP_K):
            lo, hi = _unpack_bf16_pairs(_token_row(buf_ref, n0 + j, s_tiles)[...])
            acc_lo, acc_hi = acc_lo + w_ref[n0 + j] * lo, acc_hi + w_ref[n0 + j] * hi
        _token_row(lo_ref, t, s_tiles)[...] = acc_lo
        _token_row(hi_ref, t, s_tiles)[...] = acc_hi

    _for_each(tm, token)
    routed = jnp.concatenate([_load_token_rows(lo_ref, tm, d // 2), _load_token_rows(hi_ref, tm, d // 2)], axis=1)
    x2 = x1_ref[...] + gate_ref[...] * routed
    o_ref[...] = x2 * lax.rsqrt(jnp.mean(x2 * x2, axis=-1, keepdims=True) + EPS) * nw_ref[...]


def _moe_combine(ys, idx, pos, w, gstart, x1, mod4, norm_w):
    bsz, t, d = x1.shape
    s_tiles = d // 2 // HEAD_DIM
    tm = MOE_TOK
    nt = t // tm
    smem = pl.BlockSpec((tm * TOP_K,), lambda b, i, gs: (b * nt + i,), memory_space=pltpu.SMEM)
    row = pl.BlockSpec((None, tm, d), lambda b, i, gs: (b, i, 0))
    return pl.pallas_call(
        functools.partial(_moe_combine_kernel, s_tiles=s_tiles),
        grid_spec=pltpu.PrefetchScalarGridSpec(
            num_scalar_prefetch=1,
            grid=(bsz, nt),
            in_specs=[
                smem, smem, smem,
                pl.BlockSpec(memory_space=pl.ANY),
                row,
                pl.BlockSpec((None, None, 1, d), lambda b, i, gs: (b, 5, 0, 0)),
                pl.BlockSpec((1, d), lambda b, i, gs: (0, 0)),
            ],
            out_specs=row,
            scratch_shapes=[pltpu.VMEM((tm * TOP_K * s_tiles, HEAD_DIM), ys.dtype),
                            pltpu.VMEM((tm * s_tiles, HEAD_DIM), F32), pltpu.VMEM((tm * s_tiles, HEAD_DIM), F32),
                            pltpu.SemaphoreType.DMA(())],
        ),
        out_shape=jax.ShapeDtypeStruct((bsz, t, d), F32),
        compiler_params=_cparams(("arbitrary", "arbitrary")),
        name="moe_combine",
    )(gstart, idx, pos, w, ys, x1, mod4, norm_w.reshape(1, d))


def _moe(h_rows, route, counts, x1, mod4, norm_w, wg, wu, wd):
    bsz, t, d = x1.shape
    n_exp = wg.shape[0]
    s_tiles = h_rows.shape[0] // (bsz * t)
    idx = route[..., 0:TOP_K].astype(jnp.int32).reshape(-1)
    w = route[..., TOP_K:2 * TOP_K].reshape(-1)
    pos = route[..., 2 * TOP_K:3 * TOP_K].astype(jnp.int32).reshape(-1)
    padded = (counts.reshape(n_exp).astype(jnp.int32) + MOE_TM - 1) // MOE_TM * MOE_TM
    gend = jnp.cumsum(padded)
    gstart = gend - padded
    nvalid = gend[-1:] // MOE_TM
    n_tiles = bsz * t * TOP_K // MOE_TM + n_exp
    tile_start = jnp.minimum(jnp.arange(n_tiles, dtype=jnp.int32), nvalid - 1) * MOE_TM
    tile_expert = jnp.sum((gend[None, :] <= tile_start[:, None]).astype(jnp.int32), axis=1)
    first_of_group = jnp.concatenate([jnp.ones((1,), jnp.int32),
                                      (tile_expert[1:] != tile_expert[:-1]).astype(jnp.int32)])
    tile_slot = (jnp.cumsum(first_of_group) - 1) % 2
    experts = jnp.arange(n_exp, dtype=jnp.int32)
    tile_next = jnp.sum(jnp.where(tile_expert[:, None] == experts[None, :], gend[None, :], 0), axis=1) // MOE_TM
    xs = _moe_scatter(h_rows, idx, pos, gstart, gend, n_tiles * MOE_TM, s_tiles)
    ys = _moe_experts(xs, tile_expert, tile_slot, tile_next, nvalid, wg, wu, wd)
    return _moe_combine(ys, idx, pos, w, gstart, x1, mod4, norm_w)


def _reorder_w_in(w_in):
    d = w_in.shape[0]
    o_beta = 3 * GDN_WIDTH
    o_a = o_beta + GDN_HEADS
    o_gate = o_a + GDN_HEADS
    o_q = o_gate + GDN_WIDTH
    o_kv = o_q + NSA_WIDTH
    o_ng = o_kv + 6 * NSA_KV_WIDTH
    n_small = 2 * GDN_HEADS + 3 * NSA_HEADS
    parts = [w_in[:, :o_beta], w_in[:, o_gate:o_q], w_in[:, o_q:o_kv], w_in[:, o_kv:o_ng],
             w_in[:, o_beta:o_gate], w_in[:, o_ng:o_ng + 3 * NSA_HEADS]]
    used = COL_SMALL * HEAD_DIM + n_small
    parts.append(jnp.zeros((d, PROJ_COLS - used), w_in.dtype))
    return jnp.concatenate([p.astype(BF16) for p in parts], axis=1)


def _layer(x, mod, norm_attn_w, norm_ffn_w, w_in, gdn_conv_w, gdn_a_log, gdn_dt_bias, gdn_norm_w,
           cmp_pe_k, cmp_w1_k, cmp_w2_k, cmp_pe_v, cmp_w1_v, cmp_w2_v, w_out, router_w, router_bias,
           expert_w_gate, expert_w_up, expert_w_down, shared_w_gate, shared_w_up, shared_w_down, final_w):
    bsz, t, d = x.shape
    mod4 = mod.reshape(bsz, N_MOD, 1, d)
    proj = _proj(x, mod4, norm_attn_w, _reorder_w_in(w_in))
    y_gdn = _gdn(proj, gdn_conv_w, gdn_a_log, gdn_dt_bias, gdn_norm_w)
    kc = _compress(proj, COL_KV, cmp_pe_k, cmp_w1_k, cmp_w2_k)
    vc = _compress(proj, COL_KV + NSA_KV_HEADS, cmp_pe_v, cmp_w1_v, cmp_w2_v)
    o_cmp, sel = _cmpattn(proj, kc, vc)
    o_sel = _selattn(proj, sel)
    y_nsa = _winattn(proj, o_cmp, o_sel)
    x1, h_rows, route, counts = _outproj(
        y_gdn, y_nsa, x, w_out.astype(BF16), mod4, norm_ffn_w, router_w, router_bias,
        shared_w_gate.astype(BF16), shared_w_up.astype(BF16), shared_w_down.astype(BF16))
    return _moe(h_rows, route, counts, x1, mod4, final_w, expert_w_gate, expert_w_up, expert_w_down)


def kernel(x, c, w_ada, b_ada, norm_attn_w, norm_ffn_w, norm_final_w, w_in, gdn_conv_w, gdn_a_log, gdn_dt_bias,
           gdn_norm_w, cmp_pe_k, cmp_w1_k, cmp_w2_k, cmp_pe_v, cmp_w1_v, cmp_w2_v, w_out, router_w, router_bias,
           expert_w_gate, expert_w_up, expert_w_down, shared_w_gate, shared_w_up, shared_w_down):
    depth = w_ada.shape[0]
    assert depth == 1, "the fused final norm assumes a single layer"
    l = 0
    mod = _ada(c, w_ada[l], b_ada[l])
    return _layer(x, mod, norm_attn_w[l], norm_ffn_w[l], w_in[l], gdn_conv_w[l], gdn_a_log[l], gdn_dt_bias[l],
                  gdn_norm_w[l], cmp_pe_k[l], cmp_w1_k[l], cmp_w2_k[l], cmp_pe_v[l], cmp_w1_v[l], cmp_w2_v[l],
                  w_out[l], router_w[l], router_bias[l], expert_w_gate[l], expert_w_up[l], expert_w_down[l],
                  shared_w_gate[l], shared_w_up[l], shared_w_down[l], norm_final_w)
```

```python
import functools

import jax
import jax.numpy as jnp
from jax import lax
from jax.experimental import pallas as pl
from jax.experimental.pallas import tpu as pltpu

F32 = jnp.float32
BF16 = jnp.bfloat16
HI = lax.Precision.HIGHEST

HEAD_DIM = 128
GDN_HEADS = 8
NSA_HEADS = 8
NSA_KV_HEADS = 2
NSA_GROUP = NSA_HEADS // NSA_KV_HEADS
GDN_WIDTH = GDN_HEADS * HEAD_DIM
NSA_WIDTH = NSA_HEADS * HEAD_DIM
NSA_KV_WIDTH = NSA_KV_HEADS * HEAD_DIM
CONV_WIDTH = 4
GDN_CHUNK = 64
CMP_BLOCK = 32
CMP_STRIDE = 16
SEL_BLOCK = 64
SEL_TOPK = 16
N_LOCAL_BLOCKS = 2
WINDOW = 512
Q_BLOCK = 128
TOP_K = 8
ROUTED_SCALE = 2.5
N_MOD = 6
EPS = 1e-6

COL_GDN_Q = 0
COL_GDN_K = GDN_HEADS
COL_GDN_V = 2 * GDN_HEADS
COL_GDN_GATE = 3 * GDN_HEADS
COL_NSA_Q = 4 * GDN_HEADS
COL_KV = COL_NSA_Q + NSA_HEADS
COL_SMALL = COL_KV + 6 * NSA_KV_HEADS
N_COL_BLOCKS = COL_SMALL + 1
PROJ_TN = 2304
PROJ_COLS = 3 * PROJ_TN
SMALL_BETA = 0
SMALL_A = GDN_HEADS
SMALL_NSA_GATE = 2 * GDN_HEADS

NEG_BIG = -1e30
VMEM_LIMIT = 48 * 1024 * 1024


def _cparams(semantics, vmem_limit=VMEM_LIMIT):
    return pltpu.CompilerParams(dimension_semantics=semantics, vmem_limit_bytes=vmem_limit)


def _silu(x):
    return (0.5 * x) * (1.0 + jnp.tanh(0.5 * x))


def _dot(a, b, precision=None):
    return jnp.dot(a, b, preferred_element_type=F32, precision=precision)


def _dot_nt(a, b, precision=None):
    return lax.dot_general(a, b, (((1,), (1,)), ((), ())), preferred_element_type=F32, precision=precision)


def _dot_tn(a, b, precision=None):
    return lax.dot_general(a, b, (((0,), (0,)), ((), ())), preferred_element_type=F32, precision=precision)


def _split_bf16(x):
    hi = x.astype(BF16)
    return hi, (x - hi.astype(F32)).astype(BF16)


def _dot_split(a, b, nt=False):
    (ah, al), (bh, bl) = _split_bf16(a), _split_bf16(b)
    f = _dot_nt if nt else _dot
    return f(ah, bh) + (f(ah, bl) + f(al, bh))


def _lane_pick(x, idx):
    lane = lax.broadcasted_iota(jnp.int32, x.shape, 1)
    return jnp.sum(jnp.where(lane == idx, x, 0.0), axis=-1, keepdims=True)


def _ada_kernel(ct_ref, w_ref, b_ref, o_ref):
    c = ct_ref[...]
    cond = _silu(c)
    w = w_ref[...]
    rows = [jnp.sum(w * cond[:, b:b + 1], axis=0, keepdims=True) for b in range(c.shape[1])]
    o_ref[...] = jnp.concatenate(rows, axis=0) + b_ref[...]


def _ada(c, w_ada, b_ada, tn=512):
    bsz, d = c.shape
    n = w_ada.shape[1]
    return pl.pallas_call(
        _ada_kernel,
        grid=(n // tn,),
        in_specs=[
            pl.BlockSpec((d, bsz), lambda j: (0, 0)),
            pl.BlockSpec((d, tn), lambda j: (0, j)),
            pl.BlockSpec((1, tn), lambda j: (0, j)),
        ],
        out_specs=pl.BlockSpec((bsz, tn), lambda j: (0, j)),
        out_shape=jax.ShapeDtypeStruct((bsz, n), F32),
        compiler_params=_cparams(("arbitrary",)),
        name="ada",
    )(c.T, w_ada, b_ada.reshape(1, n))


def _modulated_norm(x, nw, scale, shift):
    var = jnp.mean(x * x, axis=-1, keepdims=True)
    h = x * lax.rsqrt(var + EPS) * nw
    return h * (1.0 + scale) + shift


def _proj_kernel(x_ref, shift_ref, scale_ref, nw_ref, w_ref, o_ref):
    h = _modulated_norm(x_ref[...], nw_ref[...], scale_ref[...], shift_ref[...])
    o_ref[...] = _dot(h.astype(BF16), w_ref[...])


def _proj(x, mod4, norm_w, w_r, tm=512):
    bsz, t, d = x.shape
    n = w_r.shape[1]
    return pl.pallas_call(
        _proj_kernel,
        grid=(n // PROJ_TN, bsz, t // tm),
        in_specs=[
            pl.BlockSpec((None, tm, d), lambda j, b, i: (b, i, 0)),
            pl.BlockSpec((None, None, 1, d), lambda j, b, i: (b, 0, 0, 0)),
            pl.BlockSpec((None, None, 1, d), lambda j, b, i: (b, 1, 0, 0)),
            pl.BlockSpec((1, d), lambda j, b, i: (0, 0)),
            pl.BlockSpec((d, PROJ_TN), lambda j, b, i: (0, j)),
        ],
        out_specs=pl.BlockSpec((None, tm, PROJ_TN), lambda j, b, i: (b, i, j)),
        out_shape=jax.ShapeDtypeStruct((bsz, t, n), F32),
        compiler_params=_cparams(("arbitrary", "arbitrary", "arbitrary")),
        name="proj",
    )(x, mod4, mod4, norm_w.reshape(1, d), w_r)


def _gdn_kernel(q_ref, k_ref, v_ref, gate_ref, small_ref, hq_ref, hk_ref, hv_ref,
                cwq_ref, cwk_ref, cwv_ref, alog_ref, dtb_ref, nw_ref, o_ref, s_ref, xc_ref, *, tt, hg):
    first = pl.program_id(2) == 0

    @pl.when(first)
    def _():
        s_ref[...] = jnp.zeros_like(s_ref)

    def conv_silu(x_ref, halo_ref, cw_ref):
        xc_ref[0:8, :] = jnp.where(first, 0.0, halo_ref[...])
        xc_ref[8:tt + 8, :] = x_ref[...]
        cw = cw_ref[...]
        off = 8 - (CONV_WIDTH - 1)
        y = cw[0:1, :] * xc_ref[off:off + tt, :]
        for i in range(1, CONV_WIDTH):
            y = y + cw[i:i + 1, :] * xc_ref[off + i:off + i + tt, :]
        return _silu(y)

    q_all = conv_silu(q_ref, hq_ref, cwq_ref)
    k_all = conv_silu(k_ref, hk_ref, cwk_ref)
    v_all = conv_silu(v_ref, hv_ref, cwv_ref)
    ri = lax.broadcasted_iota(jnp.int32, (tt, tt), 0)
    ci = lax.broadcasted_iota(jnp.int32, (tt, tt), 1)
    lower = jnp.where(((ri // GDN_CHUNK) == (ci // GDN_CHUNK)) & (ci <= ri), 1.0, 0.0).astype(BF16)
    heads = [pl.program_id(1) * hg + hh for hh in range(hg)]
    outs = _gdn_heads(q_all, k_all, v_all, small_ref[...], lower, heads, alog_ref[...], dtb_ref[...], s_ref, tt)
    for hh, o in enumerate(outs):
        cols = slice(hh * HEAD_DIM, (hh + 1) * HEAD_DIM)
        o = o * lax.rsqrt(jnp.mean(o * o, axis=-1, keepdims=True) + EPS) * nw_ref[...]
        o_ref[:, cols] = (o * _silu(gate_ref[:, cols])).astype(o_ref.dtype)


def _gdn_heads(q_all, k_all, v_all, small, lower, heads, alog, dtb, s_ref, tt):
    c = GDN_CHUNK
    nch = tt // c
    hg = len(heads)
    kpos = lax.broadcasted_iota(jnp.int32, (tt, c), 0) % c
    jj = lax.broadcasted_iota(jnp.int32, (tt, c), 1)
    per_head = []
    for hh, h in enumerate(heads):
        cols = slice(hh * HEAD_DIM, (hh + 1) * HEAD_DIM)
        q, k, v = q_all[:, cols], k_all[:, cols], v_all[:, cols]
        q = q * lax.rsqrt(jnp.sum(q * q, axis=-1, keepdims=True) + EPS) * (HEAD_DIM ** -0.5)
        k = k * lax.rsqrt(jnp.sum(k * k, axis=-1, keepdims=True) + EPS)
        beta = jax.nn.sigmoid(_lane_pick(small, SMALL_BETA + h))
        z = _lane_pick(small, SMALL_A + h) + _lane_pick(dtb, h)
        softplus = jnp.maximum(z, 0.0) + jnp.log(1.0 + jnp.exp(-jnp.abs(z)))
        g = -jnp.exp(_lane_pick(alog, h)) * softplus
        g_c = jnp.broadcast_to(g, (tt, c))
        rhs = jnp.concatenate([g_c, jnp.where(kpos > jj, g_c, 0.0)], axis=1)
        r_hi = rhs.astype(BF16)
        rem = rhs - r_hi.astype(F32)
        r_mid = rem.astype(BF16)
        r_lo = (rem - r_mid.astype(F32)).astype(BF16)
        sums = _dot(lower, r_hi) + (_dot(lower, r_mid) + _dot(lower, r_lo))
        gcum = jnp.broadcast_to(sums[:, 0:1], (tt, HEAD_DIM))
        gc3 = gcum.reshape(nch, c, HEAD_DIM)
        grest = (gc3[:, c - 1:c, :] - gc3).reshape(tt, HEAD_DIM)
        decay_h = jnp.where(kpos >= jj, jnp.exp(sums[:, c:2 * c]), 0.0)
        per_head.append((q, k, v * beta, k * beta, jnp.exp(gcum), jnp.exp(grest), decay_h))

    def stack(j):
        x = jnp.concatenate([ph[j] for ph in per_head], axis=0)
        return x.reshape(hg * nch, c, x.shape[-1])

    q, k, vb, kb, egc, egr, decay = (stack(j) for j in range(7))
    kpos_all = lax.broadcasted_iota(jnp.int32, (hg * tt, c), 0) % c
    jj_all = lax.broadcasted_iota(jnp.int32, (hg * tt, c), 1)
    strict = (kpos_all > jj_all).reshape(hg * nch, c, c)
    eye = jnp.where(kpos_all == jj_all, 1.0, 0.0).reshape(hg * nch, c, c)

    def bmm(a, b):
        return jnp.einsum('nij,njk->nik', a, b, preferred_element_type=F32)

    def bmm_nt(a, b):
        return jnp.einsum('nid,njd->nij', a, b, preferred_element_type=F32)

    def bmm_split(a, b):
        (ah, al), (bh, bl) = _split_bf16(a), _split_bf16(b)
        return bmm(ah, bh) + (bmm(ah, bl) + bmm(al, bh))

    k3 = k.astype(BF16)
    a = jnp.where(strict, bmm_nt(kb.astype(BF16), k3) * decay, 0.0)
    inv = eye - a
    p = bmm_split(a, a)
    n_sq = (c - 1).bit_length() - 1
    for lvl in range(n_sq):
        inv = inv + bmm_split(inv, p)
        if lvl + 1 < n_sq:
            p = bmm_split(p, p)
    inv_b = inv.astype(BF16)
    u3 = bmm(inv_b, vb.astype(BF16)).astype(BF16)
    w3 = bmm(inv_b, (kb * egc).astype(BF16)).astype(BF16)
    qk3 = (bmm_nt(q.astype(BF16), k3) * decay).astype(BF16)
    kd3 = (k * egr).astype(BF16)
    qe3 = (q * egc - bmm(qk3, w3)).astype(BF16)
    o03 = bmm(qk3, u3)
    kw = [_dot_tn(kd3[i], w3[i]).astype(BF16) for i in range(hg * nch)]
    ku = [_dot_tn(kd3[i], u3[i]) for i in range(hg * nch)]

    states = [s_ref[hh] for hh in range(hg)]
    outs = [[] for _ in range(hg)]
    for n in range(nch):
        for hh in range(hg):
            i = hh * nch + n
            state_b = states[hh].astype(BF16)
            outs[hh].append(_dot(qe3[i], state_b) + o03[i])
            states[hh] = states[hh] * egc[i, c - 1:c, :] - _dot(kw[i], state_b) + ku[i]
    for hh in range(hg):
        s_ref[hh] = states[hh]
    return [jnp.concatenate(o, axis=0) for o in outs]


def _gdn(proj, conv_w, a_log, dt_bias, norm_w, tt=128, hg=8):
    bsz, t, _ = proj.shape
    wdt = hg * HEAD_DIM

    def col(base):
        return pl.BlockSpec((None, tt, wdt), lambda b, h, i: (b, i, base // hg + h))

    def halo(base):
        return pl.BlockSpec((None, 8, wdt), lambda b, h, i: (b, jnp.maximum(i * (tt // 8) - 1, 0), base // hg + h))

    def cw(base):
        return pl.BlockSpec((CONV_WIDTH, wdt), lambda b, h, i: (0, base // hg + h))

    heads = pl.BlockSpec((1, GDN_HEADS), lambda b, h, i: (0, 0))
    return pl.pallas_call(
        functools.partial(_gdn_kernel, tt=tt, hg=hg),
        grid=(bsz, GDN_HEADS // hg, t // tt),
        in_specs=[
            col(COL_GDN_Q), col(COL_GDN_K), col(COL_GDN_V), col(COL_GDN_GATE),
            pl.BlockSpec((None, tt, HEAD_DIM), lambda b, h, i: (b, i, COL_SMALL)),
            halo(COL_GDN_Q), halo(COL_GDN_K), halo(COL_GDN_V),
            cw(COL_GDN_Q), cw(COL_GDN_K), cw(COL_GDN_V),
            heads, heads,
            pl.BlockSpec((1, HEAD_DIM), lambda b, h, i: (0, 0)),
        ],
        out_specs=pl.BlockSpec((None, tt, wdt), lambda b, h, i: (b, i, h)),
        out_shape=jax.ShapeDtypeStruct((bsz, t, GDN_WIDTH), BF16),
        scratch_shapes=[pltpu.VMEM((hg, HEAD_DIM, HEAD_DIM), F32), pltpu.VMEM((tt + 8, wdt), F32)],
        compiler_params=_cparams(("arbitrary", "arbitrary", "arbitrary")),
        name="gdn",
    )(proj, proj, proj, proj, proj, proj, proj, proj, conv_w, conv_w, conv_w,
      a_log.reshape(1, GDN_HEADS), dt_bias.reshape(1, GDN_HEADS), norm_w.reshape(1, HEAD_DIM))


def _compress_kernel(x_ref, pe_ref, w1_ref, w2_ref, o_ref):
    r = x_ref.shape[0] // CMP_STRIDE
    d = x_ref.shape[1]
    a = b = None
    for l in range(CMP_STRIDE):
        x_l = x_ref[pl.ds(l, r, stride=CMP_STRIDE), :]
        a_l = _dot_split(x_l + pe_ref[l:l + 1, :], w1_ref[l * d:(l + 1) * d, :])
        b_l = _dot_split(x_l + pe_ref[CMP_STRIDE + l:CMP_STRIDE + l + 1, :],
                         w1_ref[(CMP_STRIDE + l) * d:(CMP_STRIDE + l + 1) * d, :])
        a, b = (a_l, b_l) if a is None else (a + a_l, b + b_l)
    hid = _silu(a + pltpu.roll(b, r - 1, 0))
    o_ref[...] = _dot_split(hid, w2_ref[...])


def _compress(proj, col_block, pe, w1, w2):
    bsz, t, _ = proj.shape
    r = t // CMP_STRIDE
    hid = w1.shape[1]
    return pl.pallas_call(
        _compress_kernel,
        grid=(bsz, NSA_KV_HEADS),
        in_specs=[
            pl.BlockSpec((None, t, HEAD_DIM), lambda b, h: (b, 0, col_block + h)),
            pl.BlockSpec((CMP_BLOCK, HEAD_DIM), lambda b, h: (0, 0)),
            pl.BlockSpec((CMP_BLOCK * HEAD_DIM, hid), lambda b, h: (0, 0)),
            pl.BlockSpec((hid, HEAD_DIM), lambda b, h: (0, 0)),
        ],
        out_specs=pl.BlockSpec((None, None, r, HEAD_DIM), lambda b, h: (b, h, 0, 0)),
        out_shape=jax.ShapeDtypeStruct((bsz, NSA_KV_HEADS, r, HEAD_DIM), F32),
        compiler_params=_cparams(("arbitrary", "arbitrary")),
        name="compress",
    )(proj, pe, w1, w2)


def _cmpattn_kernel(q_ref, kc_ref, vc_ref, o_ref, sel_ref, *, tq, nc, nsel):
    ncp = kc_ref.shape[0]
    scale = HEAD_DIM ** -0.5
    t0 = pl.program_id(2) * tq
    kc = kc_ref[...]
    vc = vc_ref[...]
    pos_r = t0 + lax.broadcasted_iota(jnp.int32, (tq, ncp), 0)
    c_r = lax.broadcasted_iota(jnp.int32, (tq, ncp), 1)
    valid_r = (c_r * CMP_STRIDE + CMP_BLOCK - 1 <= pos_r) & (c_r < nc)
    pos_c = t0 + lax.broadcasted_iota(jnp.int32, (ncp, tq), 1)
    c_c = lax.broadcasted_iota(jnp.int32, (ncp, tq), 0)
    valid_c = (c_c * CMP_STRIDE + CMP_BLOCK - 1 <= pos_c) & (c_c < nc)
    psum = jnp.zeros((ncp, tq), F32)
    kc_hi, kc_lo = _split_bf16(kc)
    vc_b = vc.astype(BF16)
    for g in range(NSA_GROUP):
        q_hi, q_lo = _split_bf16(q_ref[:, g * HEAD_DIM:(g + 1) * HEAD_DIM])
        s = _dot_nt(q_hi, kc_hi) + (_dot_nt(q_hi, kc_lo) + _dot_nt(q_lo, kc_hi))
        s = jnp.where(valid_r, s * scale, NEG_BIG)
        e = jnp.where(valid_r, jnp.exp(s - jnp.max(s, axis=-1, keepdims=True)), 0.0)
        p = e * (1.0 / jnp.maximum(jnp.sum(e, axis=-1, keepdims=True), 1e-30))
        o_ref[:, g * HEAD_DIM:(g + 1) * HEAD_DIM] = _dot(p.astype(BF16), vc_b)
        st = _dot_nt(kc_hi, q_hi) + (_dot_nt(kc_lo, q_hi) + _dot_nt(kc_hi, q_lo))
        st = jnp.where(valid_c, st * scale, NEG_BIG)
        et = jnp.where(valid_c, jnp.exp(st - jnp.max(st, axis=0, keepdims=True)), 0.0)
        psum = psum + et * (1.0 / jnp.maximum(jnp.sum(et, axis=0, keepdims=True), 1e-30))
    j_o = lax.broadcasted_iota(jnp.int32, (nsel, ncp), 0) * SEL_BLOCK
    c_o = lax.broadcasted_iota(jnp.int32, (nsel, ncp), 1) * CMP_STRIDE
    overlap = jnp.where((c_o < j_o + SEL_BLOCK) & (c_o + CMP_BLOCK > j_o), 1.0, 0.0).astype(BF16)
    p_hi, p_lo = _split_bf16(psum)
    imp = _dot(overlap, p_hi) + _dot(overlap, p_lo)
    blk = lax.broadcasted_iota(jnp.int32, (nsel, tq), 0)
    dist = (t0 + lax.broadcasted_iota(jnp.int32, (nsel, tq), 1)) // SEL_BLOCK - blk
    forced = (blk == 0) | ((dist >= 0) & (dist < N_LOCAL_BLOCKS))
    val = jnp.where(forced, -NEG_BIG, jnp.where(dist >= 0, imp, -1.0))
    rank = jnp.zeros((nsel, tq), F32)
    for j in range(nsel):
        row = val[j:j + 1, :]
        rank = rank + jnp.where((row > val) | ((row == val) & (blk > j)), 1.0, 0.0)
    sel = jnp.where((rank < min(SEL_TOPK, nsel)) & (dist >= 0), 1.0, 0.0)
    sel_ref[...] = sel.T


def _cmpattn(proj, kc, vc, tq=256):
    bsz, t, _ = proj.shape
    ncp = kc.shape[2]
    nc = (t - CMP_BLOCK) // CMP_STRIDE + 1
    nsel = t // SEL_BLOCK
    gw = NSA_GROUP * HEAD_DIM
    return pl.pallas_call(
        functools.partial(_cmpattn_kernel, tq=tq, nc=nc, nsel=nsel),
        grid=(bsz, NSA_KV_HEADS, t // tq),
        in_specs=[
            pl.BlockSpec((None, tq, gw), lambda b, h, i: (b, i, COL_NSA_Q // NSA_GROUP + h)),
            pl.BlockSpec((None, None, ncp, HEAD_DIM), lambda b, h, i: (b, h, 0, 0)),
            pl.BlockSpec((None, None, ncp, HEAD_DIM), lambda b, h, i: (b, h, 0, 0)),
        ],
        out_specs=[
            pl.BlockSpec((None, tq, gw), lambda b, h, i: (b, i, h)),
            pl.BlockSpec((None, None, tq, nsel), lambda b, h, i: (b, h, i, 0)),
        ],
        out_shape=[
            jax.ShapeDtypeStruct((bsz, t, NSA_WIDTH), F32),
            jax.ShapeDtypeStruct((bsz, NSA_KV_HEADS, t, nsel), F32),
        ],
        compiler_params=_cparams(("arbitrary", "arbitrary", "arbitrary")),
        name="cmpattn",
    )(proj, kc, vc)


def _selattn_kernel(q_ref, k_ref, v_ref, sel_ref, o_ref, m_ref, l_ref, acc_ref, *, tq, tk, nsel):
    g = NSA_GROUP
    qi = pl.program_id(2)
    m_ref[...] = jnp.full_like(m_ref, NEG_BIG)
    l_ref[...] = jnp.zeros_like(l_ref)
    acc_ref[...] = jnp.zeros_like(acc_ref)
    q4 = jnp.concatenate([q_ref[:, i * HEAD_DIM:(i + 1) * HEAD_DIM] for i in range(g)], axis=0)
    q4 = (q4 * HEAD_DIM ** -0.5).astype(BF16)
    sel = sel_ref[...].astype(BF16)
    pos = qi * tq + lax.broadcasted_iota(jnp.int32, (tq, tk), 0)

    def kv_step(kv, carry):
        k0 = pl.multiple_of(kv * tk, tk)
        s = _dot_nt(q4, k_ref[pl.ds(k0, tk), :].astype(BF16))
        key = k0 + lax.broadcasted_iota(jnp.int32, (nsel, tk), 1)
        blk = lax.broadcasted_iota(jnp.int32, (nsel, tk), 0)
        expand = jnp.where(key // SEL_BLOCK == blk, 1.0, 0.0).astype(BF16)
        picked = _dot(sel, expand)
        kpos = k0 + lax.broadcasted_iota(jnp.int32, (tq, tk), 1)
        mask = ((picked > 0.5) & (kpos <= pos))[None]
        s = jnp.where(mask, s.reshape(g, tq, tk), NEG_BIG)
        m_prev = m_ref[...]
        m_new = jnp.maximum(m_prev, jnp.max(s, axis=-1, keepdims=True))
        p = jnp.exp(s - m_new)
        alpha = jnp.exp(m_prev - m_new)
        l_ref[...] = alpha * l_ref[...] + jnp.sum(p, axis=-1, keepdims=True)
        pv = _dot(p.reshape(g * tq, tk).astype(BF16), v_ref[pl.ds(k0, tk), :].astype(BF16))
        acc_ref[...] = alpha * acc_ref[...] + pv.reshape(g, tq, HEAD_DIM)
        m_ref[...] = m_new
        return carry

    lax.fori_loop(0, (qi * tq + tq - 1) // tk + 1, kv_step, 0)
    o = acc_ref[...] * (1.0 / jnp.maximum(l_ref[...], 1e-30))
    for i in range(g):
        o_ref[:, i * HEAD_DIM:(i + 1) * HEAD_DIM] = o[i]


def _selattn(proj, sel, tq=2 * Q_BLOCK, tk=512):
    bsz, t, _ = proj.shape
    nsel = t // SEL_BLOCK
    tk = min(tk, t)
    gw = NSA_GROUP * HEAD_DIM

    def kv_spec(base):
        return pl.BlockSpec((None, t, HEAD_DIM), lambda b, h, i: (b, 0, base + h))

    return pl.pallas_call(
        functools.partial(_selattn_kernel, tq=tq, tk=tk, nsel=nsel),
        grid=(bsz, NSA_KV_HEADS, t // tq),
        in_specs=[
            pl.BlockSpec((None, tq, gw), lambda b, h, i: (b, i, COL_NSA_Q // NSA_GROUP + h)),
            kv_spec(COL_KV + 2 * NSA_KV_HEADS), kv_spec(COL_KV + 3 * NSA_KV_HEADS),
            pl.BlockSpec((None, None, tq, nsel), lambda b, h, i: (b, h, i, 0)),
        ],
        out_specs=pl.BlockSpec((None, tq, gw), lambda b, h, i: (b, i, h)),
        out_shape=jax.ShapeDtypeStruct((bsz, t, NSA_WIDTH), F32),
        scratch_shapes=[pltpu.VMEM((NSA_GROUP, tq, 1), F32), pltpu.VMEM((NSA_GROUP, tq, 1), F32),
                        pltpu.VMEM((NSA_GROUP, tq, HEAD_DIM), F32)],
        compiler_params=_cparams(("arbitrary", "arbitrary", "arbitrary")),
        name="selattn",
    )(proj, proj, proj, sel)


def _winattn_kernel(*refs, tq, nblk):
    q_ref = refs[0]
    k_refs = refs[1:1 + nblk]
    v_refs = refs[1 + nblk:1 + 2 * nblk]
    ocmp_ref, osel_ref, small_ref, o_ref = refs[1 + 2 * nblk:]
    g = NSA_GROUP
    hk = pl.program_id(1)
    qi = pl.program_id(2)
    scale = HEAD_DIM ** -0.5
    nk = nblk * tq
    keys = jnp.concatenate([r[...] for r in k_refs], axis=0).astype(BF16)
    vals = jnp.concatenate([r[...] for r in v_refs], axis=0).astype(BF16)
    q4 = jnp.concatenate([q_ref[:, i * HEAD_DIM:(i + 1) * HEAD_DIM] for i in range(g)], axis=0)
    s = (_dot_nt(q4.astype(BF16), keys) * scale).reshape(g, tq, nk)
    pos = qi * tq + lax.broadcasted_iota(jnp.int32, (tq, nk), 0)
    wpos = (qi - (nblk - 1)) * tq + lax.broadcasted_iota(jnp.int32, (tq, nk), 1)
    mask = ((wpos <= pos) & (wpos > pos - WINDOW) & (wpos >= 0))[None]
    s = jnp.where(mask, s, NEG_BIG)
    e = jnp.exp(s - jnp.max(s, axis=-1, keepdims=True)).astype(BF16)
    ov = _dot(e.reshape(g * tq, nk), jnp.concatenate([vals, jnp.ones_like(vals)], axis=1))
    o_win = (ov[:, 0:HEAD_DIM] * (1.0 / jnp.maximum(ov[:, HEAD_DIM:2 * HEAD_DIM], 1e-30))).reshape(g, tq, HEAD_DIM)
    small = small_ref[...]
    for i in range(g):
        head = hk * g + i
        gates = [jax.nn.sigmoid(_lane_pick(small, SMALL_NSA_GATE + j * NSA_HEADS + head)) for j in range(3)]
        cols = slice(i * HEAD_DIM, (i + 1) * HEAD_DIM)
        o = gates[0] * ocmp_ref[:, cols] + gates[1] * osel_ref[:, cols] + gates[2] * o_win[i]
        o_ref[:, cols] = o.astype(o_ref.dtype)


def _winattn(proj, o_cmp, o_sel, tq=2 * Q_BLOCK):
    bsz, t, _ = proj.shape
    nblk = WINDOW // tq + 1
    gw = NSA_GROUP * HEAD_DIM

    def kv_spec(base, d):
        return pl.BlockSpec((None, tq, HEAD_DIM),
                            lambda b, h, i: (b, jnp.maximum(i - (nblk - 1) + d, 0), base + h))

    grp = pl.BlockSpec((None, tq, gw), lambda b, h, i: (b, i, h))
    return pl.pallas_call(
        functools.partial(_winattn_kernel, tq=tq, nblk=nblk),
        grid=(bsz, NSA_KV_HEADS, t // tq),
        in_specs=(
            [pl.BlockSpec((None, tq, gw), lambda b, h, i: (b, i, COL_NSA_Q // NSA_GROUP + h))]
            + [kv_spec(COL_KV + 4 * NSA_KV_HEADS, d) for d in range(nblk)]
            + [kv_spec(COL_KV + 5 * NSA_KV_HEADS, d) for d in range(nblk)]
            + [grp, grp, pl.BlockSpec((None, tq, HEAD_DIM), lambda b, h, i: (b, i, COL_SMALL))]
        ),
        out_specs=grp,
        out_shape=jax.ShapeDtypeStruct((bsz, t, NSA_WIDTH), BF16),
        compiler_params=_cparams(("arbitrary", "arbitrary", "arbitrary")),
        name="winattn",
    )(proj, *([proj] * (2 * nblk)), o_cmp, o_sel, proj)


def _ffn(h, wg, wu, wd):
    act = _silu(_dot(h, wg)) * _dot(h, wu)
    return _dot(act.astype(BF16), wd)


def _pack_bf16_pairs(x):
    half = x.shape[1] // 2
    lo = pltpu.bitcast(x[:, :half].astype(BF16).astype(F32), jnp.uint32)
    hi = pltpu.bitcast(x[:, half:].astype(BF16).astype(F32), jnp.uint32)
    return hi | (lo >> 16)


def _unpack_bf16_pairs(p):
    return pltpu.bitcast(p << 16, F32), pltpu.bitcast(p & jnp.uint32(0xFFFF0000), F32)


def _store_token_rows(ref, x, first=0):
    n, d = x.shape
    s_tiles = d // HEAD_DIM
    for s in range(s_tiles):
        ref[pl.ds(first * s_tiles + s, n, stride=s_tiles), :] = x[:, s * HEAD_DIM:(s + 1) * HEAD_DIM]


def _load_token_rows(ref, n, d, first=0):
    s_tiles = d // HEAD_DIM
    return jnp.concatenate([ref[pl.ds(first * s_tiles + s, n, stride=s_tiles), :] for s in range(s_tiles)], axis=1)


def _outproj_kernel(yg_ref, yn_ref, x_ref, wo_ref, gate_ref, shift_ref, scale_ref, gate_m_ref, nw_ref,
                    rw_ref, rb_ref, sg_ref, su_ref, sd_ref, x1_ref, h_ref, route_ref, count_ref, carry_ref):
    tm = x_ref.shape[0]
    half = yg_ref.shape[1]

    @pl.when((pl.program_id(0) == 0) & (pl.program_id(1) == 0))
    def _():
        carry_ref[...] = jnp.zeros_like(carry_ref)

    y = _dot(yg_ref[...], wo_ref[0:half, :]) + _dot(yn_ref[...], wo_ref[half:2 * half, :])
    x1 = x_ref[...] + gate_ref[...] * y
    h = _modulated_norm(x1, nw_ref[...], scale_ref[...], shift_ref[...])
    _store_token_rows(h_ref, _pack_bf16_pairs(h))
    x1_ref[...] = x1 + gate_m_ref[...] * _ffn(h.astype(BF16), sg_ref[...], su_ref[...], sd_ref[...])
    scores = jax.nn.sigmoid(_dot_split(h, rw_ref[...]))
    n_exp = scores.shape[1]
    lane = lax.broadcasted_iota(jnp.int32, scores.shape, 1)
    cand = scores + rb_ref[...]
    picked = jnp.zeros(scores.shape, jnp.bool_)
    firsts = []
    for _ in range(TOP_K):
        best = jnp.max(cand, axis=-1, keepdims=True)
        first = jnp.min(jnp.where(cand == best, lane, n_exp), axis=-1, keepdims=True)
        hit = lane == first
        picked = picked | hit
        cand = jnp.where(hit, NEG_BIG, cand)
        firsts.append(first)
    norm = ROUTED_SCALE / jnp.sum(jnp.where(picked, scores, 0.0), axis=-1, keepdims=True)
    onehot = jnp.where(picked, 1.0, 0.0)
    ri = lax.broadcasted_iota(jnp.int32, (tm, tm), 0)
    ci = lax.broadcasted_iota(jnp.int32, (tm, tm), 1)
    before = jnp.where(ci < ri, 1.0, 0.0).astype(BF16)
    pos = _dot(before, onehot.astype(BF16)) + carry_ref[...]
    carry_ref[...] += jnp.sum(onehot, axis=0, keepdims=True)
    count_ref[...] = carry_ref[...]
    slot = lax.broadcasted_iota(jnp.int32, (tm, 3 * TOP_K), 1)
    route = jnp.zeros((tm, 3 * TOP_K), F32)
    for j, first in enumerate(firsts):
        hit = lane == first
        route = jnp.where(slot == j, first.astype(F32), route)
        route = jnp.where(slot == TOP_K + j, _lane_pick(scores, first) * norm, route)
        route = jnp.where(slot == 2 * TOP_K + j, jnp.sum(jnp.where(hit, pos, 0.0), axis=-1, keepdims=True), route)
    route_ref[...] = route


def _outproj(y_gdn, y_nsa, x, w_out_b, mod4, norm_w, router_w, router_bias, sg, su, sd, tm=256):
    bsz, t, d = x.shape
    half = y_gdn.shape[2]
    n_exp = router_w.shape[1]
    sff = sg.shape[1]
    s_tiles = d // 2 // HEAD_DIM

    def mod(j):
        return pl.BlockSpec((None, None, 1, d), lambda b, i: (b, j, 0, 0))

    def const(shape):
        return pl.BlockSpec(shape, lambda b, i: (0, 0))

    row = pl.BlockSpec((None, tm, d), lambda b, i: (b, i, 0))
    return pl.pallas_call(
        _outproj_kernel,
        grid=(bsz, t // tm),
        in_specs=[
            pl.BlockSpec((None, tm, half), lambda b, i: (b, i, 0)),
            pl.BlockSpec((None, tm, half), lambda b, i: (b, i, 0)),
            row,
            const((2 * half, d)),
            mod(2), mod(3), mod(4), mod(5),
            const((1, d)), const((d, n_exp)), const((1, n_exp)),
            const((d, sff)), const((d, sff)), const((sff, d)),
        ],
        out_specs=[row,
                   pl.BlockSpec((tm * s_tiles, HEAD_DIM), lambda b, i: (b * (t // tm) + i, 0)),
                   pl.BlockSpec((None, tm, 3 * TOP_K), lambda b, i: (b, i, 0)),
                   const((1, n_exp))],
        out_shape=[
            jax.ShapeDtypeStruct((bsz, t, d), F32),
            jax.ShapeDtypeStruct((bsz * t * s_tiles, HEAD_DIM), jnp.uint32),
            jax.ShapeDtypeStruct((bsz, t, 3 * TOP_K), F32),
            jax.ShapeDtypeStruct((1, n_exp), F32),
        ],
        scratch_shapes=[pltpu.VMEM((1, n_exp), F32)],
        compiler_params=_cparams(("arbitrary", "arbitrary")),
        name="outproj",
    )(y_gdn, y_nsa, x, w_out_b, mod4, mod4, mod4, mod4, norm_w.reshape(1, d), router_w,
      router_bias.reshape(1, n_exp), sg, su, sd)


MOE_TM = 256
MOE_TOK = 256
MOE_TILES_PER_STEP = 4


def _token_row(ref, r, s_tiles):
    return ref.at[pl.ds(pl.multiple_of(r * s_tiles, s_tiles), s_tiles), :]


def _for_each(n, fn, unroll=8):
    def body(i, carry):
        fn(i)
        return carry

    lax.fori_loop(0, n, body, 0, unroll=unroll)


def _moe_scatter_kernel(gstart_ref, gend_ref, idx_ref, pos_ref, h_ref, xs_hbm, zero_ref, sem, *, s_tiles):
    n_exp = gstart_ref.shape[0]

    @pl.when(pl.program_id(0) == 0)
    def _():
        zero_ref[...] = jnp.zeros_like(zero_ref)

        def tail(e):
            first = pl.multiple_of((gend_ref[e] - MOE_TM) * s_tiles, MOE_TM * s_tiles)
            return pltpu.make_async_copy(zero_ref, xs_hbm.at[pl.ds(first, MOE_TM * s_tiles), :], sem)

        def start_tail(e):
            @pl.when(gend_ref[e] > gstart_ref[e])
            def _():
                tail(e).start()

        def wait_tail(e):
            @pl.when(gend_ref[e] > gstart_ref[e])
            def _():
                tail(e).wait()

        _for_each(n_exp, start_tail)
        _for_each(n_exp, wait_tail)

    def row_copy(t, j):
        n = t * TOP_K + j
        row = gstart_ref[idx_ref[n]] + pos_ref[n]
        return pltpu.make_async_copy(_token_row(h_ref, t, s_tiles), _token_row(xs_hbm, row, s_tiles), sem)

    def start_token(t):
        for j in range(TOP_K):
            row_copy(t, j).start()

    def wait_token(t):
        for j in range(TOP_K):
            row_copy(t, j).wait()

    _for_each(MOE_TOK, start_token, unroll=2)
    _for_each(MOE_TOK, wait_token, unroll=2)


def _moe_scatter(h_rows, idx, pos, gstart, gend, n_rows, s_tiles):
    n_steps = idx.shape[0] // (MOE_TOK * TOP_K)
    smem = pl.BlockSpec((MOE_TOK * TOP_K,), lambda i, gs, ge: (i,), memory_space=pltpu.SMEM)
    return pl.pallas_call(
        functools.partial(_moe_scatter_kernel, s_tiles=s_tiles),
        grid_spec=pltpu.PrefetchScalarGridSpec(
            num_scalar_prefetch=2,
            grid=(n_steps,),
            in_specs=[smem, smem, pl.BlockSpec((MOE_TOK * s_tiles, HEAD_DIM), lambda i, gs, ge: (i, 0))],
            out_specs=pl.BlockSpec(memory_space=pl.ANY),
            scratch_shapes=[pltpu.VMEM((MOE_TM * s_tiles, HEAD_DIM), h_rows.dtype), pltpu.SemaphoreType.DMA(())],
        ),
        out_shape=jax.ShapeDtypeStruct((n_rows * s_tiles, HEAD_DIM), h_rows.dtype),
        compiler_params=_cparams(("arbitrary",)),
        name="moe_scatter",
    )(gstart, gend, idx, pos, h_rows)


def _moe_expert_kernel(texp_ref, tslot_ref, tnext_ref, nvalid_ref, xs_ref, wg_hbm, wu_hbm, wd_hbm, ys_ref,
                       wg_buf, wu_buf, wd_buf, wgb_ref, wub_ref, wdb_ref, sems):
    d = wgb_ref.shape[0]
    half = d // 2
    rows_per_tile = xs_ref.shape[0] // MOE_TILES_PER_STEP

    def fetch(expert, s):
        return [pltpu.make_async_copy(src.at[expert], dst.at[s], sems.at[s, n])
                for n, (src, dst) in enumerate(((wg_hbm, wg_buf), (wu_hbm, wu_buf), (wd_hbm, wd_buf)))]

    def tile(i, sub):
        valid = i < nvalid_ref[0]
        new_expert = (i == 0) | (texp_ref[i] != texp_ref[jnp.maximum(i - 1, 0)])
        slot = tslot_ref[i]

        @pl.when(i == 0)
        def _():
            for cp in fetch(texp_ref[0], slot):
                cp.start()

        @pl.when(valid & new_expert)
        def _():
            nxt = tnext_ref[i]

            @pl.when(nxt < nvalid_ref[0])
            def _():
                for cp in fetch(texp_ref[nxt], 1 - slot):
                    cp.start()

            for cp in fetch(texp_ref[i], slot):
                cp.wait()
            wgb_ref[...] = wg_buf[slot].astype(BF16)
            wub_ref[...] = wu_buf[slot].astype(BF16)
            wdb_ref[...] = wd_buf[slot].astype(BF16)

        @pl.when(valid)
        def _():
            lo, hi = _unpack_bf16_pairs(_load_token_rows(xs_ref, MOE_TM, half, first=sub * MOE_TM))
            lo, hi = lo.astype(BF16), hi.astype(BF16)
            gate = _dot(lo, wgb_ref[0:half, :]) + _dot(hi, wgb_ref[half:d, :])
            up = _dot(lo, wub_ref[0:half, :]) + _dot(hi, wub_ref[half:d, :])
            y = _dot((_silu(gate) * up).astype(BF16), wdb_ref[...])
            _store_token_rows(ys_ref, _pack_bf16_pairs(y), first=sub * MOE_TM)

        @pl.when(jnp.logical_not(valid))
        def _():
            ys_ref[sub * rows_per_tile:(sub + 1) * rows_per_tile, :] = jnp.zeros((rows_per_tile, HEAD_DIM), ys_ref.dtype)

    for sub in range(MOE_TILES_PER_STEP):
        tile(pl.program_id(0) * MOE_TILES_PER_STEP + sub, sub)


def _moe_experts(xs, tile_expert, tile_slot, tile_next, nvalid, wg, wu, wd):
    n_exp, d, ff = wg.shape
    n_steps = tile_expert.shape[0] // MOE_TILES_PER_STEP
    blk = xs.shape[0] // n_steps
    hbm = pl.BlockSpec(memory_space=pl.ANY)
    vmem_bytes = 3 * d * ff * (2 * 4 + 2) + 4 * blk * HEAD_DIM * 4 + 3 * MOE_TM * d * 4
    return pl.pallas_call(
        _moe_expert_kernel,
        grid_spec=pltpu.PrefetchScalarGridSpec(
            num_scalar_prefetch=4,
            grid=(n_steps,),
            in_specs=[
                pl.BlockSpec((blk, HEAD_DIM),
                             lambda i, te, ts, tn, nv: (jnp.minimum(i, (nv[0] - 1) // MOE_TILES_PER_STEP), 0)),
                hbm, hbm, hbm,
            ],
            out_specs=pl.BlockSpec((blk, HEAD_DIM), lambda i, te, ts, tn, nv: (i, 0)),
            scratch_shapes=[pltpu.VMEM((2, d, ff), wg.dtype), pltpu.VMEM((2, d, ff), wu.dtype),
                            pltpu.VMEM((2, ff, d), wd.dtype),
                            pltpu.VMEM((d, ff), BF16), pltpu.VMEM((d, ff), BF16), pltpu.VMEM((ff, d), BF16),
                            pltpu.SemaphoreType.DMA((2, 3))],
        ),
        out_shape=jax.ShapeDtypeStruct(xs.shape, xs.dtype),
        compiler_params=_cparams(("arbitrary",), vmem_bytes),
        name="moe_experts",
    )(tile_expert, tile_slot, tile_next, nvalid, xs, wg, wu, wd)


def _moe_combine_kernel(gstart_ref, idx_ref, pos_ref, w_ref, ys_hbm, x1_ref, gate_ref, nw_ref, o_ref,
                        buf_ref, lo_ref, hi_ref, sem, *, s_tiles):
    tm, d = x1_ref.shape

    def row_copy(n):
        row = gstart_ref[idx_ref[n]] + pos_ref[n]
        return pltpu.make_async_copy(_token_row(ys_hbm, row, s_tiles), _token_row(buf_ref, n, s_tiles), sem)

    _for_each(tm * TOP_K, lambda n: row_copy(n).start())
    _for_each(tm * TOP_K, lambda n: row_copy(n).wait())

    def token(t):
        n0 = t * TOP_K
        lo, hi = _unpack_bf16_pairs(_token_row(buf_ref, n0, s_tiles)[...])
        acc_lo, acc_hi = w_ref[n0] * lo, w_ref[n0] * hi
        for j in range(1, TOP_K):
            lo, hi = _unpack_bf16_pairs(_token_row(buf_ref, n0 + j, s_tiles)[...])
            acc_lo, acc_hi = acc_lo + w_ref[n0 + j] * lo, acc_hi + w_ref[n0 + j] * hi
        _token_row(lo_ref, t, s_tiles)[...] = acc_lo
        _token_row(hi_ref, t, s_tiles)[...] = acc_hi

    _for_each(tm, token)
    routed = jnp.concatenate([_load_token_rows(lo_ref, tm, d // 2), _load_token_rows(hi_ref, tm, d // 2)], axis=1)
    x2 = x1_ref[...] + gate_ref[...] * routed
    o_ref[...] = x2 * lax.rsqrt(jnp.mean(x2 * x2, axis=-1, keepdims=True) + EPS) * nw_ref[...]


def _moe_combine(ys, idx, pos, w, gstart, x1, mod4, norm_w):
    bsz, t, d = x1.shape
    s_tiles = d // 2 // HEAD_DIM
    tm = MOE_TOK
    nt = t // tm
    smem = pl.BlockSpec((tm * TOP_K,), lambda b, i, gs: (b * nt + i,), memory_space=pltpu.SMEM)
    row = pl.BlockSpec((None, tm, d), lambda b, i, gs: (b, i, 0))
    return pl.pallas_call(
        functools.partial(_moe_combine_kernel, s_tiles=s_tiles),
        grid_spec=pltpu.PrefetchScalarGridSpec(
            num_scalar_prefetch=1,
            grid=(bsz, nt),
            in_specs=[
                smem, smem, smem,
                pl.BlockSpec(memory_space=pl.ANY),
                row,
                pl.BlockSpec((None, None, 1, d), lambda b, i, gs: (b, 5, 0, 0)),
                pl.BlockSpec((1, d), lambda b, i, gs: (0, 0)),
            ],
            out_specs=row,
            scratch_shapes=[pltpu.VMEM((tm * TOP_K * s_tiles, HEAD_DIM), ys.dtype),
                            pltpu.VMEM((tm * s_tiles, HEAD_DIM), F32), pltpu.VMEM((tm * s_tiles, HEAD_DIM), F32),
                            pltpu.SemaphoreType.DMA(())],
        ),
        out_shape=jax.ShapeDtypeStruct((bsz, t, d), F32),
        compiler_params=_cparams(("arbitrary", "arbitrary")),
        name="moe_combine",
    )(gstart, idx, pos, w, ys, x1, mod4, norm_w.reshape(1, d))


def _moe(h_rows, route, counts, x1, mod4, norm_w, wg, wu, wd):
    bsz, t, d = x1.shape
    n_exp = wg.shape[0]
    s_tiles = h_rows.shape[0] // (bsz * t)
    idx = route[..., 0:TOP_K].astype(jnp.int32).reshape(-1)
    w = route[..., TOP_K:2 * TOP_K].reshape(-1)
    pos = route[..., 2 * TOP_K:3 * TOP_K].astype(jnp.int32).reshape(-1)
    padded = (counts.reshape(n_exp).astype(jnp.int32) + MOE_TM - 1) // MOE_TM * MOE_TM
    gend = jnp.cumsum(padded)
    gstart = gend - padded
    nvalid = gend[-1:] // MOE_TM
    n_tiles = bsz * t * TOP_K // MOE_TM + n_exp
    tile_start = jnp.minimum(jnp.arange(n_tiles, dtype=jnp.int32), nvalid - 1) * MOE_TM
    tile_expert = jnp.sum((gend[None, :] <= tile_start[:, None]).astype(jnp.int32), axis=1)
    first_of_group = jnp.concatenate([jnp.ones((1,), jnp.int32),
                                      (tile_expert[1:] != tile_expert[:-1]).astype(jnp.int32)])
    tile_slot = (jnp.cumsum(first_of_group) - 1) % 2
    experts = jnp.arange(n_exp, dtype=jnp.int32)
    tile_next = jnp.sum(jnp.where(tile_expert[:, None] == experts[None, :], gend[None, :], 0), axis=1) // MOE_TM
    xs = _moe_scatter(h_rows, idx, pos, gstart, gend, n_tiles * MOE_TM, s_tiles)
    ys = _moe_experts(xs, tile_expert, tile_slot, tile_next, nvalid, wg, wu, wd)
    return _moe_combine(ys, idx, pos, w, gstart, x1, mod4, norm_w)


def _reorder_w_in(w_in):
    d = w_in.shape[0]
    o_beta = 3 * GDN_WIDTH
    o_a = o_beta + GDN_HEADS
    o_gate = o_a + GDN_HEADS
    o_q = o_gate + GDN_WIDTH
    o_kv = o_q + NSA_WIDTH
    o_ng = o_kv + 6 * NSA_KV_WIDTH
    n_small = 2 * GDN_HEADS + 3 * NSA_HEADS
    parts = [w_in[:, :o_beta], w_in[:, o_gate:o_q], w_in[:, o_q:o_kv], w_in[:, o_kv:o_ng],
             w_in[:, o_beta:o_gate], w_in[:, o_ng:o_ng + 3 * NSA_HEADS]]
    used = COL_SMALL * HEAD_DIM + n_small
    parts.append(jnp.zeros((d, PROJ_COLS - used), w_in.dtype))
    return jnp.concatenate([p.astype(BF16) for p in parts], axis=1)


def _layer(x, mod, norm_attn_w, norm_ffn_w, w_in, gdn_conv_w, gdn_a_log, gdn_dt_bias, gdn_norm_w,
           cmp_pe_k, cmp_w1_k, cmp_w2_k, cmp_pe_v, cmp_w1_v, cmp_w2_v, w_out, router_w, router_bias,
           expert_w_gate, expert_w_up, expert_w_down, shared_w_gate, shared_w_up, shared_w_down, final_w):
    bsz, t, d = x.shape
    mod4 = mod.reshape(bsz, N_MOD, 1, d)
    proj = _proj(x, mod4, norm_attn_w, _reorder_w_in(w_in))
    y_gdn = _gdn(proj, gdn_conv_w, gdn_a_log, gdn_dt_bias, gdn_norm_w)
    kc = _compress(proj, COL_KV, cmp_pe_k, cmp_w1_k, cmp_w2_k)
    vc = _compress(proj, COL_KV + NSA_KV_HEADS, cmp_pe_v, cmp_w1_v, cmp_w2_v)
    o_cmp, sel = _cmpattn(proj, kc, vc)
    o_sel = _selattn(proj, sel)
    y_nsa = _winattn(proj, o_cmp, o_sel)
    x1, h_rows, route, counts = _outproj(
        y_gdn, y_nsa, x, w_out.astype(BF16), mod4, norm_ffn_w, router_w, router_bias,
        shared_w_gate.astype(BF16), shared_w_up.astype(BF16), shared_w_down.astype(BF16))
    return _moe(h_rows, route, counts, x1, mod4, final_w, expert_w_gate, expert_w_up, expert_w_down)


def kernel(x, c, w_ada, b_ada, norm_attn_w, norm_ffn_w, norm_final_w, w_in, gdn_conv_w, gdn_a_log, gdn_dt_bias,
           gdn_norm_w, cmp_pe_k, cmp_w1_k, cmp_w2_k, cmp_pe_v, cmp_w1_v, cmp_w2_v, w_out, router_w, router_bias,
           expert_w_gate, expert_w_up, expert_w_down, shared_w_gate, shared_w_up, shared_w_down):
    depth = w_ada.shape[0]
    assert depth == 1, "the fused final norm assumes a single layer"
    l = 0
    mod = _ada(c, w_ada[l], b_ada[l])
    return _layer(x, mod, norm_attn_w[l], norm_ffn_w[l], w_in[l], gdn_conv_w[l], gdn_a_log[l], gdn_dt_bias[l],
                  gdn_norm_w[l], cmp_pe_k[l], cmp_w1_k[l], cmp_w2_k[l], cmp_pe_v[l], cmp_w1_v[l], cmp_w2_v[l],
                  w_out[l], router_w[l], router_bias[l], expert_w_gate[l], expert_w_up[l], expert_w_down[l],
                  shared_w_gate[l], shared_w_up[l], shared_w_down[l], norm_final_w)
```

```python
import functools

import jax
import jax.numpy as jnp
from jax import lax
from jax.experimental import pallas as pl
from jax.experimental.pallas import tpu as pltpu

F32 = jnp.float32
BF16 = jnp.bfloat16
HI = lax.Precision.HIGHEST

HEAD_DIM = 128
GDN_HEADS = 8
NSA_HEADS = 8
NSA_KV_HEADS = 2
NSA_GROUP = NSA_HEADS // NSA_KV_HEADS
GDN_WIDTH = GDN_HEADS * HEAD_DIM
NSA_WIDTH = NSA_HEADS * HEAD_DIM
NSA_KV_WIDTH = NSA_KV_HEADS * HEAD_DIM
CONV_WIDTH = 4
GDN_CHUNK = 64
CMP_BLOCK = 32
CMP_STRIDE = 16
SEL_BLOCK = 64
SEL_TOPK = 16
N_LOCAL_BLOCKS = 2
WINDOW = 512
Q_BLOCK = 128
TOP_K = 8
ROUTED_SCALE = 2.5
N_MOD = 6
EPS = 1e-6

COL_GDN_Q = 0
COL_GDN_K = GDN_HEADS
COL_GDN_V = 2 * GDN_HEADS
COL_GDN_GATE = 3 * GDN_HEADS
COL_NSA_Q = 4 * GDN_HEADS
COL_KV = COL_NSA_Q + NSA_HEADS
COL_SMALL = COL_KV + 6 * NSA_KV_HEADS
N_COL_BLOCKS = COL_SMALL + 1
PROJ_TN = 2304
PROJ_COLS = 3 * PROJ_TN
SMALL_BETA = 0
SMALL_A = GDN_HEADS
SMALL_NSA_GATE = 2 * GDN_HEADS

NEG_BIG = -1e30
VMEM_LIMIT = 48 * 1024 * 1024


def _cparams(semantics, vmem_limit=VMEM_LIMIT):
    return pltpu.CompilerParams(dimension_semantics=semantics, vmem_limit_bytes=vmem_limit)


def _silu(x):
    return (0.5 * x) * (1.0 + jnp.tanh(0.5 * x))


def _dot(a, b, precision=None):
    return jnp.dot(a, b, preferred_element_type=F32, precision=precision)


def _dot_nt(a, b, precision=None):
    return lax.dot_general(a, b, (((1,), (1,)), ((), ())), preferred_element_type=F32, precision=precision)


def _dot_tn(a, b, precision=None):
    return lax.dot_general(a, b, (((0,), (0,)), ((), ())), preferred_element_type=F32, precision=precision)


def _split_bf16(x):
    hi = x.astype(BF16)
    return hi, (x - hi.astype(F32)).astype(BF16)


def _dot_split(a, b, nt=False):
    (ah, al), (bh, bl) = _split_bf16(a), _split_bf16(b)
    f = _dot_nt if nt else _dot
    return f(ah, bh) + (f(ah, bl) + f(al, bh))


def _lane_pick(x, idx):
    lane = lax.broadcasted_iota(jnp.int32, x.shape, 1)
    return jnp.sum(jnp.where(lane == idx, x, 0.0), axis=-1, keepdims=True)


def _ada_kernel(ct_ref, w_ref, b_ref, o_ref):
    c = ct_ref[...]
    cond = _silu(c)
    w = w_ref[...]
    rows = [jnp.sum(w * cond[:, b:b + 1], axis=0, keepdims=True) for b in range(c.shape[1])]
    o_ref[...] = jnp.concatenate(rows, axis=0) + b_ref[...]


def _ada(c, w_ada, b_ada, tn=512):
    bsz, d = c.shape
    n = w_ada.shape[1]
    return pl.pallas_call(
        _ada_kernel,
        grid=(n // tn,),
        in_specs=[
            pl.BlockSpec((d, bsz), lambda j: (0, 0)),
            pl.BlockSpec((d, tn), lambda j: (0, j)),
            pl.BlockSpec((1, tn), lambda j: (0, j)),
        ],
        out_specs=pl.BlockSpec((bsz, tn), lambda j: (0, j)),
        out_shape=jax.ShapeDtypeStruct((bsz, n), F32),
        compiler_params=_cparams(("arbitrary",)),
        name="ada",
    )(c.T, w_ada, b_ada.reshape(1, n))


def _modulated_norm(x, nw, scale, shift):
    var = jnp.mean(x * x, axis=-1, keepdims=True)
    h = x * lax.rsqrt(var + EPS) * nw
    return h * (1.0 + scale) + shift


def _proj_kernel(x_ref, shift_ref, scale_ref, nw_ref, w_ref, o_ref):
    h = _modulated_norm(x_ref[...], nw_ref[...], scale_ref[...], shift_ref[...])
    o_ref[...] = _dot(h.astype(BF16), w_ref[...])


def _proj(x, mod4, norm_w, w_r, tm=512):
    bsz, t, d = x.shape
    n = w_r.shape[1]
    return pl.pallas_call(
        _proj_kernel,
        grid=(n // PROJ_TN, bsz, t // tm),
        in_specs=[
            pl.BlockSpec((None, tm, d), lambda j, b, i: (b, i, 0)),
            pl.BlockSpec((None, None, 1, d), lambda j, b, i: (b, 0, 0, 0)),
            pl.BlockSpec((None, None, 1, d), lambda j, b, i: (b, 1, 0, 0)),
            pl.BlockSpec((1, d), lambda j, b, i: (0, 0)),
            pl.BlockSpec((d, PROJ_TN), lambda j, b, i: (0, j)),
        ],
        out_specs=pl.BlockSpec((None, tm, PROJ_TN), lambda j, b, i: (b, i, j)),
        out_shape=jax.ShapeDtypeStruct((bsz, t, n), F32),
        compiler_params=_cparams(("arbitrary", "arbitrary", "arbitrary")),
        name="proj",
    )(x, mod4, mod4, norm_w.reshape(1, d), w_r)


def _gdn_kernel(q_ref, k_ref, v_ref, gate_ref, small_ref, hq_ref, hk_ref, hv_ref,
                cwq_ref, cwk_ref, cwv_ref, alog_ref, dtb_ref, nw_ref, o_ref, s_ref, xc_ref, *, tt, hg):
    first = pl.program_id(2) == 0

    @pl.when(first)
    def _():
        s_ref[...] = jnp.zeros_like(s_ref)

    def conv_silu(x_ref, halo_ref, cw_ref):
        xc_ref[0:8, :] = jnp.where(first, 0.0, halo_ref[...])
        xc_ref[8:tt + 8, :] = x_ref[...]
        cw = cw_ref[...]
        off = 8 - (CONV_WIDTH - 1)
        y = cw[0:1, :] * xc_ref[off:off + tt, :]
        for i in range(1, CONV_WIDTH):
            y = y + cw[i:i + 1, :] * xc_ref[off + i:off + i + tt, :]
        return _silu(y)

    q_all = conv_silu(q_ref, hq_ref, cwq_ref)
    k_all = conv_silu(k_ref, hk_ref, cwk_ref)
    v_all = conv_silu(v_ref, hv_ref, cwv_ref)
    ri = lax.broadcasted_iota(jnp.int32, (tt, tt), 0)
    ci = lax.broadcasted_iota(jnp.int32, (tt, tt), 1)
    lower = jnp.where(((ri // GDN_CHUNK) == (ci // GDN_CHUNK)) & (ci <= ri), 1.0, 0.0).astype(BF16)
    heads = [pl.program_id(1) * hg + hh for hh in range(hg)]
    outs = _gdn_heads(q_all, k_all, v_all, small_ref[...], lower, heads, alog_ref[...], dtb_ref[...], s_ref, tt)
    for hh, o in enumerate(outs):
        cols = slice(hh * HEAD_DIM, (hh + 1) * HEAD_DIM)
        o = o * lax.rsqrt(jnp.mean(o * o, axis=-1, keepdims=True) + EPS) * nw_ref[...]
        o_ref[:, cols] = (o * _silu(gate_ref[:, cols])).astype(o_ref.dtype)


def _gdn_heads(q_all, k_all, v_all, small, lower, heads, alog, dtb, s_ref, tt):
    c = GDN_CHUNK
    nch = tt // c
    hg = len(heads)
    kpos = lax.broadcasted_iota(jnp.int32, (tt, c), 0) % c
    jj = lax.broadcasted_iota(jnp.int32, (tt, c), 1)
    per_head = []
    for hh, h in enumerate(heads):
        cols = slice(hh * HEAD_DIM, (hh + 1) * HEAD_DIM)
        q, k, v = q_all[:, cols], k_all[:, cols], v_all[:, cols]
        q = q * lax.rsqrt(jnp.sum(q * q, axis=-1, keepdims=True) + EPS) * (HEAD_DIM ** -0.5)
        k = k * lax.rsqrt(jnp.sum(k * k, axis=-1, keepdims=True) + EPS)
        beta = jax.nn.sigmoid(_lane_pick(small, SMALL_BETA + h))
        z = _lane_pick(small, SMALL_A + h) + _lane_pick(dtb, h)
        softplus = jnp.maximum(z, 0.0) + jnp.log(1.0 + jnp.exp(-jnp.abs(z)))
        g = -jnp.exp(_lane_pick(alog, h)) * softplus
        g_c = jnp.broadcast_to(g, (tt, c))
        rhs = jnp.concatenate([g_c, jnp.where(kpos > jj, g_c, 0.0)], axis=1)
        r_hi = rhs.astype(BF16)
        rem = rhs - r_hi.astype(F32)
        r_mid = rem.astype(BF16)
        r_lo = (rem - r_mid.astype(F32)).astype(BF16)
        sums = _dot(lower, r_hi) + (_dot(lower, r_mid) + _dot(lower, r_lo))
        gcum = jnp.broadcast_to(sums[:, 0:1], (tt, HEAD_DIM))
        gc3 = gcum.reshape(nch, c, HEAD_DIM)
        grest = (gc3[:, c - 1:c, :] - gc3).reshape(tt, HEAD_DIM)
        decay_h = jnp.where(kpos >= jj, jnp.exp(sums[:, c:2 * c]), 0.0)
        per_head.append((q, k, v * beta, k * beta, jnp.exp(gcum), jnp.exp(grest), decay_h))

    def stack(j):
        x = jnp.concatenate([ph[j] for ph in per_head], axis=0)
        return x.reshape(hg * nch, c, x.shape[-1])

    q, k, vb, kb, egc, egr, decay = (stack(j) for j in range(7))
    kpos_all = lax.broadcasted_iota(jnp.int32, (hg * tt, c), 0) % c
    jj_all = lax.broadcasted_iota(jnp.int32, (hg * tt, c), 1)
    strict = (kpos_all > jj_all).reshape(hg * nch, c, c)
    eye = jnp.where(kpos_all == jj_all, 1.0, 0.0).reshape(hg * nch, c, c)

    def bmm(a, b):
        return jnp.einsum('nij,njk->nik', a, b, preferred_element_type=F32)

    def bmm_nt(a, b):
        return jnp.einsum('nid,njd->nij', a, b, preferred_element_type=F32)

    def bmm_split(a, b):
        (ah, al), (bh, bl) = _split_bf16(a), _split_bf16(b)
        return bmm(ah, bh) + (bmm(ah, bl) + bmm(al, bh))

    k3 = k.astype(BF16)
    a = jnp.where(strict, bmm_nt(kb.astype(BF16), k3) * decay, 0.0)
    inv = eye - a
    p = bmm_split(a, a)
    n_sq = (c - 1).bit_length() - 1
    for lvl in range(n_sq):
        inv = inv + bmm_split(inv, p)
        if lvl + 1 < n_sq:
            p = bmm_split(p, p)
    inv_b = inv.astype(BF16)
    u3 = bmm(inv_b, vb.astype(BF16)).astype(BF16)
    w3 = bmm(inv_b, (kb * egc).astype(BF16)).astype(BF16)
    qk3 = (bmm_nt(q.astype(BF16), k3) * decay).astype(BF16)
    kd3 = (k * egr).astype(BF16)
    qe3 = (q * egc - bmm(qk3, w3)).astype(BF16)
    o03 = bmm(qk3, u3)
    kw = [_dot_tn(kd3[i], w3[i]).astype(BF16) for i in range(hg * nch)]
    ku = [_dot_tn(kd3[i], u3[i]) for i in range(hg * nch)]

    states = [s_ref[hh] for hh in range(hg)]
    outs = [[] for _ in range(hg)]
    for n in range(nch):
        for hh in range(hg):
            i = hh * nch + n
            state_b = states[hh].astype(BF16)
            outs[hh].append(_dot(qe3[i], state_b) + o03[i])
            states[hh] = states[hh] * egc[i, c - 1:c, :] - _dot(kw[i], state_b) + ku[i]
    for hh in range(hg):
        s_ref[hh] = states[hh]
    return [jnp.concatenate(o, axis=0) for o in outs]


def _gdn(proj, conv_w, a_log, dt_bias, norm_w, tt=128, hg=8):
    bsz, t, _ = proj.shape
    wdt = hg * HEAD_DIM

    def col(base):
        return pl.BlockSpec((None, tt, wdt), lambda b, h, i: (b, i, base // hg + h))

    def halo(base):
        return pl.BlockSpec((None, 8, wdt), lambda b, h, i: (b, jnp.maximum(i * (tt // 8) - 1, 0), base // hg + h))

    def cw(base):
        return pl.BlockSpec((CONV_WIDTH, wdt), lambda b, h, i: (0, base // hg + h))

    heads = pl.BlockSpec((1, GDN_HEADS), lambda b, h, i: (0, 0))
    return pl.pallas_call(
        functools.partial(_gdn_kernel, tt=tt, hg=hg),
        grid=(bsz, GDN_HEADS // hg, t // tt),
        in_specs=[
            col(COL_GDN_Q), col(COL_GDN_K), col(COL_GDN_V), col(COL_GDN_GATE),
            pl.BlockSpec((None, tt, HEAD_DIM), lambda b, h, i: (b, i, COL_SMALL)),
            halo(COL_GDN_Q), halo(COL_GDN_K), halo(COL_GDN_V),
            cw(COL_GDN_Q), cw(COL_GDN_K), cw(COL_GDN_V),
            heads, heads,
            pl.BlockSpec((1, HEAD_DIM), lambda b, h, i: (0, 0)),
        ],
        out_specs=pl.BlockSpec((None, tt, wdt), lambda b, h, i: (b, i, h)),
        out_shape=jax.ShapeDtypeStruct((bsz, t, GDN_WIDTH), BF16),
        scratch_shapes=[pltpu.VMEM((hg, HEAD_DIM, HEAD_DIM), F32), pltpu.VMEM((tt + 8, wdt), F32)],
        compiler_params=_cparams(("arbitrary", "arbitrary", "arbitrary")),
        name="gdn",
    )(proj, proj, proj, proj, proj, proj, proj, proj, conv_w, conv_w, conv_w,
      a_log.reshape(1, GDN_HEADS), dt_bias.reshape(1, GDN_HEADS), norm_w.reshape(1, HEAD_DIM))


def _compress_kernel(x_ref, pe_ref, w1_ref, w2_ref, o_ref):
    r = x_ref.shape[0] // CMP_STRIDE
    d = x_ref.shape[1]
    a = b = None
    for l in range(CMP_STRIDE):
        x_l = x_ref[pl.ds(l, r, stride=CMP_STRIDE), :]
        a_l = _dot_split(x_l + pe_ref[l:l + 1, :], w1_ref[l * d:(l + 1) * d, :])
        b_l = _dot_split(x_l + pe_ref[CMP_STRIDE + l:CMP_STRIDE + l + 1, :],
                         w1_ref[(CMP_STRIDE + l) * d:(CMP_STRIDE + l + 1) * d, :])
        a, b = (a_l, b_l) if a is None else (a + a_l, b + b_l)
    hid = _silu(a + pltpu.roll(b, r - 1, 0))
    o_ref[...] = _dot_split(hid, w2_ref[...])


def _compress(proj, col_block, pe, w1, w2):
    bsz, t, _ = proj.shape
    r = t // CMP_STRIDE
    hid = w1.shape[1]
    return pl.pallas_call(
        _compress_kernel,
        grid=(bsz, NSA_KV_HEADS),
        in_specs=[
            pl.BlockSpec((None, t, HEAD_DIM), lambda b, h: (b, 0, col_block + h)),
            pl.BlockSpec((CMP_BLOCK, HEAD_DIM), lambda b, h: (0, 0)),
            pl.BlockSpec((CMP_BLOCK * HEAD_DIM, hid), lambda b, h: (0, 0)),
            pl.BlockSpec((hid, HEAD_DIM), lambda b, h: (0, 0)),
        ],
        out_specs=pl.BlockSpec((None, None, r, HEAD_DIM), lambda b, h: (b, h, 0, 0)),
        out_shape=jax.ShapeDtypeStruct((bsz, NSA_KV_HEADS, r, HEAD_DIM), F32),
        compiler_params=_cparams(("arbitrary", "arbitrary")),
        name="compress",
    )(proj, pe, w1, w2)


def _cmpattn_kernel(q_ref, kc_ref, vc_ref, o_ref, sel_ref, *, tq, nc, nsel):
    ncp = kc_ref.shape[0]
    scale = HEAD_DIM ** -0.5
    t0 = pl.program_id(2) * tq
    kc = kc_ref[...]
    vc = vc_ref[...]
    pos_r = t0 + lax.broadcasted_iota(jnp.int32, (tq, ncp), 0)
    c_r = lax.broadcasted_iota(jnp.int32, (tq, ncp), 1)
    valid_r = (c_r * CMP_STRIDE + CMP_BLOCK - 1 <= pos_r) & (c_r < nc)
    pos_c = t0 + lax.broadcasted_iota(jnp.int32, (ncp, tq), 1)
    c_c = lax.broadcasted_iota(jnp.int32, (ncp, tq), 0)
    valid_c = (c_c * CMP_STRIDE + CMP_BLOCK - 1 <= pos_c) & (c_c < nc)
    psum = jnp.zeros((ncp, tq), F32)
    kc_hi, kc_lo = _split_bf16(kc)
    vc_b = vc.astype(BF16)
    for g in range(NSA_GROUP):
        q_hi, q_lo = _split_bf16(q_ref[:, g * HEAD_DIM:(g + 1) * HEAD_DIM])
        s = _dot_nt(q_hi, kc_hi) + (_dot_nt(q_hi, kc_lo) + _dot_nt(q_lo, kc_hi))
        s = jnp.where(valid_r, s * scale, NEG_BIG)
        e = jnp.where(valid_r, jnp.exp(s - jnp.max(s, axis=-1, keepdims=True)), 0.0)
        p = e * (1.0 / jnp.maximum(jnp.sum(e, axis=-1, keepdims=True), 1e-30))
        o_ref[:, g * HEAD_DIM:(g + 1) * HEAD_DIM] = _dot(p.astype(BF16), vc_b)
        st = _dot_nt(kc_hi, q_hi) + (_dot_nt(kc_lo, q_hi) + _dot_nt(kc_hi, q_lo))
        st = jnp.where(valid_c, st * scale, NEG_BIG)
        et = jnp.where(valid_c, jnp.exp(st - jnp.max(st, axis=0, keepdims=True)), 0.0)
        psum = psum + et * (1.0 / jnp.maximum(jnp.sum(et, axis=0, keepdims=True), 1e-30))
    j_o = lax.broadcasted_iota(jnp.int32, (nsel, ncp), 0) * SEL_BLOCK
    c_o = lax.broadcasted_iota(jnp.int32, (nsel, ncp), 1) * CMP_STRIDE
    overlap = jnp.where((c_o < j_o + SEL_BLOCK) & (c_o + CMP_BLOCK > j_o), 1.0, 0.0).astype(BF16)
    p_hi, p_lo = _split_bf16(psum)
    imp = _dot(overlap, p_hi) + _dot(overlap, p_lo)
    blk = lax.broadcasted_iota(jnp.int32, (nsel, tq), 0)
    dist = (t0 + lax.broadcasted_iota(jnp.int32, (nsel, tq), 1)) // SEL_BLOCK - blk
    forced = (blk == 0) | ((dist >= 0) & (dist < N_LOCAL_BLOCKS))
    val = jnp.where(forced, -NEG_BIG, jnp.where(dist >= 0, imp, -1.0))
    rank = jnp.zeros((nsel, tq), F32)
    for j in range(nsel):
        row = val[j:j + 1, :]
        rank = rank + jnp.where((row > val) | ((row == val) & (blk > j)), 1.0, 0.0)
    sel = jnp.where((rank < min(SEL_TOPK, nsel)) & (dist >= 0), 1.0, 0.0)
    sel_ref[...] = sel.T


def _cmpattn(proj, kc, vc, tq=256):
    bsz, t, _ = proj.shape
    ncp = kc.shape[2]
    nc = (t - CMP_BLOCK) // CMP_STRIDE + 1
    nsel = t // SEL_BLOCK
    gw = NSA_GROUP * HEAD_DIM
    return pl.pallas_call(
        functools.partial(_cmpattn_kernel, tq=tq, nc=nc, nsel=nsel),
        grid=(bsz, NSA_KV_HEADS, t // tq),
        in_specs=[
            pl.BlockSpec((None, tq, gw), lambda b, h, i: (b, i, COL_NSA_Q // NSA_GROUP + h)),
            pl.BlockSpec((None, None, ncp, HEAD_DIM), lambda b, h, i: (b, h, 0, 0)),
            pl.BlockSpec((None, None, ncp, HEAD_DIM), lambda b, h, i: (b, h, 0, 0)),
        ],
        out_specs=[
            pl.BlockSpec((None, tq, gw), lambda b, h, i: (b, i, h)),
            pl.BlockSpec((None, None, tq, nsel), lambda b, h, i: (b, h, i, 0)),
        ],
        out_shape=[
            jax.ShapeDtypeStruct((bsz, t, NSA_WIDTH), F32),
            jax.ShapeDtypeStruct((bsz, NSA_KV_HEADS, t, nsel), F32),
        ],
        compiler_params=_cparams(("arbitrary", "arbitrary", "arbitrary")),
        name="cmpattn",
    )(proj, kc, vc)


def _selattn_kernel(q_ref, k_ref, v_ref, sel_ref, o_ref, m_ref, l_ref, acc_ref, *, tq, tk, nsel):
    g = NSA_GROUP
    qi = pl.program_id(2)
    m_ref[...] = jnp.full_like(m_ref, NEG_BIG)
    l_ref[...] = jnp.zeros_like(l_ref)
    acc_ref[...] = jnp.zeros_like(acc_ref)
    q4 = jnp.concatenate([q_ref[:, i * HEAD_DIM:(i + 1) * HEAD_DIM] for i in range(g)], axis=0)
    q4 = (q4 * HEAD_DIM ** -0.5).astype(BF16)
    sel = sel_ref[...].astype(BF16)
    pos = qi * tq + lax.broadcasted_iota(jnp.int32, (tq, tk), 0)

    def kv_step(kv, carry):
        k0 = pl.multiple_of(kv * tk, tk)
        s = _dot_nt(q4, k_ref[pl.ds(k0, tk), :].astype(BF16))
        key = k0 + lax.broadcasted_iota(jnp.int32, (nsel, tk), 1)
        blk = lax.broadcasted_iota(jnp.int32, (nsel, tk), 0)
        expand = jnp.where(key // SEL_BLOCK == blk, 1.0, 0.0).astype(BF16)
        picked = _dot(sel, expand)
        kpos = k0 + lax.broadcasted_iota(jnp.int32, (tq, tk), 1)
        mask = ((picked > 0.5) & (kpos <= pos))[None]
        s = jnp.where(mask, s.reshape(g, tq, tk), NEG_BIG)
        m_prev = m_ref[...]
        m_new = jnp.maximum(m_prev, jnp.max(s, axis=-1, keepdims=True))
        p = jnp.exp(s - m_new)
        alpha = jnp.exp(m_prev - m_new)
        l_ref[...] = alpha * l_ref[...] + jnp.sum(p, axis=-1, keepdims=True)
        pv = _dot(p.reshape(g * tq, tk).astype(BF16), v_ref[pl.ds(k0, tk), :].astype(BF16))
        acc_ref[...] = alpha * acc_ref[...] + pv.reshape(g, tq, HEAD_DIM)
        m_ref[...] = m_new
        return carry

    lax.fori_loop(0, (qi * tq + tq - 1) // tk + 1, kv_step, 0)
    o = acc_ref[...] * (1.0 / jnp.maximum(l_ref[...], 1e-30))
    for i in range(g):
        o_ref[:, i * HEAD_DIM:(i + 1) * HEAD_DIM] = o[i]


def _selattn(proj, sel, tq=2 * Q_BLOCK, tk=512):
    bsz, t, _ = proj.shape
    nsel = t // SEL_BLOCK
    tk = min(tk, t)
    gw = NSA_GROUP * HEAD_DIM

    def kv_spec(base):
        return pl.BlockSpec((None, t, HEAD_DIM), lambda b, h, i: (b, 0, base + h))

    return pl.pallas_call(
        functools.partial(_selattn_kernel, tq=tq, tk=tk, nsel=nsel),
        grid=(bsz, NSA_KV_HEADS, t // tq),
        in_specs=[
            pl.BlockSpec((None, tq, gw), lambda b, h, i: (b, i, COL_NSA_Q // NSA_GROUP + h)),
            kv_spec(COL_KV + 2 * NSA_KV_HEADS), kv_spec(COL_KV + 3 * NSA_KV_HEADS),
            pl.BlockSpec((None, None, tq, nsel), lambda b, h, i: (b, h, i, 0)),
        ],
        out_specs=pl.BlockSpec((None, tq, gw), lambda b, h, i: (b, i, h)),
        out_shape=jax.ShapeDtypeStruct((bsz, t, NSA_WIDTH), F32),
        scratch_shapes=[pltpu.VMEM((NSA_GROUP, tq, 1), F32), pltpu.VMEM((NSA_GROUP, tq, 1), F32),
                        pltpu.VMEM((NSA_GROUP, tq, HEAD_DIM), F32)],
        compiler_params=_cparams(("arbitrary", "arbitrary", "arbitrary")),
        name="selattn",
    )(proj, proj, proj, sel)


def _winattn_kernel(*refs, tq, nblk):
    q_ref = refs[0]
    k_refs = refs[1:1 + nblk]
    v_refs = refs[1 + nblk:1 + 2 * nblk]
    ocmp_ref, osel_ref, small_ref, o_ref = refs[1 + 2 * nblk:]
    g = NSA_GROUP
    hk = pl.program_id(1)
    qi = pl.program_id(2)
    scale = HEAD_DIM ** -0.5
    nk = nblk * tq
    keys = jnp.concatenate([r[...] for r in k_refs], axis=0).astype(BF16)
    vals = jnp.concatenate([r[...] for r in v_refs], axis=0).astype(BF16)
    q4 = jnp.concatenate([q_ref[:, i * HEAD_DIM:(i + 1) * HEAD_DIM] for i in range(g)], axis=0)
    s = (_dot_nt(q4.astype(BF16), keys) * scale).reshape(g, tq, nk)
    pos = qi * tq + lax.broadcasted_iota(jnp.int32, (tq, nk), 0)
    wpos = (qi - (nblk - 1)) * tq + lax.broadcasted_iota(jnp.int32, (tq, nk), 1)
    mask = ((wpos <= pos) & (wpos > pos - WINDOW) & (wpos >= 0))[None]
    s = jnp.where(mask, s, NEG_BIG)
    e = jnp.exp(s - jnp.max(s, axis=-1, keepdims=True)).astype(BF16)
    ov = _dot(e.reshape(g * tq, nk), jnp.concatenate([vals, jnp.ones_like(vals)], axis=1))
    o_win = (ov[:, 0:HEAD_DIM] * (1.0 / jnp.maximum(ov[:, HEAD_DIM:2 * HEAD_DIM], 1e-30))).reshape(g, tq, HEAD_DIM)
    small = small_ref[...]
    for i in range(g):
        head = hk * g + i
        gates = [jax.nn.sigmoid(_lane_pick(small, SMALL_NSA_GATE + j * NSA_HEADS + head)) for j in range(3)]
        cols = slice(i * HEAD_DIM, (i + 1) * HEAD_DIM)
        o = gates[0] * ocmp_ref[:, cols] + gates[1] * osel_ref[:, cols] + gates[2] * o_win[i]
        o_ref[:, cols] = o.astype(o_ref.dtype)


def _winattn(proj, o_cmp, o_sel, tq=2 * Q_BLOCK):
    bsz, t, _ = proj.shape
    nblk = WINDOW // tq + 1
    gw = NSA_GROUP * HEAD_DIM

    def kv_spec(base, d):
        return pl.BlockSpec((None, tq, HEAD_DIM),
                            lambda b, h, i: (b, jnp.maximum(i - (nblk - 1) + d, 0), base + h))

    grp = pl.BlockSpec((None, tq, gw), lambda b, h, i: (b, i, h))
    return pl.pallas_call(
        functools.partial(_winattn_kernel, tq=tq, nblk=nblk),
        grid=(bsz, NSA_KV_HEADS, t // tq),
        in_specs=(
            [pl.BlockSpec((None, tq, gw), lambda b, h, i: (b, i, COL_NSA_Q // NSA_GROUP + h))]
            + [kv_spec(COL_KV + 4 * NSA_KV_HEADS, d) for d in range(nblk)]
            + [kv_spec(COL_KV + 5 * NSA_KV_HEADS, d) for d in range(nblk)]
            + [grp, grp, pl.BlockSpec((None, tq, HEAD_DIM), lambda b, h, i: (b, i, COL_SMALL))]
        ),
        out_specs=grp,
        out_shape=jax.ShapeDtypeStruct((bsz, t, NSA_WIDTH), BF16),
        compiler_params=_cparams(("arbitrary", "arbitrary", "arbitrary")),
        name="winattn",
    )(proj, *([proj] * (2 * nblk)), o_cmp, o_sel, proj)


def _ffn(h, wg, wu, wd):
    act = _silu(_dot(h, wg)) * _dot(h, wu)
    return _dot(act.astype(BF16), wd)


def _pack_bf16_pairs(x):
    half = x.shape[1] // 2
    lo = pltpu.bitcast(x[:, :half].astype(BF16).astype(F32), jnp.uint32)
    hi = pltpu.bitcast(x[:, half:].astype(BF16).astype(F32), jnp.uint32)
    return hi | (lo >> 16)


def _unpack_bf16_pairs(p):
    return pltpu.bitcast(p << 16, F32), pltpu.bitcast(p & jnp.uint32(0xFFFF0000), F32)


def _store_token_rows(ref, x, first=0):
    n, d = x.shape
    s_tiles = d // HEAD_DIM
    for s in range(s_tiles):
        ref[pl.ds(first * s_tiles + s, n, stride=s_tiles), :] = x[:, s * HEAD_DIM:(s + 1) * HEAD_DIM]


def _load_token_rows(ref, n, d, first=0):
    s_tiles = d // HEAD_DIM
    return jnp.concatenate([ref[pl.ds(first * s_tiles + s, n, stride=s_tiles), :] for s in range(s_tiles)], axis=1)


def _outproj_kernel(yg_ref, yn_ref, x_ref, wo_ref, gate_ref, shift_ref, scale_ref, gate_m_ref, nw_ref,
                    rw_ref, rb_ref, sg_ref, su_ref, sd_ref, x1_ref, h_ref, route_ref, count_ref, carry_ref):
    tm = x_ref.shape[0]
    half = yg_ref.shape[1]

    @pl.when((pl.program_id(0) == 0) & (pl.program_id(1) == 0))
    def _():
        carry_ref[...] = jnp.zeros_like(carry_ref)

    y = _dot(yg_ref[...], wo_ref[0:half, :]) + _dot(yn_ref[...], wo_ref[half:2 * half, :])
    x1 = x_ref[...] + gate_ref[...] * y
    h = _modulated_norm(x1, nw_ref[...], scale_ref[...], shift_ref[...])
    _store_token_rows(h_ref, _pack_bf16_pairs(h))
    x1_ref[...] = x1 + gate_m_ref[...] * _ffn(h.astype(BF16), sg_ref[...], su_ref[...], sd_ref[...])
    scores = jax.nn.sigmoid(_dot_split(h, rw_ref[...]))
    n_exp = scores.shape[1]
    lane = lax.broadcasted_iota(jnp.int32, scores.shape, 1)
    cand = scores + rb_ref[...]
    picked = jnp.zeros(scores.shape, jnp.bool_)
    firsts = []
    for _ in range(TOP_K):
        best = jnp.max(cand, axis=-1, keepdims=True)
        first = jnp.min(jnp.where(cand == best, lane, n_exp), axis=-1, keepdims=True)
        hit = lane == first
        picked = picked | hit
        cand = jnp.where(hit, NEG_BIG, cand)
        firsts.append(first)
    norm = ROUTED_SCALE / jnp.sum(jnp.where(picked, scores, 0.0), axis=-1, keepdims=True)
    onehot = jnp.where(picked, 1.0, 0.0)
    ri = lax.broadcasted_iota(jnp.int32, (tm, tm), 0)
    ci = lax.broadcasted_iota(jnp.int32, (tm, tm), 1)
    before = jnp.where(ci < ri, 1.0, 0.0).astype(BF16)
    pos = _dot(before, onehot.astype(BF16)) + carry_ref[...]
    carry_ref[...] += jnp.sum(onehot, axis=0, keepdims=True)
    count_ref[...] = carry_ref[...]
    slot = lax.broadcasted_iota(jnp.int32, (tm, 3 * TOP_K), 1)
    route = jnp.zeros((tm, 3 * TOP_K), F32)
    for j, first in enumerate(firsts):
        hit = lane == first
        route = jnp.where(slot == j, first.astype(F32), route)
        route = jnp.where(slot == TOP_K + j, _lane_pick(scores, first) * norm, route)
        route = jnp.where(slot == 2 * TOP_K + j, jnp.sum(jnp.where(hit, pos, 0.0), axis=-1, keepdims=True), route)
    route_ref[...] = route


def _outproj(y_gdn, y_nsa, x, w_out_b, mod4, norm_w, router_w, router_bias, sg, su, sd, tm=256):
    bsz, t, d = x.shape
    half = y_gdn.shape[2]
    n_exp = router_w.shape[1]
    sff = sg.shape[1]
    s_tiles = d // 2 // HEAD_DIM

    def mod(j):
        return pl.BlockSpec((None, None, 1, d), lambda b, i: (b, j, 0, 0))

    def const(shape):
        return pl.BlockSpec(shape, lambda b, i: (0, 0))

    row = pl.BlockSpec((None, tm, d), lambda b, i: (b, i, 0))
    return pl.pallas_call(
        _outproj_kernel,
        grid=(bsz, t // tm),
        in_specs=[
            pl.BlockSpec((None, tm, half), lambda b, i: (b, i, 0)),
            pl.BlockSpec((None, tm, half), lambda b, i: (b, i, 0)),
            row,
            const((2 * half, d)),
            mod(2), mod(3), mod(4), mod(5),
            const((1, d)), const((d, n_exp)), const((1, n_exp)),
            const((d, sff)), const((d, sff)), const((sff, d)),
        ],
        out_specs=[row,
                   pl.BlockSpec((tm * s_tiles, HEAD_DIM), lambda b, i: (b * (t // tm) + i, 0)),
                   pl.BlockSpec((None, tm, 3 * TOP_K), lambda b, i: (b, i, 0)),
                   const((1, n_exp))],
        out_shape=[
            jax.ShapeDtypeStruct((bsz, t, d), F32),
            jax.ShapeDtypeStruct((bsz * t * s_tiles, HEAD_DIM), jnp.uint32),
            jax.ShapeDtypeStruct((bsz, t, 3 * TOP_K), F32),
            jax.ShapeDtypeStruct((1, n_exp), F32),
        ],
        scratch_shapes=[pltpu.VMEM((1, n_exp), F32)],
        compiler_params=_cparams(("arbitrary", "arbitrary")),
        name="outproj",
    )(y_gdn, y_nsa, x, w_out_b, mod4, mod4, mod4, mod4, norm_w.reshape(1, d), router_w,
      router_bias.reshape(1, n_exp), sg, su, sd)


MOE_TM = 256
MOE_TOK = 256
MOE_TILES_PER_STEP = 4


def _token_row(ref, r, s_tiles):
    return ref.at[pl.ds(pl.multiple_of(r * s_tiles, s_tiles), s_tiles), :]


def _for_each(n, fn, unroll=8):
    def body(i, carry):
        fn(i)
        return carry

    lax.fori_loop(0, n, body, 0, unroll=unroll)


def _moe_scatter_kernel(gstart_ref, gend_ref, idx_ref, pos_ref, h_ref, xs_hbm, zero_ref, sem, *, s_tiles):
    n_exp = gstart_ref.shape[0]

    @pl.when(pl.program_id(0) == 0)
    def _():
        zero_ref[...] = jnp.zeros_like(zero_ref)

        def tail(e):
            first = pl.multiple_of((gend_ref[e] - MOE_TM) * s_tiles, MOE_TM * s_tiles)
            return pltpu.make_async_copy(zero_ref, xs_hbm.at[pl.ds(first, MOE_TM * s_tiles), :], sem)

        def start_tail(e):
            @pl.when(gend_ref[e] > gstart_ref[e])
            def _():
                tail(e).start()

        def wait_tail(e):
            @pl.when(gend_ref[e] > gstart_ref[e])
            def _():
                tail(e).wait()

        _for_each(n_exp, start_tail)
        _for_each(n_exp, wait_tail)

    def row_copy(t, j):
        n = t * TOP_K + j
        row = gstart_ref[idx_ref[n]] + pos_ref[n]
        return pltpu.make_async_copy(_token_row(h_ref, t, s_tiles), _token_row(xs_hbm, row, s_tiles), sem)

    def start_token(t):
        for j in range(TOP_K):
            row_copy(t, j).start()

    def wait_token(t):
        for j in range(TOP_K):
            row_copy(t, j).wait()

    _for_each(MOE_TOK, start_token, unroll=2)
    _for_each(MOE_TOK, wait_token, unroll=2)


def _moe_scatter(h_rows, idx, pos, gstart, gend, n_rows, s_tiles):
    n_steps = idx.shape[0] // (MOE_TOK * TOP_K)
    smem = pl.BlockSpec((MOE_TOK * TOP_K,), lambda i, gs, ge: (i,), memory_space=pltpu.SMEM)
    return pl.pallas_call(
        functools.partial(_moe_scatter_kernel, s_tiles=s_tiles),
        grid_spec=pltpu.PrefetchScalarGridSpec(
            num_scalar_prefetch=2,
            grid=(n_steps,),
            in_specs=[smem, smem, pl.BlockSpec((MOE_TOK * s_tiles, HEAD_DIM), lambda i, gs, ge: (i, 0))],
            out_specs=pl.BlockSpec(memory_space=pl.ANY),
            scratch_shapes=[pltpu.VMEM((MOE_TM * s_tiles, HEAD_DIM), h_rows.dtype), pltpu.SemaphoreType.DMA(())],
        ),
        out_shape=jax.ShapeDtypeStruct((n_rows * s_tiles, HEAD_DIM), h_rows.dtype),
        compiler_params=_cparams(("arbitrary",)),
        name="moe_scatter",
    )(gstart, gend, idx, pos, h_rows)


def _moe_expert_kernel(texp_ref, tslot_ref, tnext_ref, nvalid_ref, xs_ref, wg_hbm, wu_hbm, wd_hbm, ys_ref,
                       wg_buf, wu_buf, wd_buf, wgb_ref, wub_ref, wdb_ref, sems):
    d = wgb_ref.shape[0]
    half = d // 2
    rows_per_tile = xs_ref.shape[0] // MOE_TILES_PER_STEP

    def fetch(expert, s):
        return [pltpu.make_async_copy(src.at[expert], dst.at[s], sems.at[s, n])
                for n, (src, dst) in enumerate(((wg_hbm, wg_buf), (wu_hbm, wu_buf), (wd_hbm, wd_buf)))]

    def tile(i, sub):
        valid = i < nvalid_ref[0]
        new_expert = (i == 0) | (texp_ref[i] != texp_ref[jnp.maximum(i - 1, 0)])
        slot = tslot_ref[i]

        @pl.when(i == 0)
        def _():
            for cp in fetch(texp_ref[0], slot):
                cp.start()

        @pl.when(valid & new_expert)
        def _():
            nxt = tnext_ref[i]

            @pl.when(nxt < nvalid_ref[0])
            def _():
                for cp in fetch(texp_ref[nxt], 1 - slot):
                    cp.start()

            for cp in fetch(texp_ref[i], slot):
                cp.wait()
            wgb_ref[...] = wg_buf[slot].astype(BF16)
            wub_ref[...] = wu_buf[slot].astype(BF16)
            wdb_ref[...] = wd_buf[slot].astype(BF16)

        @pl.when(valid)
        def _():
            lo, hi = _unpack_bf16_pairs(_load_token_rows(xs_ref, MOE_TM, half, first=sub * MOE_TM))
            lo, hi = lo.astype(BF16), hi.astype(BF16)
            gate = _dot(lo, wgb_ref[0:half, :]) + _dot(hi, wgb_ref[half:d, :])
            up = _dot(lo, wub_ref[0:half, :]) + _dot(hi, wub_ref[half:d, :])
            y = _dot((_silu(gate) * up).astype(BF16), wdb_ref[...])
            _store_token_rows(ys_ref, _pack_bf16_pairs(y), first=sub * MOE_TM)

        @pl.when(jnp.logical_not(valid))
        def _():
            ys_ref[sub * rows_per_tile:(sub + 1) * rows_per_tile, :] = jnp.zeros((rows_per_tile, HEAD_DIM), ys_ref.dtype)

    for sub in range(MOE_TILES_PER_STEP):
        tile(pl.program_id(0) * MOE_TILES_PER_STEP + sub, sub)


def _moe_experts(xs, tile_expert, tile_slot, tile_next, nvalid, wg, wu, wd):
    n_exp, d, ff = wg.shape
    n_steps = tile_expert.shape[0] // MOE_TILES_PER_STEP
    blk = xs.shape[0] // n_steps
    hbm = pl.BlockSpec(memory_space=pl.ANY)
    vmem_bytes = 3 * d * ff * (2 * 4 + 2) + 4 * blk * HEAD_DIM * 4 + 3 * MOE_TM * d * 4
    return pl.pallas_call(
        _moe_expert_kernel,
        grid_spec=pltpu.PrefetchScalarGridSpec(
            num_scalar_prefetch=4,
            grid=(n_steps,),
            in_specs=[
                pl.BlockSpec((blk, HEAD_DIM),
                             lambda i, te, ts, tn, nv: (jnp.minimum(i, (nv[0] - 1) // MOE_TILES_PER_STEP), 0)),
                hbm, hbm, hbm,
            ],
            out_specs=pl.BlockSpec((blk, HEAD_DIM), lambda i, te, ts, tn, nv: (i, 0)),
            scratch_shapes=[pltpu.VMEM((2, d, ff), wg.dtype), pltpu.VMEM((2, d, ff), wu.dtype),
                            pltpu.VMEM((2, ff, d), wd.dtype),
                            pltpu.VMEM((d, ff), BF16), pltpu.VMEM((d, ff), BF16), pltpu.VMEM((ff, d), BF16),
                            pltpu.SemaphoreType.DMA((2, 3))],
        ),
        out_shape=jax.ShapeDtypeStruct(xs.shape, xs.dtype),
        compiler_params=_cparams(("arbitrary",), vmem_bytes),
        name="moe_experts",
    )(tile_expert, tile_slot, tile_next, nvalid, xs, wg, wu, wd)


def _moe_combine_kernel(gstart_ref, idx_ref, pos_ref, w_ref, ys_hbm, x1_ref, gate_ref, nw_ref, o_ref,
                        buf_ref, lo_ref, hi_ref, sems, *, s_tiles):
    tm, d = x1_ref.shape
    half_rows = tm // 2 * TOP_K

    def row_copy(n, sem):
        row = gstart_ref[idx_ref[n]] + pos_ref[n]
        return pltpu.make_async_copy(_token_row(ys_hbm, row, s_tiles), _token_row(buf_ref, n, s_tiles), sem)

    def token(t):
        n0 = t * TOP_K
        lo, hi = _unpack_bf16_pairs(_token_row(buf_ref, n0, s_tiles)[...])
        acc_lo, acc_hi = w_ref[n0] * lo, w_ref[n0] * hi
        for j in range(1, TOP_K):
            lo, hi = _unpack_bf16_pairs(_token_row(buf_ref, n0 + j, s_tiles)[...])
            acc_lo, acc_hi = acc_lo + w_ref[n0 + j] * lo, acc_hi + w_ref[n0 + j] * hi
        _token_row(lo_ref, t, s_tiles)[...] = acc_lo
        _token_row(hi_ref, t, s_tiles)[...] = acc_hi

    for part in range(2):
        _for_each(half_rows, lambda n, part=part: row_copy(part * half_rows + n, sems.at[part]).start())
    for part in range(2):
        _for_each(half_rows, lambda n, part=part: row_copy(part * half_rows + n, sems.at[part]).wait())
        _for_each(tm // 2, lambda t, part=part: token(part * (tm // 2) + t))
    routed = jnp.concatenate([_load_token_rows(lo_ref, tm, d // 2), _load_token_rows(hi_ref, tm, d // 2)], axis=1)
    x2 = x1_ref[...] + gate_ref[...] * routed
    o_ref[...] = x2 * lax.rsqrt(jnp.mean(x2 * x2, axis=-1, keepdims=True) + EPS) * nw_ref[...]


def _moe_combine(ys, idx, pos, w, gstart, x1, mod4, norm_w):
    bsz, t, d = x1.shape
    s_tiles = d // 2 // HEAD_DIM
    tm = MOE_TOK
    nt = t // tm
    smem = pl.BlockSpec((tm * TOP_K,), lambda b, i, gs: (b * nt + i,), memory_space=pltpu.SMEM)
    row = pl.BlockSpec((None, tm, d), lambda b, i, gs: (b, i, 0))
    return pl.pallas_call(
        functools.partial(_moe_combine_kernel, s_tiles=s_tiles),
        grid_spec=pltpu.PrefetchScalarGridSpec(
            num_scalar_prefetch=1,
            grid=(bsz, nt),
            in_specs=[
                smem, smem, smem,
                pl.BlockSpec(memory_space=pl.ANY),
                row,
                pl.BlockSpec((None, None, 1, d), lambda b, i, gs: (b, 5, 0, 0)),
                pl.BlockSpec((1, d), lambda b, i, gs: (0, 0)),
            ],
            out_specs=row,
            scratch_shapes=[pltpu.VMEM((tm * TOP_K * s_tiles, HEAD_DIM), ys.dtype),
                            pltpu.VMEM((tm * s_tiles, HEAD_DIM), F32), pltpu.VMEM((tm * s_tiles, HEAD_DIM), F32),
                            pltpu.SemaphoreType.DMA((2,))],
        ),
        out_shape=jax.ShapeDtypeStruct((bsz, t, d), F32),
        compiler_params=_cparams(("arbitrary", "arbitrary")),
        name="moe_combine",
    )(gstart, idx, pos, w, ys, x1, mod4, norm_w.reshape(1, d))


def _moe(h_rows, route, counts, x1, mod4, norm_w, wg, wu, wd):
    bsz, t, d = x1.shape
    n_exp = wg.shape[0]
    s_tiles = h_rows.shape[0] // (bsz * t)
    idx = route[..., 0:TOP_K].astype(jnp.int32).reshape(-1)
    w = route[..., TOP_K:2 * TOP_K].reshape(-1)
    pos = route[..., 2 * TOP_K:3 * TOP_K].astype(jnp.int32).reshape(-1)
    padded = (counts.reshape(n_exp).astype(jnp.int32) + MOE_TM - 1) // MOE_TM * MOE_TM
    gend = jnp.cumsum(padded)
    gstart = gend - padded
    nvalid = gend[-1:] // MOE_TM
    n_tiles = bsz * t * TOP_K // MOE_TM + n_exp
    tile_start = jnp.minimum(jnp.arange(n_tiles, dtype=jnp.int32), nvalid - 1) * MOE_TM
    tile_expert = jnp.sum((gend[None, :] <= tile_start[:, None]).astype(jnp.int32), axis=1)
    first_of_group = jnp.concatenate([jnp.ones((1,), jnp.int32),
                                      (tile_expert[1:] != tile_expert[:-1]).astype(jnp.int32)])
    tile_slot = (jnp.cumsum(first_of_group) - 1) % 2
    experts = jnp.arange(n_exp, dtype=jnp.int32)
    tile_next = jnp.sum(jnp.where(tile_expert[:, None] == experts[None, :], gend[None, :], 0), axis=1) // MOE_TM
    xs = _moe_scatter(h_rows, idx, pos, gstart, gend, n_tiles * MOE_TM, s_tiles)
    ys = _moe_experts(xs, tile_expert, tile_slot, tile_next, nvalid, wg, wu, wd)
    return _moe_combine(ys, idx, pos, w, gstart, x1, mod4, norm_w)


def _reorder_w_in(w_in):
    d = w_in.shape[0]
    o_beta = 3 * GDN_WIDTH
    o_a = o_beta + GDN_HEADS
    o_gate = o_a + GDN_HEADS
    o_q = o_gate + GDN_WIDTH
    o_kv = o_q + NSA_WIDTH
    o_ng = o_kv + 6 * NSA_KV_WIDTH
    n_small = 2 * GDN_HEADS + 3 * NSA_HEADS
    parts = [w_in[:, :o_beta], w_in[:, o_gate:o_q], w_in[:, o_q:o_kv], w_in[:, o_kv:o_ng],
             w_in[:, o_beta:o_gate], w_in[:, o_ng:o_ng + 3 * NSA_HEADS]]
    used = COL_SMALL * HEAD_DIM + n_small
    parts.append(jnp.zeros((d, PROJ_COLS - used), w_in.dtype))
    return jnp.concatenate([p.astype(BF16) for p in parts], axis=1)


def _layer(x, mod, norm_attn_w, norm_ffn_w, w_in, gdn_conv_w, gdn_a_log, gdn_dt_bias, gdn_norm_w,
           cmp_pe_k, cmp_w1_k, cmp_w2_k, cmp_pe_v, cmp_w1_v, cmp_w2_v, w_out, router_w, router_bias,
           expert_w_gate, expert_w_up, expert_w_down, shared_w_gate, shared_w_up, shared_w_down, final_w):
    bsz, t, d = x.shape
    mod4 = mod.reshape(bsz, N_MOD, 1, d)
    proj = _proj(x, mod4, norm_attn_w, _reorder_w_in(w_in))
    y_gdn = _gdn(proj, gdn_conv_w, gdn_a_log, gdn_dt_bias, gdn_norm_w)
    kc = _compress(proj, COL_KV, cmp_pe_k, cmp_w1_k, cmp_w2_k)
    vc = _compress(proj, COL_KV + NSA_KV_HEADS, cmp_pe_v, cmp_w1_v, cmp_w2_v)
    o_cmp, sel = _cmpattn(proj, kc, vc)
    o_sel = _selattn(proj, sel)
    y_nsa = _winattn(proj, o_cmp, o_sel)
    x1, h_rows, route, counts = _outproj(
        y_gdn, y_nsa, x, w_out.astype(BF16), mod4, norm_ffn_w, router_w, router_bias,
        shared_w_gate.astype(BF16), shared_w_up.astype(BF16), shared_w_down.astype(BF16))
    return _moe(h_rows, route, counts, x1, mod4, final_w, expert_w_gate, expert_w_up, expert_w_down)


def kernel(x, c, w_ada, b_ada, norm_attn_w, norm_ffn_w, norm_final_w, w_in, gdn_conv_w, gdn_a_log, gdn_dt_bias,
           gdn_norm_w, cmp_pe_k, cmp_w1_k, cmp_w2_k, cmp_pe_v, cmp_w1_v, cmp_w2_v, w_out, router_w, router_bias,
           expert_w_gate, expert_w_up, expert_w_down, shared_w_gate, shared_w_up, shared_w_down):
    depth = w_ada.shape[0]
    assert depth == 1, "the fused final norm assumes a single layer"
    l = 0
    mod = _ada(c, w_ada[l], b_ada[l])
    return _layer(x, mod, norm_attn_w[l], norm_ffn_w[l], w_in[l], gdn_conv_w[l], gdn_a_log[l], gdn_dt_bias[l],
                  gdn_norm_w[l], cmp_pe_k[l], cmp_w1_k[l], cmp_w2_k[l], cmp_pe_v[l], cmp_w1_v[l], cmp_w2_v[l],
                  w_out[l], router_w[l], router_bias[l], expert_w_gate[l], expert_w_up[l], expert_w_down[l],
                  shared_w_gate[l], shared_w_up[l], shared_w_down[l], norm_final_w)
```

```python
import functools

import jax
import jax.numpy as jnp
from jax import lax
from jax.experimental import pallas as pl
from jax.experimental.pallas import tpu as pltpu

F32 = jnp.float32
BF16 = jnp.bfloat16
HI = lax.Precision.HIGHEST

HEAD_DIM = 128
GDN_HEADS = 8
NSA_HEADS = 8
NSA_KV_HEADS = 2
NSA_GROUP = NSA_HEADS // NSA_KV_HEADS
GDN_WIDTH = GDN_HEADS * HEAD_DIM
NSA_WIDTH = NSA_HEADS * HEAD_DIM
NSA_KV_WIDTH = NSA_KV_HEADS * HEAD_DIM
CONV_WIDTH = 4
GDN_CHUNK = 64
CMP_BLOCK = 32
CMP_STRIDE = 16
SEL_BLOCK = 64
SEL_TOPK = 16
N_LOCAL_BLOCKS = 2
WINDOW = 512
Q_BLOCK = 128
TOP_K = 8
ROUTED_SCALE = 2.5
N_MOD = 6
EPS = 1e-6

COL_GDN_Q = 0
COL_GDN_K = GDN_HEADS
COL_GDN_V = 2 * GDN_HEADS
COL_GDN_GATE = 3 * GDN_HEADS
COL_NSA_Q = 4 * GDN_HEADS
COL_KV = COL_NSA_Q + NSA_HEADS
COL_SMALL = COL_KV + 6 * NSA_KV_HEADS
N_COL_BLOCKS = COL_SMALL + 1
PROJ_TN = 2304
PROJ_COLS = 3 * PROJ_TN
SMALL_BETA = 0
SMALL_A = GDN_HEADS
SMALL_NSA_GATE = 2 * GDN_HEADS

NEG_BIG = -1e30
VMEM_LIMIT = 48 * 1024 * 1024


def _cparams(semantics, vmem_limit=VMEM_LIMIT):
    return pltpu.CompilerParams(dimension_semantics=semantics, vmem_limit_bytes=vmem_limit)


def _silu(x):
    return (0.5 * x) * (1.0 + jnp.tanh(0.5 * x))


def _dot(a, b, precision=None):
    return jnp.dot(a, b, preferred_element_type=F32, precision=precision)


def _dot_nt(a, b, precision=None):
    return lax.dot_general(a, b, (((1,), (1,)), ((), ())), preferred_element_type=F32, precision=precision)


def _dot_tn(a, b, precision=None):
    return lax.dot_general(a, b, (((0,), (0,)), ((), ())), preferred_element_type=F32, precision=precision)


def _split_bf16(x):
    hi = x.astype(BF16)
    return hi, (x - hi.astype(F32)).astype(BF16)


def _dot_split(a, b, nt=False):
    (ah, al), (bh, bl) = _split_bf16(a), _split_bf16(b)
    f = _dot_nt if nt else _dot
    return f(ah, bh) + (f(ah, bl) + f(al, bh))


def _lane_pick(x, idx):
    lane = lax.broadcasted_iota(jnp.int32, x.shape, 1)
    return jnp.sum(jnp.where(lane == idx, x, 0.0), axis=-1, keepdims=True)


def _ada_kernel(ct_ref, w_ref, b_ref, o_ref):
    c = ct_ref[...]
    cond = _silu(c)
    w = w_ref[...]
    rows = [jnp.sum(w * cond[:, b:b + 1], axis=0, keepdims=True) for b in range(c.shape[1])]
    o_ref[...] = jnp.concatenate(rows, axis=0) + b_ref[...]


def _ada(c, w_ada, b_ada, tn=512):
    bsz, d = c.shape
    n = w_ada.shape[1]
    return pl.pallas_call(
        _ada_kernel,
        grid=(n // tn,),
        in_specs=[
            pl.BlockSpec((d, bsz), lambda j: (0, 0)),
            pl.BlockSpec((d, tn), lambda j: (0, j)),
            pl.BlockSpec((1, tn), lambda j: (0, j)),
        ],
        out_specs=pl.BlockSpec((bsz, tn), lambda j: (0, j)),
        out_shape=jax.ShapeDtypeStruct((bsz, n), F32),
        compiler_params=_cparams(("arbitrary",)),
        name="ada",
    )(c.T, w_ada, b_ada.reshape(1, n))


def _modulated_norm(x, nw, scale, shift):
    var = jnp.mean(x * x, axis=-1, keepdims=True)
    h = x * lax.rsqrt(var + EPS) * nw
    return h * (1.0 + scale) + shift


def _proj_kernel(x_ref, shift_ref, scale_ref, nw_ref, w_ref, o_ref):
    h = _modulated_norm(x_ref[...], nw_ref[...], scale_ref[...], shift_ref[...])
    o_ref[...] = _dot(h.astype(BF16), w_ref[...])


def _proj(x, mod4, norm_w, w_r, tm=512):
    bsz, t, d = x.shape
    n = w_r.shape[1]
    return pl.pallas_call(
        _proj_kernel,
        grid=(n // PROJ_TN, bsz, t // tm),
        in_specs=[
            pl.BlockSpec((None, tm, d), lambda j, b, i: (b, i, 0)),
            pl.BlockSpec((None, None, 1, d), lambda j, b, i: (b, 0, 0, 0)),
            pl.BlockSpec((None, None, 1, d), lambda j, b, i: (b, 1, 0, 0)),
            pl.BlockSpec((1, d), lambda j, b, i: (0, 0)),
            pl.BlockSpec((d, PROJ_TN), lambda j, b, i: (0, j)),
        ],
        out_specs=pl.BlockSpec((None, tm, PROJ_TN), lambda j, b, i: (b, i, j)),
        out_shape=jax.ShapeDtypeStruct((bsz, t, n), F32),
        compiler_params=_cparams(("arbitrary", "arbitrary", "arbitrary")),
        name="proj",
    )(x, mod4, mod4, norm_w.reshape(1, d), w_r)


def _gdn_kernel(q_ref, k_ref, v_ref, gate_ref, small_ref, hq_ref, hk_ref, hv_ref,
                cwq_ref, cwk_ref, cwv_ref, alog_ref, dtb_ref, nw_ref, o_ref, s_ref, xc_ref, *, tt, hg):
    first = pl.program_id(2) == 0

    @pl.when(first)
    def _():
        s_ref[...] = jnp.zeros_like(s_ref)

    def conv_silu(x_ref, halo_ref, cw_ref):
        xc_ref[0:8, :] = jnp.where(first, 0.0, halo_ref[...])
        xc_ref[8:tt + 8, :] = x_ref[...]
        cw = cw_ref[...]
        off = 8 - (CONV_WIDTH - 1)
        y = cw[0:1, :] * xc_ref[off:off + tt, :]
        for i in range(1, CONV_WIDTH):
            y = y + cw[i:i + 1, :] * xc_ref[off + i:off + i + tt, :]
        return _silu(y)

    q_all = conv_silu(q_ref, hq_ref, cwq_ref)
    k_all = conv_silu(k_ref, hk_ref, cwk_ref)
    v_all = conv_silu(v_ref, hv_ref, cwv_ref)
    ri = lax.broadcasted_iota(jnp.int32, (tt, tt), 0)
    ci = lax.broadcasted_iota(jnp.int32, (tt, tt), 1)
    lower = jnp.where(((ri // GDN_CHUNK) == (ci // GDN_CHUNK)) & (ci <= ri), 1.0, 0.0).astype(BF16)
    heads = [pl.program_id(1) * hg + hh for hh in range(hg)]
    outs = _gdn_heads(q_all, k_all, v_all, small_ref[...], lower, heads, alog_ref[...], dtb_ref[...], s_ref, tt)
    for hh, o in enumerate(outs):
        cols = slice(hh * HEAD_DIM, (hh + 1) * HEAD_DIM)
        o = o * lax.rsqrt(jnp.mean(o * o, axis=-1, keepdims=True) + EPS) * nw_ref[...]
        o_ref[:, cols] = (o * _silu(gate_ref[:, cols])).astype(o_ref.dtype)


def _gdn_heads(q_all, k_all, v_all, small, lower, heads, alog, dtb, s_ref, tt):
    c = GDN_CHUNK
    nch = tt // c
    hg = len(heads)
    kpos = lax.broadcasted_iota(jnp.int32, (tt, c), 0) % c
    jj = lax.broadcasted_iota(jnp.int32, (tt, c), 1)
    per_head = []
    for hh, h in enumerate(heads):
        cols = slice(hh * HEAD_DIM, (hh + 1) * HEAD_DIM)
        q, k, v = q_all[:, cols], k_all[:, cols], v_all[:, cols]
        q = q * lax.rsqrt(jnp.sum(q * q, axis=-1, keepdims=True) + EPS) * (HEAD_DIM ** -0.5)
        k = k * lax.rsqrt(jnp.sum(k * k, axis=-1, keepdims=True) + EPS)
        beta = jax.nn.sigmoid(_lane_pick(small, SMALL_BETA + h))
        z = _lane_pick(small, SMALL_A + h) + _lane_pick(dtb, h)
        softplus = jnp.maximum(z, 0.0) + jnp.log(1.0 + jnp.exp(-jnp.abs(z)))
        g = -jnp.exp(_lane_pick(alog, h)) * softplus
        g_c = jnp.broadcast_to(g, (tt, c))
        rhs = jnp.concatenate([g_c, jnp.where(kpos > jj, g_c, 0.0)], axis=1)
        r_hi = rhs.astype(BF16)
        rem = rhs - r_hi.astype(F32)
        r_mid = rem.astype(BF16)
        r_lo = (rem - r_mid.astype(F32)).astype(BF16)
        sums = _dot(lower, r_hi) + (_dot(lower, r_mid) + _dot(lower, r_lo))
        gcum = jnp.broadcast_to(sums[:, 0:1], (tt, HEAD_DIM))
        gc3 = gcum.reshape(nch, c, HEAD_DIM)
        grest = (gc3[:, c - 1:c, :] - gc3).reshape(tt, HEAD_DIM)
        decay_h = jnp.where(kpos >= jj, jnp.exp(sums[:, c:2 * c]), 0.0)
        per_head.append((q, k, v * beta, k * beta, jnp.exp(gcum), jnp.exp(grest), decay_h))

    def stack(j):
        x = jnp.concatenate([ph[j] for ph in per_head], axis=0)
        return x.reshape(hg * nch, c, x.shape[-1])

    q, k, vb, kb, egc, egr, decay = (stack(j) for j in range(7))
    kpos_all = lax.broadcasted_iota(jnp.int32, (hg * tt, c), 0) % c
    jj_all = lax.broadcasted_iota(jnp.int32, (hg * tt, c), 1)
    strict = (kpos_all > jj_all).reshape(hg * nch, c, c)
    eye = jnp.where(kpos_all == jj_all, 1.0, 0.0).reshape(hg * nch, c, c)

    def bmm(a, b):
        return jnp.einsum('nij,njk->nik', a, b, preferred_element_type=F32)

    def bmm_nt(a, b):
        return jnp.einsum('nid,njd->nij', a, b, preferred_element_type=F32)

    def bmm_split(a, b):
        (ah, al), (bh, bl) = _split_bf16(a), _split_bf16(b)
        return bmm(ah, bh) + (bmm(ah, bl) + bmm(al, bh))

    k3 = k.astype(BF16)
    a = jnp.where(strict, bmm_nt(kb.astype(BF16), k3) * decay, 0.0)
    inv = eye - a
    p = bmm_split(a, a)
    n_sq = (c - 1).bit_length() - 1
    for lvl in range(n_sq):
        inv = inv + bmm_split(inv, p)
        if lvl + 1 < n_sq:
            p = bmm_split(p, p)
    inv_b = inv.astype(BF16)
    u3 = bmm(inv_b, vb.astype(BF16)).astype(BF16)
    w3 = bmm(inv_b, (kb * egc).astype(BF16)).astype(BF16)
    qk3 = (bmm_nt(q.astype(BF16), k3) * decay).astype(BF16)
    kd3 = (k * egr).astype(BF16)
    qe3 = (q * egc - bmm(qk3, w3)).astype(BF16)
    o03 = bmm(qk3, u3)
    kw = [_dot_tn(kd3[i], w3[i]).astype(BF16) for i in range(hg * nch)]
    ku = [_dot_tn(kd3[i], u3[i]) for i in range(hg * nch)]

    states = [s_ref[hh] for hh in range(hg)]
    outs = [[] for _ in range(hg)]
    for n in range(nch):
        for hh in range(hg):
            i = hh * nch + n
            state_b = states[hh].astype(BF16)
            outs[hh].append(_dot(qe3[i], state_b) + o03[i])
            states[hh] = states[hh] * egc[i, c - 1:c, :] - _dot(kw[i], state_b) + ku[i]
    for hh in range(hg):
        s_ref[hh] = states[hh]
    return [jnp.concatenate(o, axis=0) for o in outs]


def _gdn(proj, conv_w, a_log, dt_bias, norm_w, tt=128, hg=8):
    bsz, t, _ = proj.shape
    wdt = hg * HEAD_DIM

    def col(base):
        return pl.BlockSpec((None, tt, wdt), lambda b, h, i: (b, i, base // hg + h))

    def halo(base):
        return pl.BlockSpec((None, 8, wdt), lambda b, h, i: (b, jnp.maximum(i * (tt // 8) - 1, 0), base // hg + h))

    def cw(base):
        return pl.BlockSpec((CONV_WIDTH, wdt), lambda b, h, i: (0, base // hg + h))

    heads = pl.BlockSpec((1, GDN_HEADS), lambda b, h, i: (0, 0))
    return pl.pallas_call(
        functools.partial(_gdn_kernel, tt=tt, hg=hg),
        grid=(bsz, GDN_HEADS // hg, t // tt),
        in_specs=[
            col(COL_GDN_Q), col(COL_GDN_K), col(COL_GDN_V), col(COL_GDN_GATE),
            pl.BlockSpec((None, tt, HEAD_DIM), lambda b, h, i: (b, i, COL_SMALL)),
            halo(COL_GDN_Q), halo(COL_GDN_K), halo(COL_GDN_V),
            cw(COL_GDN_Q), cw(COL_GDN_K), cw(COL_GDN_V),
            heads, heads,
            pl.BlockSpec((1, HEAD_DIM), lambda b, h, i: (0, 0)),
        ],
        out_specs=pl.BlockSpec((None, tt, wdt), lambda b, h, i: (b, i, h)),
        out_shape=jax.ShapeDtypeStruct((bsz, t, GDN_WIDTH), BF16),
        scratch_shapes=[pltpu.VMEM((hg, HEAD_DIM, HEAD_DIM), F32), pltpu.VMEM((tt + 8, wdt), F32)],
        compiler_params=_cparams(("arbitrary", "arbitrary", "arbitrary")),
        name="gdn",
    )(proj, proj, proj, proj, proj, proj, proj, proj, conv_w, conv_w, conv_w,
      a_log.reshape(1, GDN_HEADS), dt_bias.reshape(1, GDN_HEADS), norm_w.reshape(1, HEAD_DIM))


def _compress_kernel(x_ref, pe_ref, w1_ref, w2_ref, o_ref):
    r = x_ref.shape[0] // CMP_STRIDE
    d = x_ref.shape[1]
    a = b = None
    for l in range(CMP_STRIDE):
        x_l = x_ref[pl.ds(l, r, stride=CMP_STRIDE), :]
        a_l = _dot_split(x_l + pe_ref[l:l + 1, :], w1_ref[l * d:(l + 1) * d, :])
        b_l = _dot_split(x_l + pe_ref[CMP_STRIDE + l:CMP_STRIDE + l + 1, :],
                         w1_ref[(CMP_STRIDE + l) * d:(CMP_STRIDE + l + 1) * d, :])
        a, b = (a_l, b_l) if a is None else (a + a_l, b + b_l)
    hid = _silu(a + pltpu.roll(b, r - 1, 0))
    o_ref[...] = _dot_split(hid, w2_ref[...])


def _compress(proj, col_block, pe, w1, w2):
    bsz, t, _ = proj.shape
    r = t // CMP_STRIDE
    hid = w1.shape[1]
    return pl.pallas_call(
        _compress_kernel,
        grid=(bsz, NSA_KV_HEADS),
        in_specs=[
            pl.BlockSpec((None, t, HEAD_DIM), lambda b, h: (b, 0, col_block + h)),
            pl.BlockSpec((CMP_BLOCK, HEAD_DIM), lambda b, h: (0, 0)),
            pl.BlockSpec((CMP_BLOCK * HEAD_DIM, hid), lambda b, h: (0, 0)),
            pl.BlockSpec((hid, HEAD_DIM), lambda b, h: (0, 0)),
        ],
        out_specs=pl.BlockSpec((None, None, r, HEAD_DIM), lambda b, h: (b, h, 0, 0)),
        out_shape=jax.ShapeDtypeStruct((bsz, NSA_KV_HEADS, r, HEAD_DIM), F32),
        compiler_params=_cparams(("arbitrary", "arbitrary")),
        name="compress",
    )(proj, pe, w1, w2)


def _cmpattn_kernel(q_ref, kc_ref, vc_ref, o_ref, sel_ref, *, tq, nc, nsel):
    ncp = kc_ref.shape[0]
    scale = HEAD_DIM ** -0.5
    t0 = pl.program_id(2) * tq
    kc = kc_ref[...]
    vc = vc_ref[...]
    pos_r = t0 + lax.broadcasted_iota(jnp.int32, (tq, ncp), 0)
    c_r = lax.broadcasted_iota(jnp.int32, (tq, ncp), 1)
    valid_r = (c_r * CMP_STRIDE + CMP_BLOCK - 1 <= pos_r) & (c_r < nc)
    pos_c = t0 + lax.broadcasted_iota(jnp.int32, (ncp, tq), 1)
    c_c = lax.broadcasted_iota(jnp.int32, (ncp, tq), 0)
    valid_c = (c_c * CMP_STRIDE + CMP_BLOCK - 1 <= pos_c) & (c_c < nc)
    psum = jnp.zeros((ncp, tq), F32)
    kc_hi, kc_lo = _split_bf16(kc)
    vc_b = vc.astype(BF16)
    for g in range(NSA_GROUP):
        q_hi, q_lo = _split_bf16(q_ref[:, g * HEAD_DIM:(g + 1) * HEAD_DIM])
        s = _dot_nt(q_hi, kc_hi) + (_dot_nt(q_hi, kc_lo) + _dot_nt(q_lo, kc_hi))
        s = jnp.where(valid_r, s * scale, NEG_BIG)
        e = jnp.where(valid_r, jnp.exp(s - jnp.max(s, axis=-1, keepdims=True)), 0.0)
        p = e * (1.0 / jnp.maximum(jnp.sum(e, axis=-1, keepdims=True), 1e-30))
        o_ref[:, g * HEAD_DIM:(g + 1) * HEAD_DIM] = _dot(p.astype(BF16), vc_b)
        st = _dot_nt(kc_hi, q_hi) + (_dot_nt(kc_lo, q_hi) + _dot_nt(kc_hi, q_lo))
        st = jnp.where(valid_c, st * scale, NEG_BIG)
        et = jnp.where(valid_c, jnp.exp(st - jnp.max(st, axis=0, keepdims=True)), 0.0)
        psum = psum + et * (1.0 / jnp.maximum(jnp.sum(et, axis=0, keepdims=True), 1e-30))
    j_o = lax.broadcasted_iota(jnp.int32, (nsel, ncp), 0) * SEL_BLOCK
    c_o = lax.broadcasted_iota(jnp.int32, (nsel, ncp), 1) * CMP_STRIDE
    overlap = jnp.where((c_o < j_o + SEL_BLOCK) & (c_o + CMP_BLOCK > j_o), 1.0, 0.0).astype(BF16)
    p_hi, p_lo = _split_bf16(psum)
    imp = _dot(overlap, p_hi) + _dot(overlap, p_lo)
    blk = lax.broadcasted_iota(jnp.int32, (nsel, tq), 0)
    dist = (t0 + lax.broadcasted_iota(jnp.int32, (nsel, tq), 1)) // SEL_BLOCK - blk
    forced = (blk == 0) | ((dist >= 0) & (dist < N_LOCAL_BLOCKS))
    val = jnp.where(forced, -NEG_BIG, jnp.where(dist >= 0, imp, -1.0))
    rank = jnp.zeros((nsel, tq), F32)
    for j in range(nsel):
        row = val[j:j + 1, :]
        rank = rank + jnp.where((row > val) | ((row == val) & (blk > j)), 1.0, 0.0)
    sel = jnp.where((rank < min(SEL_TOPK, nsel)) & (dist >= 0), 1.0, 0.0)
    sel_ref[...] = sel.T


def _cmpattn(proj, kc, vc, tq=512):
    bsz, t, _ = proj.shape
    ncp = kc.shape[2]
    nc = (t - CMP_BLOCK) // CMP_STRIDE + 1
    nsel = t // SEL_BLOCK
    gw = NSA_GROUP * HEAD_DIM
    return pl.pallas_call(
        functools.partial(_cmpattn_kernel, tq=tq, nc=nc, nsel=nsel),
        grid=(bsz, NSA_KV_HEADS, t // tq),
        in_specs=[
            pl.BlockSpec((None, tq, gw), lambda b, h, i: (b, i, COL_NSA_Q // NSA_GROUP + h)),
            pl.BlockSpec((None, None, ncp, HEAD_DIM), lambda b, h, i: (b, h, 0, 0)),
            pl.BlockSpec((None, None, ncp, HEAD_DIM), lambda b, h, i: (b, h, 0, 0)),
        ],
        out_specs=[
            pl.BlockSpec((None, tq, gw), lambda b, h, i: (b, i, h)),
            pl.BlockSpec((None, None, tq, nsel), lambda b, h, i: (b, h, i, 0)),
        ],
        out_shape=[
            jax.ShapeDtypeStruct((bsz, t, NSA_WIDTH), F32),
            jax.ShapeDtypeStruct((bsz, NSA_KV_HEADS, t, nsel), F32),
        ],
        compiler_params=_cparams(("arbitrary", "arbitrary", "arbitrary")),
        name="cmpattn",
    )(proj, kc, vc)


def _selattn_kernel(q_ref, k_ref, v_ref, sel_ref, o_ref, m_ref, l_ref, acc_ref, *, tq, tk, nsel):
    g = NSA_GROUP
    qi = pl.program_id(2)
    m_ref[...] = jnp.full_like(m_ref, NEG_BIG)
    l_ref[...] = jnp.zeros_like(l_ref)
    acc_ref[...] = jnp.zeros_like(acc_ref)
    q4 = jnp.concatenate([q_ref[:, i * HEAD_DIM:(i + 1) * HEAD_DIM] for i in range(g)], axis=0)
    q4 = (q4 * HEAD_DIM ** -0.5).astype(BF16)
    sel = sel_ref[...].astype(BF16)
    pos = qi * tq + lax.broadcasted_iota(jnp.int32, (tq, tk), 0)

    def kv_step(kv, carry):
        k0 = pl.multiple_of(kv * tk, tk)
        s = _dot_nt(q4, k_ref[pl.ds(k0, tk), :].astype(BF16))
        key = k0 + lax.broadcasted_iota(jnp.int32, (nsel, tk), 1)
        blk = lax.broadcasted_iota(jnp.int32, (nsel, tk), 0)
        expand = jnp.where(key // SEL_BLOCK == blk, 1.0, 0.0).astype(BF16)
        picked = _dot(sel, expand)
        kpos = k0 + lax.broadcasted_iota(jnp.int32, (tq, tk), 1)
        mask = ((picked > 0.5) & (kpos <= pos))[None]
        s = jnp.where(mask, s.reshape(g, tq, tk), NEG_BIG)
        m_prev = m_ref[...]
        m_new = jnp.maximum(m_prev, jnp.max(s, axis=-1, keepdims=True))
        p = jnp.exp(s - m_new)
        alpha = jnp.exp(m_prev - m_new)
        l_ref[...] = alpha * l_ref[...] + jnp.sum(p, axis=-1, keepdims=True)
        pv = _dot(p.reshape(g * tq, tk).astype(BF16), v_ref[pl.ds(k0, tk), :].astype(BF16))
        acc_ref[...] = alpha * acc_ref[...] + pv.reshape(g, tq, HEAD_DIM)
        m_ref[...] = m_new
        return carry

    lax.fori_loop(0, (qi * tq + tq - 1) // tk + 1, kv_step, 0)
    o = acc_ref[...] * (1.0 / jnp.maximum(l_ref[...], 1e-30))
    for i in range(g):
        o_ref[:, i * HEAD_DIM:(i + 1) * HEAD_DIM] = o[i]


def _selattn(proj, sel, tq=2 * Q_BLOCK, tk=512):
    bsz, t, _ = proj.shape
    nsel = t // SEL_BLOCK
    tk = min(tk, t)
    gw = NSA_GROUP * HEAD_DIM

    def kv_spec(base):
        return pl.BlockSpec((None, t, HEAD_DIM), lambda b, h, i: (b, 0, base + h))

    return pl.pallas_call(
        functools.partial(_selattn_kernel, tq=tq, tk=tk, nsel=nsel),
        grid=(bsz, NSA_KV_HEADS, t // tq),
        in_specs=[
            pl.BlockSpec((None, tq, gw), lambda b, h, i: (b, i, COL_NSA_Q // NSA_GROUP + h)),
            kv_spec(COL_KV + 2 * NSA_KV_HEADS), kv_spec(COL_KV + 3 * NSA_KV_HEADS),
            pl.BlockSpec((None, None, tq, nsel), lambda b, h, i: (b, h, i, 0)),
        ],
        out_specs=pl.BlockSpec((None, tq, gw), lambda b, h, i: (b, i, h)),
        out_shape=jax.ShapeDtypeStruct((bsz, t, NSA_WIDTH), F32),
        scratch_shapes=[pltpu.VMEM((NSA_GROUP, tq, 1), F32), pltpu.VMEM((NSA_GROUP, tq, 1), F32),
                        pltpu.VMEM((NSA_GROUP, tq, HEAD_DIM), F32)],
        compiler_params=_cparams(("arbitrary", "arbitrary", "arbitrary")),
        name="selattn",
    )(proj, proj, proj, sel)


def _winattn_kernel(*refs, tq, nblk):
    q_ref = refs[0]
    k_refs = refs[1:1 + nblk]
    v_refs = refs[1 + nblk:1 + 2 * nblk]
    ocmp_ref, osel_ref, small_ref, o_ref = refs[1 + 2 * nblk:]
    g = NSA_GROUP
    hk = pl.program_id(1)
    qi = pl.program_id(2)
    scale = HEAD_DIM ** -0.5
    nk = nblk * tq
    keys = jnp.concatenate([r[...] for r in k_refs], axis=0).astype(BF16)
    vals = jnp.concatenate([r[...] for r in v_refs], axis=0).astype(BF16)
    q4 = jnp.concatenate([q_ref[:, i * HEAD_DIM:(i + 1) * HEAD_DIM] for i in range(g)], axis=0)
    s = (_dot_nt(q4.astype(BF16), keys) * scale).reshape(g, tq, nk)
    pos = qi * tq + lax.broadcasted_iota(jnp.int32, (tq, nk), 0)
    wpos = (qi - (nblk - 1)) * tq + lax.broadcasted_iota(jnp.int32, (tq, nk), 1)
    mask = ((wpos <= pos) & (wpos > pos - WINDOW) & (wpos >= 0))[None]
    s = jnp.where(mask, s, NEG_BIG)
    e = jnp.exp(s - jnp.max(s, axis=-1, keepdims=True)).astype(BF16)
    ov = _dot(e.reshape(g * tq, nk), jnp.concatenate([vals, jnp.ones_like(vals)], axis=1))
    o_win = (ov[:, 0:HEAD_DIM] * (1.0 / jnp.maximum(ov[:, HEAD_DIM:2 * HEAD_DIM], 1e-30))).reshape(g, tq, HEAD_DIM)
    small = small_ref[...]
    for i in range(g):
        head = hk * g + i
        gates = [jax.nn.sigmoid(_lane_pick(small, SMALL_NSA_GATE + j * NSA_HEADS + head)) for j in range(3)]
        cols = slice(i * HEAD_DIM, (i + 1) * HEAD_DIM)
        o = gates[0] * ocmp_ref[:, cols] + gates[1] * osel_ref[:, cols] + gates[2] * o_win[i]
        o_ref[:, cols] = o.astype(o_ref.dtype)


def _winattn(proj, o_cmp, o_sel, tq=2 * Q_BLOCK):
    bsz, t, _ = proj.shape
    nblk = WINDOW // tq + 1
    gw = NSA_GROUP * HEAD_DIM

    def kv_spec(base, d):
        return pl.BlockSpec((None, tq, HEAD_DIM),
                            lambda b, h, i: (b, jnp.maximum(i - (nblk - 1) + d, 0), base + h))

    grp = pl.BlockSpec((None, tq, gw), lambda b, h, i: (b, i, h))
    return pl.pallas_call(
        functools.partial(_winattn_kernel, tq=tq, nblk=nblk),
        grid=(bsz, NSA_KV_HEADS, t // tq),
        in_specs=(
            [pl.BlockSpec((None, tq, gw), lambda b, h, i: (b, i, COL_NSA_Q // NSA_GROUP + h))]
            + [kv_spec(COL_KV + 4 * NSA_KV_HEADS, d) for d in range(nblk)]
            + [kv_spec(COL_KV + 5 * NSA_KV_HEADS, d) for d in range(nblk)]
            + [grp, grp, pl.BlockSpec((None, tq, HEAD_DIM), lambda b, h, i: (b, i, COL_SMALL))]
        ),
        out_specs=grp,
        out_shape=jax.ShapeDtypeStruct((bsz, t, NSA_WIDTH), BF16),
        compiler_params=_cparams(("arbitrary", "arbitrary", "arbitrary")),
        name="winattn",
    )(proj, *([proj] * (2 * nblk)), o_cmp, o_sel, proj)


def _ffn(h, wg, wu, wd):
    act = _silu(_dot(h, wg)) * _dot(h, wu)
    return _dot(act.astype(BF16), wd)


def _pack_bf16_pairs(x):
    half = x.shape[1] // 2
    lo = pltpu.bitcast(x[:, :half].astype(BF16).astype(F32), jnp.uint32)
    hi = pltpu.bitcast(x[:, half:].astype(BF16).astype(F32), jnp.uint32)
    return hi | (lo >> 16)


def _unpack_bf16_pairs(p):
    return pltpu.bitcast(p << 16, F32), pltpu.bitcast(p & jnp.uint32(0xFFFF0000), F32)


def _store_token_rows(ref, x, first=0):
    n, d = x.shape
    s_tiles = d // HEAD_DIM
    for s in range(s_tiles):
        ref[pl.ds(first * s_tiles + s, n, stride=s_tiles), :] = x[:, s * HEAD_DIM:(s + 1) * HEAD_DIM]


def _load_token_rows(ref, n, d, first=0):
    s_tiles = d // HEAD_DIM
    return jnp.concatenate([ref[pl.ds(first * s_tiles + s, n, stride=s_tiles), :] for s in range(s_tiles)], axis=1)


def _outproj_kernel(yg_ref, yn_ref, x_ref, wo_ref, gate_ref, shift_ref, scale_ref, gate_m_ref, nw_ref,
                    rw_ref, rb_ref, sg_ref, su_ref, sd_ref, x1_ref, h_ref, route_ref, count_ref, carry_ref):
    tm = x_ref.shape[0]
    half = yg_ref.shape[1]

    @pl.when((pl.program_id(0) == 0) & (pl.program_id(1) == 0))
    def _():
        carry_ref[...] = jnp.zeros_like(carry_ref)

    y = _dot(yg_ref[...], wo_ref[0:half, :]) + _dot(yn_ref[...], wo_ref[half:2 * half, :])
    x1 = x_ref[...] + gate_ref[...] * y
    h = _modulated_norm(x1, nw_ref[...], scale_ref[...], shift_ref[...])
    _store_token_rows(h_ref, _pack_bf16_pairs(h))
    x1_ref[...] = x1 + gate_m_ref[...] * _ffn(h.astype(BF16), sg_ref[...], su_ref[...], sd_ref[...])
    scores = jax.nn.sigmoid(_dot_split(h, rw_ref[...]))
    n_exp = scores.shape[1]
    lane = lax.broadcasted_iota(jnp.int32, scores.shape, 1)
    cand = scores + rb_ref[...]
    picked = jnp.zeros(scores.shape, jnp.bool_)
    firsts = []
    for _ in range(TOP_K):
        best = jnp.max(cand, axis=-1, keepdims=True)
        first = jnp.min(jnp.where(cand == best, lane, n_exp), axis=-1, keepdims=True)
        hit = lane == first
        picked = picked | hit
        cand = jnp.where(hit, NEG_BIG, cand)
        firsts.append(first)
    norm = ROUTED_SCALE / jnp.sum(jnp.where(picked, scores, 0.0), axis=-1, keepdims=True)
    onehot = jnp.where(picked, 1.0, 0.0)
    ri = lax.broadcasted_iota(jnp.int32, (tm, tm), 0)
    ci = lax.broadcasted_iota(jnp.int32, (tm, tm), 1)
    before = jnp.where(ci < ri, 1.0, 0.0).astype(BF16)
    pos = _dot(before, onehot.astype(BF16)) + carry_ref[...]
    carry_ref[...] += jnp.sum(onehot, axis=0, keepdims=True)
    count_ref[...] = carry_ref[...]
    slot = lax.broadcasted_iota(jnp.int32, (tm, 3 * TOP_K), 1)
    route = jnp.zeros((tm, 3 * TOP_K), F32)
    for j, first in enumerate(firsts):
        hit = lane == first
        route = jnp.where(slot == j, first.astype(F32), route)
        route = jnp.where(slot == TOP_K + j, _lane_pick(scores, first) * norm, route)
        route = jnp.where(slot == 2 * TOP_K + j, jnp.sum(jnp.where(hit, pos, 0.0), axis=-1, keepdims=True), route)
    route_ref[...] = route


def _outproj(y_gdn, y_nsa, x, w_out_b, mod4, norm_w, router_w, router_bias, sg, su, sd, tm=256):
    bsz, t, d = x.shape
    half = y_gdn.shape[2]
    n_exp = router_w.shape[1]
    sff = sg.shape[1]
    s_tiles = d // 2 // HEAD_DIM

    def mod(j):
        return pl.BlockSpec((None, None, 1, d), lambda b, i: (b, j, 0, 0))

    def const(shape):
        return pl.BlockSpec(shape, lambda b, i: (0, 0))

    row = pl.BlockSpec((None, tm, d), lambda b, i: (b, i, 0))
    return pl.pallas_call(
        _outproj_kernel,
        grid=(bsz, t // tm),
        in_specs=[
            pl.BlockSpec((None, tm, half), lambda b, i: (b, i, 0)),
            pl.BlockSpec((None, tm, half), lambda b, i: (b, i, 0)),
            row,
            const((2 * half, d)),
            mod(2), mod(3), mod(4), mod(5),
            const((1, d)), const((d, n_exp)), const((1, n_exp)),
            const((d, sff)), const((d, sff)), const((sff, d)),
        ],
        out_specs=[row,
                   pl.BlockSpec((tm * s_tiles, HEAD_DIM), lambda b, i: (b * (t // tm) + i, 0)),
                   pl.BlockSpec((None, tm, 3 * TOP_K), lambda b, i: (b, i, 0)),
                   const((1, n_exp))],
        out_shape=[
            jax.ShapeDtypeStruct((bsz, t, d), F32),
            jax.ShapeDtypeStruct((bsz * t * s_tiles, HEAD_DIM), jnp.uint32),
            jax.ShapeDtypeStruct((bsz, t, 3 * TOP_K), F32),
            jax.ShapeDtypeStruct((1, n_exp), F32),
        ],
        scratch_shapes=[pltpu.VMEM((1, n_exp), F32)],
        compiler_params=_cparams(("arbitrary", "arbitrary")),
        name="outproj",
    )(y_gdn, y_nsa, x, w_out_b, mod4, mod4, mod4, mod4, norm_w.reshape(1, d), router_w,
      router_bias.reshape(1, n_exp), sg, su, sd)


MOE_TM = 256
MOE_TOK = 256
MOE_TILES_PER_STEP = 4


def _token_row(ref, r, s_tiles):
    return ref.at[pl.ds(pl.multiple_of(r * s_tiles, s_tiles), s_tiles), :]


def _for_each(n, fn, unroll=8):
    def body(i, carry):
        fn(i)
        return carry

    lax.fori_loop(0, n, body, 0, unroll=unroll)


def _moe_scatter_kernel(gstart_ref, gend_ref, idx_ref, pos_ref, h_ref, xs_hbm, zero_ref, sem, *, s_tiles):
    n_exp = gstart_ref.shape[0]

    @pl.when(pl.program_id(0) == 0)
    def _():
        zero_ref[...] = jnp.zeros_like(zero_ref)

        def tail(e):
            first = pl.multiple_of((gend_ref[e] - MOE_TM) * s_tiles, MOE_TM * s_tiles)
            return pltpu.make_async_copy(zero_ref, xs_hbm.at[pl.ds(first, MOE_TM * s_tiles), :], sem)

        def start_tail(e):
            @pl.when(gend_ref[e] > gstart_ref[e])
            def _():
                tail(e).start()

        def wait_tail(e):
            @pl.when(gend_ref[e] > gstart_ref[e])
            def _():
                tail(e).wait()

        _for_each(n_exp, start_tail)
        _for_each(n_exp, wait_tail)

    def row_copy(t, j):
        n = t * TOP_K + j
        row = gstart_ref[idx_ref[n]] + pos_ref[n]
        return pltpu.make_async_copy(_token_row(h_ref, t, s_tiles), _token_row(xs_hbm, row, s_tiles), sem)

    def start_token(t):
        for j in range(TOP_K):
            row_copy(t, j).start()

    def wait_token(t):
        for j in range(TOP_K):
            row_copy(t, j).wait()

    _for_each(MOE_TOK, start_token, unroll=2)
    _for_each(MOE_TOK, wait_token, unroll=2)


def _moe_scatter(h_rows, idx, pos, gstart, gend, n_rows, s_tiles):
    n_steps = idx.shape[0] // (MOE_TOK * TOP_K)
    smem = pl.BlockSpec((MOE_TOK * TOP_K,), lambda i, gs, ge: (i,), memory_space=pltpu.SMEM)
    return pl.pallas_call(
        functools.partial(_moe_scatter_kernel, s_tiles=s_tiles),
        grid_spec=pltpu.PrefetchScalarGridSpec(
            num_scalar_prefetch=2,
            grid=(n_steps,),
            in_specs=[smem, smem, pl.BlockSpec((MOE_TOK * s_tiles, HEAD_DIM), lambda i, gs, ge: (i, 0))],
            out_specs=pl.BlockSpec(memory_space=pl.ANY),
            scratch_shapes=[pltpu.VMEM((MOE_TM * s_tiles, HEAD_DIM), h_rows.dtype), pltpu.SemaphoreType.DMA(())],
        ),
        out_shape=jax.ShapeDtypeStruct((n_rows * s_tiles, HEAD_DIM), h_rows.dtype),
        compiler_params=_cparams(("arbitrary",)),
        name="moe_scatter",
    )(gstart, gend, idx, pos, h_rows)


def _moe_expert_kernel(texp_ref, tslot_ref, tnext_ref, nvalid_ref, xs_ref, wg_hbm, wu_hbm, wd_hbm, ys_ref,
                       wg_buf, wu_buf, wd_buf, wgb_ref, wub_ref, wdb_ref, sems):
    d = wgb_ref.shape[0]
    half = d // 2
    rows_per_tile = xs_ref.shape[0] // MOE_TILES_PER_STEP

    def fetch(expert, s):
        return [pltpu.make_async_copy(src.at[expert], dst.at[s], sems.at[s, n])
                for n, (src, dst) in enumerate(((wg_hbm, wg_buf), (wu_hbm, wu_buf), (wd_hbm, wd_buf)))]

    def tile(i, sub):
        valid = i < nvalid_ref[0]
        new_expert = (i == 0) | (texp_ref[i] != texp_ref[jnp.maximum(i - 1, 0)])
        slot = tslot_ref[i]

        @pl.when(i == 0)
        def _():
            for cp in fetch(texp_ref[0], slot):
                cp.start()

        @pl.when(valid & new_expert)
        def _():
            nxt = tnext_ref[i]

            @pl.when(nxt < nvalid_ref[0])
            def _():
                for cp in fetch(texp_ref[nxt], 1 - slot):
                    cp.start()

            for cp in fetch(texp_ref[i], slot):
                cp.wait()
            wgb_ref[...] = wg_buf[slot].astype(BF16)
            wub_ref[...] = wu_buf[slot].astype(BF16)
            wdb_ref[...] = wd_buf[slot].astype(BF16)

        @pl.when(valid)
        def _():
            lo, hi = _unpack_bf16_pairs(_load_token_rows(xs_ref, MOE_TM, half, first=sub * MOE_TM))
            lo, hi = lo.astype(BF16), hi.astype(BF16)
            gate = _dot(lo, wgb_ref[0:half, :]) + _dot(hi, wgb_ref[half:d, :])
            up = _dot(lo, wub_ref[0:half, :]) + _dot(hi, wub_ref[half:d, :])
            y = _dot((_silu(gate) * up).astype(BF16), wdb_ref[...])
            _store_token_rows(ys_ref, _pack_bf16_pairs(y), first=sub * MOE_TM)

        @pl.when(jnp.logical_not(valid))
        def _():
            ys_ref[sub * rows_per_tile:(sub + 1) * rows_per_tile, :] = jnp.zeros((rows_per_tile, HEAD_DIM), ys_ref.dtype)

    for sub in range(MOE_TILES_PER_STEP):
        tile(pl.program_id(0) * MOE_TILES_PER_STEP + sub, sub)


def _moe_experts(xs, tile_expert, tile_slot, tile_next, nvalid, wg, wu, wd):
    n_exp, d, ff = wg.shape
    n_steps = tile_expert.shape[0] // MOE_TILES_PER_STEP
    blk = xs.shape[0] // n_steps
    hbm = pl.BlockSpec(memory_space=pl.ANY)
    vmem_bytes = 3 * d * ff * (2 * 4 + 2) + 4 * blk * HEAD_DIM * 4 + 3 * MOE_TM * d * 4
    return pl.pallas_call(
        _moe_expert_kernel,
        grid_spec=pltpu.PrefetchScalarGridSpec(
            num_scalar_prefetch=4,
            grid=(n_steps,),
            in_specs=[
                pl.BlockSpec((blk, HEAD_DIM),
                             lambda i, te, ts, tn, nv: (jnp.minimum(i, (nv[0] - 1) // MOE_TILES_PER_STEP), 0)),
                hbm, hbm, hbm,
            ],
            out_specs=pl.BlockSpec((blk, HEAD_DIM), lambda i, te, ts, tn, nv: (i, 0)),
            scratch_shapes=[pltpu.VMEM((2, d, ff), wg.dtype), pltpu.VMEM((2, d, ff), wu.dtype),
                            pltpu.VMEM((2, ff, d), wd.dtype),
                            pltpu.VMEM((d, ff), BF16), pltpu.VMEM((d, ff), BF16), pltpu.VMEM((ff, d), BF16),
                            pltpu.SemaphoreType.DMA((2, 3))],
        ),
        out_shape=jax.ShapeDtypeStruct(xs.shape, xs.dtype),
        compiler_params=_cparams(("arbitrary",), vmem_bytes),
        name="moe_experts",
    )(tile_expert, tile_slot, tile_next, nvalid, xs, wg, wu, wd)


def _moe_combine_kernel(gstart_ref, idx_ref, pos_ref, w_ref, ys_hbm, x1_ref, gate_ref, nw_ref, o_ref,
                        buf_ref, lo_ref, hi_ref, sems, *, s_tiles):
    tm, d = x1_ref.shape
    n_parts = sems.shape[0]
    part_tok = tm // n_parts
    part_rows = part_tok * TOP_K

    def row_copy(n, sem):
        row = gstart_ref[idx_ref[n]] + pos_ref[n]
        return pltpu.make_async_copy(_token_row(ys_hbm, row, s_tiles), _token_row(buf_ref, n, s_tiles), sem)

    def token(t):
        n0 = t * TOP_K
        lo, hi = _unpack_bf16_pairs(_token_row(buf_ref, n0, s_tiles)[...])
        acc_lo, acc_hi = w_ref[n0] * lo, w_ref[n0] * hi
        for j in range(1, TOP_K):
            lo, hi = _unpack_bf16_pairs(_token_row(buf_ref, n0 + j, s_tiles)[...])
            acc_lo, acc_hi = acc_lo + w_ref[n0 + j] * lo, acc_hi + w_ref[n0 + j] * hi
        _token_row(lo_ref, t, s_tiles)[...] = acc_lo
        _token_row(hi_ref, t, s_tiles)[...] = acc_hi

    for part in range(n_parts):
        _for_each(part_rows, lambda n, part=part: row_copy(part * part_rows + n, sems.at[part]).start())
    for part in range(n_parts):
        _for_each(part_rows, lambda n, part=part: row_copy(part * part_rows + n, sems.at[part]).wait())
        _for_each(part_tok, lambda t, part=part: token(part * part_tok + t))
    routed = jnp.concatenate([_load_token_rows(lo_ref, tm, d // 2), _load_token_rows(hi_ref, tm, d // 2)], axis=1)
    x2 = x1_ref[...] + gate_ref[...] * routed
    o_ref[...] = x2 * lax.rsqrt(jnp.mean(x2 * x2, axis=-1, keepdims=True) + EPS) * nw_ref[...]


def _moe_combine(ys, idx, pos, w, gstart, x1, mod4, norm_w):
    bsz, t, d = x1.shape
    s_tiles = d // 2 // HEAD_DIM
    tm = MOE_TOK
    nt = t // tm
    smem = pl.BlockSpec((tm * TOP_K,), lambda b, i, gs: (b * nt + i,), memory_space=pltpu.SMEM)
    row = pl.BlockSpec((None, tm, d), lambda b, i, gs: (b, i, 0))
    return pl.pallas_call(
        functools.partial(_moe_combine_kernel, s_tiles=s_tiles),
        grid_spec=pltpu.PrefetchScalarGridSpec(
            num_scalar_prefetch=1,
            grid=(bsz, nt),
            in_specs=[
                smem, smem, smem,
                pl.BlockSpec(memory_space=pl.ANY),
                row,
                pl.BlockSpec((None, None, 1, d), lambda b, i, gs: (b, 5, 0, 0)),
                pl.BlockSpec((1, d), lambda b, i, gs: (0, 0)),
            ],
            out_specs=row,
            scratch_shapes=[pltpu.VMEM((tm * TOP_K * s_tiles, HEAD_DIM), ys.dtype),
                            pltpu.VMEM((tm * s_tiles, HEAD_DIM), F32), pltpu.VMEM((tm * s_tiles, HEAD_DIM), F32),
                            pltpu.SemaphoreType.DMA((4,))],
        ),
        out_shape=jax.ShapeDtypeStruct((bsz, t, d), F32),
        compiler_params=_cparams(("arbitrary", "arbitrary")),
        name="moe_combine",
    )(gstart, idx, pos, w, ys, x1, mod4, norm_w.reshape(1, d))


def _moe(h_rows, route, counts, x1, mod4, norm_w, wg, wu, wd):
    bsz, t, d = x1.shape
    n_exp = wg.shape[0]
    s_tiles = h_rows.shape[0] // (bsz * t)
    idx = route[..., 0:TOP_K].astype(jnp.int32).reshape(-1)
    w = route[..., TOP_K:2 * TOP_K].reshape(-1)
    pos = route[..., 2 * TOP_K:3 * TOP_K].astype(jnp.int32).reshape(-1)
    padded = (counts.reshape(n_exp).astype(jnp.int32) + MOE_TM - 1) // MOE_TM * MOE_TM
    gend = jnp.cumsum(padded)
    gstart = gend - padded
    nvalid = gend[-1:] // MOE_TM
    n_tiles = bsz * t * TOP_K // MOE_TM + n_exp
    tile_start = jnp.minimum(jnp.arange(n_tiles, dtype=jnp.int32), nvalid - 1) * MOE_TM
    tile_expert = jnp.sum((gend[None, :] <= tile_start[:, None]).astype(jnp.int32), axis=1)
    first_of_group = jnp.concatenate([jnp.ones((1,), jnp.int32),
                                      (tile_expert[1:] != tile_expert[:-1]).astype(jnp.int32)])
    tile_slot = (jnp.cumsum(first_of_group) - 1) % 2
    experts = jnp.arange(n_exp, dtype=jnp.int32)
    tile_next = jnp.sum(jnp.where(tile_expert[:, None] == experts[None, :], gend[None, :], 0), axis=1) // MOE_TM
    xs = _moe_scatter(h_rows, idx, pos, gstart, gend, n_tiles * MOE_TM, s_tiles)
    ys = _moe_experts(xs, tile_expert, tile_slot, tile_next, nvalid, wg, wu, wd)
    return _moe_combine(ys, idx, pos, w, gstart, x1, mod4, norm_w)


def _reorder_w_in(w_in):
    d = w_in.shape[0]
    o_beta = 3 * GDN_WIDTH
    o_a = o_beta + GDN_HEADS
    o_gate = o_a + GDN_HEADS
    o_q = o_gate + GDN_WIDTH
    o_kv = o_q + NSA_WIDTH
    o_ng = o_kv + 6 * NSA_KV_WIDTH
    n_small = 2 * GDN_HEADS + 3 * NSA_HEADS
    parts = [w_in[:, :o_beta], w_in[:, o_gate:o_q], w_in[:, o_q:o_kv], w_in[:, o_kv:o_ng],
             w_in[:, o_beta:o_gate], w_in[:, o_ng:o_ng + 3 * NSA_HEADS]]
    used = COL_SMALL * HEAD_DIM + n_small
    parts.append(jnp.zeros((d, PROJ_COLS - used), w_in.dtype))
    return jnp.concatenate([p.astype(BF16) for p in parts], axis=1)


def _layer(x, mod, norm_attn_w, norm_ffn_w, w_in, gdn_conv_w, gdn_a_log, gdn_dt_bias, gdn_norm_w,
           cmp_pe_k, cmp_w1_k, cmp_w2_k, cmp_pe_v, cmp_w1_v, cmp_w2_v, w_out, router_w, router_bias,
           expert_w_gate, expert_w_up, expert_w_down, shared_w_gate, shared_w_up, shared_w_down, final_w):
    bsz, t, d = x.shape
    mod4 = mod.reshape(bsz, N_MOD, 1, d)
    proj = _proj(x, mod4, norm_attn_w, _reorder_w_in(w_in))
    y_gdn = _gdn(proj, gdn_conv_w, gdn_a_log, gdn_dt_bias, gdn_norm_w)
    kc = _compress(proj, COL_KV, cmp_pe_k, cmp_w1_k, cmp_w2_k)
    vc = _compress(proj, COL_KV + NSA_KV_HEADS, cmp_pe_v, cmp_w1_v, cmp_w2_v)
    o_cmp, sel = _cmpattn(proj, kc, vc)
    o_sel = _selattn(proj, sel)
    y_nsa = _winattn(proj, o_cmp, o_sel)
    x1, h_rows, route, counts = _outproj(
        y_gdn, y_nsa, x, w_out.astype(BF16), mod4, norm_ffn_w, router_w, router_bias,
        shared_w_gate.astype(BF16), shared_w_up.astype(BF16), shared_w_down.astype(BF16))
    return _moe(h_rows, route, counts, x1, mod4, final_w, expert_w_gate, expert_w_up, expert_w_down)


def kernel(x, c, w_ada, b_ada, norm_attn_w, norm_ffn_w, norm_final_w, w_in, gdn_conv_w, gdn_a_log, gdn_dt_bias,
           gdn_norm_w, cmp_pe_k, cmp_w1_k, cmp_w2_k, cmp_pe_v, cmp_w1_v, cmp_w2_v, w_out, router_w, router_bias,
           expert_w_gate, expert_w_up, expert_w_down, shared_w_gate, shared_w_up, shared_w_down):
    depth = w_ada.shape[0]
    assert depth == 1, "the fused final norm assumes a single layer"
    l = 0
    mod = _ada(c, w_ada[l], b_ada[l])
    return _layer(x, mod, norm_attn_w[l], norm_ffn_w[l], w_in[l], gdn_conv_w[l], gdn_a_log[l], gdn_dt_bias[l],
                  gdn_norm_w[l], cmp_pe_k[l], cmp_w1_k[l], cmp_w2_k[l], cmp_pe_v[l], cmp_w1_v[l], cmp_w2_v[l],
                  w_out[l], router_w[l], router_bias[l], expert_w_gate[l], expert_w_up[l], expert_w_down[l],
                  shared_w_gate[l], shared_w_up[l], shared_w_down[l], norm_final_w)
```

```python
import functools

import jax
import jax.numpy as jnp
from jax import lax
from jax.experimental import pallas as pl
from jax.experimental.pallas import tpu as pltpu

F32 = jnp.float32
BF16 = jnp.bfloat16
HI = lax.Precision.HIGHEST

HEAD_DIM = 128
GDN_HEADS = 8
NSA_HEADS = 8
NSA_KV_HEADS = 2
NSA_GROUP = NSA_HEADS // NSA_KV_HEADS
GDN_WIDTH = GDN_HEADS * HEAD_DIM
NSA_WIDTH = NSA_HEADS * HEAD_DIM
NSA_KV_WIDTH = NSA_KV_HEADS * HEAD_DIM
CONV_WIDTH = 4
GDN_CHUNK = 64
CMP_BLOCK = 32
CMP_STRIDE = 16
SEL_BLOCK = 64
SEL_TOPK = 16
N_LOCAL_BLOCKS = 2
WINDOW = 512
Q_BLOCK = 128
TOP_K = 8
ROUTED_SCALE = 2.5
N_MOD = 6
EPS = 1e-6

COL_GDN_Q = 0
COL_GDN_K = GDN_HEADS
COL_GDN_V = 2 * GDN_HEADS
COL_GDN_GATE = 3 * GDN_HEADS
COL_NSA_Q = 4 * GDN_HEADS
COL_KV = COL_NSA_Q + NSA_HEADS
COL_SMALL = COL_KV + 6 * NSA_KV_HEADS
N_COL_BLOCKS = COL_SMALL + 1
PROJ_TN = 2304
PROJ_COLS = 3 * PROJ_TN
SMALL_BETA = 0
SMALL_A = GDN_HEADS
SMALL_NSA_GATE = 2 * GDN_HEADS

NEG_BIG = -1e30
VMEM_LIMIT = 48 * 1024 * 1024


def _cparams(semantics, vmem_limit=VMEM_LIMIT):
    return pltpu.CompilerParams(dimension_semantics=semantics, vmem_limit_bytes=vmem_limit)


def _silu(x):
    return (0.5 * x) * (1.0 + jnp.tanh(0.5 * x))


def _dot(a, b, precision=None):
    return jnp.dot(a, b, preferred_element_type=F32, precision=precision)


def _dot_nt(a, b, precision=None):
    return lax.dot_general(a, b, (((1,), (1,)), ((), ())), preferred_element_type=F32, precision=precision)


def _dot_tn(a, b, precision=None):
    return lax.dot_general(a, b, (((0,), (0,)), ((), ())), preferred_element_type=F32, precision=precision)


def _split_bf16(x):
    hi = x.astype(BF16)
    return hi, (x - hi.astype(F32)).astype(BF16)


def _dot_split(a, b, nt=False):
    (ah, al), (bh, bl) = _split_bf16(a), _split_bf16(b)
    f = _dot_nt if nt else _dot
    return f(ah, bh) + (f(ah, bl) + f(al, bh))


def _lane_pick(x, idx):
    lane = lax.broadcasted_iota(jnp.int32, x.shape, 1)
    return jnp.sum(jnp.where(lane == idx, x, 0.0), axis=-1, keepdims=True)


def _ada_kernel(ct_ref, w_ref, b_ref, o_ref):
    c = ct_ref[...]
    cond = _silu(c)
    w = w_ref[...]
    rows = [jnp.sum(w * cond[:, b:b + 1], axis=0, keepdims=True) for b in range(c.shape[1])]
    o_ref[...] = jnp.concatenate(rows, axis=0) + b_ref[...]


def _ada(c, w_ada, b_ada, tn=512):
    bsz, d = c.shape
    n = w_ada.shape[1]
    return pl.pallas_call(
        _ada_kernel,
        grid=(n // tn,),
        in_specs=[
            pl.BlockSpec((d, bsz), lambda j: (0, 0)),
            pl.BlockSpec((d, tn), lambda j: (0, j)),
            pl.BlockSpec((1, tn), lambda j: (0, j)),
        ],
        out_specs=pl.BlockSpec((bsz, tn), lambda j: (0, j)),
        out_shape=jax.ShapeDtypeStruct((bsz, n), F32),
        compiler_params=_cparams(("arbitrary",)),
        name="ada",
    )(c.T, w_ada, b_ada.reshape(1, n))


def _modulated_norm(x, nw, scale, shift):
    var = jnp.mean(x * x, axis=-1, keepdims=True)
    h = x * lax.rsqrt(var + EPS) * nw
    return h * (1.0 + scale) + shift


def _proj_kernel(x_ref, shift_ref, scale_ref, nw_ref, w_ref, o_ref):
    h = _modulated_norm(x_ref[...], nw_ref[...], scale_ref[...], shift_ref[...])
    o_ref[...] = _dot(h.astype(BF16), w_ref[...])


def _proj(x, mod4, norm_w, w_r, tm=512):
    bsz, t, d = x.shape
    n = w_r.shape[1]
    return pl.pallas_call(
        _proj_kernel,
        grid=(n // PROJ_TN, bsz, t // tm),
        in_specs=[
            pl.BlockSpec((None, tm, d), lambda j, b, i: (b, i, 0)),
            pl.BlockSpec((None, None, 1, d), lambda j, b, i: (b, 0, 0, 0)),
            pl.BlockSpec((None, None, 1, d), lambda j, b, i: (b, 1, 0, 0)),
            pl.BlockSpec((1, d), lambda j, b, i: (0, 0)),
            pl.BlockSpec((d, PROJ_TN), lambda j, b, i: (0, j)),
        ],
        out_specs=pl.BlockSpec((None, tm, PROJ_TN), lambda j, b, i: (b, i, j)),
        out_shape=jax.ShapeDtypeStruct((bsz, t, n), F32),
        compiler_params=_cparams(("arbitrary", "arbitrary", "arbitrary")),
        name="proj",
    )(x, mod4, mod4, norm_w.reshape(1, d), w_r)


def _gdn_kernel(q_ref, k_ref, v_ref, gate_ref, small_ref, hq_ref, hk_ref, hv_ref,
                cwq_ref, cwk_ref, cwv_ref, alog_ref, dtb_ref, nw_ref, o_ref, s_ref, xc_ref, *, tt, hg):
    first = pl.program_id(2) == 0

    @pl.when(first)
    def _():
        s_ref[...] = jnp.zeros_like(s_ref)

    def conv_silu(x_ref, halo_ref, cw_ref):
        xc_ref[0:8, :] = jnp.where(first, 0.0, halo_ref[...])
        xc_ref[8:tt + 8, :] = x_ref[...]
        cw = cw_ref[...]
        off = 8 - (CONV_WIDTH - 1)
        y = cw[0:1, :] * xc_ref[off:off + tt, :]
        for i in range(1, CONV_WIDTH):
            y = y + cw[i:i + 1, :] * xc_ref[off + i:off + i + tt, :]
        return _silu(y)

    q_all = conv_silu(q_ref, hq_ref, cwq_ref)
    k_all = conv_silu(k_ref, hk_ref, cwk_ref)
    v_all = conv_silu(v_ref, hv_ref, cwv_ref)
    ri = lax.broadcasted_iota(jnp.int32, (tt, tt), 0)
    ci = lax.broadcasted_iota(jnp.int32, (tt, tt), 1)
    lower = jnp.where(((ri // GDN_CHUNK) == (ci // GDN_CHUNK)) & (ci <= ri), 1.0, 0.0).astype(BF16)
    heads = [pl.program_id(1) * hg + hh for hh in range(hg)]
    outs = _gdn_heads(q_all, k_all, v_all, small_ref[...], lower, heads, alog_ref[...], dtb_ref[...], s_ref, tt)
    for hh, o in enumerate(outs):
        cols = slice(hh * HEAD_DIM, (hh + 1) * HEAD_DIM)
        o = o * lax.rsqrt(jnp.mean(o * o, axis=-1, keepdims=True) + EPS) * nw_ref[...]
        o_ref[:, cols] = (o * _silu(gate_ref[:, cols])).astype(o_ref.dtype)


def _gdn_heads(q_all, k_all, v_all, small, lower, heads, alog, dtb, s_ref, tt):
    c = GDN_CHUNK
    nch = tt // c
    hg = len(heads)
    kpos = lax.broadcasted_iota(jnp.int32, (tt, c), 0) % c
    jj = lax.broadcasted_iota(jnp.int32, (tt, c), 1)
    per_head = []
    for hh, h in enumerate(heads):
        cols = slice(hh * HEAD_DIM, (hh + 1) * HEAD_DIM)
        q, k, v = q_all[:, cols], k_all[:, cols], v_all[:, cols]
        q = q * lax.rsqrt(jnp.sum(q * q, axis=-1, keepdims=True) + EPS) * (HEAD_DIM ** -0.5)
        k = k * lax.rsqrt(jnp.sum(k * k, axis=-1, keepdims=True) + EPS)
        beta = jax.nn.sigmoid(_lane_pick(small, SMALL_BETA + h))
        z = _lane_pick(small, SMALL_A + h) + _lane_pick(dtb, h)
        softplus = jnp.maximum(z, 0.0) + jnp.log(1.0 + jnp.exp(-jnp.abs(z)))
        g = -jnp.exp(_lane_pick(alog, h)) * softplus
        g_c = jnp.broadcast_to(g, (tt, c))
        rhs = jnp.concatenate([g_c, jnp.where(kpos > jj, g_c, 0.0)], axis=1)
        r_hi = rhs.astype(BF16)
        rem = rhs - r_hi.astype(F32)
        r_mid = rem.astype(BF16)
        r_lo = (rem - r_mid.astype(F32)).astype(BF16)
        sums = _dot(lower, r_hi) + (_dot(lower, r_mid) + _dot(lower, r_lo))
        gcum = jnp.broadcast_to(sums[:, 0:1], (tt, HEAD_DIM))
        gc3 = gcum.reshape(nch, c, HEAD_DIM)
        grest = (gc3[:, c - 1:c, :] - gc3).reshape(tt, HEAD_DIM)
        decay_h = jnp.where(kpos >= jj, jnp.exp(sums[:, c:2 * c]), 0.0)
        per_head.append((q, k, v * beta, k * beta, jnp.exp(gcum), jnp.exp(grest), decay_h))

    def stack(j):
        x = jnp.concatenate([ph[j] for ph in per_head], axis=0)
        return x.reshape(hg * nch, c, x.shape[-1])

    q, k, vb, kb, egc, egr, decay = (stack(j) for j in range(7))
    kpos_all = lax.broadcasted_iota(jnp.int32, (hg * tt, c), 0) % c
    jj_all = lax.broadcasted_iota(jnp.int32, (hg * tt, c), 1)
    strict = (kpos_all > jj_all).reshape(hg * nch, c, c)
    eye = jnp.where(kpos_all == jj_all, 1.0, 0.0).reshape(hg * nch, c, c)

    def bmm(a, b):
        return jnp.einsum('nij,njk->nik', a, b, preferred_element_type=F32)

    def bmm_nt(a, b):
        return jnp.einsum('nid,njd->nij', a, b, preferred_element_type=F32)

    def bmm_split(a, b):
        (ah, al), (bh, bl) = _split_bf16(a), _split_bf16(b)
        return bmm(ah, bh) + (bmm(ah, bl) + bmm(al, bh))

    k3 = k.astype(BF16)
    a = jnp.where(strict, bmm_nt(kb.astype(BF16), k3) * decay, 0.0)
    inv = eye - a
    p = bmm_split(a, a)
    n_sq = (c - 1).bit_length() - 1
    for lvl in range(n_sq):
        inv = inv + bmm_split(inv, p)
        if lvl + 1 < n_sq:
            p = bmm_split(p, p)
    inv_b = inv.astype(BF16)
    u3 = bmm(inv_b, vb.astype(BF16)).astype(BF16)
    w3 = bmm(inv_b, (kb * egc).astype(BF16)).astype(BF16)
    qk3 = (bmm_nt(q.astype(BF16), k3) * decay).astype(BF16)
    kd3 = (k * egr).astype(BF16)
    qe3 = (q * egc - bmm(qk3, w3)).astype(BF16)
    o03 = bmm(qk3, u3)
    kw = [_dot_tn(kd3[i], w3[i]).astype(BF16) for i in range(hg * nch)]
    ku = [_dot_tn(kd3[i], u3[i]) for i in range(hg * nch)]

    states = [s_ref[hh] for hh in range(hg)]
    outs = [[] for _ in range(hg)]
    for n in range(nch):
        for hh in range(hg):
            i = hh * nch + n
            state_b = states[hh].astype(BF16)
            outs[hh].append(_dot(qe3[i], state_b) + o03[i])
            states[hh] = states[hh] * egc[i, c - 1:c, :] - _dot(kw[i], state_b) + ku[i]
    for hh in range(hg):
        s_ref[hh] = states[hh]
    return [jnp.concatenate(o, axis=0) for o in outs]


def _gdn(proj, conv_w, a_log, dt_bias, norm_w, tt=256, hg=8):
    bsz, t, _ = proj.shape
    wdt = hg * HEAD_DIM

    def col(base):
        return pl.BlockSpec((None, tt, wdt), lambda b, h, i: (b, i, base // hg + h))

    def halo(base):
        return pl.BlockSpec((None, 8, wdt), lambda b, h, i: (b, jnp.maximum(i * (tt // 8) - 1, 0), base // hg + h))

    def cw(base):
        return pl.BlockSpec((CONV_WIDTH, wdt), lambda b, h, i: (0, base // hg + h))

    heads = pl.BlockSpec((1, GDN_HEADS), lambda b, h, i: (0, 0))
    return pl.pallas_call(
        functools.partial(_gdn_kernel, tt=tt, hg=hg),
        grid=(bsz, GDN_HEADS // hg, t // tt),
        in_specs=[
            col(COL_GDN_Q), col(COL_GDN_K), col(COL_GDN_V), col(COL_GDN_GATE),
            pl.BlockSpec((None, tt, HEAD_DIM), lambda b, h, i: (b, i, COL_SMALL)),
            halo(COL_GDN_Q), halo(COL_GDN_K), halo(COL_GDN_V),
            cw(COL_GDN_Q), cw(COL_GDN_K), cw(COL_GDN_V),
            heads, heads,
            pl.BlockSpec((1, HEAD_DIM), lambda b, h, i: (0, 0)),
        ],
        out_specs=pl.BlockSpec((None, tt, wdt), lambda b, h, i: (b, i, h)),
        out_shape=jax.ShapeDtypeStruct((bsz, t, GDN_WIDTH), BF16),
        scratch_shapes=[pltpu.VMEM((hg, HEAD_DIM, HEAD_DIM), F32), pltpu.VMEM((tt + 8, wdt), F32)],
        compiler_params=_cparams(("arbitrary", "arbitrary", "arbitrary")),
        name="gdn",
    )(proj, proj, proj, proj, proj, proj, proj, proj, conv_w, conv_w, conv_w,
      a_log.reshape(1, GDN_HEADS), dt_bias.reshape(1, GDN_HEADS), norm_w.reshape(1, HEAD_DIM))


def _compress_kernel(x_ref, pe_ref, w1_ref, w2_ref, o_ref):
    r = x_ref.shape[0] // CMP_STRIDE
    d = x_ref.shape[1]
    a = b = None
    for l in range(CMP_STRIDE):
        x_l = x_ref[pl.ds(l, r, stride=CMP_STRIDE), :]
        a_l = _dot_split(x_l + pe_ref[l:l + 1, :], w1_ref[l * d:(l + 1) * d, :])
        b_l = _dot_split(x_l + pe_ref[CMP_STRIDE + l:CMP_STRIDE + l + 1, :],
                         w1_ref[(CMP_STRIDE + l) * d:(CMP_STRIDE + l + 1) * d, :])
        a, b = (a_l, b_l) if a is None else (a + a_l, b + b_l)
    hid = _silu(a + pltpu.roll(b, r - 1, 0))
    o_ref[...] = _dot_split(hid, w2_ref[...])


def _compress(proj, col_block, pe, w1, w2):
    bsz, t, _ = proj.shape
    r = t // CMP_STRIDE
    hid = w1.shape[1]
    return pl.pallas_call(
        _compress_kernel,
        grid=(bsz, NSA_KV_HEADS),
        in_specs=[
            pl.BlockSpec((None, t, HEAD_DIM), lambda b, h: (b, 0, col_block + h)),
            pl.BlockSpec((CMP_BLOCK, HEAD_DIM), lambda b, h: (0, 0)),
            pl.BlockSpec((CMP_BLOCK * HEAD_DIM, hid), lambda b, h: (0, 0)),
            pl.BlockSpec((hid, HEAD_DIM), lambda b, h: (0, 0)),
        ],
        out_specs=pl.BlockSpec((None, None, r, HEAD_DIM), lambda b, h: (b, h, 0, 0)),
        out_shape=jax.ShapeDtypeStruct((bsz, NSA_KV_HEADS, r, HEAD_DIM), F32),
        compiler_params=_cparams(("arbitrary", "arbitrary")),
        name="compress",
    )(proj, pe, w1, w2)


def _cmpattn_kernel(q_ref, kc_ref, vc_ref, o_ref, sel_ref, *, tq, nc, nsel):
    ncp = kc_ref.shape[0]
    scale = HEAD_DIM ** -0.5
    t0 = pl.program_id(2) * tq
    kc = kc_ref[...]
    vc = vc_ref[...]
    pos_r = t0 + lax.broadcasted_iota(jnp.int32, (tq, ncp), 0)
    c_r = lax.broadcasted_iota(jnp.int32, (tq, ncp), 1)
    valid_r = (c_r * CMP_STRIDE + CMP_BLOCK - 1 <= pos_r) & (c_r < nc)
    pos_c = t0 + lax.broadcasted_iota(jnp.int32, (ncp, tq), 1)
    c_c = lax.broadcasted_iota(jnp.int32, (ncp, tq), 0)
    valid_c = (c_c * CMP_STRIDE + CMP_BLOCK - 1 <= pos_c) & (c_c < nc)
    psum = jnp.zeros((ncp, tq), F32)
    kc_hi, kc_lo = _split_bf16(kc)
    vc_b = vc.astype(BF16)
    for g in range(NSA_GROUP):
        q_hi, q_lo = _split_bf16(q_ref[:, g * HEAD_DIM:(g + 1) * HEAD_DIM])
        s = _dot_nt(q_hi, kc_hi) + (_dot_nt(q_hi, kc_lo) + _dot_nt(q_lo, kc_hi))
        s = jnp.where(valid_r, s * scale, NEG_BIG)
        e = jnp.where(valid_r, jnp.exp(s - jnp.max(s, axis=-1, keepdims=True)), 0.0)
        p = e * (1.0 / jnp.maximum(jnp.sum(e, axis=-1, keepdims=True), 1e-30))
        o_ref[:, g * HEAD_DIM:(g + 1) * HEAD_DIM] = _dot(p.astype(BF16), vc_b)
        st = _dot_nt(kc_hi, q_hi) + (_dot_nt(kc_lo, q_hi) + _dot_nt(kc_hi, q_lo))
        st = jnp.where(valid_c, st * scale, NEG_BIG)
        et = jnp.where(valid_c, jnp.exp(st - jnp.max(st, axis=0, keepdims=True)), 0.0)
        psum = psum + et * (1.0 / jnp.maximum(jnp.sum(et, axis=0, keepdims=True), 1e-30))
    j_o = lax.broadcasted_iota(jnp.int32, (nsel, ncp), 0) * SEL_BLOCK
    c_o = lax.broadcasted_iota(jnp.int32, (nsel, ncp), 1) * CMP_STRIDE
    overlap = jnp.where((c_o < j_o + SEL_BLOCK) & (c_o + CMP_BLOCK > j_o), 1.0, 0.0).astype(BF16)
    p_hi, p_lo = _split_bf16(psum)
    imp = _dot(overlap, p_hi) + _dot(overlap, p_lo)
    blk = lax.broadcasted_iota(jnp.int32, (nsel, tq), 0)
    dist = (t0 + lax.broadcasted_iota(jnp.int32, (nsel, tq), 1)) // SEL_BLOCK - blk
    forced = (blk == 0) | ((dist >= 0) & (dist < N_LOCAL_BLOCKS))
    val = jnp.where(forced, -NEG_BIG, jnp.where(dist >= 0, imp, -1.0))
    rank = jnp.zeros((nsel, tq), F32)
    for j in range(nsel):
        row = val[j:j + 1, :]
        rank = rank + jnp.where((row > val) | ((row == val) & (blk > j)), 1.0, 0.0)
    sel = jnp.where((rank < min(SEL_TOPK, nsel)) & (dist >= 0), 1.0, 0.0)
    sel_ref[...] = sel.T


def _cmpattn(proj, kc, vc, tq=512):
    bsz, t, _ = proj.shape
    ncp = kc.shape[2]
    nc = (t - CMP_BLOCK) // CMP_STRIDE + 1
    nsel = t // SEL_BLOCK
    gw = NSA_GROUP * HEAD_DIM
    return pl.pallas_call(
        functools.partial(_cmpattn_kernel, tq=tq, nc=nc, nsel=nsel),
        grid=(bsz, NSA_KV_HEADS, t // tq),
        in_specs=[
            pl.BlockSpec((None, tq, gw), lambda b, h, i: (b, i, COL_NSA_Q // NSA_GROUP + h)),
            pl.BlockSpec((None, None, ncp, HEAD_DIM), lambda b, h, i: (b, h, 0, 0)),
            pl.BlockSpec((None, None, ncp, HEAD_DIM), lambda b, h, i: (b, h, 0, 0)),
        ],
        out_specs=[
            pl.BlockSpec((None, tq, gw), lambda b, h, i: (b, i, h)),
            pl.BlockSpec((None, None, tq, nsel), lambda b, h, i: (b, h, i, 0)),
        ],
        out_shape=[
            jax.ShapeDtypeStruct((bsz, t, NSA_WIDTH), F32),
            jax.ShapeDtypeStruct((bsz, NSA_KV_HEADS, t, nsel), F32),
        ],
        compiler_params=_cparams(("arbitrary", "arbitrary", "arbitrary")),
        name="cmpattn",
    )(proj, kc, vc)


def _selattn_kernel(q_ref, k_ref, v_ref, sel_ref, o_ref, m_ref, l_ref, acc_ref, *, tq, tk, nsel):
    g = NSA_GROUP
    qi = pl.program_id(2)
    m_ref[...] = jnp.full_like(m_ref, NEG_BIG)
    l_ref[...] = jnp.zeros_like(l_ref)
    acc_ref[...] = jnp.zeros_like(acc_ref)
    q4 = jnp.concatenate([q_ref[:, i * HEAD_DIM:(i + 1) * HEAD_DIM] for i in range(g)], axis=0)
    q4 = (q4 * HEAD_DIM ** -0.5).astype(BF16)
    sel = sel_ref[...].astype(BF16)
    pos = qi * tq + lax.broadcasted_iota(jnp.int32, (tq, tk), 0)

    def kv_step(kv, carry):
        k0 = pl.multiple_of(kv * tk, tk)
        s = _dot_nt(q4, k_ref[pl.ds(k0, tk), :].astype(BF16))
        key = k0 + lax.broadcasted_iota(jnp.int32, (nsel, tk), 1)
        blk = lax.broadcasted_iota(jnp.int32, (nsel, tk), 0)
        expand = jnp.where(key // SEL_BLOCK == blk, 1.0, 0.0).astype(BF16)
        picked = _dot(sel, expand)
        kpos = k0 + lax.broadcasted_iota(jnp.int32, (tq, tk), 1)
        mask = ((picked > 0.5) & (kpos <= pos))[None]
        s = jnp.where(mask, s.reshape(g, tq, tk), NEG_BIG)
        m_prev = m_ref[...]
        m_new = jnp.maximum(m_prev, jnp.max(s, axis=-1, keepdims=True))
        p = jnp.exp(s - m_new)
        alpha = jnp.exp(m_prev - m_new)
        l_ref[...] = alpha * l_ref[...] + jnp.sum(p, axis=-1, keepdims=True)
        pv = _dot(p.reshape(g * tq, tk).astype(BF16), v_ref[pl.ds(k0, tk), :].astype(BF16))
        acc_ref[...] = alpha * acc_ref[...] + pv.reshape(g, tq, HEAD_DIM)
        m_ref[...] = m_new
        return carry

    lax.fori_loop(0, (qi * tq + tq - 1) // tk + 1, kv_step, 0)
    o = acc_ref[...] * (1.0 / jnp.maximum(l_ref[...], 1e-30))
    for i in range(g):
        o_ref[:, i * HEAD_DIM:(i + 1) * HEAD_DIM] = o[i]


def _selattn(proj, sel, tq=2 * Q_BLOCK, tk=512):
    bsz, t, _ = proj.shape
    nsel = t // SEL_BLOCK
    tk = min(tk, t)
    gw = NSA_GROUP * HEAD_DIM

    def kv_spec(base):
        return pl.BlockSpec((None, t, HEAD_DIM), lambda b, h, i: (b, 0, base + h))

    return pl.pallas_call(
        functools.partial(_selattn_kernel, tq=tq, tk=tk, nsel=nsel),
        grid=(bsz, NSA_KV_HEADS, t // tq),
        in_specs=[
            pl.BlockSpec((None, tq, gw), lambda b, h, i: (b, i, COL_NSA_Q // NSA_GROUP + h)),
            kv_spec(COL_KV + 2 * NSA_KV_HEADS), kv_spec(COL_KV + 3 * NSA_KV_HEADS),
            pl.BlockSpec((None, None, tq, nsel), lambda b, h, i: (b, h, i, 0)),
        ],
        out_specs=pl.BlockSpec((None, tq, gw), lambda b, h, i: (b, i, h)),
        out_shape=jax.ShapeDtypeStruct((bsz, t, NSA_WIDTH), F32),
        scratch_shapes=[pltpu.VMEM((NSA_GROUP, tq, 1), F32), pltpu.VMEM((NSA_GROUP, tq, 1), F32),
                        pltpu.VMEM((NSA_GROUP, tq, HEAD_DIM), F32)],
        compiler_params=_cparams(("arbitrary", "arbitrary", "arbitrary")),
        name="selattn",
    )(proj, proj, proj, sel)


def _winattn_kernel(*refs, tq, nblk):
    q_ref = refs[0]
    k_refs = refs[1:1 + nblk]
    v_refs = refs[1 + nblk:1 + 2 * nblk]
    ocmp_ref, osel_ref, small_ref, o_ref = refs[1 + 2 * nblk:]
    g = NSA_GROUP
    hk = pl.program_id(1)
    qi = pl.program_id(2)
    scale = HEAD_DIM ** -0.5
    nk = nblk * tq
    keys = jnp.concatenate([r[...] for r in k_refs], axis=0).astype(BF16)
    vals = jnp.concatenate([r[...] for r in v_refs], axis=0).astype(BF16)
    q4 = jnp.concatenate([q_ref[:, i * HEAD_DIM:(i + 1) * HEAD_DIM] for i in range(g)], axis=0)
    s = (_dot_nt(q4.astype(BF16), keys) * scale).reshape(g, tq, nk)
    pos = qi * tq + lax.broadcasted_iota(jnp.int32, (tq, nk), 0)
    wpos = (qi - (nblk - 1)) * tq + lax.broadcasted_iota(jnp.int32, (tq, nk), 1)
    mask = ((wpos <= pos) & (wpos > pos - WINDOW) & (wpos >= 0))[None]
    s = jnp.where(mask, s, NEG_BIG)
    e = jnp.exp(s - jnp.max(s, axis=-1, keepdims=True)).astype(BF16)
    ov = _dot(e.reshape(g * tq, nk), jnp.concatenate([vals, jnp.ones_like(vals)], axis=1))
    o_win = (ov[:, 0:HEAD_DIM] * (1.0 / jnp.maximum(ov[:, HEAD_DIM:2 * HEAD_DIM], 1e-30))).reshape(g, tq, HEAD_DIM)
    small = small_ref[...]
    for i in range(g):
        head = hk * g + i
        gates = [jax.nn.sigmoid(_lane_pick(small, SMALL_NSA_GATE + j * NSA_HEADS + head)) for j in range(3)]
        cols = slice(i * HEAD_DIM, (i + 1) * HEAD_DIM)
        o = gates[0] * ocmp_ref[:, cols] + gates[1] * osel_ref[:, cols] + gates[2] * o_win[i]
        o_ref[:, cols] = o.astype(o_ref.dtype)


def _winattn(proj, o_cmp, o_sel, tq=2 * Q_BLOCK):
    bsz, t, _ = proj.shape
    nblk = WINDOW // tq + 1
    gw = NSA_GROUP * HEAD_DIM

    def kv_spec(base, d):
        return pl.BlockSpec((None, tq, HEAD_DIM),
                            lambda b, h, i: (b, jnp.maximum(i - (nblk - 1) + d, 0), base + h))

    grp = pl.BlockSpec((None, tq, gw), lambda b, h, i: (b, i, h))
    return pl.pallas_call(
        functools.partial(_winattn_kernel, tq=tq, nblk=nblk),
        grid=(bsz, NSA_KV_HEADS, t // tq),
        in_specs=(
            [pl.BlockSpec((None, tq, gw), lambda b, h, i: (b, i, COL_NSA_Q // NSA_GROUP + h))]
            + [kv_spec(COL_KV + 4 * NSA_KV_HEADS, d) for d in range(nblk)]
            + [kv_spec(COL_KV + 5 * NSA_KV_HEADS, d) for d in range(nblk)]
            + [grp, grp, pl.BlockSpec((None, tq, HEAD_DIM), lambda b, h, i: (b, i, COL_SMALL))]
        ),
        out_specs=grp,
        out_shape=jax.ShapeDtypeStruct((bsz, t, NSA_WIDTH), BF16),
        compiler_params=_cparams(("arbitrary", "arbitrary", "arbitrary")),
        name="winattn",
    )(proj, *([proj] * (2 * nblk)), o_cmp, o_sel, proj)


def _ffn(h, wg, wu, wd):
    act = _silu(_dot(h, wg)) * _dot(h, wu)
    return _dot(act.astype(BF16), wd)


def _pack_bf16_pairs(x):
    half = x.shape[1] // 2
    lo = pltpu.bitcast(x[:, :half].astype(BF16).astype(F32), jnp.uint32)
    hi = pltpu.bitcast(x[:, half:].astype(BF16).astype(F32), jnp.uint32)
    return hi | (lo >> 16)


def _unpack_bf16_pairs(p):
    return pltpu.bitcast(p << 16, F32), pltpu.bitcast(p & jnp.uint32(0xFFFF0000), F32)


def _store_token_rows(ref, x, first=0):
    n, d = x.shape
    s_tiles = d // HEAD_DIM
    for s in range(s_tiles):
        ref[pl.ds(first * s_tiles + s, n, stride=s_tiles), :] = x[:, s * HEAD_DIM:(s + 1) * HEAD_DIM]


def _load_token_rows(ref, n, d, first=0):
    s_tiles = d // HEAD_DIM
    return jnp.concatenate([ref[pl.ds(first * s_tiles + s, n, stride=s_tiles), :] for s in range(s_tiles)], axis=1)


def _outproj_kernel(yg_ref, yn_ref, x_ref, wo_ref, gate_ref, shift_ref, scale_ref, gate_m_ref, nw_ref,
                    rw_ref, rb_ref, sg_ref, su_ref, sd_ref, x1_ref, h_ref, route_ref, count_ref, carry_ref):
    tm = x_ref.shape[0]
    half = yg_ref.shape[1]

    @pl.when((pl.program_id(0) == 0) & (pl.program_id(1) == 0))
    def _():
        carry_ref[...] = jnp.zeros_like(carry_ref)

    y = _dot(yg_ref[...], wo_ref[0:half, :]) + _dot(yn_ref[...], wo_ref[half:2 * half, :])
    x1 = x_ref[...] + gate_ref[...] * y
    h = _modulated_norm(x1, nw_ref[...], scale_ref[...], shift_ref[...])
    _store_token_rows(h_ref, _pack_bf16_pairs(h))
    x1_ref[...] = x1 + gate_m_ref[...] * _ffn(h.astype(BF16), sg_ref[...], su_ref[...], sd_ref[...])
    scores = jax.nn.sigmoid(_dot_split(h, rw_ref[...]))
    n_exp = scores.shape[1]
    lane = lax.broadcasted_iota(jnp.int32, scores.shape, 1)
    cand = scores + rb_ref[...]
    picked = jnp.zeros(scores.shape, jnp.bool_)
    firsts = []
    for _ in range(TOP_K):
        best = jnp.max(cand, axis=-1, keepdims=True)
        first = jnp.min(jnp.where(cand == best, lane, n_exp), axis=-1, keepdims=True)
        hit = lane == first
        picked = picked | hit
        cand = jnp.where(hit, NEG_BIG, cand)
        firsts.append(first)
    norm = ROUTED_SCALE / jnp.sum(jnp.where(picked, scores, 0.0), axis=-1, keepdims=True)
    onehot = jnp.where(picked, 1.0, 0.0)
    ri = lax.broadcasted_iota(jnp.int32, (tm, tm), 0)
    ci = lax.broadcasted_iota(jnp.int32, (tm, tm), 1)
    before = jnp.where(ci < ri, 1.0, 0.0).astype(BF16)
    pos = _dot(before, onehot.astype(BF16)) + carry_ref[...]
    carry_ref[...] += jnp.sum(onehot, axis=0, keepdims=True)
    count_ref[...] = carry_ref[...]
    slot = lax.broadcasted_iota(jnp.int32, (tm, 3 * TOP_K), 1)
    route = jnp.zeros((tm, 3 * TOP_K), F32)
    for j, first in enumerate(firsts):
        hit = lane == first
        route = jnp.where(slot == j, first.astype(F32), route)
        route = jnp.where(slot == TOP_K + j, _lane_pick(scores, first) * norm, route)
        route = jnp.where(slot == 2 * TOP_K + j, jnp.sum(jnp.where(hit, pos, 0.0), axis=-1, keepdims=True), route)
    route_ref[...] = route


def _outproj(y_gdn, y_nsa, x, w_out_b, mod4, norm_w, router_w, router_bias, sg, su, sd, tm=256):
    bsz, t, d = x.shape
    half = y_gdn.shape[2]
    n_exp = router_w.shape[1]
    sff = sg.shape[1]
    s_tiles = d // 2 // HEAD_DIM

    def mod(j):
        return pl.BlockSpec((None, None, 1, d), lambda b, i: (b, j, 0, 0))

    def const(shape):
        return pl.BlockSpec(shape, lambda b, i: (0, 0))

    row = pl.BlockSpec((None, tm, d), lambda b, i: (b, i, 0))
    return pl.pallas_call(
        _outproj_kernel,
        grid=(bsz, t // tm),
        in_specs=[
            pl.BlockSpec((None, tm, half), lambda b, i: (b, i, 0)),
            pl.BlockSpec((None, tm, half), lambda b, i: (b, i, 0)),
            row,
            const((2 * half, d)),
            mod(2), mod(3), mod(4), mod(5),
            const((1, d)), const((d, n_exp)), const((1, n_exp)),
            const((d, sff)), const((d, sff)), const((sff, d)),
        ],
        out_specs=[row,
                   pl.BlockSpec((tm * s_tiles, HEAD_DIM), lambda b, i: (b * (t // tm) + i, 0)),
                   pl.BlockSpec((None, tm, 3 * TOP_K), lambda b, i: (b, i, 0)),
                   const((1, n_exp))],
        out_shape=[
            jax.ShapeDtypeStruct((bsz, t, d), F32),
            jax.ShapeDtypeStruct((bsz * t * s_tiles, HEAD_DIM), jnp.uint32),
            jax.ShapeDtypeStruct((bsz, t, 3 * TOP_K), F32),
            jax.ShapeDtypeStruct((1, n_exp), F32),
        ],
        scratch_shapes=[pltpu.VMEM((1, n_exp), F32)],
        compiler_params=_cparams(("arbitrary", "arbitrary")),
        name="outproj",
    )(y_gdn, y_nsa, x, w_out_b, mod4, mod4, mod4, mod4, norm_w.reshape(1, d), router_w,
      router_bias.reshape(1, n_exp), sg, su, sd)


MOE_TM = 256
MOE_TOK = 256
MOE_TILES_PER_STEP = 4


def _token_row(ref, r, s_tiles):
    return ref.at[pl.ds(pl.multiple_of(r * s_tiles, s_tiles), s_tiles), :]


def _for_each(n, fn, unroll=8):
    def body(i, carry):
        fn(i)
        return carry

    lax.fori_loop(0, n, body, 0, unroll=unroll)


def _moe_scatter_kernel(gstart_ref, gend_ref, idx_ref, pos_ref, h_ref, xs_hbm, zero_ref, sem, *, s_tiles):
    n_exp = gstart_ref.shape[0]

    @pl.when(pl.program_id(0) == 0)
    def _():
        zero_ref[...] = jnp.zeros_like(zero_ref)

        def tail(e):
            first = pl.multiple_of((gend_ref[e] - MOE_TM) * s_tiles, MOE_TM * s_tiles)
            return pltpu.make_async_copy(zero_ref, xs_hbm.at[pl.ds(first, MOE_TM * s_tiles), :], sem)

        def start_tail(e):
            @pl.when(gend_ref[e] > gstart_ref[e])
            def _():
                tail(e).start()

        def wait_tail(e):
            @pl.when(gend_ref[e] > gstart_ref[e])
            def _():
                tail(e).wait()

        _for_each(n_exp, start_tail)
        _for_each(n_exp, wait_tail)

    def row_copy(t, j):
        n = t * TOP_K + j
        row = gstart_ref[idx_ref[n]] + pos_ref[n]
        return pltpu.make_async_copy(_token_row(h_ref, t, s_tiles), _token_row(xs_hbm, row, s_tiles), sem)

    def start_token(t):
        for j in range(TOP_K):
            row_copy(t, j).start()

    def wait_token(t):
        for j in range(TOP_K):
            row_copy(t, j).wait()

    _for_each(MOE_TOK, start_token, unroll=2)
    _for_each(MOE_TOK, wait_token, unroll=2)


def _moe_scatter(h_rows, idx, pos, gstart, gend, n_rows, s_tiles):
    n_steps = idx.shape[0] // (MOE_TOK * TOP_K)
    smem = pl.BlockSpec((MOE_TOK * TOP_K,), lambda i, gs, ge: (i,), memory_space=pltpu.SMEM)
    return pl.pallas_call(
        functools.partial(_moe_scatter_kernel, s_tiles=s_tiles),
        grid_spec=pltpu.PrefetchScalarGridSpec(
            num_scalar_prefetch=2,
            grid=(n_steps,),
            in_specs=[smem, smem, pl.BlockSpec((MOE_TOK * s_tiles, HEAD_DIM), lambda i, gs, ge: (i, 0))],
            out_specs=pl.BlockSpec(memory_space=pl.ANY),
            scratch_shapes=[pltpu.VMEM((MOE_TM * s_tiles, HEAD_DIM), h_rows.dtype), pltpu.SemaphoreType.DMA(())],
        ),
        out_shape=jax.ShapeDtypeStruct((n_rows * s_tiles, HEAD_DIM), h_rows.dtype),
        compiler_params=_cparams(("arbitrary",)),
        name="moe_scatter",
    )(gstart, gend, idx, pos, h_rows)


def _moe_expert_kernel(texp_ref, tslot_ref, tnext_ref, nvalid_ref, xs_ref, wg_hbm, wu_hbm, wd_hbm, ys_ref,
                       wg_buf, wu_buf, wd_buf, wgb_ref, wub_ref, wdb_ref, sems):
    d = wgb_ref.shape[0]
    half = d // 2
    rows_per_tile = xs_ref.shape[0] // MOE_TILES_PER_STEP

    def fetch(expert, s):
        return [pltpu.make_async_copy(src.at[expert], dst.at[s], sems.at[s, n])
                for n, (src, dst) in enumerate(((wg_hbm, wg_buf), (wu_hbm, wu_buf), (wd_hbm, wd_buf)))]

    def tile(i, sub):
        valid = i < nvalid_ref[0]
        new_expert = (i == 0) | (texp_ref[i] != texp_ref[jnp.maximum(i - 1, 0)])
        slot = tslot_ref[i]

        @pl.when(i == 0)
        def _():
            for cp in fetch(texp_ref[0], slot):
                cp.start()

        @pl.when(valid & new_expert)
        def _():
            nxt = tnext_ref[i]

            @pl.when(nxt < nvalid_ref[0])
            def _():
                for cp in fetch(texp_ref[nxt], 1 - slot):
                    cp.start()

            for cp in fetch(texp_ref[i], slot):
                cp.wait()
            wgb_ref[...] = wg_buf[slot].astype(BF16)
            wub_ref[...] = wu_buf[slot].astype(BF16)
            wdb_ref[...] = wd_buf[slot].astype(BF16)

        @pl.when(valid)
        def _():
            lo, hi = _unpack_bf16_pairs(_load_token_rows(xs_ref, MOE_TM, half, first=sub * MOE_TM))
            lo, hi = lo.astype(BF16), hi.astype(BF16)
            gate = _dot(lo, wgb_ref[0:half, :]) + _dot(hi, wgb_ref[half:d, :])
            up = _dot(lo, wub_ref[0:half, :]) + _dot(hi, wub_ref[half:d, :])
            y = _dot((_silu(gate) * up).astype(BF16), wdb_ref[...])
            _store_token_rows(ys_ref, _pack_bf16_pairs(y), first=sub * MOE_TM)

        @pl.when(jnp.logical_not(valid))
        def _():
            ys_ref[sub * rows_per_tile:(sub + 1) * rows_per_tile, :] = jnp.zeros((rows_per_tile, HEAD_DIM), ys_ref.dtype)

    for sub in range(MOE_TILES_PER_STEP):
        tile(pl.program_id(0) * MOE_TILES_PER_STEP + sub, sub)


def _moe_experts(xs, tile_expert, tile_slot, tile_next, nvalid, wg, wu, wd):
    n_exp, d, ff = wg.shape
    n_steps = tile_expert.shape[0] // MOE_TILES_PER_STEP
    blk = xs.shape[0] // n_steps
    hbm = pl.BlockSpec(memory_space=pl.ANY)
    vmem_bytes = 3 * d * ff * (2 * 4 + 2) + 4 * blk * HEAD_DIM * 4 + 3 * MOE_TM * d * 4
    return pl.pallas_call(
        _moe_expert_kernel,
        grid_spec=pltpu.PrefetchScalarGridSpec(
            num_scalar_prefetch=4,
            grid=(n_steps,),
            in_specs=[
                pl.BlockSpec((blk, HEAD_DIM),
                             lambda i, te, ts, tn, nv: (jnp.minimum(i, (nv[0] - 1) // MOE_TILES_PER_STEP), 0)),
                hbm, hbm, hbm,
            ],
            out_specs=pl.BlockSpec((blk, HEAD_DIM), lambda i, te, ts, tn, nv: (i, 0)),
            scratch_shapes=[pltpu.VMEM((2, d, ff), wg.dtype), pltpu.VMEM((2, d, ff), wu.dtype),
                            pltpu.VMEM((2, ff, d), wd.dtype),
                            pltpu.VMEM((d, ff), BF16), pltpu.VMEM((d, ff), BF16), pltpu.VMEM((ff, d), BF16),
                            pltpu.SemaphoreType.DMA((2, 3))],
        ),
        out_shape=jax.ShapeDtypeStruct(xs.shape, xs.dtype),
        compiler_params=_cparams(("arbitrary",), vmem_bytes),
        name="moe_experts",
    )(tile_expert, tile_slot, tile_next, nvalid, xs, wg, wu, wd)


def _moe_combine_kernel(gstart_ref, idx_ref, pos_ref, w_ref, ys_hbm, x1_ref, gate_ref, nw_ref, o_ref,
                        buf_ref, lo_ref, hi_ref, sems, *, s_tiles):
    tm, d = x1_ref.shape
    n_parts = sems.shape[0]
    part_tok = tm // n_parts
    part_rows = part_tok * TOP_K

    def row_copy(n, sem):
        row = gstart_ref[idx_ref[n]] + pos_ref[n]
        return pltpu.make_async_copy(_token_row(ys_hbm, row, s_tiles), _token_row(buf_ref, n, s_tiles), sem)

    def token(t):
        n0 = t * TOP_K
        lo, hi = _unpack_bf16_pairs(_token_row(buf_ref, n0, s_tiles)[...])
        acc_lo, acc_hi = w_ref[n0] * lo, w_ref[n0] * hi
        for j in range(1, TOP_K):
            lo, hi = _unpack_bf16_pairs(_token_row(buf_ref, n0 + j, s_tiles)[...])
            acc_lo, acc_hi = acc_lo + w_ref[n0 + j] * lo, acc_hi + w_ref[n0 + j] * hi
        _token_row(lo_ref, t, s_tiles)[...] = acc_lo
        _token_row(hi_ref, t, s_tiles)[...] = acc_hi

    for part in range(n_parts):
        _for_each(part_rows, lambda n, part=part: row_copy(part * part_rows + n, sems.at[part]).start())
    for part in range(n_parts):
        _for_each(part_rows, lambda n, part=part: row_copy(part * part_rows + n, sems.at[part]).wait())
        _for_each(part_tok, lambda t, part=part: token(part * part_tok + t))
    routed = jnp.concatenate([_load_token_rows(lo_ref, tm, d // 2), _load_token_rows(hi_ref, tm, d // 2)], axis=1)
    x2 = x1_ref[...] + gate_ref[...] * routed
    o_ref[...] = x2 * lax.rsqrt(jnp.mean(x2 * x2, axis=-1, keepdims=True) + EPS) * nw_ref[...]


def _moe_combine(ys, idx, pos, w, gstart, x1, mod4, norm_w):
    bsz, t, d = x1.shape
    s_tiles = d // 2 // HEAD_DIM
    tm = MOE_TOK
    nt = t // tm
    smem = pl.BlockSpec((tm * TOP_K,), lambda b, i, gs: (b * nt + i,), memory_space=pltpu.SMEM)
    row = pl.BlockSpec((None, tm, d), lambda b, i, gs: (b, i, 0))
    return pl.pallas_call(
        functools.partial(_moe_combine_kernel, s_tiles=s_tiles),
        grid_spec=pltpu.PrefetchScalarGridSpec(
            num_scalar_prefetch=1,
            grid=(bsz, nt),
            in_specs=[
                smem, smem, smem,
                pl.BlockSpec(memory_space=pl.ANY),
                row,
                pl.BlockSpec((None, None, 1, d), lambda b, i, gs: (b, 5, 0, 0)),
                pl.BlockSpec((1, d), lambda b, i, gs: (0, 0)),
            ],
            out_specs=row,
            scratch_shapes=[pltpu.VMEM((tm * TOP_K * s_tiles, HEAD_DIM), ys.dtype),
                            pltpu.VMEM((tm * s_tiles, HEAD_DIM), F32), pltpu.VMEM((tm * s_tiles, HEAD_DIM), F32),
                            pltpu.SemaphoreType.DMA((4,))],
        ),
        out_shape=jax.ShapeDtypeStruct((bsz, t, d), F32),
        compiler_params=_cparams(("arbitrary", "arbitrary")),
        name="moe_combine",
    )(gstart, idx, pos, w, ys, x1, mod4, norm_w.reshape(1, d))


def _moe(h_rows, route, counts, x1, mod4, norm_w, wg, wu, wd):
    bsz, t, d = x1.shape
    n_exp = wg.shape[0]
    s_tiles = h_rows.shape[0] // (bsz * t)
    idx = route[..., 0:TOP_K].astype(jnp.int32).reshape(-1)
    w = route[..., TOP_K:2 * TOP_K].reshape(-1)
    pos = route[..., 2 * TOP_K:3 * TOP_K].astype(jnp.int32).reshape(-1)
    padded = (counts.reshape(n_exp).astype(jnp.int32) + MOE_TM - 1) // MOE_TM * MOE_TM
    gend = jnp.cumsum(padded)
    gstart = gend - padded
    nvalid = gend[-1:] // MOE_TM
    n_tiles = bsz * t * TOP_K // MOE_TM + n_exp
    tile_start = jnp.minimum(jnp.arange(n_tiles, dtype=jnp.int32), nvalid - 1) * MOE_TM
    tile_expert = jnp.sum((gend[None, :] <= tile_start[:, None]).astype(jnp.int32), axis=1)
    first_of_group = jnp.concatenate([jnp.ones((1,), jnp.int32),
                                      (tile_expert[1:] != tile_expert[:-1]).astype(jnp.int32)])
    tile_slot = (jnp.cumsum(first_of_group) - 1) % 2
    experts = jnp.arange(n_exp, dtype=jnp.int32)
    tile_next = jnp.sum(jnp.where(tile_expert[:, None] == experts[None, :], gend[None, :], 0), axis=1) // MOE_TM
    xs = _moe_scatter(h_rows, idx, pos, gstart, gend, n_tiles * MOE_TM, s_tiles)
    ys = _moe_experts(xs, tile_expert, tile_slot, tile_next, nvalid, wg, wu, wd)
    return _moe_combine(ys, idx, pos, w, gstart, x1, mod4, norm_w)


def _reorder_w_in(w_in):
    d = w_in.shape[0]
    o_beta = 3 * GDN_WIDTH
    o_a = o_beta + GDN_HEADS
    o_gate = o_a + GDN_HEADS
    o_q = o_gate + GDN_WIDTH
    o_kv = o_q + NSA_WIDTH
    o_ng = o_kv + 6 * NSA_KV_WIDTH
    n_small = 2 * GDN_HEADS + 3 * NSA_HEADS
    parts = [w_in[:, :o_beta], w_in[:, o_gate:o_q], w_in[:, o_q:o_kv], w_in[:, o_kv:o_ng],
             w_in[:, o_beta:o_gate], w_in[:, o_ng:o_ng + 3 * NSA_HEADS]]
    used = COL_SMALL * HEAD_DIM + n_small
    parts.append(jnp.zeros((d, PROJ_COLS - used), w_in.dtype))
    return jnp.concatenate([p.astype(BF16) for p in parts], axis=1)


def _layer(x, mod, norm_attn_w, norm_ffn_w, w_in, gdn_conv_w, gdn_a_log, gdn_dt_bias, gdn_norm_w,
           cmp_pe_k, cmp_w1_k, cmp_w2_k, cmp_pe_v, cmp_w1_v, cmp_w2_v, w_out, router_w, router_bias,
           expert_w_gate, expert_w_up, expert_w_down, shared_w_gate, shared_w_up, shared_w_down, final_w):
    bsz, t, d = x.shape
    mod4 = mod.reshape(bsz, N_MOD, 1, d)
    proj = _proj(x, mod4, norm_attn_w, _reorder_w_in(w_in))
    y_gdn = _gdn(proj, gdn_conv_w, gdn_a_log, gdn_dt_bias, gdn_norm_w)
    kc = _compress(proj, COL_KV, cmp_pe_k, cmp_w1_k, cmp_w2_k)
    vc = _compress(proj, COL_KV + NSA_KV_HEADS, cmp_pe_v, cmp_w1_v, cmp_w2_v)
    o_cmp, sel = _cmpattn(proj, kc, vc)
    o_sel = _selattn(proj, sel)
    y_nsa = _winattn(proj, o_cmp, o_sel)
    x1, h_rows, route, counts = _outproj(
        y_gdn, y_nsa, x, w_out.astype(BF16), mod4, norm_ffn_w, router_w, router_bias,
        shared_w_gate.astype(BF16), shared_w_up.astype(BF16), shared_w_down.astype(BF16))
    return _moe(h_rows, route, counts, x1, mod4, final_w, expert_w_gate, expert_w_up, expert_w_down)


def kernel(x, c, w_ada, b_ada, norm_attn_w, norm_ffn_w, norm_final_w, w_in, gdn_conv_w, gdn_a_log, gdn_dt_bias,
           gdn_norm_w, cmp_pe_k, cmp_w1_k, cmp_w2_k, cmp_pe_v, cmp_w1_v, cmp_w2_v, w_out, router_w, router_bias,
           expert_w_gate, expert_w_up, expert_w_down, shared_w_gate, shared_w_up, shared_w_down):
    depth = w_ada.shape[0]
    assert depth == 1, "the fused final norm assumes a single layer"
    l = 0
    mod = _ada(c, w_ada[l], b_ada[l])
    return _layer(x, mod, norm_attn_w[l], norm_ffn_w[l], w_in[l], gdn_conv_w[l], gdn_a_log[l], gdn_dt_bias[l],
                  gdn_norm_w[l], cmp_pe_k[l], cmp_w1_k[l], cmp_w2_k[l], cmp_pe_v[l], cmp_w1_v[l], cmp_w2_v[l],
                  w_out[l], router_w[l], router_bias[l], expert_w_gate[l], expert_w_up[l], expert_w_down[l],
                  shared_w_gate[l], shared_w_up[l], shared_w_down[l], norm_final_w)
```

```python
import functools

import jax
import jax.numpy as jnp
from jax import lax
from jax.experimental import pallas as pl
from jax.experimental.pallas import tpu as pltpu

F32 = jnp.float32
BF16 = jnp.bfloat16
HI = lax.Precision.HIGHEST

HEAD_DIM = 128
GDN_HEADS = 8
NSA_HEADS = 8
NSA_KV_HEADS = 2
NSA_GROUP = NSA_HEADS // NSA_KV_HEADS
GDN_WIDTH = GDN_HEADS * HEAD_DIM
NSA_WIDTH = NSA_HEADS * HEAD_DIM
NSA_KV_WIDTH = NSA_KV_HEADS * HEAD_DIM
CONV_WIDTH = 4
GDN_CHUNK = 64
CMP_BLOCK = 32
CMP_STRIDE = 16
SEL_BLOCK = 64
SEL_TOPK = 16
N_LOCAL_BLOCKS = 2
WINDOW = 512
Q_BLOCK = 128
TOP_K = 8
ROUTED_SCALE = 2.5
N_MOD = 6
EPS = 1e-6

COL_GDN_Q = 0
COL_GDN_K = GDN_HEADS
COL_GDN_V = 2 * GDN_HEADS
COL_GDN_GATE = 3 * GDN_HEADS
COL_NSA_Q = 4 * GDN_HEADS
COL_KV = COL_NSA_Q + NSA_HEADS
COL_SMALL = COL_KV + 6 * NSA_KV_HEADS
N_COL_BLOCKS = COL_SMALL + 1
PROJ_TN = 2304
PROJ_COLS = 3 * PROJ_TN
SMALL_BETA = 0
SMALL_A = GDN_HEADS
SMALL_NSA_GATE = 2 * GDN_HEADS

NEG_BIG = -1e30
VMEM_LIMIT = 48 * 1024 * 1024


def _cparams(semantics, vmem_limit=VMEM_LIMIT):
    return pltpu.CompilerParams(dimension_semantics=semantics, vmem_limit_bytes=vmem_limit)


def _silu(x):
    return (0.5 * x) * (1.0 + jnp.tanh(0.5 * x))


def _dot(a, b, precision=None):
    return jnp.dot(a, b, preferred_element_type=F32, precision=precision)


def _dot_nt(a, b, precision=None):
    return lax.dot_general(a, b, (((1,), (1,)), ((), ())), preferred_element_type=F32, precision=precision)


def _dot_tn(a, b, precision=None):
    return lax.dot_general(a, b, (((0,), (0,)), ((), ())), preferred_element_type=F32, precision=precision)


def _split_bf16(x):
    hi = x.astype(BF16)
    return hi, (x - hi.astype(F32)).astype(BF16)


def _dot_split(a, b, nt=False):
    (ah, al), (bh, bl) = _split_bf16(a), _split_bf16(b)
    f = _dot_nt if nt else _dot
    return f(ah, bh) + (f(ah, bl) + f(al, bh))


def _lane_pick(x, idx):
    lane = lax.broadcasted_iota(jnp.int32, x.shape, 1)
    return jnp.sum(jnp.where(lane == idx, x, 0.0), axis=-1, keepdims=True)


def _ada_kernel(ct_ref, w_ref, b_ref, o_ref):
    c = ct_ref[...]
    cond = _silu(c)
    w = w_ref[...]
    rows = [jnp.sum(w * cond[:, b:b + 1], axis=0, keepdims=True) for b in range(c.shape[1])]
    o_ref[...] = jnp.concatenate(rows, axis=0) + b_ref[...]


def _ada(c, w_ada, b_ada, tn=512):
    bsz, d = c.shape
    n = w_ada.shape[1]
    return pl.pallas_call(
        _ada_kernel,
        grid=(n // tn,),
        in_specs=[
            pl.BlockSpec((d, bsz), lambda j: (0, 0)),
            pl.BlockSpec((d, tn), lambda j: (0, j)),
            pl.BlockSpec((1, tn), lambda j: (0, j)),
        ],
        out_specs=pl.BlockSpec((bsz, tn), lambda j: (0, j)),
        out_shape=jax.ShapeDtypeStruct((bsz, n), F32),
        compiler_params=_cparams(("arbitrary",)),
        name="ada",
    )(c.T, w_ada, b_ada.reshape(1, n))


def _modulated_norm(x, nw, scale, shift):
    var = jnp.mean(x * x, axis=-1, keepdims=True)
    h = x * lax.rsqrt(var + EPS) * nw
    return h * (1.0 + scale) + shift


def _proj_kernel(x_ref, shift_ref, scale_ref, nw_ref, w_ref, o_ref):
    h = _modulated_norm(x_ref[...], nw_ref[...], scale_ref[...], shift_ref[...])
    o_ref[...] = _dot(h.astype(BF16), w_ref[...])


def _proj(x, mod4, norm_w, w_r, tm=512):
    bsz, t, d = x.shape
    n = w_r.shape[1]
    return pl.pallas_call(
        _proj_kernel,
        grid=(n // PROJ_TN, bsz, t // tm),
        in_specs=[
            pl.BlockSpec((None, tm, d), lambda j, b, i: (b, i, 0)),
            pl.BlockSpec((None, None, 1, d), lambda j, b, i: (b, 0, 0, 0)),
            pl.BlockSpec((None, None, 1, d), lambda j, b, i: (b, 1, 0, 0)),
            pl.BlockSpec((1, d), lambda j, b, i: (0, 0)),
            pl.BlockSpec((d, PROJ_TN), lambda j, b, i: (0, j)),
        ],
        out_specs=pl.BlockSpec((None, tm, PROJ_TN), lambda j, b, i: (b, i, j)),
        out_shape=jax.ShapeDtypeStruct((bsz, t, n), F32),
        compiler_params=_cparams(("arbitrary", "arbitrary", "arbitrary")),
        name="proj",
    )(x, mod4, mod4, norm_w.reshape(1, d), w_r)


def _gdn_kernel(q_ref, k_ref, v_ref, gate_ref, small_ref, hq_ref, hk_ref, hv_ref,
                cwq_ref, cwk_ref, cwv_ref, alog_ref, dtb_ref, nw_ref, o_ref, s_ref, xc_ref, *, tt, hg):
    first = pl.program_id(2) == 0

    @pl.when(first)
    def _():
        s_ref[...] = jnp.zeros_like(s_ref)

    def conv_silu(x_ref, halo_ref, cw_ref):
        xc_ref[0:8, :] = jnp.where(first, 0.0, halo_ref[...])
        xc_ref[8:tt + 8, :] = x_ref[...]
        cw = cw_ref[...]
        off = 8 - (CONV_WIDTH - 1)
        y = cw[0:1, :] * xc_ref[off:off + tt, :]
        for i in range(1, CONV_WIDTH):
            y = y + cw[i:i + 1, :] * xc_ref[off + i:off + i + tt, :]
        return _silu(y)

    q_all = conv_silu(q_ref, hq_ref, cwq_ref)
    k_all = conv_silu(k_ref, hk_ref, cwk_ref)
    v_all = conv_silu(v_ref, hv_ref, cwv_ref)
    ri = lax.broadcasted_iota(jnp.int32, (tt, tt), 0)
    ci = lax.broadcasted_iota(jnp.int32, (tt, tt), 1)
    lower = jnp.where(((ri // GDN_CHUNK) == (ci // GDN_CHUNK)) & (ci <= ri), 1.0, 0.0).astype(BF16)
    heads = [pl.program_id(1) * hg + hh for hh in range(hg)]
    outs = _gdn_heads(q_all, k_all, v_all, small_ref[...], lower, heads, alog_ref[...], dtb_ref[...], s_ref, tt)
    for hh, o in enumerate(outs):
        cols = slice(hh * HEAD_DIM, (hh + 1) * HEAD_DIM)
        o = o * lax.rsqrt(jnp.mean(o * o, axis=-1, keepdims=True) + EPS) * nw_ref[...]
        o_ref[:, cols] = (o * _silu(gate_ref[:, cols])).astype(o_ref.dtype)


def _gdn_heads(q_all, k_all, v_all, small, lower, heads, alog, dtb, s_ref, tt):
    c = GDN_CHUNK
    nch = tt // c
    hg = len(heads)
    kpos = lax.broadcasted_iota(jnp.int32, (tt, c), 0) % c
    jj = lax.broadcasted_iota(jnp.int32, (tt, c), 1)
    per_head = []
    for hh, h in enumerate(heads):
        cols = slice(hh * HEAD_DIM, (hh + 1) * HEAD_DIM)
        q, k, v = q_all[:, cols], k_all[:, cols], v_all[:, cols]
        q = q * lax.rsqrt(jnp.sum(q * q, axis=-1, keepdims=True) + EPS) * (HEAD_DIM ** -0.5)
        k = k * lax.rsqrt(jnp.sum(k * k, axis=-1, keepdims=True) + EPS)
        beta = jax.nn.sigmoid(_lane_pick(small, SMALL_BETA + h))
        z = _lane_pick(small, SMALL_A + h) + _lane_pick(dtb, h)
        softplus = jnp.maximum(z, 0.0) + jnp.log(1.0 + jnp.exp(-jnp.abs(z)))
        g = -jnp.exp(_lane_pick(alog, h)) * softplus
        g_c = jnp.broadcast_to(g, (tt, c))
        rhs = jnp.concatenate([g_c, jnp.where(kpos > jj, g_c, 0.0)], axis=1)
        r_hi = rhs.astype(BF16)
        rem = rhs - r_hi.astype(F32)
        r_mid = rem.astype(BF16)
        r_lo = (rem - r_mid.astype(F32)).astype(BF16)
        sums = _dot(lower, r_hi) + (_dot(lower, r_mid) + _dot(lower, r_lo))
        gcum = jnp.broadcast_to(sums[:, 0:1], (tt, HEAD_DIM))
        gc3 = gcum.reshape(nch, c, HEAD_DIM)
        grest = (gc3[:, c - 1:c, :] - gc3).reshape(tt, HEAD_DIM)
        decay_h = jnp.where(kpos >= jj, jnp.exp(sums[:, c:2 * c]), 0.0)
        per_head.append((q, k, v * beta, k * beta, jnp.exp(gcum), jnp.exp(grest), decay_h))

    def stack(j):
        x = jnp.concatenate([ph[j] for ph in per_head], axis=0)
        return x.reshape(hg * nch, c, x.shape[-1])

    q, k, vb, kb, egc, egr, decay = (stack(j) for j in range(7))
    kpos_all = lax.broadcasted_iota(jnp.int32, (hg * tt, c), 0) % c
    jj_all = lax.broadcasted_iota(jnp.int32, (hg * tt, c), 1)
    strict = (kpos_all > jj_all).reshape(hg * nch, c, c)
    eye = jnp.where(kpos_all == jj_all, 1.0, 0.0).reshape(hg * nch, c, c)

    def bmm(a, b):
        return jnp.einsum('nij,njk->nik', a, b, preferred_element_type=F32)

    def bmm_nt(a, b):
        return jnp.einsum('nid,njd->nij', a, b, preferred_element_type=F32)

    def bmm_split(a, b):
        (ah, al), (bh, bl) = _split_bf16(a), _split_bf16(b)
        return bmm(ah, bh) + (bmm(ah, bl) + bmm(al, bh))

    k3 = k.astype(BF16)
    a = jnp.where(strict, bmm_nt(kb.astype(BF16), k3) * decay, 0.0)
    inv = eye - a
    p = bmm_split(a, a)
    n_sq = (c - 1).bit_length() - 1
    for lvl in range(n_sq):
        inv = inv + bmm_split(inv, p)
        if lvl + 1 < n_sq:
            p = bmm_split(p, p)
    inv_b = inv.astype(BF16)
    u3 = bmm(inv_b, vb.astype(BF16)).astype(BF16)
    w3 = bmm(inv_b, (kb * egc).astype(BF16)).astype(BF16)
    qk3 = (bmm_nt(q.astype(BF16), k3) * decay).astype(BF16)
    kd3 = (k * egr).astype(BF16)
    qe3 = (q * egc - bmm(qk3, w3)).astype(BF16)
    o03 = bmm(qk3, u3)
    kw = [_dot_tn(kd3[i], w3[i]).astype(BF16) for i in range(hg * nch)]
    ku = [_dot_tn(kd3[i], u3[i]) for i in range(hg * nch)]

    states = [s_ref[hh] for hh in range(hg)]
    outs = [[] for _ in range(hg)]
    for n in range(nch):
        for hh in range(hg):
            i = hh * nch + n
            state_b = states[hh].astype(BF16)
            outs[hh].append(_dot(qe3[i], state_b) + o03[i])
            states[hh] = states[hh] * egc[i, c - 1:c, :] - _dot(kw[i], state_b) + ku[i]
    for hh in range(hg):
        s_ref[hh] = states[hh]
    return [jnp.concatenate(o, axis=0) for o in outs]


def _gdn(proj, conv_w, a_log, dt_bias, norm_w, tt=256, hg=8):
    bsz, t, _ = proj.shape
    wdt = hg * HEAD_DIM

    def col(base):
        return pl.BlockSpec((None, tt, wdt), lambda b, h, i: (b, i, base // hg + h))

    def halo(base):
        return pl.BlockSpec((None, 8, wdt), lambda b, h, i: (b, jnp.maximum(i * (tt // 8) - 1, 0), base // hg + h))

    def cw(base):
        return pl.BlockSpec((CONV_WIDTH, wdt), lambda b, h, i: (0, base // hg + h))

    heads = pl.BlockSpec((1, GDN_HEADS), lambda b, h, i: (0, 0))
    return pl.pallas_call(
        functools.partial(_gdn_kernel, tt=tt, hg=hg),
        grid=(bsz, GDN_HEADS // hg, t // tt),
        in_specs=[
            col(COL_GDN_Q), col(COL_GDN_K), col(COL_GDN_V), col(COL_GDN_GATE),
            pl.BlockSpec((None, tt, HEAD_DIM), lambda b, h, i: (b, i, COL_SMALL)),
            halo(COL_GDN_Q), halo(COL_GDN_K), halo(COL_GDN_V),
            cw(COL_GDN_Q), cw(COL_GDN_K), cw(COL_GDN_V),
            heads, heads,
            pl.BlockSpec((1, HEAD_DIM), lambda b, h, i: (0, 0)),
        ],
        out_specs=pl.BlockSpec((None, tt, wdt), lambda b, h, i: (b, i, h)),
        out_shape=jax.ShapeDtypeStruct((bsz, t, GDN_WIDTH), BF16),
        scratch_shapes=[pltpu.VMEM((hg, HEAD_DIM, HEAD_DIM), F32), pltpu.VMEM((tt + 8, wdt), F32)],
        compiler_params=_cparams(("arbitrary", "arbitrary", "arbitrary")),
        name="gdn",
    )(proj, proj, proj, proj, proj, proj, proj, proj, conv_w, conv_w, conv_w,
      a_log.reshape(1, GDN_HEADS), dt_bias.reshape(1, GDN_HEADS), norm_w.reshape(1, HEAD_DIM))


def _compress_kernel(x_ref, pe_ref, w1_ref, w2_ref, o_ref):
    r = x_ref.shape[0] // CMP_STRIDE
    d = x_ref.shape[1]
    a = b = None
    for l in range(CMP_STRIDE):
        x_l = x_ref[pl.ds(l, r, stride=CMP_STRIDE), :]
        a_l = _dot_split(x_l + pe_ref[l:l + 1, :], w1_ref[l * d:(l + 1) * d, :])
        b_l = _dot_split(x_l + pe_ref[CMP_STRIDE + l:CMP_STRIDE + l + 1, :],
                         w1_ref[(CMP_STRIDE + l) * d:(CMP_STRIDE + l + 1) * d, :])
        a, b = (a_l, b_l) if a is None else (a + a_l, b + b_l)
    hid = _silu(a + pltpu.roll(b, r - 1, 0))
    o_ref[...] = _dot_split(hid, w2_ref[...])


def _compress(proj, col_block, pe, w1, w2):
    bsz, t, _ = proj.shape
    r = t // CMP_STRIDE
    hid = w1.shape[1]
    return pl.pallas_call(
        _compress_kernel,
        grid=(bsz, NSA_KV_HEADS),
        in_specs=[
            pl.BlockSpec((None, t, HEAD_DIM), lambda b, h: (b, 0, col_block + h)),
            pl.BlockSpec((CMP_BLOCK, HEAD_DIM), lambda b, h: (0, 0)),
            pl.BlockSpec((CMP_BLOCK * HEAD_DIM, hid), lambda b, h: (0, 0)),
            pl.BlockSpec((hid, HEAD_DIM), lambda b, h: (0, 0)),
        ],
        out_specs=pl.BlockSpec((None, None, r, HEAD_DIM), lambda b, h: (b, h, 0, 0)),
        out_shape=jax.ShapeDtypeStruct((bsz, NSA_KV_HEADS, r, HEAD_DIM), F32),
        compiler_params=_cparams(("arbitrary", "arbitrary")),
        name="compress",
    )(proj, pe, w1, w2)


def _cmpattn_kernel(q_ref, kc_ref, vc_ref, o_ref, sel_ref, *, tq, nc, nsel):
    ncp = kc_ref.shape[0]
    scale = HEAD_DIM ** -0.5
    t0 = pl.program_id(2) * tq
    kc = kc_ref[...]
    vc = vc_ref[...]
    pos_r = t0 + lax.broadcasted_iota(jnp.int32, (tq, ncp), 0)
    c_r = lax.broadcasted_iota(jnp.int32, (tq, ncp), 1)
    valid_r = (c_r * CMP_STRIDE + CMP_BLOCK - 1 <= pos_r) & (c_r < nc)
    pos_c = t0 + lax.broadcasted_iota(jnp.int32, (ncp, tq), 1)
    c_c = lax.broadcasted_iota(jnp.int32, (ncp, tq), 0)
    valid_c = (c_c * CMP_STRIDE + CMP_BLOCK - 1 <= pos_c) & (c_c < nc)
    psum = jnp.zeros((ncp, tq), F32)
    kc_hi, kc_lo = _split_bf16(kc)
    vc_b = vc.astype(BF16)
    for g in range(NSA_GROUP):
        q_hi, q_lo = _split_bf16(q_ref[:, g * HEAD_DIM:(g + 1) * HEAD_DIM])
        s = _dot_nt(q_hi, kc_hi) + (_dot_nt(q_hi, kc_lo) + _dot_nt(q_lo, kc_hi))
        s = jnp.where(valid_r, s * scale, NEG_BIG)
        e = jnp.where(valid_r, jnp.exp(s - jnp.max(s, axis=-1, keepdims=True)), 0.0)
        p = e * (1.0 / jnp.maximum(jnp.sum(e, axis=-1, keepdims=True), 1e-30))
        o_ref[:, g * HEAD_DIM:(g + 1) * HEAD_DIM] = _dot(p.astype(BF16), vc_b)
        st = _dot_nt(kc_hi, q_hi) + (_dot_nt(kc_lo, q_hi) + _dot_nt(kc_hi, q_lo))
        st = jnp.where(valid_c, st * scale, NEG_BIG)
        et = jnp.where(valid_c, jnp.exp(st - jnp.max(st, axis=0, keepdims=True)), 0.0)
        psum = psum + et * (1.0 / jnp.maximum(jnp.sum(et, axis=0, keepdims=True), 1e-30))
    j_o = lax.broadcasted_iota(jnp.int32, (nsel, ncp), 0) * SEL_BLOCK
    c_o = lax.broadcasted_iota(jnp.int32, (nsel, ncp), 1) * CMP_STRIDE
    overlap = jnp.where((c_o < j_o + SEL_BLOCK) & (c_o + CMP_BLOCK > j_o), 1.0, 0.0).astype(BF16)
    p_hi, p_lo = _split_bf16(psum)
    imp = _dot(overlap, p_hi) + _dot(overlap, p_lo)
    blk = lax.broadcasted_iota(jnp.int32, (nsel, tq), 0)
    dist = (t0 + lax.broadcasted_iota(jnp.int32, (nsel, tq), 1)) // SEL_BLOCK - blk
    forced = (blk == 0) | ((dist >= 0) & (dist < N_LOCAL_BLOCKS))
    val = jnp.where(forced, -NEG_BIG, jnp.where(dist >= 0, imp, -1.0))
    rank = jnp.zeros((nsel, tq), F32)
    for j in range(nsel):
        row = val[j:j + 1, :]
        rank = rank + jnp.where((row > val) | ((row == val) & (blk > j)), 1.0, 0.0)
    sel = jnp.where((rank < min(SEL_TOPK, nsel)) & (dist >= 0), 1.0, 0.0)
    sel_ref[...] = sel.T


def _cmpattn(proj, kc, vc, tq=512):
    bsz, t, _ = proj.shape
    ncp = kc.shape[2]
    nc = (t - CMP_BLOCK) // CMP_STRIDE + 1
    nsel = t // SEL_BLOCK
    gw = NSA_GROUP * HEAD_DIM
    return pl.pallas_call(
        functools.partial(_cmpattn_kernel, tq=tq, nc=nc, nsel=nsel),
        grid=(bsz, NSA_KV_HEADS, t // tq),
        in_specs=[
            pl.BlockSpec((None, tq, gw), lambda b, h, i: (b, i, COL_NSA_Q // NSA_GROUP + h)),
            pl.BlockSpec((None, None, ncp, HEAD_DIM), lambda b, h, i: (b, h, 0, 0)),
            pl.BlockSpec((None, None, ncp, HEAD_DIM), lambda b, h, i: (b, h, 0, 0)),
        ],
        out_specs=[
            pl.BlockSpec((None, tq, gw), lambda b, h, i: (b, i, h)),
            pl.BlockSpec((None, None, tq, nsel), lambda b, h, i: (b, h, i, 0)),
        ],
        out_shape=[
            jax.ShapeDtypeStruct((bsz, t, NSA_WIDTH), F32),
            jax.ShapeDtypeStruct((bsz, NSA_KV_HEADS, t, nsel), F32),
        ],
        compiler_params=_cparams(("arbitrary", "arbitrary", "arbitrary")),
        name="cmpattn",
    )(proj, kc, vc)


def _selattn_kernel(q_ref, k_ref, v_ref, sel_ref, o_ref, m_ref, l_ref, acc_ref, *, tq, tk, nsel):
    g = NSA_GROUP
    qi = pl.program_id(2)
    m_ref[...] = jnp.full_like(m_ref, NEG_BIG)
    l_ref[...] = jnp.zeros_like(l_ref)
    acc_ref[...] = jnp.zeros_like(acc_ref)
    q4 = jnp.concatenate([q_ref[:, i * HEAD_DIM:(i + 1) * HEAD_DIM] for i in range(g)], axis=0)
    q4 = (q4 * HEAD_DIM ** -0.5).astype(BF16)
    sel = sel_ref[...].astype(BF16)
    pos = qi * tq + lax.broadcasted_iota(jnp.int32, (tq, tk), 0)

    def kv_step(kv, carry):
        k0 = pl.multiple_of(kv * tk, tk)
        s = _dot_nt(q4, k_ref[pl.ds(k0, tk), :].astype(BF16))
        key = k0 + lax.broadcasted_iota(jnp.int32, (nsel, tk), 1)
        blk = lax.broadcasted_iota(jnp.int32, (nsel, tk), 0)
        expand = jnp.where(key // SEL_BLOCK == blk, 1.0, 0.0).astype(BF16)
        picked = _dot(sel, expand)
        kpos = k0 + lax.broadcasted_iota(jnp.int32, (tq, tk), 1)
        mask = ((picked > 0.5) & (kpos <= pos))[None]
        s = jnp.where(mask, s.reshape(g, tq, tk), NEG_BIG)
        m_prev = m_ref[...]
        m_new = jnp.maximum(m_prev, jnp.max(s, axis=-1, keepdims=True))
        p = jnp.exp(s - m_new)
        alpha = jnp.exp(m_prev - m_new)
        l_ref[...] = alpha * l_ref[...] + jnp.sum(p, axis=-1, keepdims=True)
        pv = _dot(p.reshape(g * tq, tk).astype(BF16), v_ref[pl.ds(k0, tk), :].astype(BF16))
        acc_ref[...] = alpha * acc_ref[...] + pv.reshape(g, tq, HEAD_DIM)
        m_ref[...] = m_new
        return carry

    lax.fori_loop(0, (qi * tq + tq - 1) // tk + 1, kv_step, 0)
    o = acc_ref[...] * (1.0 / jnp.maximum(l_ref[...], 1e-30))
    for i in range(g):
        o_ref[:, i * HEAD_DIM:(i + 1) * HEAD_DIM] = o[i]


def _selattn(proj, sel, tq=2 * Q_BLOCK, tk=512):
    bsz, t, _ = proj.shape
    nsel = t // SEL_BLOCK
    tk = min(tk, t)
    gw = NSA_GROUP * HEAD_DIM

    def kv_spec(base):
        return pl.BlockSpec((None, t, HEAD_DIM), lambda b, h, i: (b, 0, base + h))

    return pl.pallas_call(
        functools.partial(_selattn_kernel, tq=tq, tk=tk, nsel=nsel),
        grid=(bsz, NSA_KV_HEADS, t // tq),
        in_specs=[
            pl.BlockSpec((None, tq, gw), lambda b, h, i: (b, i, COL_NSA_Q // NSA_GROUP + h)),
            kv_spec(COL_KV + 2 * NSA_KV_HEADS), kv_spec(COL_KV + 3 * NSA_KV_HEADS),
            pl.BlockSpec((None, None, tq, nsel), lambda b, h, i: (b, h, i, 0)),
        ],
        out_specs=pl.BlockSpec((None, tq, gw), lambda b, h, i: (b, i, h)),
        out_shape=jax.ShapeDtypeStruct((bsz, t, NSA_WIDTH), F32),
        scratch_shapes=[pltpu.VMEM((NSA_GROUP, tq, 1), F32), pltpu.VMEM((NSA_GROUP, tq, 1), F32),
                        pltpu.VMEM((NSA_GROUP, tq, HEAD_DIM), F32)],
        compiler_params=_cparams(("arbitrary", "arbitrary", "arbitrary")),
        name="selattn",
    )(proj, proj, proj, sel)


def _winattn_kernel(*refs, tq, nblk):
    q_ref = refs[0]
    k_refs = refs[1:1 + nblk]
    v_refs = refs[1 + nblk:1 + 2 * nblk]
    ocmp_ref, osel_ref, small_ref, o_ref = refs[1 + 2 * nblk:]
    g = NSA_GROUP
    hk = pl.program_id(1)
    qi = pl.program_id(2)
    scale = HEAD_DIM ** -0.5
    nk = nblk * tq
    keys = jnp.concatenate([r[...] for r in k_refs], axis=0).astype(BF16)
    vals = jnp.concatenate([r[...] for r in v_refs], axis=0).astype(BF16)
    q4 = jnp.concatenate([q_ref[:, i * HEAD_DIM:(i + 1) * HEAD_DIM] for i in range(g)], axis=0)
    s = (_dot_nt(q4.astype(BF16), keys) * scale).reshape(g, tq, nk)
    pos = qi * tq + lax.broadcasted_iota(jnp.int32, (tq, nk), 0)
    wpos = (qi - (nblk - 1)) * tq + lax.broadcasted_iota(jnp.int32, (tq, nk), 1)
    mask = ((wpos <= pos) & (wpos > pos - WINDOW) & (wpos >= 0))[None]
    s = jnp.where(mask, s, NEG_BIG)
    e = jnp.exp(s - jnp.max(s, axis=-1, keepdims=True)).astype(BF16)
    ov = _dot(e.reshape(g * tq, nk), jnp.concatenate([vals, jnp.ones_like(vals)], axis=1))
    o_win = (ov[:, 0:HEAD_DIM] * (1.0 / jnp.maximum(ov[:, HEAD_DIM:2 * HEAD_DIM], 1e-30))).reshape(g, tq, HEAD_DIM)
    small = small_ref[...]
    for i in range(g):
        head = hk * g + i
        gates = [jax.nn.sigmoid(_lane_pick(small, SMALL_NSA_GATE + j * NSA_HEADS + head)) for j in range(3)]
        cols = slice(i * HEAD_DIM, (i + 1) * HEAD_DIM)
        o = gates[0] * ocmp_ref[:, cols] + gates[1] * osel_ref[:, cols] + gates[2] * o_win[i]
        o_ref[:, cols] = o.astype(o_ref.dtype)


def _winattn(proj, o_cmp, o_sel, tq=2 * Q_BLOCK):
    bsz, t, _ = proj.shape
    nblk = WINDOW // tq + 1
    gw = NSA_GROUP * HEAD_DIM

    def kv_spec(base, d):
        return pl.BlockSpec((None, tq, HEAD_DIM),
                            lambda b, h, i: (b, jnp.maximum(i - (nblk - 1) + d, 0), base + h))

    grp = pl.BlockSpec((None, tq, gw), lambda b, h, i: (b, i, h))
    return pl.pallas_call(
        functools.partial(_winattn_kernel, tq=tq, nblk=nblk),
        grid=(bsz, NSA_KV_HEADS, t // tq),
        in_specs=(
            [pl.BlockSpec((None, tq, gw), lambda b, h, i: (b, i, COL_NSA_Q // NSA_GROUP + h))]
            + [kv_spec(COL_KV + 4 * NSA_KV_HEADS, d) for d in range(nblk)]
            + [kv_spec(COL_KV + 5 * NSA_KV_HEADS, d) for d in range(nblk)]
            + [grp, grp, pl.BlockSpec((None, tq, HEAD_DIM), lambda b, h, i: (b, i, COL_SMALL))]
        ),
        out_specs=grp,
        out_shape=jax.ShapeDtypeStruct((bsz, t, NSA_WIDTH), BF16),
        compiler_params=_cparams(("arbitrary", "arbitrary", "arbitrary")),
        name="winattn",
    )(proj, *([proj] * (2 * nblk)), o_cmp, o_sel, proj)


def _ffn(h, wg, wu, wd):
    act = _silu(_dot(h, wg)) * _dot(h, wu)
    return _dot(act.astype(BF16), wd)


def _pack_bf16_pairs(x):
    half = x.shape[1] // 2
    lo = pltpu.bitcast(x[:, :half].astype(BF16).astype(F32), jnp.uint32)
    hi = pltpu.bitcast(x[:, half:].astype(BF16).astype(F32), jnp.uint32)
    return hi | (lo >> 16)


def _unpack_bf16_pairs(p):
    return pltpu.bitcast(p << 16, F32), pltpu.bitcast(p & jnp.uint32(0xFFFF0000), F32)


def _store_token_rows(ref, x, first=0):
    n, d = x.shape
    s_tiles = d // HEAD_DIM
    for s in range(s_tiles):
        ref[pl.ds(first * s_tiles + s, n, stride=s_tiles), :] = x[:, s * HEAD_DIM:(s + 1) * HEAD_DIM]


def _load_token_rows(ref, n, d, first=0):
    s_tiles = d // HEAD_DIM
    return jnp.concatenate([ref[pl.ds(first * s_tiles + s, n, stride=s_tiles), :] for s in range(s_tiles)], axis=1)


def _outproj_kernel(yg_ref, yn_ref, x_ref, wo_ref, gate_ref, shift_ref, scale_ref, gate_m_ref, nw_ref,
                    rw_ref, rb_ref, sg_ref, su_ref, sd_ref, x1_ref, h_ref, route_ref, count_ref, carry_ref):
    tm = x_ref.shape[0]
    half = yg_ref.shape[1]

    @pl.when((pl.program_id(0) == 0) & (pl.program_id(1) == 0))
    def _():
        carry_ref[...] = jnp.zeros_like(carry_ref)

    y = _dot(yg_ref[...], wo_ref[0:half, :]) + _dot(yn_ref[...], wo_ref[half:2 * half, :])
    x1 = x_ref[...] + gate_ref[...] * y
    h = _modulated_norm(x1, nw_ref[...], scale_ref[...], shift_ref[...])
    _store_token_rows(h_ref, _pack_bf16_pairs(h))
    x1_ref[...] = x1 + gate_m_ref[...] * _ffn(h.astype(BF16), sg_ref[...], su_ref[...], sd_ref[...])
    scores = jax.nn.sigmoid(_dot_split(h, rw_ref[...]))
    n_exp = scores.shape[1]
    lane = lax.broadcasted_iota(jnp.int32, scores.shape, 1)
    cand = scores + rb_ref[...]
    picked = jnp.zeros(scores.shape, jnp.bool_)
    firsts = []
    for _ in range(TOP_K):
        best = jnp.max(cand, axis=-1, keepdims=True)
        first = jnp.min(jnp.where(cand == best, lane, n_exp), axis=-1, keepdims=True)
        hit = lane == first
        picked = picked | hit
        cand = jnp.where(hit, NEG_BIG, cand)
        firsts.append(first)
    norm = ROUTED_SCALE / jnp.sum(jnp.where(picked, scores, 0.0), axis=-1, keepdims=True)
    onehot = jnp.where(picked, 1.0, 0.0)
    ri = lax.broadcasted_iota(jnp.int32, (tm, tm), 0)
    ci = lax.broadcasted_iota(jnp.int32, (tm, tm), 1)
    before = jnp.where(ci < ri, 1.0, 0.0).astype(BF16)
    pos = _dot(before, onehot.astype(BF16)) + carry_ref[...]
    carry_ref[...] += jnp.sum(onehot, axis=0, keepdims=True)
    count_ref[...] = carry_ref[...]
    slot = lax.broadcasted_iota(jnp.int32, (tm, 3 * TOP_K), 1)
    route = jnp.zeros((tm, 3 * TOP_K), F32)
    for j, first in enumerate(firsts):
        hit = lane == first
        route = jnp.where(slot == j, first.astype(F32), route)
        route = jnp.where(slot == TOP_K + j, _lane_pick(scores, first) * norm, route)
        route = jnp.where(slot == 2 * TOP_K + j, jnp.sum(jnp.where(hit, pos, 0.0), axis=-1, keepdims=True), route)
    route_ref[...] = route


def _outproj(y_gdn, y_nsa, x, w_out_b, mod4, norm_w, router_w, router_bias, sg, su, sd, tm=256):
    bsz, t, d = x.shape
    half = y_gdn.shape[2]
    n_exp = router_w.shape[1]
    sff = sg.shape[1]
    s_tiles = d // 2 // HEAD_DIM

    def mod(j):
        return pl.BlockSpec((None, None, 1, d), lambda b, i: (b, j, 0, 0))

    def const(shape):
        return pl.BlockSpec(shape, lambda b, i: (0, 0))

    row = pl.BlockSpec((None, tm, d), lambda b, i: (b, i, 0))
    return pl.pallas_call(
        _outproj_kernel,
        grid=(bsz, t // tm),
        in_specs=[
            pl.BlockSpec((None, tm, half), lambda b, i: (b, i, 0)),
            pl.BlockSpec((None, tm, half), lambda b, i: (b, i, 0)),
            row,
            const((2 * half, d)),
            mod(2), mod(3), mod(4), mod(5),
            const((1, d)), const((d, n_exp)), const((1, n_exp)),
            const((d, sff)), const((d, sff)), const((sff, d)),
        ],
        out_specs=[row,
                   pl.BlockSpec((tm * s_tiles, HEAD_DIM), lambda b, i: (b * (t // tm) + i, 0)),
                   pl.BlockSpec((None, tm, 3 * TOP_K), lambda b, i: (b, i, 0)),
                   const((1, n_exp))],
        out_shape=[
            jax.ShapeDtypeStruct((bsz, t, d), F32),
            jax.ShapeDtypeStruct((bsz * t * s_tiles, HEAD_DIM), jnp.uint32),
            jax.ShapeDtypeStruct((bsz, t, 3 * TOP_K), F32),
            jax.ShapeDtypeStruct((1, n_exp), F32),
        ],
        scratch_shapes=[pltpu.VMEM((1, n_exp), F32)],
        compiler_params=_cparams(("arbitrary", "arbitrary")),
        name="outproj",
    )(y_gdn, y_nsa, x, w_out_b, mod4, mod4, mod4, mod4, norm_w.reshape(1, d), router_w,
      router_bias.reshape(1, n_exp), sg, su, sd)


MOE_TM = 256
MOE_TOK = 512
MOE_TILES_PER_STEP = 4


def _token_row(ref, r, s_tiles):
    return ref.at[pl.ds(pl.multiple_of(r * s_tiles, s_tiles), s_tiles), :]


def _for_each(n, fn, unroll=8):
    def body(i, carry):
        fn(i)
        return carry

    lax.fori_loop(0, n, body, 0, unroll=unroll)


def _moe_scatter_kernel(gstart_ref, gend_ref, idx_ref, pos_ref, h_ref, xs_hbm, zero_ref, sem, *, s_tiles):
    n_exp = gstart_ref.shape[0]

    @pl.when(pl.program_id(0) == 0)
    def _():
        zero_ref[...] = jnp.zeros_like(zero_ref)

        def tail(e):
            first = pl.multiple_of((gend_ref[e] - MOE_TM) * s_tiles, MOE_TM * s_tiles)
            return pltpu.make_async_copy(zero_ref, xs_hbm.at[pl.ds(first, MOE_TM * s_tiles), :], sem)

        def start_tail(e):
            @pl.when(gend_ref[e] > gstart_ref[e])
            def _():
                tail(e).start()

        def wait_tail(e):
            @pl.when(gend_ref[e] > gstart_ref[e])
            def _():
                tail(e).wait()

        _for_each(n_exp, start_tail)
        _for_each(n_exp, wait_tail)

    def row_copy(t, j):
        n = t * TOP_K + j
        row = gstart_ref[idx_ref[n]] + pos_ref[n]
        return pltpu.make_async_copy(_token_row(h_ref, t, s_tiles), _token_row(xs_hbm, row, s_tiles), sem)

    def start_token(t):
        for j in range(TOP_K):
            row_copy(t, j).start()

    def wait_token(t):
        for j in range(TOP_K):
            row_copy(t, j).wait()

    _for_each(MOE_TOK, start_token, unroll=2)
    _for_each(MOE_TOK, wait_token, unroll=2)


def _moe_scatter(h_rows, idx, pos, gstart, gend, n_rows, s_tiles):
    n_steps = idx.shape[0] // (MOE_TOK * TOP_K)
    smem = pl.BlockSpec((MOE_TOK * TOP_K,), lambda i, gs, ge: (i,), memory_space=pltpu.SMEM)
    return pl.pallas_call(
        functools.partial(_moe_scatter_kernel, s_tiles=s_tiles),
        grid_spec=pltpu.PrefetchScalarGridSpec(
            num_scalar_prefetch=2,
            grid=(n_steps,),
            in_specs=[smem, smem, pl.BlockSpec((MOE_TOK * s_tiles, HEAD_DIM), lambda i, gs, ge: (i, 0))],
            out_specs=pl.BlockSpec(memory_space=pl.ANY),
            scratch_shapes=[pltpu.VMEM((MOE_TM * s_tiles, HEAD_DIM), h_rows.dtype), pltpu.SemaphoreType.DMA(())],
        ),
        out_shape=jax.ShapeDtypeStruct((n_rows * s_tiles, HEAD_DIM), h_rows.dtype),
        compiler_params=_cparams(("arbitrary",)),
        name="moe_scatter",
    )(gstart, gend, idx, pos, h_rows)


def _moe_expert_kernel(texp_ref, tslot_ref, tnext_ref, nvalid_ref, xs_ref, wg_hbm, wu_hbm, wd_hbm, ys_ref,
                       wg_buf, wu_buf, wd_buf, wgb_ref, wub_ref, wdb_ref, sems):
    d = wgb_ref.shape[0]
    half = d // 2
    rows_per_tile = xs_ref.shape[0] // MOE_TILES_PER_STEP

    def fetch(expert, s):
        return [pltpu.make_async_copy(src.at[expert], dst.at[s], sems.at[s, n])
                for n, (src, dst) in enumerate(((wg_hbm, wg_buf), (wu_hbm, wu_buf), (wd_hbm, wd_buf)))]

    def tile(i, sub):
        valid = i < nvalid_ref[0]
        new_expert = (i == 0) | (texp_ref[i] != texp_ref[jnp.maximum(i - 1, 0)])
        slot = tslot_ref[i]

        @pl.when(i == 0)
        def _():
            for cp in fetch(texp_ref[0], slot):
                cp.start()

        @pl.when(valid & new_expert)
        def _():
            nxt = tnext_ref[i]

            @pl.when(nxt < nvalid_ref[0])
            def _():
                for cp in fetch(texp_ref[nxt], 1 - slot):
                    cp.start()

            for cp in fetch(texp_ref[i], slot):
                cp.wait()
            wgb_ref[...] = wg_buf[slot].astype(BF16)
            wub_ref[...] = wu_buf[slot].astype(BF16)
            wdb_ref[...] = wd_buf[slot].astype(BF16)

        @pl.when(valid)
        def _():
            lo, hi = _unpack_bf16_pairs(_load_token_rows(xs_ref, MOE_TM, half, first=sub * MOE_TM))
            lo, hi = lo.astype(BF16), hi.astype(BF16)
            gate = _dot(lo, wgb_ref[0:half, :]) + _dot(hi, wgb_ref[half:d, :])
            up = _dot(lo, wub_ref[0:half, :]) + _dot(hi, wub_ref[half:d, :])
            y = _dot((_silu(gate) * up).astype(BF16), wdb_ref[...])
            _store_token_rows(ys_ref, _pack_bf16_pairs(y), first=sub * MOE_TM)

        @pl.when(jnp.logical_not(valid))
        def _():
            ys_ref[sub * rows_per_tile:(sub + 1) * rows_per_tile, :] = jnp.zeros((rows_per_tile, HEAD_DIM), ys_ref.dtype)

    for sub in range(MOE_TILES_PER_STEP):
        tile(pl.program_id(0) * MOE_TILES_PER_STEP + sub, sub)


def _moe_experts(xs, tile_expert, tile_slot, tile_next, nvalid, wg, wu, wd):
    n_exp, d, ff = wg.shape
    n_steps = tile_expert.shape[0] // MOE_TILES_PER_STEP
    blk = xs.shape[0] // n_steps
    hbm = pl.BlockSpec(memory_space=pl.ANY)
    vmem_bytes = 3 * d * ff * (2 * 4 + 2) + 4 * blk * HEAD_DIM * 4 + 3 * MOE_TM * d * 4
    return pl.pallas_call(
        _moe_expert_kernel,
        grid_spec=pltpu.PrefetchScalarGridSpec(
            num_scalar_prefetch=4,
            grid=(n_steps,),
            in_specs=[
                pl.BlockSpec((blk, HEAD_DIM),
                             lambda i, te, ts, tn, nv: (jnp.minimum(i, (nv[0] - 1) // MOE_TILES_PER_STEP), 0)),
                hbm, hbm, hbm,
            ],
            out_specs=pl.BlockSpec((blk, HEAD_DIM), lambda i, te, ts, tn, nv: (i, 0)),
            scratch_shapes=[pltpu.VMEM((2, d, ff), wg.dtype), pltpu.VMEM((2, d, ff), wu.dtype),
                            pltpu.VMEM((2, ff, d), wd.dtype),
                            pltpu.VMEM((d, ff), BF16), pltpu.VMEM((d, ff), BF16), pltpu.VMEM((ff, d), BF16),
                            pltpu.SemaphoreType.DMA((2, 3))],
        ),
        out_shape=jax.ShapeDtypeStruct(xs.shape, xs.dtype),
        compiler_params=_cparams(("arbitrary",), vmem_bytes),
        name="moe_experts",
    )(tile_expert, tile_slot, tile_next, nvalid, xs, wg, wu, wd)


def _moe_combine_kernel(gstart_ref, idx_ref, pos_ref, w_ref, ys_hbm, x1_ref, gate_ref, nw_ref, o_ref,
                        buf_ref, lo_ref, hi_ref, sems, *, s_tiles):
    tm, d = x1_ref.shape
    n_parts = sems.shape[0]
    part_tok = tm // n_parts
    part_rows = part_tok * TOP_K

    def row_copy(n, sem):
        row = gstart_ref[idx_ref[n]] + pos_ref[n]
        return pltpu.make_async_copy(_token_row(ys_hbm, row, s_tiles), _token_row(buf_ref, n, s_tiles), sem)

    def token(t):
        n0 = t * TOP_K
        lo, hi = _unpack_bf16_pairs(_token_row(buf_ref, n0, s_tiles)[...])
        acc_lo, acc_hi = w_ref[n0] * lo, w_ref[n0] * hi
        for j in range(1, TOP_K):
            lo, hi = _unpack_bf16_pairs(_token_row(buf_ref, n0 + j, s_tiles)[...])
            acc_lo, acc_hi = acc_lo + w_ref[n0 + j] * lo, acc_hi + w_ref[n0 + j] * hi
        _token_row(lo_ref, t, s_tiles)[...] = acc_lo
        _token_row(hi_ref, t, s_tiles)[...] = acc_hi

    for part in range(n_parts):
        _for_each(part_rows, lambda n, part=part: row_copy(part * part_rows + n, sems.at[part]).start())
    for part in range(n_parts):
        _for_each(part_rows, lambda n, part=part: row_copy(part * part_rows + n, sems.at[part]).wait())
        _for_each(part_tok, lambda t, part=part: token(part * part_tok + t))
    routed = jnp.concatenate([_load_token_rows(lo_ref, tm, d // 2), _load_token_rows(hi_ref, tm, d // 2)], axis=1)
    x2 = x1_ref[...] + gate_ref[...] * routed
    o_ref[...] = x2 * lax.rsqrt(jnp.mean(x2 * x2, axis=-1, keepdims=True) + EPS) * nw_ref[...]


def _moe_combine(ys, idx, pos, w, gstart, x1, mod4, norm_w):
    bsz, t, d = x1.shape
    s_tiles = d // 2 // HEAD_DIM
    tm = MOE_TOK
    nt = t // tm
    smem = pl.BlockSpec((tm * TOP_K,), lambda b, i, gs: (b * nt + i,), memory_space=pltpu.SMEM)
    row = pl.BlockSpec((None, tm, d), lambda b, i, gs: (b, i, 0))
    return pl.pallas_call(
        functools.partial(_moe_combine_kernel, s_tiles=s_tiles),
        grid_spec=pltpu.PrefetchScalarGridSpec(
            num_scalar_prefetch=1,
            grid=(bsz, nt),
            in_specs=[
                smem, smem, smem,
                pl.BlockSpec(memory_space=pl.ANY),
                row,
                pl.BlockSpec((None, None, 1, d), lambda b, i, gs: (b, 5, 0, 0)),
                pl.BlockSpec((1, d), lambda b, i, gs: (0, 0)),
            ],
            out_specs=row,
            scratch_shapes=[pltpu.VMEM((tm * TOP_K * s_tiles, HEAD_DIM), ys.dtype),
                            pltpu.VMEM((tm * s_tiles, HEAD_DIM), F32), pltpu.VMEM((tm * s_tiles, HEAD_DIM), F32),
                            pltpu.SemaphoreType.DMA((4,))],
        ),
        out_shape=jax.ShapeDtypeStruct((bsz, t, d), F32),
        compiler_params=_cparams(("arbitrary", "arbitrary")),
        name="moe_combine",
    )(gstart, idx, pos, w, ys, x1, mod4, norm_w.reshape(1, d))


def _moe(h_rows, route, counts, x1, mod4, norm_w, wg, wu, wd):
    bsz, t, d = x1.shape
    n_exp = wg.shape[0]
    s_tiles = h_rows.shape[0] // (bsz * t)
    idx = route[..., 0:TOP_K].astype(jnp.int32).reshape(-1)
    w = route[..., TOP_K:2 * TOP_K].reshape(-1)
    pos = route[..., 2 * TOP_K:3 * TOP_K].astype(jnp.int32).reshape(-1)
    padded = (counts.reshape(n_exp).astype(jnp.int32) + MOE_TM - 1) // MOE_TM * MOE_TM
    gend = jnp.cumsum(padded)
    gstart = gend - padded
    nvalid = gend[-1:] // MOE_TM
    n_tiles = bsz * t * TOP_K // MOE_TM + n_exp
    tile_start = jnp.minimum(jnp.arange(n_tiles, dtype=jnp.int32), nvalid - 1) * MOE_TM
    tile_expert = jnp.sum((gend[None, :] <= tile_start[:, None]).astype(jnp.int32), axis=1)
    first_of_group = jnp.concatenate([jnp.ones((1,), jnp.int32),
                                      (tile_expert[1:] != tile_expert[:-1]).astype(jnp.int32)])
    tile_slot = (jnp.cumsum(first_of_group) - 1) % 2
    experts = jnp.arange(n_exp, dtype=jnp.int32)
    tile_next = jnp.sum(jnp.where(tile_expert[:, None] == experts[None, :], gend[None, :], 0), axis=1) // MOE_TM
    xs = _moe_scatter(h_rows, idx, pos, gstart, gend, n_tiles * MOE_TM, s_tiles)
    ys = _moe_experts(xs, tile_expert, tile_slot, tile_next, nvalid, wg, wu, wd)
    return _moe_combine(ys, idx, pos, w, gstart, x1, mod4, norm_w)


def _reorder_w_in(w_in):
    d = w_in.shape[0]
    o_beta = 3 * GDN_WIDTH
    o_a = o_beta + GDN_HEADS
    o_gate = o_a + GDN_HEADS
    o_q = o_gate + GDN_WIDTH
    o_kv = o_q + NSA_WIDTH
    o_ng = o_kv + 6 * NSA_KV_WIDTH
    n_small = 2 * GDN_HEADS + 3 * NSA_HEADS
    parts = [w_in[:, :o_beta], w_in[:, o_gate:o_q], w_in[:, o_q:o_kv], w_in[:, o_kv:o_ng],
             w_in[:, o_beta:o_gate], w_in[:, o_ng:o_ng + 3 * NSA_HEADS]]
    used = COL_SMALL * HEAD_DIM + n_small
    parts.append(jnp.zeros((d, PROJ_COLS - used), w_in.dtype))
    return jnp.concatenate([p.astype(BF16) for p in parts], axis=1)


def _layer(x, mod, norm_attn_w, norm_ffn_w, w_in, gdn_conv_w, gdn_a_log, gdn_dt_bias, gdn_norm_w,
           cmp_pe_k, cmp_w1_k, cmp_w2_k, cmp_pe_v, cmp_w1_v, cmp_w2_v, w_out, router_w, router_bias,
           expert_w_gate, expert_w_up, expert_w_down, shared_w_gate, shared_w_up, shared_w_down, final_w):
    bsz, t, d = x.shape
    mod4 = mod.reshape(bsz, N_MOD, 1, d)
    proj = _proj(x, mod4, norm_attn_w, _reorder_w_in(w_in))
    y_gdn = _gdn(proj, gdn_conv_w, gdn_a_log, gdn_dt_bias, gdn_norm_w)
    kc = _compress(proj, COL_KV, cmp_pe_k, cmp_w1_k, cmp_w2_k)
    vc = _compress(proj, COL_KV + NSA_KV_HEADS, cmp_pe_v, cmp_w1_v, cmp_w2_v)
    o_cmp, sel = _cmpattn(proj, kc, vc)
    o_sel = _selattn(proj, sel)
    y_nsa = _winattn(proj, o_cmp, o_sel)
    x1, h_rows, route, counts = _outproj(
        y_gdn, y_nsa, x, w_out.astype(BF16), mod4, norm_ffn_w, router_w, router_bias,
        shared_w_gate.astype(BF16), shared_w_up.astype(BF16), shared_w_down.astype(BF16))
    return _moe(h_rows, route, counts, x1, mod4, final_w, expert_w_gate, expert_w_up, expert_w_down)


def kernel(x, c, w_ada, b_ada, norm_attn_w, norm_ffn_w, norm_final_w, w_in, gdn_conv_w, gdn_a_log, gdn_dt_bias,
           gdn_norm_w, cmp_pe_k, cmp_w1_k, cmp_w2_k, cmp_pe_v, cmp_w1_v, cmp_w2_v, w_out, router_w, router_bias,
           expert_w_gate, expert_w_up, expert_w_down, shared_w_gate, shared_w_up, shared_w_down):
    depth = w_ada.shape[0]
    assert depth == 1, "the fused final norm assumes a single layer"
    l = 0
    mod = _ada(c, w_ada[l], b_ada[l])
    return _layer(x, mod, norm_attn_w[l], norm_ffn_w[l], w_in[l], gdn_conv_w[l], gdn_a_log[l], gdn_dt_bias[l],
                  gdn_norm_w[l], cmp_pe_k[l], cmp_w1_k[l], cmp_w2_k[l], cmp_pe_v[l], cmp_w1_v[l], cmp_w2_v[l],
                  w_out[l], router_w[l], router_bias[l], expert_w_gate[l], expert_w_up[l], expert_w_down[l],
                  shared_w_gate[l], shared_w_up[l], shared_w_down[l], norm_final_w)
```
